```python
import math
import jax, jax.numpy as jnp
from jax import lax
import numpy as np

D_MODEL = 1024
BATCH = 8
SEQ = 4096
DEPTH = 2

CHUNK = 64
QBLOCK = 128
A_HEADS = 8
A_QK_DIM = 64
A_V_DIM = 64
A_KV_RANK = 256
IDX_HEADS = 4
IDX_DIM = 64
TOPK_MAX = 256
REL_BUCKETS = 32
REL_MAX_DIST = 1024
B_WIDTH = 512
B_GROUPS = 8
SHORT_CONV = 3
SSM_HEADS = 16
SSM_HEAD_DIM = 64
SSM_INNER = SSM_HEADS * SSM_HEAD_DIM
SSM_GROUPS = 2
SSM_HEADS_PER_GROUP = SSM_HEADS // SSM_GROUPS
SSM_STATE = 128
SSM_CONV = 4
SSM_XBC = SSM_INNER + 2 * SSM_GROUPS * SSM_STATE
MIX_WIDTH = A_HEADS * A_V_DIM + B_WIDTH + SSM_INNER
D_FF = -(-8 * D_MODEL // (3 * 256)) * 256
PLE_DIM = 256
NORM_EPS = 1e-6
IN_SPLITS = (A_HEADS * A_QK_DIM, A_KV_RANK, IDX_HEADS * IDX_DIM, IDX_DIM, IDX_HEADS,
             B_WIDTH, B_WIDTH, B_WIDTH,
             SSM_INNER, SSM_XBC, SSM_HEADS)
IN_WIDTH = sum(IN_SPLITS)

kernel_name = "hybrid_dsa_shortconv_ssd_trunk"


def rmsnorm(x, g):
    xf = x.astype(jnp.float32)
    y = xf * lax.rsqrt(jnp.mean(xf * xf, axis=-1, keepdims=True) + NORM_EPS)
    return (y * g.astype(jnp.float32)).astype(x.dtype)


def layernorm(x, g, b):
    xf = x.astype(jnp.float32)
    mu = jnp.mean(xf, axis=-1, keepdims=True)
    var = jnp.mean(jnp.square(xf - mu), axis=-1, keepdims=True)
    y = (xf - mu) * lax.rsqrt(var + NORM_EPS)
    return (y * g.astype(jnp.float32) + b.astype(jnp.float32)).astype(x.dtype)


def causal_depthwise_conv(u, w):
    k_w = w.shape[0]
    s = u.shape[1]
    up = jnp.pad(u, ((0, 0), (k_w - 1, 0), (0, 0)))
    y = up[:, 0:s] * w[0]
    for j in range(1, k_w):
        y = y + up[:, j:j + s] * w[j]
    return y


def t5_bucket(rel):
    half = REL_BUCKETS // 2
    max_exact = half // 2
    ret = jnp.where(rel > 0, half, 0)
    n = jnp.abs(rel)
    nf = jnp.maximum(n, 1).astype(jnp.float32)
    large = max_exact + (jnp.log(nf / max_exact) / math.log(REL_MAX_DIST / max_exact)
                         * (half - max_exact)).astype(jnp.int32)
    large = jnp.minimum(large, half - 1)
    return ret + jnp.where(n < max_exact, n, large)


def sparse_indexed_attention(q, c_kv, iq, ik, iw, w_uk, w_uv, rel_bias):
    bsz, s = q.shape[0], q.shape[1]
    k_top = min(TOPK_MAX, s // 4)
    n_blocks = s // QBLOCK
    q_lat = jnp.einsum("bshd,rhd->bshr", q, w_uk) * (A_QK_DIM ** -0.5)
    iq = iq * (IDX_DIM ** -0.5)
    iw = iw * (IDX_HEADS ** -0.5)
    key_chunk = jnp.arange(s, dtype=jnp.int32) // CHUNK

    def to_blocks(a):
        return jnp.moveaxis(a.reshape((bsz, n_blocks, QBLOCK) + a.shape[2:]), 1, 0)

    def block(args):
        ql, iqb, iwb, t0 = args
        qpos = t0 + jnp.arange(QBLOCK, dtype=jnp.int32)
        qchunk = qpos // CHUNK
        raw = jnp.einsum("bthd,bsd->bths", iqb, ik).astype(jnp.float32)
        idx_score = jnp.einsum("bths,bth->bts", jax.nn.relu(raw), iwb.astype(jnp.float32))
        admissible = key_chunk[None, :] <= qchunk[:, None]
        idx_score = jnp.where(admissible[None], idx_score, -jnp.inf)
        _, sel = lax.top_k(idx_score, k_top)
        valid = (sel // CHUNK) <= qchunk[None, :, None]
        kv_sel = jax.vmap(lambda c, i: c[i])(c_kv, sel)
        logits = jnp.einsum("bthr,btkr->bthk", ql, kv_sel).astype(jnp.float32)
        bias = rel_bias[t5_bucket(sel - qpos[None, :, None])].astype(jnp.float32)
        logits = logits + jnp.transpose(bias, (0, 1, 3, 2))
        logits = jnp.where(valid[:, :, None, :], logits, -jnp.inf)
        probs = jax.nn.softmax(logits, axis=-1).astype(kv_sel.dtype)
        return jnp.einsum("bthk,btkr->bthr", probs, kv_sel)

    t0s = jnp.arange(n_blocks, dtype=jnp.int32) * QBLOCK
    out = lax.map(block, (to_blocks(q_lat), to_blocks(iq), to_blocks(iw), t0s))
    out = jnp.moveaxis(out, 0, 1).reshape(bsz, s, A_HEADS, A_KV_RANK)
    out = jnp.einsum("bshr,rhd->bshd", out, w_uv)
    return out.reshape(bsz, s, A_HEADS * A_V_DIM)


def gated_short_conv(h, b_gate, c_gate, w_conv):
    return b_gate * causal_depthwise_conv(c_gate * h, w_conv)


def segsum(x):
    t = x.shape[-1]
    xr = jnp.broadcast_to(x[..., :, None], x.shape + (t,))
    xr = jnp.where(jnp.tril(jnp.ones((t, t), dtype=bool), -1), xr, 0.0)
    cs = jnp.cumsum(xr, axis=-2)
    return jnp.where(jnp.tril(jnp.ones((t, t), dtype=bool)), cs, -jnp.inf)


def ssd_chunked(xh, a, bm, cm):
    b, s, g, r, p = xh.shape
    n = bm.shape[-1]
    nc = s // CHUNK
    xh = xh.reshape(b, nc, CHUNK, g, r, p)
    bm = bm.reshape(b, nc, CHUNK, g, n)
    cm = cm.reshape(b, nc, CHUNK, g, n)
    a = jnp.transpose(a.reshape(b, nc, CHUNK, g, r), (0, 3, 4, 1, 2))
    a_cs = jnp.cumsum(a, axis=-1)
    l_mat = jnp.exp(segsum(a))
    cb = jnp.einsum("bclgn,bcsgn->bcgls", cm, bm)
    y_diag = jnp.einsum("bcgls,bgrcls,bcsgrp->bclgrp", cb, l_mat, xh)
    decay_states = jnp.exp(a_cs[..., -1:] - a_cs)
    states = jnp.einsum("bclgn,bgrcl,bclgrp->bcgrpn", bm, decay_states, xh)
    chunk_decay = jnp.exp(a_cs[..., -1])

    def step(carry, inp):
        st, dec = inp
        return carry * dec[..., None, None] + st, carry

    init = jnp.zeros((b, g, r, p, n), dtype=states.dtype)
    _, prev = lax.scan(step, init, (jnp.moveaxis(states, 1, 0), jnp.moveaxis(chunk_decay, -1, 0)))
    prev = jnp.moveaxis(prev, 0, 1)
    y_off = jnp.einsum("bclgn,bcgrpn,bgrcl->bclgrp", cm, prev, jnp.exp(a_cs))
    return (y_diag + y_off).reshape(b, s, g, r, p)


def mamba2_mixer(z, xbc, dt_raw, conv_w, conv_b, dt_bias, a_log, d_skip, norm_w):
    bsz, s = z.shape[0], z.shape[1]
    xbc = jax.nn.silu(causal_depthwise_conv(xbc, conv_w) + conv_b)
    xs = xbc[..., :SSM_INNER].reshape(bsz, s, SSM_GROUPS, SSM_HEADS_PER_GROUP, SSM_HEAD_DIM)
    bm = xbc[..., SSM_INNER:SSM_INNER + SSM_GROUPS * SSM_STATE].reshape(bsz, s, SSM_GROUPS, SSM_STATE)
    cm = xbc[..., SSM_INNER + SSM_GROUPS * SSM_STATE:].reshape(bsz, s, SSM_GROUPS, SSM_STATE)
    dt = jax.nn.softplus(dt_raw.astype(jnp.float32) + dt_bias.astype(jnp.float32))
    dt = dt.reshape(bsz, s, SSM_GROUPS, SSM_HEADS_PER_GROUP)
    a = -jnp.exp(a_log.astype(jnp.float32)).reshape(SSM_GROUPS, SSM_HEADS_PER_GROUP)
    xf = xs.astype(jnp.float32)
    y = ssd_chunked(xf * dt[..., None], dt * a, bm.astype(jnp.float32), cm.astype(jnp.float32))
    y = y + d_skip.astype(jnp.float32).reshape(SSM_GROUPS, SSM_HEADS_PER_GROUP)[..., None] * xf
    gw = SSM_HEADS_PER_GROUP * SSM_HEAD_DIM
    y = y.reshape(bsz, s, SSM_GROUPS, gw) * jax.nn.silu(z.astype(jnp.float32)).reshape(bsz, s, SSM_GROUPS, gw)
    y = rmsnorm(y, norm_w.reshape(SSM_GROUPS, gw))
    return y.reshape(bsz, s, SSM_INNER).astype(z.dtype)


def split_offsets():
    return [int(v) for v in np.cumsum(np.array(IN_SPLITS))[:-1]]


def setup_inputs(seed: int = 0) -> dict:
    key = jax.random.key(seed)
    ks = jax.random.split(key, 26)
    f32 = jnp.float32

    def nrm(k, shape, scale):
        return jax.random.normal(k, shape, f32) * scale

    def gain(k, shape):
        return 1.0 + 0.05 * jax.random.normal(k, shape, f32)

    u_dt = jax.random.uniform(ks[13], (DEPTH, SSM_HEADS), f32)
    dt0 = jnp.exp(u_dt * (math.log(0.1) - math.log(0.001)) + math.log(0.001))
    return {
        "x": nrm(ks[0], (BATCH, SEQ, D_MODEL), 1.0),
        "p": nrm(ks[1], (DEPTH, BATCH, SEQ, PLE_DIM), 1.0),
        "pre_mix_norm": gain(ks[2], (DEPTH, D_MODEL)),
        "post_mix_norm": gain(ks[3], (DEPTH, D_MODEL)),
        "pre_ffn_norm": gain(ks[4], (DEPTH, D_MODEL)),
        "post_ffn_norm": gain(ks[5], (DEPTH, D_MODEL)),
        "w_in": nrm(ks[6], (DEPTH, D_MODEL, IN_WIDTH), D_MODEL ** -0.5),
        "kv_norm": gain(ks[7], (DEPTH, A_KV_RANK)),
        "idx_k_norm_g": gain(ks[8], (DEPTH, IDX_DIM)),
        "idx_k_norm_b": nrm(ks[9], (DEPTH, IDX_DIM), 0.02),
        "w_uk": nrm(ks[10], (DEPTH, A_KV_RANK, A_HEADS, A_QK_DIM), A_KV_RANK ** -0.5),
        "w_uv": nrm(ks[11], (DEPTH, A_KV_RANK, A_HEADS, A_V_DIM), A_KV_RANK ** -0.5),
        "rel_bias": nrm(ks[12], (REL_BUCKETS, A_HEADS), 0.5),
        "short_conv_w": nrm(ks[14], (DEPTH, SHORT_CONV, B_WIDTH), SHORT_CONV ** -0.5),
        "ssm_conv_w": nrm(ks[15], (DEPTH, SSM_CONV, SSM_XBC), SSM_CONV ** -0.5),
        "ssm_conv_b": nrm(ks[16], (DEPTH, SSM_XBC), 0.02),
        "ssm_dt_bias": dt0 + jnp.log(-jnp.expm1(-dt0)),
        "ssm_a_log": jnp.log(jax.random.uniform(ks[17], (DEPTH, SSM_HEADS), f32, 1.0, 16.0)),
        "ssm_d": 1.0 + 0.1 * jax.random.normal(ks[18], (DEPTH, SSM_HEADS), f32),
        "ssm_norm": gain(ks[19], (DEPTH, SSM_INNER)),
        "w_out": nrm(ks[20], (DEPTH, MIX_WIDTH, D_MODEL), MIX_WIDTH ** -0.5),
        "w_ffn_gate": nrm(ks[21], (DEPTH, D_MODEL, D_FF), D_MODEL ** -0.5),
        "w_ffn_up": nrm(ks[22], (DEPTH, D_MODEL, D_FF), D_MODEL ** -0.5),
        "w_ffn_down": nrm(ks[23], (DEPTH, D_FF, D_MODEL), D_FF ** -0.5),
        "w_ple_proj": nrm(ks[24], (DEPTH, PLE_DIM, D_MODEL), PLE_DIM ** -0.5),
        "w_ple_gate": nrm(ks[25], (DEPTH, D_MODEL, D_MODEL), D_MODEL ** -0.5),
    }


def reference(x, p, pre_mix_norm, post_mix_norm, pre_ffn_norm, post_ffn_norm, w_in,
              kv_norm, idx_k_norm_g, idx_k_norm_b, w_uk, w_uv, rel_bias, short_conv_w,
              ssm_conv_w, ssm_conv_b, ssm_dt_bias, ssm_a_log, ssm_d, ssm_norm, w_out,
              w_ffn_gate, w_ffn_up, w_ffn_down, w_ple_proj, w_ple_gate):
    bsz, s = x.shape[0], x.shape[1]
    offsets = split_offsets()
    for i in range(DEPTH):
        h = rmsnorm(x, pre_mix_norm[i])
        proj = h @ w_in[i]
        q, ckv, iq, ik, iw, b_gate, c_gate, hb, z, xbc, dt_raw = jnp.split(proj, offsets, axis=-1)
        a_out = sparse_indexed_attention(
            q.reshape(bsz, s, A_HEADS, A_QK_DIM),
            rmsnorm(ckv, kv_norm[i]),
            iq.reshape(bsz, s, IDX_HEADS, IDX_DIM),
            layernorm(ik, idx_k_norm_g[i], idx_k_norm_b[i]),
            iw, w_uk[i], w_uv[i], rel_bias)
        b_out = gated_short_conv(hb, b_gate, c_gate, short_conv_w[i])
        c_out = mamba2_mixer(z, xbc, dt_raw, ssm_conv_w[i], ssm_conv_b[i], ssm_dt_bias[i],
                             ssm_a_log[i], ssm_d[i], ssm_norm[i])
        mix = jnp.concatenate([a_out, b_out, c_out], axis=-1) @ w_out[i]
        x = x + rmsnorm(mix, post_mix_norm[i])
        h = rmsnorm(x, pre_ffn_norm[i])
        f = (jax.nn.silu(h @ w_ffn_gate[i]) * (h @ w_ffn_up[i])) @ w_ffn_down[i]
        x = x + rmsnorm(f, post_ffn_norm[i])
        x = x + jax.nn.sigmoid(x @ w_ple_gate[i]) * (p[i] @ w_ple_proj[i])
    return x
```

```python
import functools
import math

import numpy as np
import jax
import jax.numpy as jnp
from jax import lax
from jax.experimental import pallas as pl
from jax.experimental.pallas import tpu as pltpu

F32, BF16, I32 = jnp.float32, jnp.bfloat16, jnp.int32

D_MODEL = 1024
DEPTH = 2
CHUNK = 64
A_HEADS = 8
A_QK_DIM = 64
A_V_DIM = 64
A_KV_RANK = 256
IDX_HEADS = 4
IDX_DIM = 64
TOPK_MAX = 256
REL_BUCKETS = 32
REL_MAX_DIST = 1024
B_WIDTH = 512
SHORT_CONV = 3
SSM_HEADS = 16
SSM_HEAD_DIM = 64
SSM_INNER = SSM_HEADS * SSM_HEAD_DIM
SSM_GROUPS = 2
SSM_STATE = 128
SSM_CONV = 4
SSM_XBC = SSM_INNER + 2 * SSM_GROUPS * SSM_STATE
MIX_WIDTH = A_HEADS * A_V_DIM + B_WIDTH + SSM_INNER
D_FF = -(-8 * D_MODEL // (3 * 256)) * 256
PLE_DIM = 256
NORM_EPS = 1e-6
IN_SPLITS = (A_HEADS * A_QK_DIM, A_KV_RANK, IDX_HEADS * IDX_DIM, IDX_DIM, IDX_HEADS,
             B_WIDTH, B_WIDTH, B_WIDTH, SSM_INNER, SSM_XBC, SSM_HEADS)

LANES = 128
SUBLANES = 8
VMEM_LIMIT_BYTES = 56 * 1024 * 1024

QB = LANES
KT = 256
SSD_L = 128
TM_IN = 256
TM_OUT = 256
SMALL_W = LANES
IW_OFF = IDX_DIM
DT_OFF = IDX_DIM + IDX_HEADS
INT_MIN = np.int32(-2 ** 31)


def _const_spec(shape):
    nd = len(shape)
    return pl.BlockSpec(shape, lambda *_: (0,) * nd, pipeline_mode=pl.Buffered(1))


def _params(sem):
    return pltpu.CompilerParams(dimension_semantics=sem, vmem_limit_bytes=VMEM_LIMIT_BYTES)


def _rms(x, g):
    return x * lax.rsqrt(jnp.mean(x * x, axis=-1, keepdims=True) + NORM_EPS) * g


def _sigmoid(x):
    return 1.0 / (1.0 + jnp.exp(-x))


def _split3(v):
    hi = v.astype(BF16)
    r1 = v - hi.astype(F32)
    mid = r1.astype(BF16)
    lo = (r1 - mid.astype(F32)).astype(BF16)
    return hi, mid, lo


def _dot(a, b):
    return jnp.dot(a, b, preferred_element_type=F32)


def _t5_bucket_np(rel):
    half = REL_BUCKETS // 2
    max_exact = half // 2
    ret = np.where(rel > 0, half, 0)
    n = np.abs(rel)
    nf = np.maximum(n, 1).astype(np.float32)
    large = max_exact + (np.log(nf / np.float32(max_exact)) / np.float32(math.log(REL_MAX_DIST / max_exact))
                         * np.float32(half - max_exact)).astype(np.int32)
    large = np.minimum(large, half - 1)
    return (ret + np.where(n < max_exact, n, large)).astype(np.int32)


@functools.lru_cache(maxsize=None)
def _bucket_tiles():
    kr = np.arange(KT)[:, None]
    qc = np.arange(QB)[None, :]
    tiles, near = [], None
    for o in range(64):
        t = _t5_bucket_np(kr - qc - o * QB)
        tiles.append(t)
    for o in range(63, -1, -1):
        if not (np.all(tiles[o] == tiles[63][0, 0])):
            near = o + 1
            break
    assert near is not None and near < 63
    return np.stack(tiles[:near + 1]), near


def _bias_kernel(rb_ref, bk_ref, out_ref):
    bk = bk_ref[0]
    for h in range(A_HEADS):
        acc = jnp.zeros(bk.shape, F32)
        for b in range(REL_BUCKETS):
            acc = jnp.where(bk == b, rb_ref[b, h], acc)
        out_ref[0, :, h * QB:(h + 1) * QB] = acc


def _bias_tiles_call(rel_bias, bucket_tiles):
    n = bucket_tiles.shape[0]
    return pl.pallas_call(
        _bias_kernel,
        grid=(n,),
        in_specs=[pl.BlockSpec(memory_space=pltpu.SMEM),
                  pl.BlockSpec((1, KT, QB), lambda o: (o, 0, 0))],
        out_specs=pl.BlockSpec((1, KT, A_HEADS * QB), lambda o: (o, 0, 0)),
        out_shape=jax.ShapeDtypeStruct((n, KT, A_HEADS * QB), F32),
        compiler_params=_params(("arbitrary",)),
        name="rel_bias_tiles",
    )(rel_bias, bucket_tiles)


def _inproj_kernel(x_ref, g_ref, kvn_ref, ikg_ref, ikb_ref,
                   wq_ref, wckv_ref, wiq_ref, wsm_ref, wb_ref, wz_ref, wxbc_ref,
                   q_out, ckv_out, iq_out, ik_out, sm_out, b_out, z_out, xbc_out):
    h = _rms(x_ref[...], g_ref[...]).astype(BF16)
    q_out[...] = _dot(h, wq_ref[...]).astype(BF16)
    ckv_out[...] = _rms(_dot(h, wckv_ref[...]), kvn_ref[...]).astype(BF16)
    iq_out[...] = (_dot(h, wiq_ref[...]) * (IDX_DIM ** -0.5)).astype(BF16)
    sm = _dot(h, wsm_ref[...])
    sm_out[...] = sm
    ik = sm[:, :IDX_DIM]
    mu = jnp.mean(ik, axis=-1, keepdims=True)
    var = jnp.mean(jnp.square(ik - mu), axis=-1, keepdims=True)
    ik_out[...] = ((ik - mu) * lax.rsqrt(var + NORM_EPS) * ikg_ref[...] + ikb_ref[...]).astype(BF16)
    b_out[...] = _dot(h, wb_ref[...])
    z_out[...] = _dot(h, wz_ref[...])
    xbc_out[...] = _dot(h, wxbc_ref[...])


def _inproj_call(x2, g, kvn, ikg, ikb, wq, wckv, wiq, wsm, wb, wz, wxbc):
    t = x2.shape[0]
    tm = TM_IN
    row = lambda w: pl.BlockSpec((tm, w), lambda r: (r, 0))
    outs = [(wq.shape[1], BF16), (wckv.shape[1], BF16), (wiq.shape[1], BF16), (IDX_DIM, BF16),
            (SMALL_W, F32), (wb.shape[1], F32), (wz.shape[1], F32), (wxbc.shape[1], F32)]
    consts = [g, kvn, ikg, ikb, wq, wckv, wiq, wsm, wb, wz, wxbc]
    return pl.pallas_call(
        _inproj_kernel,
        grid=(t // tm,),
        in_specs=[row(D_MODEL)] + [_const_spec(c.shape) for c in consts],
        out_specs=[row(w) for w, _ in outs],
        out_shape=[jax.ShapeDtypeStruct((t, w), dt) for w, dt in outs],
        compiler_params=_params(("parallel",)),
        name="inproj",
    )(x2, *consts)


def _attn_kernel(q_ref, iq_ref, iw_ref, ckv_ref, ckvt_ref, ik_ref, wuk_ref, wuvt_ref, bias_ref, tri_ref,
                 o_ref, keys_ref, qlat_ref, acc_ref, *, k_top, near):
    i = pl.program_id(1)
    nt = (i + 2) // 2
    lane = lax.broadcasted_iota(I32, (1, QB), 1)
    qchunk = (i * QB + lane) >> int(math.log2(CHUNK))
    krow = lax.broadcasted_iota(I32, (KT, QB), 0)

    for h in range(A_HEADS):
        ql = lax.dot_general(wuk_ref[h], q_ref[0, 0, h], (((1,), (1,)), ((), ())),
                             preferred_element_type=F32)
        qlat_ref[:, h * QB:(h + 1) * QB] = (ql * (A_QK_DIM ** -0.5)).astype(BF16)

    iw = iw_ref[0] * (IDX_HEADS ** -0.5)

    def score_tile(j, carry):
        r0 = pl.multiple_of(j * KT, KT)
        ikt = ik_ref[0, pl.ds(r0, KT), :]
        s = jnp.zeros((KT, QB), F32)
        for h in range(IDX_HEADS):
            raw = lax.dot_general(ikt, iq_ref[0, 0, h], (((1,), (1,)), ((), ())),
                                  preferred_element_type=F32)
            s = s + jnp.maximum(raw, 0.0) * iw[h:h + 1, :]
        adm = ((r0 + krow) >> int(math.log2(CHUNK))) <= qchunk
        s = jnp.where(adm, s, -jnp.inf)
        bits = lax.bitcast_convert_type(s, I32)
        bits = jnp.where(bits == INT_MIN, 0, bits)
        keys_ref[pl.ds(r0, KT), :] = bits ^ ((bits >> 31) & np.int32(0x7FFFFFFF))
        return carry

    lax.fori_loop(0, nt, score_tile, 0)

    def count(pred):
        def body(j, cnt):
            r0 = pl.multiple_of(j * KT, KT)
            kt = keys_ref[pl.ds(r0, KT), :]
            return cnt + jnp.sum(jnp.where(pred(kt), 1.0, 0.0), axis=0, keepdims=True)
        return lax.fori_loop(0, nt, body, jnp.zeros((1, QB), F32))

    def search_step(step, theta):
        bit = jnp.left_shift(jnp.int32(1), (31 - step).astype(I32))
        cand = theta | bit
        cand_s = cand ^ INT_MIN
        cnt = count(lambda kt: kt >= cand_s)
        return jnp.where(cnt >= float(k_top), cand, theta)

    theta = lax.fori_loop(0, 32, search_step, jnp.zeros((1, QB), I32))
    thr = theta ^ INT_MIN
    need = float(k_top) - count(lambda kt: kt > thr)

    acc_ref[...] = jnp.zeros(acc_ref.shape, F32)
    bias_far = bias_ref[near, 0:1, :]

    def attn_tile(j, carry, is_near):
        ms, ls, cnt_eq = carry
        r0 = pl.multiple_of(j * KT, KT)
        kt = keys_ref[pl.ds(r0, KT), :]
        eq = kt == thr
        eqf = jnp.where(eq, 1.0, 0.0)
        rank = _dot(tri_ref[...], eqf.astype(BF16)) + cnt_eq
        sel = (kt > thr) | (eq & (rank <= need))
        if is_near:
            sel = sel & (((r0 + krow) >> int(math.log2(CHUNK))) <= qchunk)
            bias = bias_ref[i - 2 * j]
        else:
            bias = bias_far
        cnt_eq = cnt_eq + jnp.sum(eqf, axis=0, keepdims=True)
        logits = _dot(ckv_ref[0, pl.ds(r0, KT), :], qlat_ref[...])
        new_ms, new_ls, alphas, ps = [], [], [], []
        for h in range(A_HEADS):
            sl = slice(h * QB, (h + 1) * QB)
            lh = jnp.where(sel, logits[:, sl] + bias[:, sl], -jnp.inf)
            m_new = jnp.maximum(ms[h], jnp.max(lh, axis=0, keepdims=True))
            m_safe = jnp.where(m_new == -jnp.inf, 0.0, m_new)
            p = jnp.exp(lh - m_safe)
            alpha = jnp.exp(ms[h] - m_safe)
            new_ms.append(m_new)
            new_ls.append(alpha * ls[h] + jnp.sum(p, axis=0, keepdims=True))
            alphas.append(alpha)
            ps.append(p.astype(BF16))
        pv = _dot(ckvt_ref[0, j], jnp.concatenate(ps, axis=1))
        acc_ref[...] = acc_ref[...] * jnp.concatenate(alphas, axis=1) + pv
        return tuple(new_ms), tuple(new_ls), cnt_eq

    n_far = jnp.maximum(i - near + 2, 0) // 2
    init = (tuple(jnp.full((1, QB), -jnp.inf, F32) for _ in range(A_HEADS)),
            tuple(jnp.zeros((1, QB), F32) for _ in range(A_HEADS)),
            jnp.zeros((1, QB), F32))
    carry = lax.fori_loop(0, n_far, functools.partial(attn_tile, is_near=False), init)
    _, ls, _ = lax.fori_loop(n_far, nt, functools.partial(attn_tile, is_near=True), carry)

    inv_l = jnp.concatenate([1.0 / l for l in ls], axis=1)
    out = (acc_ref[...] * inv_l).astype(BF16)
    a_t = jnp.concatenate(
        [_dot(wuvt_ref[h], out[:, h * QB:(h + 1) * QB]) for h in range(A_HEADS)], axis=0)
    o_ref[...] = a_t.T.astype(BF16)


def _attn_call(q5, iq5, iw_t, ckv3, ckvt4, ik3, wuk, wuvt, bias_tiles, tri, *, k_top, near):
    b, nb = q5.shape[0], q5.shape[1]
    s = ckv3.shape[1]
    n_kt = s // KT
    kern = functools.partial(_attn_kernel, k_top=k_top, near=near)
    return pl.pallas_call(
        kern,
        grid=(b, nb),
        in_specs=[
            pl.BlockSpec((1, 1, A_HEADS, QB, A_QK_DIM), lambda bb, i: (bb, i, 0, 0, 0)),
            pl.BlockSpec((1, 1, IDX_HEADS, QB, IDX_DIM), lambda bb, i: (bb, i, 0, 0, 0)),
            pl.BlockSpec((1, IDX_HEADS, QB), lambda bb, i: (bb, 0, i)),
            pl.BlockSpec((1, s, A_KV_RANK), lambda bb, i: (bb, 0, 0)),
            pl.BlockSpec((1, n_kt, A_KV_RANK, KT), lambda bb, i: (bb, 0, 0, 0)),
            pl.BlockSpec((1, s, IDX_DIM), lambda bb, i: (bb, 0, 0)),
            _const_spec(wuk.shape), _const_spec(wuvt.shape),
            _const_spec(bias_tiles.shape), _const_spec(tri.shape),
        ],
        out_specs=pl.BlockSpec((QB, A_HEADS * A_V_DIM), lambda bb, i: (bb * nb + i, 0)),
        out_shape=jax.ShapeDtypeStruct((b * s, A_HEADS * A_V_DIM), BF16),
        scratch_shapes=[pltpu.VMEM((s, QB), I32),
                        pltpu.VMEM((A_KV_RANK, A_HEADS * QB), BF16),
                        pltpu.VMEM((A_KV_RANK, A_HEADS * QB), F32)],
        compiler_params=_params(("parallel", "arbitrary")),
        name="dsa_attention",
    )(q5, iq5, iw_t, ckv3, ckvt4, ik3, wuk, wuvt, bias_tiles, tri)


def _softplus(x):
    return jnp.maximum(x, 0.0) + jnp.log1p(jnp.exp(-jnp.abs(x)))


def _mixer_kernel(bm_ref, z_ref, xbc_ref, sm_ref, dtt_ref, scw_ref, cw_ref, cb_ref,
                  dtb_r_ref, dtb_c_ref, alog_r_ref, alog_c_ref, dexp_ref, nw_ref,
                  tril_ref, triu_ref, eh_ref,
                  bo_ref, co_ref, ubuf, xbuf, state):
    L = SSD_L
    P = SUBLANES
    gw = SSM_INNER // SSM_GROUPS

    @pl.when(pl.program_id(1) == 0)
    def _():
        ubuf[0:P, :] = jnp.zeros((P, B_WIDTH), F32)
        xbuf[0:P, :] = jnp.zeros((P, SSM_XBC), F32)
        state[...] = jnp.zeros(state.shape, F32)

    bm = bm_ref[...]
    u = bm[:, B_WIDTH:2 * B_WIDTH] * bm[:, 2 * B_WIDTH:3 * B_WIDTH]
    ubuf[P:P + L, :] = u
    conv = scw_ref[SHORT_CONV - 1:SHORT_CONV, :] * u
    for j in range(SHORT_CONV - 1):
        off = P - (SHORT_CONV - 1) + j
        conv = conv + scw_ref[j:j + 1, :] * ubuf[off:off + L, :]
    bo_ref[...] = (bm[:, :B_WIDTH] * conv).astype(BF16)
    ubuf[0:P, :] = u[L - P:L, :]

    xr = xbc_ref[...]
    xbuf[P:P + L, :] = xr
    xc = cw_ref[SSM_CONV - 1:SSM_CONV, :] * xr + cb_ref[...]
    for j in range(SSM_CONV - 1):
        off = P - (SSM_CONV - 1) + j
        xc = xc + cw_ref[j:j + 1, :] * xbuf[off:off + L, :]
    xbuf[0:P, :] = xr[L - P:L, :]
    xa = xc * _sigmoid(xc)
    xs = xa[:, :SSM_INNER]

    dt = _softplus(sm_ref[:, DT_OFF:DT_OFF + SSM_HEADS] + dtb_r_ref[...])
    a = dt * (-jnp.exp(alog_r_ref[...]))
    a_cs = sum(_dot(tril_ref[...], part) for part in _split3(a))
    dt_t = _softplus(dtt_ref[0] + dtb_c_ref[...])
    a_t = dt_t * (-jnp.exp(alog_c_ref[...]))
    a_cs_t = sum(_dot(part, triu_ref[...]) for part in _split3(a_t))

    def expand(v):
        return sum(_dot(part, eh_ref[...]) for part in _split3(v))

    xdt = xs * expand(dt)
    e_cs = expand(jnp.exp(a_cs))
    dec = expand(jnp.exp(a_cs[L - 1:L, :] - a_cs))
    chunk_dec = e_cs[L - 1:L, :]

    row = lax.broadcasted_iota(I32, (L, L), 0)
    col = lax.broadcasted_iota(I32, (L, L), 1)
    causal = row >= col
    lo_half = lax.broadcasted_iota(I32, (L, LANES), 1) < SSM_HEAD_DIM
    hpg = SSM_HEADS // SSM_GROUPS

    y_parts = []
    for g in range(SSM_GROUPS):
        bg = xa[:, SSM_INNER + g * SSM_STATE:SSM_INNER + (g + 1) * SSM_STATE].astype(BF16)
        cg = xa[:, SSM_INNER + (SSM_GROUPS + g) * SSM_STATE:
                SSM_INNER + (SSM_GROUPS + g + 1) * SSM_STATE].astype(BF16)
        cb = lax.dot_general(cg, bg, (((1,), (1,)), ((), ())), preferred_element_type=F32)
        sl = slice(g * gw, (g + 1) * gw)
        st = state[g]
        y_off = _dot(cg, st.astype(BF16)) * e_cs[:, sl]
        for k in range(hpg // 2):
            xp = xdt[:, g * gw + k * LANES:g * gw + (k + 1) * LANES]
            halves = (jnp.where(lo_half, xp, 0.0).astype(BF16), jnp.where(lo_half, 0.0, xp).astype(BF16))
            y = y_off[:, k * LANES:(k + 1) * LANES]
            for hh in range(2):
                h = g * hpg + 2 * k + hh
                seg = a_cs[:, h:h + 1] - a_cs_t[h:h + 1, :]
                m = (cb * jnp.exp(jnp.where(causal, seg, -jnp.inf))).astype(BF16)
                y = y + _dot(m, halves[hh])
            y_parts.append(y)
        xw = (dec[:, sl] * xdt[:, sl]).astype(BF16)
        upd = lax.dot_general(bg, xw, (((0,), (0,)), ((), ())), preferred_element_type=F32)
        state[g] = chunk_dec[:, sl] * st + upd

    y = jnp.concatenate(y_parts, axis=1) + dexp_ref[...] * xs
    zz = z_ref[...]
    y = y * (zz * _sigmoid(zz))
    outs = []
    for g in range(SSM_GROUPS):
        sl = slice(g * gw, (g + 1) * gw)
        outs.append(_rms(y[:, sl], nw_ref[:, sl]))
    co_ref[...] = jnp.concatenate(outs, axis=1).astype(BF16)


def _mixer_call(bmix, z, xbc, sm, dt_t, scw, cw, cb, dtb, alog, dexp, nw, b, s):
    L = SSD_L
    nc = s // L
    tril = jnp.asarray(np.tril(np.ones((L, L), np.float32)), BF16)
    triu = jnp.asarray(np.triu(np.ones((L, L), np.float32)), BF16)
    eh = jnp.asarray(np.repeat(np.eye(SSM_HEADS, dtype=np.float32), SSM_HEAD_DIM, axis=1), BF16)
    row = lambda w: pl.BlockSpec((L, w), lambda bb, c: (bb * nc + c, 0))
    consts = [scw, cw, cb.reshape(1, -1), dtb.reshape(1, -1), dtb.reshape(-1, 1),
              alog.reshape(1, -1), alog.reshape(-1, 1), dexp.reshape(1, -1), nw.reshape(1, -1),
              tril, triu, eh]
    return pl.pallas_call(
        _mixer_kernel,
        grid=(b, nc),
        in_specs=[row(3 * B_WIDTH), row(SSM_INNER), row(SSM_XBC), row(SMALL_W),
                  pl.BlockSpec((1, SSM_HEADS, L), lambda bb, c: (bb, 0, c))]
                 + [_const_spec(c.shape) for c in consts],
        out_specs=[row(B_WIDTH), row(SSM_INNER)],
        out_shape=[jax.ShapeDtypeStruct((b * s, B_WIDTH), BF16),
                   jax.ShapeDtypeStruct((b * s, SSM_INNER), BF16)],
        scratch_shapes=[pltpu.VMEM((L + SUBLANES, B_WIDTH), F32),
                        pltpu.VMEM((L + SUBLANES, SSM_XBC), F32),
                        pltpu.VMEM((SSM_GROUPS, SSM_STATE, SSM_INNER // SSM_GROUPS), F32)],
        compiler_params=_params(("parallel", "arbitrary")),
        name="conv_ssd_mixers",
    )(bmix, z, xbc, sm, dt_t, *consts)


def _dense_kernel(x_ref, a_ref, b_ref, c_ref, p_ref, g_post_ref, g_pre_ref, g_fpost_ref,
                  woa_ref, wob_ref, woc_ref, wg_ref, wu_ref, wd_ref, wpg_ref, wpp_ref, o_ref):
    mix = _dot(a_ref[...], woa_ref[...]) + _dot(b_ref[...], wob_ref[...]) + _dot(c_ref[...], woc_ref[...])
    x = x_ref[...] + _rms(mix, g_post_ref[...])
    h = _rms(x, g_pre_ref[...]).astype(BF16)
    gate = _dot(h, wg_ref[...])
    up = _dot(h, wu_ref[...])
    act = (gate * _sigmoid(gate) * up).astype(BF16)
    x = x + _rms(_dot(act, wd_ref[...]), g_fpost_ref[...])
    pg = _sigmoid(_dot(x.astype(BF16), wpg_ref[...]))
    o_ref[...] = x + pg * _dot(p_ref[...].astype(BF16), wpp_ref[...])


def _dense_call(x2, a, b, c, p2, g_post, g_pre, g_fpost, woa, wob, woc, wg, wu, wd, wpg, wpp):
    t = x2.shape[0]
    tm = TM_OUT
    row = lambda w: pl.BlockSpec((tm, w), lambda r: (r, 0))
    consts = [g_post, g_pre, g_fpost, woa, wob, woc, wg, wu, wd, wpg, wpp]
    return pl.pallas_call(
        _dense_kernel,
        grid=(t // tm,),
        in_specs=[row(D_MODEL), row(a.shape[1]), row(b.shape[1]), row(c.shape[1]), row(PLE_DIM)]
                 + [_const_spec(w.shape) for w in consts],
        out_specs=row(D_MODEL),
        out_shape=jax.ShapeDtypeStruct((t, D_MODEL), F32),
        compiler_params=_params(("parallel",)),
        name="outproj_ffn_ple",
    )(x2, a, b, c, p2, *consts)


def kernel(x, p, pre_mix_norm, post_mix_norm, pre_ffn_norm, post_ffn_norm, w_in, kv_norm, idx_k_norm_g, idx_k_norm_b, w_uk, w_uv, rel_bias, short_conv_w, ssm_conv_w, ssm_conv_b, ssm_dt_bias, ssm_a_log, ssm_d, ssm_norm, w_out, w_ffn_gate, w_ffn_up, w_ffn_down, w_ple_proj, w_ple_gate):
    bsz, s, d = x.shape
    assert d == D_MODEL and s % KT == 0 and s % SSD_L == 0 and (bsz * s) % max(TM_IN, TM_OUT) == 0
    t = bsz * s
    nb = s // QB
    k_top = min(TOPK_MAX, s // 4)
    offs = [0] + [int(v) for v in np.cumsum(np.array(IN_SPLITS))]
    (o_q, o_ckv, o_iq, o_ik, o_iw, o_bg, _o_cg, _o_hb, o_z, o_xbc, o_dt, o_end) = offs

    bucket_np, near = _bucket_tiles()
    bias_tiles = _bias_tiles_call(rel_bias.astype(F32), jnp.asarray(bucket_np))
    tri = jnp.asarray(np.tril(np.ones((KT, KT), np.float32)), BF16)
    row1 = lambda v: v.reshape(1, -1).astype(F32)

    x2 = x.reshape(t, d)
    for i in range(DEPTH):
        w = w_in[i]
        w_small = jnp.concatenate(
            [w[:, o_ik:o_bg], w[:, o_dt:o_end],
             jnp.zeros((d, SMALL_W - (o_bg - o_ik) - (o_end - o_dt)), w.dtype)], axis=1)
        q, ckv, iq, ik, sm, bmix, z, xbc = _inproj_call(
            x2, row1(pre_mix_norm[i]), row1(kv_norm[i]), row1(idx_k_norm_g[i]), row1(idx_k_norm_b[i]),
            w[:, o_q:o_ckv].astype(BF16), w[:, o_ckv:o_iq].astype(BF16), w[:, o_iq:o_ik].astype(BF16),
            w_small.astype(BF16), w[:, o_bg:o_z].astype(BF16), w[:, o_z:o_xbc].astype(BF16),
            w[:, o_xbc:o_dt].astype(BF16))

        q5 = q.reshape(bsz, nb, QB, A_HEADS, A_QK_DIM).transpose(0, 1, 3, 2, 4)
        iq5 = iq.reshape(bsz, nb, QB, IDX_HEADS, IDX_DIM).transpose(0, 1, 3, 2, 4)
        iw_t = sm[:, IW_OFF:IW_OFF + IDX_HEADS].reshape(bsz, s, IDX_HEADS).transpose(0, 2, 1)
        ckv3 = ckv.reshape(bsz, s, A_KV_RANK)
        ckvt4 = ckv3.reshape(bsz, s // KT, KT, A_KV_RANK).transpose(0, 1, 3, 2)
        a_out = _attn_call(
            q5, iq5, iw_t, ckv3, ckvt4, ik.reshape(bsz, s, IDX_DIM),
            jnp.transpose(w_uk[i], (1, 0, 2)).astype(BF16),
            jnp.transpose(w_uv[i], (1, 2, 0)).astype(BF16),
            bias_tiles, tri, k_top=k_top, near=near)

        dt_t = sm[:, DT_OFF:DT_OFF + SSM_HEADS].reshape(bsz, s, SSM_HEADS).transpose(0, 2, 1)
        b_out, c_out = _mixer_call(
            bmix, z, xbc, sm, dt_t, short_conv_w[i].astype(F32), ssm_conv_w[i].astype(F32),
            ssm_conv_b[i].astype(F32), ssm_dt_bias[i].astype(F32), ssm_a_log[i].astype(F32),
            jnp.repeat(ssm_d[i].astype(F32), SSM_HEAD_DIM), ssm_norm[i].astype(F32), bsz, s)

        wo = w_out[i].astype(BF16)
        na, nbw = A_HEADS * A_V_DIM, A_HEADS * A_V_DIM + B_WIDTH
        x2 = _dense_call(
            x2, a_out, b_out, c_out, p[i].reshape(t, PLE_DIM),
            row1(post_mix_norm[i]), row1(pre_ffn_norm[i]), row1(post_ffn_norm[i]),
            wo[:na], wo[na:nbw], wo[nbw:], w_ffn_gate[i].astype(BF16), w_ffn_up[i].astype(BF16),
            w_ffn_down[i].astype(BF16), w_ple_gate[i].astype(BF16), w_ple_proj[i].astype(BF16))
    return x2.reshape(bsz, s, d)
```

```python
import functools
import math

import numpy as np
import jax
import jax.numpy as jnp
from jax import lax
from jax.experimental import pallas as pl
from jax.experimental.pallas import tpu as pltpu

F32, BF16, I32 = jnp.float32, jnp.bfloat16, jnp.int32

D_MODEL = 1024
DEPTH = 2
CHUNK = 64
A_HEADS = 8
A_QK_DIM = 64
A_V_DIM = 64
A_KV_RANK = 256
IDX_HEADS = 4
IDX_DIM = 64
TOPK_MAX = 256
REL_BUCKETS = 32
REL_MAX_DIST = 1024
B_WIDTH = 512
SHORT_CONV = 3
SSM_HEADS = 16
SSM_HEAD_DIM = 64
SSM_INNER = SSM_HEADS * SSM_HEAD_DIM
SSM_GROUPS = 2
SSM_STATE = 128
SSM_CONV = 4
SSM_XBC = SSM_INNER + 2 * SSM_GROUPS * SSM_STATE
MIX_WIDTH = A_HEADS * A_V_DIM + B_WIDTH + SSM_INNER
D_FF = -(-8 * D_MODEL // (3 * 256)) * 256
PLE_DIM = 256
NORM_EPS = 1e-6
IN_SPLITS = (A_HEADS * A_QK_DIM, A_KV_RANK, IDX_HEADS * IDX_DIM, IDX_DIM, IDX_HEADS,
             B_WIDTH, B_WIDTH, B_WIDTH, SSM_INNER, SSM_XBC, SSM_HEADS)

LANES = 128
SUBLANES = 8
VMEM_LIMIT_BYTES = 56 * 1024 * 1024

QB = LANES
KT = 256
SSD_L = 128
TM_OUT = 256
SMALL_W = LANES
IW_OFF = IDX_DIM
DT_OFF = IDX_DIM + IDX_HEADS
INT_MIN = np.int32(-2 ** 31)
SEARCH_TILES = 4
SEARCH_ROWS = SEARCH_TILES * KT

def _const_spec(shape):
    nd = len(shape)
    return pl.BlockSpec(shape, lambda *_: (0,) * nd, pipeline_mode=pl.Buffered(1))


def _params(sem):
    return pltpu.CompilerParams(dimension_semantics=sem, vmem_limit_bytes=VMEM_LIMIT_BYTES)


def _rms(x, g):
    return x * lax.rsqrt(jnp.mean(x * x, axis=-1, keepdims=True) + NORM_EPS) * g


def _sigmoid(x):
    return 1.0 / (1.0 + jnp.exp(-x))


def _split3(v):
    hi = v.astype(BF16)
    r1 = v - hi.astype(F32)
    mid = r1.astype(BF16)
    lo = (r1 - mid.astype(F32)).astype(BF16)
    return hi, mid, lo


def _dot(a, b):
    return jnp.dot(a, b, preferred_element_type=F32)


def _t5_bucket_np(rel):
    half = REL_BUCKETS // 2
    max_exact = half // 2
    ret = np.where(rel > 0, half, 0)
    n = np.abs(rel)
    nf = np.maximum(n, 1).astype(np.float32)
    large = max_exact + (np.log(nf / np.float32(max_exact)) / np.float32(math.log(REL_MAX_DIST / max_exact))
                         * np.float32(half - max_exact)).astype(np.int32)
    large = np.minimum(large, half - 1)
    return (ret + np.where(n < max_exact, n, large)).astype(np.int32)


@functools.lru_cache(maxsize=None)
def _bucket_tiles():
    kr = np.arange(KT)[:, None]
    qc = np.arange(QB)[None, :]
    tiles, near = [], None
    for o in range(64):
        t = _t5_bucket_np(kr - qc - o * QB)
        tiles.append(t)
    for o in range(63, -1, -1):
        if not (np.all(tiles[o] == tiles[63][0, 0])):
            near = o + 1
            break
    assert near is not None and near < 63
    return np.stack(tiles[:near + 1]), near


def _bias_kernel(rb_ref, bk_ref, out_ref):
    bk = bk_ref[0]
    for h in range(A_HEADS):
        acc = jnp.zeros(bk.shape, F32)
        for b in range(REL_BUCKETS):
            acc = jnp.where(bk == b, rb_ref[b, h], acc)
        out_ref[0, :, h * QB:(h + 1) * QB] = acc


def _bias_tiles_call(rel_bias, bucket_tiles):
    n = bucket_tiles.shape[0]
    return pl.pallas_call(
        _bias_kernel,
        grid=(n,),
        in_specs=[pl.BlockSpec(memory_space=pltpu.SMEM),
                  pl.BlockSpec((1, KT, QB), lambda o: (o, 0, 0))],
        out_specs=pl.BlockSpec((1, KT, A_HEADS * QB), lambda o: (o, 0, 0)),
        out_shape=jax.ShapeDtypeStruct((n, KT, A_HEADS * QB), F32),
        compiler_params=_params(("arbitrary",)),
        name="rel_bias_tiles",
    )(rel_bias, bucket_tiles)


def _inproj_kernel(x_ref, g_ref, kvn_ref, ikg_ref, ikb_ref,
                   wq_ref, wckv_ref, wiq_ref, wsm_ref, wb_ref, wz_ref, wxbc_ref, wuv_ref,
                   q_out, ckv_out, vt_out, iq_out, ik_out, sm_out, b_out, z_out, xbc_out):
    h = _rms(x_ref[...], g_ref[...]).astype(BF16)
    q_out[...] = _dot(h, wq_ref[...]).astype(BF16)
    ckv = _rms(_dot(h, wckv_ref[...]), kvn_ref[...]).astype(BF16)
    ckv_out[...] = ckv
    vt_out[0, 0] = lax.dot_general(wuv_ref[...], ckv, (((1,), (1,)), ((), ())),
                                   preferred_element_type=F32).astype(BF16)
    iq_out[...] = (_dot(h, wiq_ref[...]) * (IDX_DIM ** -0.5)).astype(BF16)
    sm = _dot(h, wsm_ref[...])
    sm_out[...] = sm
    ik = sm[:, :IDX_DIM]
    mu = jnp.mean(ik, axis=-1, keepdims=True)
    var = jnp.mean(jnp.square(ik - mu), axis=-1, keepdims=True)
    ik_out[...] = ((ik - mu) * lax.rsqrt(var + NORM_EPS) * ikg_ref[...] + ikb_ref[...]).astype(BF16)
    b_out[...] = _dot(h, wb_ref[...])
    z_out[...] = _dot(h, wz_ref[...])
    xbc_out[...] = _dot(h, wxbc_ref[...])


def _inproj_call(x2, g, kvn, ikg, ikb, wq, wckv, wiq, wsm, wb, wz, wxbc, wuv, n_kt):
    t = x2.shape[0]
    tm = KT
    row = lambda w: pl.BlockSpec((tm, w), lambda r: (r, 0))
    rows = lambda w, dt: (row(w), jax.ShapeDtypeStruct((t, w), dt))
    hv = wuv.shape[0]
    outs = [rows(wq.shape[1], BF16), rows(wckv.shape[1], BF16),
            (pl.BlockSpec((1, 1, hv, tm), lambda r: (r // n_kt, r % n_kt, 0, 0)),
             jax.ShapeDtypeStruct((t // (n_kt * tm), n_kt, hv, tm), BF16)),
            rows(wiq.shape[1], BF16), rows(IDX_DIM, BF16), rows(SMALL_W, F32),
            rows(wb.shape[1], F32), rows(wz.shape[1], F32), rows(wxbc.shape[1], F32)]
    consts = [g, kvn, ikg, ikb, wq, wckv, wiq, wsm, wb, wz, wxbc, wuv]
    return pl.pallas_call(
        _inproj_kernel,
        grid=(t // tm,),
        in_specs=[row(D_MODEL)] + [_const_spec(c.shape) for c in consts],
        out_specs=[spec for spec, _ in outs],
        out_shape=[shape for _, shape in outs],
        compiler_params=_params(("parallel",)),
        name="inproj",
    )(x2, *consts)


def _attn_kernel(q_ref, iq_ref, sm_ref, ckv_ref, vt_ref, ik_ref, wuk_ref, bias_ref, tri_ref,
                 o_ref, keys_ref, dg3_ref, dg2_ref, dg1_ref, dg0_ref, qlat_ref, acc_ref,
                 *, k_top, near):
    digit_refs = (dg3_ref, dg2_ref, dg1_ref, dg0_ref)
    i = pl.program_id(1)
    nt = (i + 2) // 2
    nch = (nt + SEARCH_TILES - 1) // SEARCH_TILES
    lane = lax.broadcasted_iota(I32, (1, QB), 1)
    qchunk = (i * QB + lane) >> int(math.log2(CHUNK))
    krow = lax.broadcasted_iota(I32, (KT, QB), 0)

    q_t = q_ref[...].astype(F32).T.astype(BF16)
    for h in range(A_HEADS):
        ql = _dot(wuk_ref[h], q_t[h * A_QK_DIM:(h + 1) * A_QK_DIM, :])
        qlat_ref[:, h * QB:(h + 1) * QB] = (ql * (A_QK_DIM ** -0.5)).astype(BF16)

    iq_t = iq_ref[...].astype(F32).T.astype(BF16)
    iw = sm_ref[...].T[IW_OFF:IW_OFF + IDX_HEADS, :] * (IDX_HEADS ** -0.5)

    def to_b(v):
        return v.astype(F32).astype(BF16)

    def score_tile(j, carry):
        r0 = pl.multiple_of(j * KT, KT)
        ikt = ik_ref[0, pl.ds(r0, KT), :]
        s = jnp.zeros((KT, QB), F32)
        for h in range(IDX_HEADS):
            raw = _dot(ikt, iq_t[h * IDX_DIM:(h + 1) * IDX_DIM, :])
            s = s + jnp.maximum(raw, 0.0) * iw[h:h + 1, :]
        adm = ((r0 + krow) >> int(math.log2(CHUNK))) <= qchunk
        s = jnp.where(adm, s, -jnp.inf)
        bits = lax.bitcast_convert_type(s, I32)
        bits = jnp.where(bits == INT_MIN, 0, bits)
        key = bits ^ ((bits >> 31) & np.int32(0x7FFFFFFF))
        keys_ref[pl.ds(r0, KT), :] = key
        dg3_ref[pl.ds(r0, KT), :] = to_b((key >> 24) + 128)
        dg2_ref[pl.ds(r0, KT), :] = to_b((key >> 16) & 0xFF)
        dg1_ref[pl.ds(r0, KT), :] = to_b((key >> 8) & 0xFF)
        dg0_ref[pl.ds(r0, KT), :] = to_b(key & 0xFF)
        return carry

    lax.fori_loop(0, nt, score_tile, 0)
    for t in range(SEARCH_TILES - 1):
        rp = pl.multiple_of((nt + t) * KT, KT)
        for ref in digit_refs:
            ref[pl.ds(rp, KT), :] = jnp.zeros((KT, QB), BF16)

    one_b, zero_b = jnp.ones((), BF16), jnp.zeros((), BF16)
    pack_rows = 2 * SUBLANES
    pack_vregs = SEARCH_ROWS // pack_rows

    def count(ref, cand, strict):
        def body(c, acc):
            r0 = pl.multiple_of(c * SEARCH_ROWS, SEARCH_ROWS)
            d = ref[pl.ds(r0, SEARCH_ROWS), :]
            hit = (d > cand) if strict else (d >= cand)
            h3 = jnp.where(hit, one_b, zero_b).reshape(pack_vregs, pack_rows, QB)
            parts = [h3[k] for k in range(pack_vregs)]
            while len(parts) > 1:
                parts = [parts[k] + parts[k + 1] for k in range(0, len(parts), 2)]
            return acc + parts[0].astype(F32)
        acc = lax.fori_loop(0, nch, body, jnp.zeros((pack_rows, QB), F32))
        return jnp.sum(acc, axis=0, keepdims=True)

    def keep_where_equal(ref, sel_ref, sel_val):
        def body(c, carry):
            rows = pl.ds(pl.multiple_of(c * SEARCH_ROWS, SEARCH_ROWS), SEARCH_ROWS)
            ref[rows, :] = jnp.where(sel_ref[rows, :] == sel_val, ref[rows, :], -jnp.ones((), BF16))
            return carry
        lax.fori_loop(0, nch, body, 0)

    def search_digit(ref, kk):
        def step(s, theta):
            cand = theta | jnp.left_shift(jnp.int32(1), (7 - s).astype(I32))
            return jnp.where(count(ref, to_b(cand), False) >= kk, cand, theta)
        return lax.fori_loop(0, 8, step, jnp.zeros((1, QB), I32))

    kk = jnp.full((1, QB), float(k_top), F32)
    thr = jnp.zeros((1, QB), I32)
    for lvl, ref in enumerate(digit_refs):
        if lvl > 0:
            keep_where_equal(ref, digit_refs[lvl - 1], t_b)
        t = search_digit(ref, kk)
        t_b = to_b(t)
        kk = kk - count(ref, t_b, True)
        thr = (thr << 8) + (t - 128 if lvl == 0 else t)
    need = kk

    acc_ref[...] = jnp.zeros(acc_ref.shape, F32)
    bias_far = bias_ref[near, 0:1, :]

    def attn_tile(j, carry, is_near):
        ms, ls, cnt_eq = carry
        r0 = pl.multiple_of(j * KT, KT)
        kt = keys_ref[pl.ds(r0, KT), :]
        eq = kt == thr
        eqf = jnp.where(eq, 1.0, 0.0)
        rank = _dot(tri_ref[...], eqf.astype(BF16)) + cnt_eq
        sel = (kt > thr) | (eq & (rank <= need))
        if is_near:
            sel = sel & (((r0 + krow) >> int(math.log2(CHUNK))) <= qchunk)
            bias = bias_ref[i - 2 * j]
        else:
            bias = bias_far
        cnt_eq = cnt_eq + jnp.sum(eqf, axis=0, keepdims=True)
        logits = _dot(ckv_ref[0, pl.ds(r0, KT), :], qlat_ref[...])
        new_ms, new_ls = [], []
        for h in range(A_HEADS):
            sl = slice(h * QB, (h + 1) * QB)
            vs = slice(h * A_V_DIM, (h + 1) * A_V_DIM)
            lh = jnp.where(sel, logits[:, sl] + bias[:, sl], -jnp.inf)
            m_new = jnp.maximum(ms[h], jnp.max(lh, axis=0, keepdims=True))
            m_safe = jnp.where(m_new == -jnp.inf, 0.0, m_new)
            p = jnp.exp(lh - m_safe)
            alpha = jnp.exp(ms[h] - m_safe)
            new_ms.append(m_new)
            new_ls.append(alpha * ls[h] + jnp.sum(p, axis=0, keepdims=True))
            pv = _dot(vt_ref[0, j, vs, :], p.astype(BF16))
            acc_ref[vs, :] = acc_ref[vs, :] * alpha + pv
        return tuple(new_ms), tuple(new_ls), cnt_eq

    n_far = jnp.maximum(i - near + 2, 0) // 2
    init = (tuple(jnp.full((1, QB), -jnp.inf, F32) for _ in range(A_HEADS)),
            tuple(jnp.zeros((1, QB), F32) for _ in range(A_HEADS)),
            jnp.zeros((1, QB), F32))
    carry = lax.fori_loop(0, n_far, functools.partial(attn_tile, is_near=False), init)
    _, ls, _ = lax.fori_loop(n_far, nt, functools.partial(attn_tile, is_near=True), carry)

    a_t = jnp.concatenate(
        [acc_ref[h * A_V_DIM:(h + 1) * A_V_DIM, :] * (1.0 / ls[h]) for h in range(A_HEADS)], axis=0)
    o_ref[...] = a_t.T.astype(BF16)


def _attn_call(q, iq, sm, ckv3, vt4, ik3, wuk, bias_tiles, tri, *, k_top, near):
    b, s = ckv3.shape[0], ckv3.shape[1]
    nb = s // QB
    n_kt = s // KT
    hv = A_HEADS * A_V_DIM
    kern = functools.partial(_attn_kernel, k_top=k_top, near=near)
    row = lambda w: pl.BlockSpec((QB, w), lambda bb, i: (bb * nb + i, 0))
    digits = pltpu.VMEM((s + (SEARCH_TILES - 1) * KT, QB), BF16)
    return pl.pallas_call(
        kern,
        grid=(b, nb),
        in_specs=[
            row(q.shape[1]), row(iq.shape[1]), row(SMALL_W),
            pl.BlockSpec((1, s, A_KV_RANK), lambda bb, i: (bb, 0, 0)),
            pl.BlockSpec((1, n_kt, hv, KT), lambda bb, i: (bb, 0, 0, 0)),
            pl.BlockSpec((1, s, IDX_DIM), lambda bb, i: (bb, 0, 0)),
            _const_spec(wuk.shape), _const_spec(bias_tiles.shape), _const_spec(tri.shape),
        ],
        out_specs=row(hv),
        out_shape=jax.ShapeDtypeStruct((b * s, hv), BF16),
        scratch_shapes=[pltpu.VMEM((s, QB), I32), digits, digits, digits, digits,
                        pltpu.VMEM((A_KV_RANK, A_HEADS * QB), BF16),
                        pltpu.VMEM((hv, QB), F32)],
        compiler_params=_params(("parallel", "arbitrary")),
        name="dsa_attention",
    )(q, iq, sm, ckv3, vt4, ik3, wuk, bias_tiles, tri)


def _softplus(x):
    return jnp.maximum(x, 0.0) + jnp.log1p(jnp.exp(-jnp.abs(x)))


def _mixer_kernel(bm_ref, z_ref, xbc_ref, sm_ref, dtt_ref, scw_ref, cw_ref, cb_ref,
                  dtb_r_ref, dtb_c_ref, alog_r_ref, alog_c_ref, dexp_ref, nw_ref,
                  tril_ref, triu_ref, eh_ref,
                  bo_ref, co_ref, ubuf, xbuf, state):
    L = SSD_L
    P = SUBLANES
    gw = SSM_INNER // SSM_GROUPS

    @pl.when(pl.program_id(1) == 0)
    def _():
        ubuf[0:P, :] = jnp.zeros((P, B_WIDTH), F32)
        xbuf[0:P, :] = jnp.zeros((P, SSM_XBC), F32)
        state[...] = jnp.zeros(state.shape, F32)

    bm = bm_ref[...]
    u = bm[:, B_WIDTH:2 * B_WIDTH] * bm[:, 2 * B_WIDTH:3 * B_WIDTH]
    ubuf[P:P + L, :] = u
    conv = scw_ref[SHORT_CONV - 1:SHORT_CONV, :] * u
    for j in range(SHORT_CONV - 1):
        off = P - (SHORT_CONV - 1) + j
        conv = conv + scw_ref[j:j + 1, :] * ubuf[off:off + L, :]
    bo_ref[...] = (bm[:, :B_WIDTH] * conv).astype(BF16)
    ubuf[0:P, :] = u[L - P:L, :]

    xr = xbc_ref[...]
    xbuf[P:P + L, :] = xr
    xc = cw_ref[SSM_CONV - 1:SSM_CONV, :] * xr + cb_ref[...]
    for j in range(SSM_CONV - 1):
        off = P - (SSM_CONV - 1) + j
        xc = xc + cw_ref[j:j + 1, :] * xbuf[off:off + L, :]
    xbuf[0:P, :] = xr[L - P:L, :]
    xa = xc * _sigmoid(xc)
    xs = xa[:, :SSM_INNER]

    dt = _softplus(sm_ref[:, DT_OFF:DT_OFF + SSM_HEADS] + dtb_r_ref[...])
    a = dt * (-jnp.exp(alog_r_ref[...]))
    a_cs = sum(_dot(tril_ref[...], part) for part in _split3(a))
    dt_t = _softplus(dtt_ref[0] + dtb_c_ref[...])
    a_t = dt_t * (-jnp.exp(alog_c_ref[...]))
    a_cs_t = sum(_dot(part, triu_ref[...]) for part in _split3(a_t))

    def expand(v):
        return sum(_dot(part, eh_ref[...]) for part in _split3(v))

    xdt = xs * expand(dt)
    e_cs = expand(jnp.exp(a_cs))
    dec = expand(jnp.exp(a_cs[L - 1:L, :] - a_cs))
    chunk_dec = e_cs[L - 1:L, :]

    row = lax.broadcasted_iota(I32, (L, L), 0)
    col = lax.broadcasted_iota(I32, (L, L), 1)
    causal = row >= col
    lo_half = lax.broadcasted_iota(I32, (L, LANES), 1) < SSM_HEAD_DIM
    hpg = SSM_HEADS // SSM_GROUPS

    y_parts = []
    for g in range(SSM_GROUPS):
        bg = xa[:, SSM_INNER + g * SSM_STATE:SSM_INNER + (g + 1) * SSM_STATE].astype(BF16)
        cg = xa[:, SSM_INNER + (SSM_GROUPS + g) * SSM_STATE:
                SSM_INNER + (SSM_GROUPS + g + 1) * SSM_STATE].astype(BF16)
        cb = lax.dot_general(cg, bg, (((1,), (1,)), ((), ())), preferred_element_type=F32)
        sl = slice(g * gw, (g + 1) * gw)
        st = state[g]
        y_off = _dot(cg, st.astype(BF16)) * e_cs[:, sl]
        for k in range(hpg // 2):
            xp = xdt[:, g * gw + k * LANES:g * gw + (k + 1) * LANES]
            halves = (jnp.where(lo_half, xp, 0.0).astype(BF16), jnp.where(lo_half, 0.0, xp).astype(BF16))
            y = y_off[:, k * LANES:(k + 1) * LANES]
            for hh in range(2):
                h = g * hpg + 2 * k + hh
                seg = a_cs[:, h:h + 1] - a_cs_t[h:h + 1, :]
                m = (cb * jnp.exp(jnp.where(causal, seg, -jnp.inf))).astype(BF16)
                y = y + _dot(m, halves[hh])
            y_parts.append(y)
        xw = (dec[:, sl] * xdt[:, sl]).astype(BF16)
        upd = lax.dot_general(bg, xw, (((0,), (0,)), ((), ())), preferred_element_type=F32)
        state[g] = chunk_dec[:, sl] * st + upd

    y = jnp.concatenate(y_parts, axis=1) + dexp_ref[...] * xs
    zz = z_ref[...]
    y = y * (zz * _sigmoid(zz))
    outs = []
    for g in range(SSM_GROUPS):
        sl = slice(g * gw, (g + 1) * gw)
        outs.append(_rms(y[:, sl], nw_ref[:, sl]))
    co_ref[...] = jnp.concatenate(outs, axis=1).astype(BF16)


def _mixer_call(bmix, z, xbc, sm, dt_t, scw, cw, cb, dtb, alog, dexp, nw, b, s):
    L = SSD_L
    nc = s // L
    tril = jnp.asarray(np.tril(np.ones((L, L), np.float32)), BF16)
    triu = jnp.asarray(np.triu(np.ones((L, L), np.float32)), BF16)
    eh = jnp.asarray(np.repeat(np.eye(SSM_HEADS, dtype=np.float32), SSM_HEAD_DIM, axis=1), BF16)
    row = lambda w: pl.BlockSpec((L, w), lambda bb, c: (bb * nc + c, 0))
    consts = [scw, cw, cb.reshape(1, -1), dtb.reshape(1, -1), dtb.reshape(-1, 1),
              alog.reshape(1, -1), alog.reshape(-1, 1), dexp.reshape(1, -1), nw.reshape(1, -1),
              tril, triu, eh]
    return pl.pallas_call(
        _mixer_kernel,
        grid=(b, nc),
        in_specs=[row(3 * B_WIDTH), row(SSM_INNER), row(SSM_XBC), row(SMALL_W),
                  pl.BlockSpec((1, SSM_HEADS, L), lambda bb, c: (bb, 0, c))]
                 + [_const_spec(c.shape) for c in consts],
        out_specs=[row(B_WIDTH), row(SSM_INNER)],
        out_shape=[jax.ShapeDtypeStruct((b * s, B_WIDTH), BF16),
                   jax.ShapeDtypeStruct((b * s, SSM_INNER), BF16)],
        scratch_shapes=[pltpu.VMEM((L + SUBLANES, B_WIDTH), F32),
                        pltpu.VMEM((L + SUBLANES, SSM_XBC), F32),
                        pltpu.VMEM((SSM_GROUPS, SSM_STATE, SSM_INNER // SSM_GROUPS), F32)],
        compiler_params=_params(("parallel", "arbitrary")),
        name="conv_ssd_mixers",
    )(bmix, z, xbc, sm, dt_t, *consts)


def _dense_kernel(x_ref, a_ref, b_ref, c_ref, p_ref, g_post_ref, g_pre_ref, g_fpost_ref,
                  woa_ref, wob_ref, woc_ref, wg_ref, wu_ref, wd_ref, wpg_ref, wpp_ref, o_ref):
    mix = _dot(a_ref[...], woa_ref[...]) + _dot(b_ref[...], wob_ref[...]) + _dot(c_ref[...], woc_ref[...])
    x = x_ref[...] + _rms(mix, g_post_ref[...])
    h = _rms(x, g_pre_ref[...]).astype(BF16)
    gate = _dot(h, wg_ref[...])
    up = _dot(h, wu_ref[...])
    act = (gate * _sigmoid(gate) * up).astype(BF16)
    x = x + _rms(_dot(act, wd_ref[...]), g_fpost_ref[...])
    pg = _sigmoid(_dot(x.astype(BF16), wpg_ref[...]))
    o_ref[...] = x + pg * _dot(p_ref[...].astype(BF16), wpp_ref[...])


def _dense_call(x2, a, b, c, p2, g_post, g_pre, g_fpost, woa, wob, woc, wg, wu, wd, wpg, wpp):
    t = x2.shape[0]
    tm = TM_OUT
    row = lambda w: pl.BlockSpec((tm, w), lambda r: (r, 0))
    consts = [g_post, g_pre, g_fpost, woa, wob, woc, wg, wu, wd, wpg, wpp]
    return pl.pallas_call(
        _dense_kernel,
        grid=(t // tm,),
        in_specs=[row(D_MODEL), row(a.shape[1]), row(b.shape[1]), row(c.shape[1]), row(PLE_DIM)]
                 + [_const_spec(w.shape) for w in consts],
        out_specs=row(D_MODEL),
        out_shape=jax.ShapeDtypeStruct((t, D_MODEL), F32),
        compiler_params=_params(("parallel",)),
        name="outproj_ffn_ple",
    )(x2, a, b, c, p2, *consts)


def kernel(x, p, pre_mix_norm, post_mix_norm, pre_ffn_norm, post_ffn_norm, w_in, kv_norm, idx_k_norm_g, idx_k_norm_b, w_uk, w_uv, rel_bias, short_conv_w, ssm_conv_w, ssm_conv_b, ssm_dt_bias, ssm_a_log, ssm_d, ssm_norm, w_out, w_ffn_gate, w_ffn_up, w_ffn_down, w_ple_proj, w_ple_gate):
    bsz, s, d = x.shape
    assert d == D_MODEL and s % KT == 0 and s % SSD_L == 0 and (bsz * s) % TM_OUT == 0
    t = bsz * s
    k_top = min(TOPK_MAX, s // 4)
    offs = [0] + [int(v) for v in np.cumsum(np.array(IN_SPLITS))]
    (o_q, o_ckv, o_iq, o_ik, o_iw, o_bg, _o_cg, _o_hb, o_z, o_xbc, o_dt, o_end) = offs

    bucket_np, near = _bucket_tiles()
    bias_tiles = _bias_tiles_call(rel_bias.astype(F32), jnp.asarray(bucket_np))
    tri = jnp.asarray(np.tril(np.ones((KT, KT), np.float32)), BF16)
    row1 = lambda v: v.reshape(1, -1).astype(F32)

    x2 = x.reshape(t, d)
    for i in range(DEPTH):
        w = w_in[i]
        w_small = jnp.concatenate(
            [w[:, o_ik:o_bg], w[:, o_dt:o_end],
             jnp.zeros((d, SMALL_W - (o_bg - o_ik) - (o_end - o_dt)), w.dtype)], axis=1)
        q, ckv, vt4, iq, ik, sm, bmix, z, xbc = _inproj_call(
            x2, row1(pre_mix_norm[i]), row1(kv_norm[i]), row1(idx_k_norm_g[i]), row1(idx_k_norm_b[i]),
            w[:, o_q:o_ckv].astype(BF16), w[:, o_ckv:o_iq].astype(BF16), w[:, o_iq:o_ik].astype(BF16),
            w_small.astype(BF16), w[:, o_bg:o_z].astype(BF16), w[:, o_z:o_xbc].astype(BF16),
            w[:, o_xbc:o_dt].astype(BF16),
            w_uv[i].reshape(A_KV_RANK, A_HEADS * A_V_DIM).T.astype(BF16), s // KT)

        a_out = _attn_call(
            q, iq, sm, ckv.reshape(bsz, s, A_KV_RANK), vt4, ik.reshape(bsz, s, IDX_DIM),
            jnp.transpose(w_uk[i], (1, 0, 2)).astype(BF16),
            bias_tiles, tri, k_top=k_top, near=near)

        dt_t = sm[:, DT_OFF:DT_OFF + SSM_HEADS].reshape(bsz, s, SSM_HEADS).transpose(0, 2, 1)
        b_out, c_out = _mixer_call(
            bmix, z, xbc, sm, dt_t, short_conv_w[i].astype(F32), ssm_conv_w[i].astype(F32),
            ssm_conv_b[i].astype(F32), ssm_dt_bias[i].astype(F32), ssm_a_log[i].astype(F32),
            jnp.repeat(ssm_d[i].astype(F32), SSM_HEAD_DIM), ssm_norm[i].astype(F32), bsz, s)

        wo = w_out[i].astype(BF16)
        na, nbw = A_HEADS * A_V_DIM, A_HEADS * A_V_DIM + B_WIDTH
        x2 = _dense_call(
            x2, a_out, b_out, c_out, p[i].reshape(t, PLE_DIM),
            row1(post_mix_norm[i]), row1(pre_ffn_norm[i]), row1(post_ffn_norm[i]),
            wo[:na], wo[na:nbw], wo[nbw:], w_ffn_gate[i].astype(BF16), w_ffn_up[i].astype(BF16),
            w_ffn_down[i].astype(BF16), w_ple_gate[i].astype(BF16), w_ple_proj[i].astype(BF16))
    return x2.reshape(bsz, s, d)
```

```python
import functools
import math

import numpy as np
import jax
import jax.numpy as jnp
from jax import lax
from jax.experimental import pallas as pl
from jax.experimental.pallas import tpu as pltpu

F32, BF16, I32 = jnp.float32, jnp.bfloat16, jnp.int32

D_MODEL = 1024
DEPTH = 2
CHUNK = 64
A_HEADS = 8
A_QK_DIM = 64
A_V_DIM = 64
A_KV_RANK = 256
IDX_HEADS = 4
IDX_DIM = 64
TOPK_MAX = 256
REL_BUCKETS = 32
REL_MAX_DIST = 1024
B_WIDTH = 512
SHORT_CONV = 3
SSM_HEADS = 16
SSM_HEAD_DIM = 64
SSM_INNER = SSM_HEADS * SSM_HEAD_DIM
SSM_GROUPS = 2
SSM_STATE = 128
SSM_CONV = 4
SSM_XBC = SSM_INNER + 2 * SSM_GROUPS * SSM_STATE
MIX_WIDTH = A_HEADS * A_V_DIM + B_WIDTH + SSM_INNER
D_FF = -(-8 * D_MODEL // (3 * 256)) * 256
PLE_DIM = 256
NORM_EPS = 1e-6
IN_SPLITS = (A_HEADS * A_QK_DIM, A_KV_RANK, IDX_HEADS * IDX_DIM, IDX_DIM, IDX_HEADS,
             B_WIDTH, B_WIDTH, B_WIDTH, SSM_INNER, SSM_XBC, SSM_HEADS)

LANES = 128
SUBLANES = 8
VMEM_LIMIT_BYTES = 56 * 1024 * 1024

QB = LANES
KT = 256
SSD_L = 128
TM_OUT = 256
SMALL_W = LANES
IW_OFF = IDX_DIM
DT_OFF = IDX_DIM + IDX_HEADS
INT_MIN = np.int32(-2 ** 31)

def _const_spec(shape):
    nd = len(shape)
    return pl.BlockSpec(shape, lambda *_: (0,) * nd, pipeline_mode=pl.Buffered(1))


def _params(sem):
    return pltpu.CompilerParams(dimension_semantics=sem, vmem_limit_bytes=VMEM_LIMIT_BYTES)


def _rms(x, g):
    return x * lax.rsqrt(jnp.mean(x * x, axis=-1, keepdims=True) + NORM_EPS) * g


def _sigmoid(x):
    return 1.0 / (1.0 + jnp.exp(-x))


def _split3(v):
    hi = v.astype(BF16)
    r1 = v - hi.astype(F32)
    mid = r1.astype(BF16)
    lo = (r1 - mid.astype(F32)).astype(BF16)
    return hi, mid, lo


def _dot(a, b):
    return jnp.dot(a, b, preferred_element_type=F32)


def _t5_bucket_np(rel):
    half = REL_BUCKETS // 2
    max_exact = half // 2
    ret = np.where(rel > 0, half, 0)
    n = np.abs(rel)
    nf = np.maximum(n, 1).astype(np.float32)
    large = max_exact + (np.log(nf / np.float32(max_exact)) / np.float32(math.log(REL_MAX_DIST / max_exact))
                         * np.float32(half - max_exact)).astype(np.int32)
    large = np.minimum(large, half - 1)
    return (ret + np.where(n < max_exact, n, large)).astype(np.int32)


@functools.lru_cache(maxsize=None)
def _bucket_tiles():
    kr = np.arange(KT)[:, None]
    qc = np.arange(QB)[None, :]
    tiles, near = [], None
    for o in range(64):
        t = _t5_bucket_np(kr - qc - o * QB)
        tiles.append(t)
    for o in range(63, -1, -1):
        if not (np.all(tiles[o] == tiles[63][0, 0])):
            near = o + 1
            break
    assert near is not None and near < 63
    return np.stack(tiles[:near + 1]), near


def _bias_kernel(rb_ref, bk_ref, out_ref):
    bk = bk_ref[0]
    for h in range(A_HEADS):
        acc = jnp.zeros(bk.shape, F32)
        for b in range(REL_BUCKETS):
            acc = jnp.where(bk == b, rb_ref[b, h], acc)
        out_ref[0, :, h * QB:(h + 1) * QB] = acc


def _bias_tiles_call(rel_bias, bucket_tiles):
    n = bucket_tiles.shape[0]
    return pl.pallas_call(
        _bias_kernel,
        grid=(n,),
        in_specs=[pl.BlockSpec(memory_space=pltpu.SMEM),
                  pl.BlockSpec((1, KT, QB), lambda o: (o, 0, 0))],
        out_specs=pl.BlockSpec((1, KT, A_HEADS * QB), lambda o: (o, 0, 0)),
        out_shape=jax.ShapeDtypeStruct((n, KT, A_HEADS * QB), F32),
        compiler_params=_params(("arbitrary",)),
        name="rel_bias_tiles",
    )(rel_bias, bucket_tiles)


def _inproj_kernel(x_ref, g_ref, kvn_ref, ikg_ref, ikb_ref,
                   wq_ref, wckv_ref, wiq_ref, wsm_ref, wb_ref, wz_ref, wxbc_ref, wuv_ref,
                   q_out, ckv_out, vt_out, iq_out, ik_out, sm_out, b_out, z_out, xbc_out):
    h = _rms(x_ref[...], g_ref[...]).astype(BF16)
    q_out[...] = _dot(h, wq_ref[...]).astype(BF16)
    ckv = _rms(_dot(h, wckv_ref[...]), kvn_ref[...]).astype(BF16)
    ckv_out[...] = ckv
    vt_out[0, 0] = lax.dot_general(wuv_ref[...], ckv, (((1,), (1,)), ((), ())),
                                   preferred_element_type=F32).astype(BF16)
    iq_out[...] = (_dot(h, wiq_ref[...]) * (IDX_DIM ** -0.5)).astype(BF16)
    sm = _dot(h, wsm_ref[...])
    sm_out[...] = sm
    ik = sm[:, :IDX_DIM]
    mu = jnp.mean(ik, axis=-1, keepdims=True)
    var = jnp.mean(jnp.square(ik - mu), axis=-1, keepdims=True)
    ik_out[...] = ((ik - mu) * lax.rsqrt(var + NORM_EPS) * ikg_ref[...] + ikb_ref[...]).astype(BF16)
    b_out[...] = _dot(h, wb_ref[...])
    z_out[...] = _dot(h, wz_ref[...])
    xbc_out[...] = _dot(h, wxbc_ref[...])


def _inproj_call(x2, g, kvn, ikg, ikb, wq, wckv, wiq, wsm, wb, wz, wxbc, wuv, n_kt):
    t = x2.shape[0]
    tm = KT
    row = lambda w: pl.BlockSpec((tm, w), lambda r: (r, 0))
    rows = lambda w, dt: (row(w), jax.ShapeDtypeStruct((t, w), dt))
    hv = wuv.shape[0]
    outs = [rows(wq.shape[1], BF16), rows(wckv.shape[1], BF16),
            (pl.BlockSpec((1, 1, hv, tm), lambda r: (r // n_kt, r % n_kt, 0, 0)),
             jax.ShapeDtypeStruct((t // (n_kt * tm), n_kt, hv, tm), BF16)),
            rows(wiq.shape[1], BF16), rows(IDX_DIM, BF16), rows(SMALL_W, F32),
            rows(wb.shape[1], F32), rows(wz.shape[1], F32), rows(wxbc.shape[1], F32)]
    consts = [g, kvn, ikg, ikb, wq, wckv, wiq, wsm, wb, wz, wxbc, wuv]
    return pl.pallas_call(
        _inproj_kernel,
        grid=(t // tm,),
        in_specs=[row(D_MODEL)] + [_const_spec(c.shape) for c in consts],
        out_specs=[spec for spec, _ in outs],
        out_shape=[shape for _, shape in outs],
        compiler_params=_params(("parallel",)),
        name="inproj",
    )(x2, *consts)


def _bit_transpose32(words):
    a = list(words)
    m, j = 0x0000FFFF, 16
    while j:
        k = 0
        while k < 32:
            t = (a[k] ^ lax.shift_right_logical(a[k + j], np.int32(j))) & np.int32(m)
            a[k] = a[k] ^ t
            a[k + j] = a[k + j] ^ (t << j)
            k = (k + j + 1) & ~j
        j >>= 1
        m = (m ^ (m << j)) & 0xFFFFFFFF
    return a


def _attn_kernel(q_ref, iq_ref, sm_ref, ckv_ref, vt_ref, ik_ref, wuk_ref, bias_ref, tri_ref,
                 o_ref, keys_ref, planes_ref, qlat_ref, acc_ref, *, k_top, near):
    i = pl.program_id(1)
    nt = (i + 2) // 2
    lane = lax.broadcasted_iota(I32, (1, QB), 1)
    qchunk = (i * QB + lane) >> int(math.log2(CHUNK))
    krow = lax.broadcasted_iota(I32, (KT, QB), 0)

    q_t = q_ref[...].astype(F32).T.astype(BF16)
    for h in range(A_HEADS):
        ql = _dot(wuk_ref[h], q_t[h * A_QK_DIM:(h + 1) * A_QK_DIM, :])
        qlat_ref[:, h * QB:(h + 1) * QB] = (ql * (A_QK_DIM ** -0.5)).astype(BF16)

    iq_t = iq_ref[...].astype(F32).T.astype(BF16)
    iw = sm_ref[...].T[IW_OFF:IW_OFF + IDX_HEADS, :] * (IDX_HEADS ** -0.5)

    def score_tile(j, carry):
        r0 = pl.multiple_of(j * KT, KT)
        ikt = ik_ref[0, pl.ds(r0, KT), :]
        s = jnp.zeros((KT, QB), F32)
        for h in range(IDX_HEADS):
            raw = _dot(ikt, iq_t[h * IDX_DIM:(h + 1) * IDX_DIM, :])
            s = s + jnp.maximum(raw, 0.0) * iw[h:h + 1, :]
        adm = ((r0 + krow) >> int(math.log2(CHUNK))) <= qchunk
        s = jnp.where(adm, s, -jnp.inf)
        bits = lax.bitcast_convert_type(s, I32)
        bits = jnp.where(bits == INT_MIN, 0, bits)
        key = bits ^ ((bits >> 31) & np.int32(0x7FFFFFFF))
        keys_ref[pl.ds(r0, KT), :] = key
        ukey = key ^ INT_MIN
        planes = _bit_transpose32([ukey[v * SUBLANES:(v + 1) * SUBLANES, :] for v in range(32)])
        p0 = pl.multiple_of(j * SUBLANES, SUBLANES)
        for b in range(32):
            planes_ref[b, pl.ds(p0, SUBLANES), :] = planes[31 - b]
        return carry

    @pl.when(i == 0)
    def _():
        planes_ref[...] = jnp.zeros(planes_ref.shape, I32)

    lax.fori_loop(0, nt, score_tile, 0)

    n_prow = planes_ref.shape[1]
    prow = lax.broadcasted_iota(I32, (n_prow, QB), 0)
    live = jnp.where(prow < nt * SUBLANES, np.int32(-1), np.int32(0))
    kf = float(k_top)
    c_gt = jnp.zeros((1, QB), F32)
    thr_u = jnp.zeros((1, QB), I32)
    for b in range(31, -1, -1):
        x = live & planes_ref[b]
        pc = lax.population_count(x)
        parts = [pc[r * SUBLANES:(r + 1) * SUBLANES, :] for r in range(n_prow // SUBLANES)]
        while len(parts) > 1:
            parts = [parts[k] + parts[k + 1] for k in range(0, len(parts) - 1, 2)] + (
                [parts[-1]] if len(parts) % 2 else [])
        cnt = c_gt + jnp.sum(parts[0].astype(F32), axis=0, keepdims=True)
        take = cnt >= kf
        live = jnp.where(take, x, live ^ x)
        c_gt = jnp.where(take, c_gt, cnt)
        thr_u = thr_u | jnp.where(take, np.int32(-2 ** 31 if b == 31 else 2 ** b), np.int32(0))
    thr = thr_u ^ INT_MIN
    need = kf - c_gt

    acc_ref[...] = jnp.zeros(acc_ref.shape, F32)
    bias_far = bias_ref[near, 0:1, :]

    def attn_tile(j, carry, is_near):
        ms, ls, cnt_eq = carry
        r0 = pl.multiple_of(j * KT, KT)
        kt = keys_ref[pl.ds(r0, KT), :]
        eq = kt == thr
        eqf = jnp.where(eq, 1.0, 0.0)
        rank = _dot(tri_ref[...], eqf.astype(BF16)) + cnt_eq
        sel = (kt > thr) | (eq & (rank <= need))
        if is_near:
            sel = sel & (((r0 + krow) >> int(math.log2(CHUNK))) <= qchunk)
            bias = bias_ref[i - 2 * j]
        else:
            bias = bias_far
        cnt_eq = cnt_eq + jnp.sum(eqf, axis=0, keepdims=True)
        logits = _dot(ckv_ref[0, pl.ds(r0, KT), :], qlat_ref[...])
        new_ms, new_ls = [], []
        for h in range(A_HEADS):
            sl = slice(h * QB, (h + 1) * QB)
            vs = slice(h * A_V_DIM, (h + 1) * A_V_DIM)
            lh = jnp.where(sel, logits[:, sl] + bias[:, sl], -jnp.inf)
            m_new = jnp.maximum(ms[h], jnp.max(lh, axis=0, keepdims=True))
            m_safe = jnp.where(m_new == -jnp.inf, 0.0, m_new)
            p = jnp.exp(lh - m_safe)
            alpha = jnp.exp(ms[h] - m_safe)
            new_ms.append(m_new)
            new_ls.append(alpha * ls[h] + jnp.sum(p, axis=0, keepdims=True))
            pv = _dot(vt_ref[0, j, vs, :], p.astype(BF16))
            acc_ref[vs, :] = acc_ref[vs, :] * alpha + pv
        return tuple(new_ms), tuple(new_ls), cnt_eq

    n_far = jnp.maximum(i - near + 2, 0) // 2
    init = (tuple(jnp.full((1, QB), -jnp.inf, F32) for _ in range(A_HEADS)),
            tuple(jnp.zeros((1, QB), F32) for _ in range(A_HEADS)),
            jnp.zeros((1, QB), F32))
    carry = lax.fori_loop(0, n_far, functools.partial(attn_tile, is_near=False), init)
    _, ls, _ = lax.fori_loop(n_far, nt, functools.partial(attn_tile, is_near=True), carry)

    a_t = jnp.concatenate(
        [acc_ref[h * A_V_DIM:(h + 1) * A_V_DIM, :] * (1.0 / ls[h]) for h in range(A_HEADS)], axis=0)
    o_ref[...] = a_t.T.astype(BF16)


def _attn_call(q, iq, sm, ckv3, vt4, ik3, wuk, bias_tiles, tri, *, k_top, near):
    b, s = ckv3.shape[0], ckv3.shape[1]
    nb = s // QB
    n_kt = s // KT
    hv = A_HEADS * A_V_DIM
    kern = functools.partial(_attn_kernel, k_top=k_top, near=near)
    row = lambda w: pl.BlockSpec((QB, w), lambda bb, i: (bb * nb + i, 0))
    return pl.pallas_call(
        kern,
        grid=(b, nb),
        in_specs=[
            row(q.shape[1]), row(iq.shape[1]), row(SMALL_W),
            pl.BlockSpec((1, s, A_KV_RANK), lambda bb, i: (bb, 0, 0)),
            pl.BlockSpec((1, n_kt, hv, KT), lambda bb, i: (bb, 0, 0, 0)),
            pl.BlockSpec((1, s, IDX_DIM), lambda bb, i: (bb, 0, 0)),
            _const_spec(wuk.shape), _const_spec(bias_tiles.shape), _const_spec(tri.shape),
        ],
        out_specs=row(hv),
        out_shape=jax.ShapeDtypeStruct((b * s, hv), BF16),
        scratch_shapes=[pltpu.VMEM((s, QB), I32), pltpu.VMEM((32, s // 32, QB), I32),
                        pltpu.VMEM((A_KV_RANK, A_HEADS * QB), BF16),
                        pltpu.VMEM((hv, QB), F32)],
        compiler_params=_params(("parallel", "arbitrary")),
        name="dsa_attention",
    )(q, iq, sm, ckv3, vt4, ik3, wuk, bias_tiles, tri)


def _softplus(x):
    return jnp.maximum(x, 0.0) + jnp.log1p(jnp.exp(-jnp.abs(x)))


def _mixer_kernel(bm_ref, z_ref, xbc_ref, sm_ref, dtt_ref, scw_ref, cw_ref, cb_ref,
                  dtb_r_ref, dtb_c_ref, alog_r_ref, alog_c_ref, dexp_ref, nw_ref,
                  tril_ref, triu_ref, eh_ref,
                  bo_ref, co_ref, ubuf, xbuf, state):
    L = SSD_L
    P = SUBLANES
    gw = SSM_INNER // SSM_GROUPS

    @pl.when(pl.program_id(1) == 0)
    def _():
        ubuf[0:P, :] = jnp.zeros((P, B_WIDTH), F32)
        xbuf[0:P, :] = jnp.zeros((P, SSM_XBC), F32)
        state[...] = jnp.zeros(state.shape, F32)

    bm = bm_ref[...]
    u = bm[:, B_WIDTH:2 * B_WIDTH] * bm[:, 2 * B_WIDTH:3 * B_WIDTH]
    ubuf[P:P + L, :] = u
    conv = scw_ref[SHORT_CONV - 1:SHORT_CONV, :] * u
    for j in range(SHORT_CONV - 1):
        off = P - (SHORT_CONV - 1) + j
        conv = conv + scw_ref[j:j + 1, :] * ubuf[off:off + L, :]
    bo_ref[...] = (bm[:, :B_WIDTH] * conv).astype(BF16)
    ubuf[0:P, :] = u[L - P:L, :]

    xr = xbc_ref[...]
    xbuf[P:P + L, :] = xr
    xc = cw_ref[SSM_CONV - 1:SSM_CONV, :] * xr + cb_ref[...]
    for j in range(SSM_CONV - 1):
        off = P - (SSM_CONV - 1) + j
        xc = xc + cw_ref[j:j + 1, :] * xbuf[off:off + L, :]
    xbuf[0:P, :] = xr[L - P:L, :]
    xa = xc * _sigmoid(xc)
    xs = xa[:, :SSM_INNER]

    dt = _softplus(sm_ref[:, DT_OFF:DT_OFF + SSM_HEADS] + dtb_r_ref[...])
    a = dt * (-jnp.exp(alog_r_ref[...]))
    a_cs = sum(_dot(tril_ref[...], part) for part in _split3(a))
    dt_t = _softplus(dtt_ref[0] + dtb_c_ref[...])
    a_t = dt_t * (-jnp.exp(alog_c_ref[...]))
    a_cs_t = sum(_dot(part, triu_ref[...]) for part in _split3(a_t))

    def expand(v):
        return sum(_dot(part, eh_ref[...]) for part in _split3(v))

    xdt = xs * expand(dt)
    e_cs = expand(jnp.exp(a_cs))
    dec = expand(jnp.exp(a_cs[L - 1:L, :] - a_cs))
    chunk_dec = e_cs[L - 1:L, :]

    row = lax.broadcasted_iota(I32, (L, L), 0)
    col = lax.broadcasted_iota(I32, (L, L), 1)
    causal = row >= col
    lo_half = lax.broadcasted_iota(I32, (L, LANES), 1) < SSM_HEAD_DIM
    hpg = SSM_HEADS // SSM_GROUPS

    y_parts = []
    for g in range(SSM_GROUPS):
        bg = xa[:, SSM_INNER + g * SSM_STATE:SSM_INNER + (g + 1) * SSM_STATE].astype(BF16)
        cg = xa[:, SSM_INNER + (SSM_GROUPS + g) * SSM_STATE:
                SSM_INNER + (SSM_GROUPS + g + 1) * SSM_STATE].astype(BF16)
        cb = lax.dot_general(cg, bg, (((1,), (1,)), ((), ())), preferred_element_type=F32)
        sl = slice(g * gw, (g + 1) * gw)
        st = state[g]
        y_off = _dot(cg, st.astype(BF16)) * e_cs[:, sl]
        for k in range(hpg // 2):
            xp = xdt[:, g * gw + k * LANES:g * gw + (k + 1) * LANES]
            halves = (jnp.where(lo_half, xp, 0.0).astype(BF16), jnp.where(lo_half, 0.0, xp).astype(BF16))
            y = y_off[:, k * LANES:(k + 1) * LANES]
            for hh in range(2):
                h = g * hpg + 2 * k + hh
                seg = a_cs[:, h:h + 1] - a_cs_t[h:h + 1, :]
                m = (cb * jnp.exp(jnp.where(causal, seg, -jnp.inf))).astype(BF16)
                y = y + _dot(m, halves[hh])
            y_parts.append(y)
        xw = (dec[:, sl] * xdt[:, sl]).astype(BF16)
        upd = lax.dot_general(bg, xw, (((0,), (0,)), ((), ())), preferred_element_type=F32)
        state[g] = chunk_dec[:, sl] * st + upd

    y = jnp.concatenate(y_parts, axis=1) + dexp_ref[...] * xs
    zz = z_ref[...]
    y = y * (zz * _sigmoid(zz))
    outs = []
    for g in range(SSM_GROUPS):
        sl = slice(g * gw, (g + 1) * gw)
        outs.append(_rms(y[:, sl], nw_ref[:, sl]))
    co_ref[...] = jnp.concatenate(outs, axis=1).astype(BF16)


def _mixer_call(bmix, z, xbc, sm, dt_t, scw, cw, cb, dtb, alog, dexp, nw, b, s):
    L = SSD_L
    nc = s // L
    tril = jnp.asarray(np.tril(np.ones((L, L), np.float32)), BF16)
    triu = jnp.asarray(np.triu(np.ones((L, L), np.float32)), BF16)
    eh = jnp.asarray(np.repeat(np.eye(SSM_HEADS, dtype=np.float32), SSM_HEAD_DIM, axis=1), BF16)
    row = lambda w: pl.BlockSpec((L, w), lambda bb, c: (bb * nc + c, 0))
    consts = [scw, cw, cb.reshape(1, -1), dtb.reshape(1, -1), dtb.reshape(-1, 1),
              alog.reshape(1, -1), alog.reshape(-1, 1), dexp.reshape(1, -1), nw.reshape(1, -1),
              tril, triu, eh]
    return pl.pallas_call(
        _mixer_kernel,
        grid=(b, nc),
        in_specs=[row(3 * B_WIDTH), row(SSM_INNER), row(SSM_XBC), row(SMALL_W),
                  pl.BlockSpec((1, SSM_HEADS, L), lambda bb, c: (bb, 0, c))]
                 + [_const_spec(c.shape) for c in consts],
        out_specs=[row(B_WIDTH), row(SSM_INNER)],
        out_shape=[jax.ShapeDtypeStruct((b * s, B_WIDTH), BF16),
                   jax.ShapeDtypeStruct((b * s, SSM_INNER), BF16)],
        scratch_shapes=[pltpu.VMEM((L + SUBLANES, B_WIDTH), F32),
                        pltpu.VMEM((L + SUBLANES, SSM_XBC), F32),
                        pltpu.VMEM((SSM_GROUPS, SSM_STATE, SSM_INNER // SSM_GROUPS), F32)],
        compiler_params=_params(("parallel", "arbitrary")),
        name="conv_ssd_mixers",
    )(bmix, z, xbc, sm, dt_t, *consts)


def _dense_kernel(x_ref, a_ref, b_ref, c_ref, p_ref, g_post_ref, g_pre_ref, g_fpost_ref,
                  woa_ref, wob_ref, woc_ref, wg_ref, wu_ref, wd_ref, wpg_ref, wpp_ref, o_ref):
    mix = _dot(a_ref[...], woa_ref[...]) + _dot(b_ref[...], wob_ref[...]) + _dot(c_ref[...], woc_ref[...])
    x = x_ref[...] + _rms(mix, g_post_ref[...])
    h = _rms(x, g_pre_ref[...]).astype(BF16)
    gate = _dot(h, wg_ref[...])
    up = _dot(h, wu_ref[...])
    act = (gate * _sigmoid(gate) * up).astype(BF16)
    x = x + _rms(_dot(act, wd_ref[...]), g_fpost_ref[...])
    pg = _sigmoid(_dot(x.astype(BF16), wpg_ref[...]))
    o_ref[...] = x + pg * _dot(p_ref[...].astype(BF16), wpp_ref[...])


def _dense_call(x2, a, b, c, p2, g_post, g_pre, g_fpost, woa, wob, woc, wg, wu, wd, wpg, wpp):
    t = x2.shape[0]
    tm = TM_OUT
    row = lambda w: pl.BlockSpec((tm, w), lambda r: (r, 0))
    consts = [g_post, g_pre, g_fpost, woa, wob, woc, wg, wu, wd, wpg, wpp]
    return pl.pallas_call(
        _dense_kernel,
        grid=(t // tm,),
        in_specs=[row(D_MODEL), row(a.shape[1]), row(b.shape[1]), row(c.shape[1]), row(PLE_DIM)]
                 + [_const_spec(w.shape) for w in consts],
        out_specs=row(D_MODEL),
        out_shape=jax.ShapeDtypeStruct((t, D_MODEL), F32),
        compiler_params=_params(("parallel",)),
        name="outproj_ffn_ple",
    )(x2, a, b, c, p2, *consts)


def kernel(x, p, pre_mix_norm, post_mix_norm, pre_ffn_norm, post_ffn_norm, w_in, kv_norm, idx_k_norm_g, idx_k_norm_b, w_uk, w_uv, rel_bias, short_conv_w, ssm_conv_w, ssm_conv_b, ssm_dt_bias, ssm_a_log, ssm_d, ssm_norm, w_out, w_ffn_gate, w_ffn_up, w_ffn_down, w_ple_proj, w_ple_gate):
    bsz, s, d = x.shape
    assert d == D_MODEL and s % KT == 0 and s % SSD_L == 0 and (bsz * s) % TM_OUT == 0
    t = bsz * s
    k_top = min(TOPK_MAX, s // 4)
    offs = [0] + [int(v) for v in np.cumsum(np.array(IN_SPLITS))]
    (o_q, o_ckv, o_iq, o_ik, o_iw, o_bg, _o_cg, _o_hb, o_z, o_xbc, o_dt, o_end) = offs

    bucket_np, near = _bucket_tiles()
    bias_tiles = _bias_tiles_call(rel_bias.astype(F32), jnp.asarray(bucket_np))
    tri = jnp.asarray(np.tril(np.ones((KT, KT), np.float32)), BF16)
    row1 = lambda v: v.reshape(1, -1).astype(F32)

    x2 = x.reshape(t, d)
    for i in range(DEPTH):
        w = w_in[i]
        w_small = jnp.concatenate(
            [w[:, o_ik:o_bg], w[:, o_dt:o_end],
             jnp.zeros((d, SMALL_W - (o_bg - o_ik) - (o_end - o_dt)), w.dtype)], axis=1)
        q, ckv, vt4, iq, ik, sm, bmix, z, xbc = _inproj_call(
            x2, row1(pre_mix_norm[i]), row1(kv_norm[i]), row1(idx_k_norm_g[i]), row1(idx_k_norm_b[i]),
            w[:, o_q:o_ckv].astype(BF16), w[:, o_ckv:o_iq].astype(BF16), w[:, o_iq:o_ik].astype(BF16),
            w_small.astype(BF16), w[:, o_bg:o_z].astype(BF16), w[:, o_z:o_xbc].astype(BF16),
            w[:, o_xbc:o_dt].astype(BF16),
            w_uv[i].reshape(A_KV_RANK, A_HEADS * A_V_DIM).T.astype(BF16), s // KT)

        a_out = _attn_call(
            q, iq, sm, ckv.reshape(bsz, s, A_KV_RANK), vt4, ik.reshape(bsz, s, IDX_DIM),
            jnp.transpose(w_uk[i], (1, 0, 2)).astype(BF16),
            bias_tiles, tri, k_top=k_top, near=near)

        dt_t = sm[:, DT_OFF:DT_OFF + SSM_HEADS].reshape(bsz, s, SSM_HEADS).transpose(0, 2, 1)
        b_out, c_out = _mixer_call(
            bmix, z, xbc, sm, dt_t, short_conv_w[i].astype(F32), ssm_conv_w[i].astype(F32),
            ssm_conv_b[i].astype(F32), ssm_dt_bias[i].astype(F32), ssm_a_log[i].astype(F32),
            jnp.repeat(ssm_d[i].astype(F32), SSM_HEAD_DIM), ssm_norm[i].astype(F32), bsz, s)

        wo = w_out[i].astype(BF16)
        na, nbw = A_HEADS * A_V_DIM, A_HEADS * A_V_DIM + B_WIDTH
        x2 = _dense_call(
            x2, a_out, b_out, c_out, p[i].reshape(t, PLE_DIM),
            row1(post_mix_norm[i]), row1(pre_ffn_norm[i]), row1(post_ffn_norm[i]),
            wo[:na], wo[na:nbw], wo[nbw:], w_ffn_gate[i].astype(BF16), w_ffn_up[i].astype(BF16),
            w_ffn_down[i].astype(BF16), w_ple_gate[i].astype(BF16), w_ple_proj[i].astype(BF16))
    return x2.reshape(bsz, s, d)
```

```python
import functools
import math

import numpy as np
import jax
import jax.numpy as jnp
from jax import lax
from jax.experimental import pallas as pl
from jax.experimental.pallas import tpu as pltpu

F32, BF16, I32 = jnp.float32, jnp.bfloat16, jnp.int32

D_MODEL = 1024
DEPTH = 2
CHUNK = 64
A_HEADS = 8
A_QK_DIM = 64
A_V_DIM = 64
A_KV_RANK = 256
IDX_HEADS = 4
IDX_DIM = 64
TOPK_MAX = 256
REL_BUCKETS = 32
REL_MAX_DIST = 1024
B_WIDTH = 512
SHORT_CONV = 3
SSM_HEADS = 16
SSM_HEAD_DIM = 64
SSM_INNER = SSM_HEADS * SSM_HEAD_DIM
SSM_GROUPS = 2
SSM_STATE = 128
SSM_CONV = 4
SSM_XBC = SSM_INNER + 2 * SSM_GROUPS * SSM_STATE
MIX_WIDTH = A_HEADS * A_V_DIM + B_WIDTH + SSM_INNER
D_FF = -(-8 * D_MODEL // (3 * 256)) * 256
PLE_DIM = 256
NORM_EPS = 1e-6
IN_SPLITS = (A_HEADS * A_QK_DIM, A_KV_RANK, IDX_HEADS * IDX_DIM, IDX_DIM, IDX_HEADS,
             B_WIDTH, B_WIDTH, B_WIDTH, SSM_INNER, SSM_XBC, SSM_HEADS)
IN_OFFSETS = tuple(int(v) for v in np.concatenate([[0], np.cumsum(IN_SPLITS)]))

LANES = 128
SUBLANES = 8
VMEM_LIMIT_BYTES = 56 * 1024 * 1024

QB = LANES
KT = 256
SSD_L = 128
TM_OUT = 256
SMALL_W = LANES
IW_OFF = IDX_DIM
INT_MIN = np.int32(-2 ** 31)
LOG2E = math.log2(math.e)
BF16_ROWS = 2 * SUBLANES
V_ROWS = A_V_DIM + BF16_ROWS

def _const_spec(shape):
    nd = len(shape)
    return pl.BlockSpec(shape, lambda *_: (0,) * nd, pipeline_mode=pl.Buffered(1))


def _params(sem):
    return pltpu.CompilerParams(dimension_semantics=sem, vmem_limit_bytes=VMEM_LIMIT_BYTES)


def _rms(x, g):
    return x * lax.rsqrt(jnp.mean(x * x, axis=-1, keepdims=True) + NORM_EPS) * g


def _sigmoid(x):
    return 1.0 / (1.0 + jnp.exp(-x))


def _split3(v):
    hi = v.astype(BF16)
    r1 = v - hi.astype(F32)
    mid = r1.astype(BF16)
    lo = (r1 - mid.astype(F32)).astype(BF16)
    return hi, mid, lo


def _dot(a, b):
    return jnp.dot(a, b, preferred_element_type=F32)


def _t5_bucket_np(rel):
    half = REL_BUCKETS // 2
    max_exact = half // 2
    ret = np.where(rel > 0, half, 0)
    n = np.abs(rel)
    nf = np.maximum(n, 1).astype(np.float32)
    large = max_exact + (np.log(nf / np.float32(max_exact)) / np.float32(math.log(REL_MAX_DIST / max_exact))
                         * np.float32(half - max_exact)).astype(np.int32)
    large = np.minimum(large, half - 1)
    return (ret + np.where(n < max_exact, n, large)).astype(np.int32)


@functools.lru_cache(maxsize=None)
def _bucket_tiles():
    kr = np.arange(KT)[:, None]
    qc = np.arange(QB)[None, :]
    tiles, near = [], None
    for o in range(64):
        t = _t5_bucket_np(kr - qc - o * QB)
        tiles.append(t)
    for o in range(63, -1, -1):
        if not (np.all(tiles[o] == tiles[63][0, 0])):
            near = o + 1
            break
    assert near is not None and 0 < near < 63
    return np.stack(tiles[:near + 1]), near, int(tiles[63][0, 0])


def _bias_kernel(rb_ref, bk_ref, out_ref, *, far_bucket):
    bk = bk_ref[0]
    for h in range(A_HEADS):
        acc = jnp.zeros(bk.shape, F32)
        for b in range(REL_BUCKETS):
            acc = jnp.where(bk == b, (rb_ref[b, h] - rb_ref[far_bucket, h]) * LOG2E, acc)
        out_ref[0, :, h * QB:(h + 1) * QB] = acc


def _bias_tiles_call(rel_bias, bucket_tiles, far_bucket):
    n = bucket_tiles.shape[0]
    return pl.pallas_call(
        functools.partial(_bias_kernel, far_bucket=far_bucket),
        grid=(n,),
        in_specs=[pl.BlockSpec(memory_space=pltpu.SMEM),
                  pl.BlockSpec((1, KT, QB), lambda o: (o, 0, 0))],
        out_specs=pl.BlockSpec((1, KT, A_HEADS * QB), lambda o: (o, 0, 0)),
        out_shape=jax.ShapeDtypeStruct((n, KT, A_HEADS * QB), F32),
        compiler_params=_params(("arbitrary",)),
        name="rel_bias_tiles",
    )(rel_bias, bucket_tiles)


def _wprep_kernel(w_ref, wq_out, wckv_out, wiq_out, wsm_out, wb_out, wz_out, wxbc_out, wdt_out):
    o_q, o_ckv, o_iq, o_ik, _, o_bg, _, _, o_z, o_xbc, o_dt, o_end = IN_OFFSETS
    wq_out[...] = w_ref[:, o_q:o_ckv].astype(BF16)
    wckv_out[...] = w_ref[:, o_ckv:o_iq].astype(BF16)
    wiq_out[...] = w_ref[:, o_iq:o_ik].astype(BF16)
    wsm_out[...] = w_ref[:, o_ik:o_ik + SMALL_W].astype(BF16)
    tail = w_ref[:, o_bg:o_end].astype(BF16)
    wb_out[...] = tail[:, :o_z - o_bg]
    wz_out[...] = tail[:, o_z - o_bg:o_xbc - o_bg]
    wxbc_out[...] = tail[:, o_xbc - o_bg:o_dt - o_bg]
    wdt_out[...] = jnp.zeros(wdt_out.shape, BF16)
    wdt_out[:, :o_end - o_dt] = tail[:, o_dt - o_bg:]


def _wprep_call(w):
    d, n = w.shape
    tr = 256
    widths = [IN_SPLITS[0], IN_SPLITS[1], IN_SPLITS[2], SMALL_W, 3 * B_WIDTH, SSM_INNER, SSM_XBC, SMALL_W]
    return pl.pallas_call(
        _wprep_kernel,
        grid=(d // tr,),
        in_specs=[pl.BlockSpec((tr, n), lambda r: (r, 0))],
        out_specs=[pl.BlockSpec((tr, wd), lambda r: (r, 0)) for wd in widths],
        out_shape=[jax.ShapeDtypeStruct((d, wd), BF16) for wd in widths],
        compiler_params=_params(("parallel",)),
        name="inproj_weight_prep",
    )(w)


def _inproj_kernel(x_ref, g_ref, kvn_ref, ikg_ref, ikb_ref,
                   wq_ref, wckv_ref, wiq_ref, wsm_ref, wb_ref, wz_ref, wxbc_ref, wdt_ref, wuv_ref, vone_ref,
                   q_out, ckv_out, vt_out, iq_out, ik_out, sm_out, b_out, z_out, xbc_out, dt_out):
    h = _rms(x_ref[...], g_ref[...]).astype(BF16)
    q_out[...] = _dot(h, wq_ref[...]).astype(BF16)
    ckv = _rms(_dot(h, wckv_ref[...]), kvn_ref[...]).astype(BF16)
    ckv_out[...] = ckv
    vt_out[0, 0] = (lax.dot_general(wuv_ref[...], ckv, (((1,), (1,)), ((), ())),
                                    preferred_element_type=F32) + vone_ref[...]).astype(BF16)
    iq_out[...] = (_dot(h, wiq_ref[...]) * (IDX_DIM ** -0.5)).astype(BF16)
    sm = _dot(h, wsm_ref[...])
    sm_out[...] = sm
    ik = sm[:, :IDX_DIM]
    mu = jnp.mean(ik, axis=-1, keepdims=True)
    var = jnp.mean(jnp.square(ik - mu), axis=-1, keepdims=True)
    ik_out[...] = ((ik - mu) * lax.rsqrt(var + NORM_EPS) * ikg_ref[...] + ikb_ref[...]).astype(BF16)
    b_out[...] = _dot(h, wb_ref[...])
    z_out[...] = _dot(h, wz_ref[...])
    xbc_out[...] = _dot(h, wxbc_ref[...])
    dt_out[...] = _dot(h, wdt_ref[...])


def _inproj_call(x2, g, kvn, ikg, ikb, wq, wckv, wiq, wsm, wb, wz, wxbc, wdt, wuv, vone, n_kt):
    t = x2.shape[0]
    tm = KT
    row = lambda w: pl.BlockSpec((tm, w), lambda r: (r, 0))
    rows = lambda w, dt: (row(w), jax.ShapeDtypeStruct((t, w), dt))
    hv = wuv.shape[0]
    outs = [rows(wq.shape[1], BF16), rows(wckv.shape[1], BF16),
            (pl.BlockSpec((1, 1, hv, tm), lambda r: (r // n_kt, r % n_kt, 0, 0)),
             jax.ShapeDtypeStruct((t // (n_kt * tm), n_kt, hv, tm), BF16)),
            rows(wiq.shape[1], BF16), rows(IDX_DIM, BF16), rows(SMALL_W, F32),
            rows(wb.shape[1], F32), rows(wz.shape[1], F32), rows(wxbc.shape[1], F32),
            rows(SMALL_W, F32)]
    consts = [g, kvn, ikg, ikb, wq, wckv, wiq, wsm, wb, wz, wxbc, wdt, wuv, vone]
    return pl.pallas_call(
        _inproj_kernel,
        grid=(t // tm,),
        in_specs=[row(D_MODEL)] + [_const_spec(c.shape) for c in consts],
        out_specs=[spec for spec, _ in outs],
        out_shape=[shape for _, shape in outs],
        compiler_params=_params(("parallel",)),
        name="inproj",
    )(x2, *consts)


def _bit_transpose32(words):
    a = list(words)
    m, j = 0x0000FFFF, 16
    while j:
        k = 0
        while k < 32:
            t = (a[k] ^ lax.shift_right_logical(a[k + j], np.int32(j))) & np.int32(m)
            a[k] = a[k] ^ t
            a[k + j] = a[k + j] ^ (t << j)
            k = (k + j + 1) & ~j
        j >>= 1
        m = (m ^ (m << j)) & 0xFFFFFFFF
    return a


def _attn_kernel(q_ref, iq_ref, sm_ref, ckv_ref, vt_ref, ik_ref, wuk_ref, bias_ref, tri_ref,
                 o_ref, keys_ref, planes_ref, qlat_ref, acc_ref, p0_ref, p1_ref, lg0_ref, lg1_ref,
                 *, k_top, near):
    i = pl.program_id(1)
    nt = (i + 2) // 2
    lane = lax.broadcasted_iota(I32, (1, QB), 1)
    qchunk = (i * QB + lane) >> int(math.log2(CHUNK))
    krow = lax.broadcasted_iota(I32, (KT, QB), 0)

    q_t = q_ref[...].astype(F32).T.astype(BF16)
    for h in range(A_HEADS):
        ql = _dot(wuk_ref[h], q_t[h * A_QK_DIM:(h + 1) * A_QK_DIM, :])
        qlat_ref[:, h * QB:(h + 1) * QB] = (ql * (A_QK_DIM ** -0.5 * LOG2E)).astype(BF16)

    iq_t = iq_ref[...].astype(F32).T.astype(BF16)
    iw = sm_ref[...].T[IW_OFF:IW_OFF + IDX_HEADS, :] * (IDX_HEADS ** -0.5)

    def score_tile(j, carry, masked):
        r0 = pl.multiple_of(j * KT, KT)
        ikt = ik_ref[0, pl.ds(r0, KT), :]
        s = jnp.zeros((KT, QB), F32)
        for h in range(IDX_HEADS):
            raw = _dot(ikt, iq_t[h * IDX_DIM:(h + 1) * IDX_DIM, :])
            s = s + jnp.maximum(raw, 0.0) * iw[h:h + 1, :]
        if masked:
            adm = ((r0 + krow) >> int(math.log2(CHUNK))) <= qchunk
            s = jnp.where(adm, s, -jnp.inf)
        bits = lax.bitcast_convert_type(s, I32)
        bits = jnp.where(bits == INT_MIN, 0, bits)
        key = bits ^ ((bits >> 31) & np.int32(0x7FFFFFFF))
        keys_ref[pl.ds(r0, KT), :] = key
        ukey = key ^ INT_MIN
        planes = _bit_transpose32([ukey[v * SUBLANES:(v + 1) * SUBLANES, :] for v in range(32)])
        p0 = pl.multiple_of(j * SUBLANES, SUBLANES)
        for b in range(32):
            planes_ref[b, pl.ds(p0, SUBLANES), :] = planes[31 - b]
        return carry

    @pl.when(i == 0)
    def _():
        planes_ref[...] = jnp.zeros(planes_ref.shape, I32)

    lax.fori_loop(0, nt - 1, functools.partial(score_tile, masked=False), 0)
    score_tile(nt - 1, 0, masked=True)

    n_prow = planes_ref.shape[1]
    prow = lax.broadcasted_iota(I32, (n_prow, QB), 0)
    live = jnp.where(prow < nt * SUBLANES, np.int32(-1), np.int32(0))
    kf = float(k_top)
    c_gt = jnp.zeros((1, QB), F32)
    thr_u = jnp.zeros((1, QB), I32)
    for b in range(31, -1, -1):
        x = live & planes_ref[b]
        pc = lax.population_count(x)
        parts = [pc[r * SUBLANES:(r + 1) * SUBLANES, :] for r in range(n_prow // SUBLANES)]
        while len(parts) > 1:
            parts = [parts[k] + parts[k + 1] for k in range(0, len(parts) - 1, 2)] + (
                [parts[-1]] if len(parts) % 2 else [])
        cnt = c_gt + jnp.sum(parts[0].astype(F32), axis=0, keepdims=True)
        take = cnt >= kf
        live = jnp.where(take, x, live ^ x)
        c_gt = jnp.where(take, c_gt, cnt)
        thr_u = thr_u | jnp.where(take, np.int32(-2 ** 31 if b == 31 else 2 ** b), np.int32(0))
    thr = thr_u ^ INT_MIN
    need = kf - c_gt

    p_refs, lg_refs = (p0_ref, p1_ref), (lg0_ref, lg1_ref)
    acc_ref[...] = jnp.zeros(acc_ref.shape, F32)
    p1_ref[...] = jnp.zeros(p1_ref.shape, BF16)

    def stage_scores(j, cnt_eq, slot):
        r0 = pl.multiple_of(j * KT, KT)
        kt = keys_ref[pl.ds(r0, KT), :]
        eq = kt == thr
        eqf = jnp.where(eq, 1.0, 0.0)
        rank = _dot(tri_ref[...], eqf.astype(BF16)) + cnt_eq
        sel = (kt > thr) | (eq & (rank <= need))
        sel = sel & (((r0 + krow) >> int(math.log2(CHUNK))) <= qchunk)
        bias = bias_ref[jnp.minimum(i - 2 * j, near)]
        logits = _dot(ckv_ref[0, pl.ds(r0, KT), :], qlat_ref[...])
        tmax = []
        for h in range(A_HEADS):
            sl = slice(h * QB, (h + 1) * QB)
            lh = jnp.where(sel, logits[:, sl] + bias[:, sl], -jnp.inf)
            lg_refs[slot][:, sl] = lh
            tmax.append(jnp.max(lh, axis=0, keepdims=True))
        return tuple(tmax), cnt_eq + jnp.sum(eqf, axis=0, keepdims=True)

    def stage_probs(ms, tmax, slot):
        new_ms, alphas = [], []
        for h in range(A_HEADS):
            sl = slice(h * QB, (h + 1) * QB)
            m_new = jnp.maximum(ms[h], tmax[h])
            m_safe = jnp.where(m_new == -jnp.inf, 0.0, m_new)
            p_refs[slot][:, sl] = jnp.exp2(lg_refs[slot][:, sl] - m_safe).astype(BF16)
            alphas.append(jnp.exp2(ms[h] - m_safe))
            new_ms.append(m_new)
        return tuple(new_ms), tuple(alphas)

    def stage_values(alphas, jp, slot):
        for h in range(A_HEADS):
            vs = slice(h * V_ROWS, (h + 1) * V_ROWS)
            pv = _dot(vt_ref[0, jp, vs, :], p_refs[slot][:, h * QB:(h + 1) * QB])
            acc_ref[vs, :] = acc_ref[vs, :] * alphas[h] + pv

    def trip(t, carry, slot):
        ms, tmax, cnt_eq, alphas = carry
        stage_values(alphas, jnp.maximum(t - 1, 0), 1 - slot)
        ms, alphas = stage_probs(ms, tmax, slot)
        tmax, cnt_eq = stage_scores(t + 1, cnt_eq, 1 - slot)
        return ms, tmax, cnt_eq, alphas

    def finish(t, carry, slot):
        ms, tmax, _, alphas = carry
        stage_values(alphas, jnp.maximum(t - 1, 0), 1 - slot)
        _, alphas = stage_probs(ms, tmax, slot)
        stage_values(alphas, t, slot)

    def pair(u, carry):
        return trip(2 * u + 1, trip(2 * u, carry, 0), 1)

    tmax0, cnt_eq0 = stage_scores(0, jnp.zeros((1, QB), F32), 0)
    init = (tuple(jnp.full((1, QB), -jnp.inf, F32) for _ in range(A_HEADS)), tmax0, cnt_eq0,
            tuple(jnp.ones((1, QB), F32) for _ in range(A_HEADS)))
    n_trips = nt - 1
    carry = lax.fori_loop(0, n_trips // 2, pair, init)

    @pl.when(n_trips % 2 == 0)
    def _():
        finish(nt - 1, carry, 0)

    @pl.when(n_trips % 2 == 1)
    def _():
        finish(nt - 1, trip(nt - 2, carry, 0), 1)

    a_t = jnp.concatenate(
        [acc_ref[h * V_ROWS:h * V_ROWS + A_V_DIM, :]
         * (1.0 / acc_ref[h * V_ROWS + A_V_DIM:h * V_ROWS + A_V_DIM + 1, :]) for h in range(A_HEADS)],
        axis=0)
    o_ref[...] = a_t.T.astype(BF16)


def _attn_call(q, iq, sm, ckv3, vt4, ik3, wuk, bias_tiles, tri, *, k_top, near):
    b, s = ckv3.shape[0], ckv3.shape[1]
    nb = s // QB
    n_kt = s // KT
    hv = A_HEADS * A_V_DIM
    hvr = A_HEADS * V_ROWS
    kern = functools.partial(_attn_kernel, k_top=k_top, near=near)
    row = lambda w: pl.BlockSpec((QB, w), lambda bb, i: (bb * nb + i, 0))
    return pl.pallas_call(
        kern,
        grid=(b, nb),
        in_specs=[
            row(q.shape[1]), row(iq.shape[1]), row(SMALL_W),
            pl.BlockSpec((1, s, A_KV_RANK), lambda bb, i: (bb, 0, 0)),
            pl.BlockSpec((1, n_kt, hvr, KT), lambda bb, i: (bb, 0, 0, 0)),
            pl.BlockSpec((1, s, IDX_DIM), lambda bb, i: (bb, 0, 0)),
            _const_spec(wuk.shape), _const_spec(bias_tiles.shape), _const_spec(tri.shape),
        ],
        out_specs=row(hv),
        out_shape=jax.ShapeDtypeStruct((b * s, hv), BF16),
        scratch_shapes=[pltpu.VMEM((s, QB), I32), pltpu.VMEM((32, s // 32, QB), I32),
                        pltpu.VMEM((A_KV_RANK, A_HEADS * QB), BF16),
                        pltpu.VMEM((hvr, QB), F32),
                        pltpu.VMEM((KT, A_HEADS * QB), BF16), pltpu.VMEM((KT, A_HEADS * QB), BF16),
                        pltpu.VMEM((KT, A_HEADS * QB), F32), pltpu.VMEM((KT, A_HEADS * QB), F32)],
        compiler_params=_params(("parallel", "arbitrary")),
        name="dsa_attention",
    )(q, iq, sm, ckv3, vt4, ik3, wuk, bias_tiles, tri)


def _softplus(x):
    return jnp.maximum(x, 0.0) + jnp.log1p(jnp.exp(-jnp.abs(x)))


def _mixer_kernel(bm_ref, z_ref, xbc_ref, dtp_ref, dtt_ref, scw_ref, cw_ref, cb_ref,
                  dtb_r_ref, dtb_c_ref, alog_r_ref, alog_c_ref, dexp_ref, nw_ref,
                  tril_ref, triu_ref, eh_ref,
                  bo_ref, co_ref, ubuf, xbuf, state):
    L = SSD_L
    P = SUBLANES
    gw = SSM_INNER // SSM_GROUPS

    @pl.when(pl.program_id(1) == 0)
    def _():
        ubuf[0:P, :] = jnp.zeros((P, B_WIDTH), F32)
        xbuf[0:P, :] = jnp.zeros((P, SSM_XBC), F32)
        state[...] = jnp.zeros(state.shape, F32)

    bm = bm_ref[...]
    u = bm[:, B_WIDTH:2 * B_WIDTH] * bm[:, 2 * B_WIDTH:3 * B_WIDTH]
    ubuf[P:P + L, :] = u
    conv = scw_ref[SHORT_CONV - 1:SHORT_CONV, :] * u
    for j in range(SHORT_CONV - 1):
        off = P - (SHORT_CONV - 1) + j
        conv = conv + scw_ref[j:j + 1, :] * ubuf[off:off + L, :]
    bo_ref[...] = (bm[:, :B_WIDTH] * conv).astype(BF16)
    ubuf[0:P, :] = u[L - P:L, :]

    xr = xbc_ref[...]
    xbuf[P:P + L, :] = xr
    xc = cw_ref[SSM_CONV - 1:SSM_CONV, :] * xr + cb_ref[...]
    for j in range(SSM_CONV - 1):
        off = P - (SSM_CONV - 1) + j
        xc = xc + cw_ref[j:j + 1, :] * xbuf[off:off + L, :]
    xbuf[0:P, :] = xr[L - P:L, :]
    xa = xc * _sigmoid(xc)
    xs = xa[:, :SSM_INNER]

    dt = _softplus(dtp_ref[:, :SSM_HEADS] + dtb_r_ref[...])
    a = dt * (-jnp.exp(alog_r_ref[...]))
    a_cs = sum(_dot(tril_ref[...], part) for part in _split3(a))
    dt_t = _softplus(dtt_ref[0] + dtb_c_ref[...])
    a_t = dt_t * (-jnp.exp(alog_c_ref[...]))
    a_cs_t = sum(_dot(part, triu_ref[...]) for part in _split3(a_t))

    def expand(v):
        return sum(_dot(part, eh_ref[...]) for part in _split3(v))

    xdt = xs * expand(dt)
    e_cs = expand(jnp.exp(a_cs))
    dec = expand(jnp.exp(a_cs[L - 1:L, :] - a_cs))
    chunk_dec = e_cs[L - 1:L, :]

    row = lax.broadcasted_iota(I32, (L, L), 0)
    col = lax.broadcasted_iota(I32, (L, L), 1)
    causal = row >= col
    lo_half = lax.broadcasted_iota(I32, (L, LANES), 1) < SSM_HEAD_DIM
    hpg = SSM_HEADS // SSM_GROUPS

    y_parts = []
    for g in range(SSM_GROUPS):
        bg = xa[:, SSM_INNER + g * SSM_STATE:SSM_INNER + (g + 1) * SSM_STATE].astype(BF16)
        cg = xa[:, SSM_INNER + (SSM_GROUPS + g) * SSM_STATE:
                SSM_INNER + (SSM_GROUPS + g + 1) * SSM_STATE].astype(BF16)
        cb = lax.dot_general(cg, bg, (((1,), (1,)), ((), ())), preferred_element_type=F32)
        sl = slice(g * gw, (g + 1) * gw)
        st = state[g]
        y_off = _dot(cg, st.astype(BF16)) * e_cs[:, sl]
        for k in range(hpg // 2):
            xp = xdt[:, g * gw + k * LANES:g * gw + (k + 1) * LANES]
            halves = (jnp.where(lo_half, xp, 0.0).astype(BF16), jnp.where(lo_half, 0.0, xp).astype(BF16))
            y = y_off[:, k * LANES:(k + 1) * LANES]
            for hh in range(2):
                h = g * hpg + 2 * k + hh
                seg = a_cs[:, h:h + 1] - a_cs_t[h:h + 1, :]
                m = (cb * jnp.exp(jnp.where(causal, seg, -jnp.inf))).astype(BF16)
                y = y + _dot(m, halves[hh])
            y_parts.append(y)
        xw = (dec[:, sl] * xdt[:, sl]).astype(BF16)
        upd = lax.dot_general(bg, xw, (((0,), (0,)), ((), ())), preferred_element_type=F32)
        state[g] = chunk_dec[:, sl] * st + upd

    y = jnp.concatenate(y_parts, axis=1) + dexp_ref[...] * xs
    zz = z_ref[...]
    y = y * (zz * _sigmoid(zz))
    outs = []
    for g in range(SSM_GROUPS):
        sl = slice(g * gw, (g + 1) * gw)
        outs.append(_rms(y[:, sl], nw_ref[:, sl]))
    co_ref[...] = jnp.concatenate(outs, axis=1).astype(BF16)


def _mixer_call(bmix, z, xbc, sm, dt_t, scw, cw, cb, dtb, alog, dexp, nw, b, s):
    L = SSD_L
    nc = s // L
    tril = jnp.asarray(np.tril(np.ones((L, L), np.float32)), BF16)
    triu = jnp.asarray(np.triu(np.ones((L, L), np.float32)), BF16)
    eh = jnp.asarray(np.repeat(np.eye(SSM_HEADS, dtype=np.float32), SSM_HEAD_DIM, axis=1), BF16)
    row = lambda w: pl.BlockSpec((L, w), lambda bb, c: (bb * nc + c, 0))
    consts = [scw, cw, cb.reshape(1, -1), dtb.reshape(1, -1), dtb.reshape(-1, 1),
              alog.reshape(1, -1), alog.reshape(-1, 1), dexp.reshape(1, -1), nw.reshape(1, -1),
              tril, triu, eh]
    return pl.pallas_call(
        _mixer_kernel,
        grid=(b, nc),
        in_specs=[row(3 * B_WIDTH), row(SSM_INNER), row(SSM_XBC), row(SMALL_W),
                  pl.BlockSpec((1, SSM_HEADS, L), lambda bb, c: (bb, 0, c))]
                 + [_const_spec(c.shape) for c in consts],
        out_specs=[row(B_WIDTH), row(SSM_INNER)],
        out_shape=[jax.ShapeDtypeStruct((b * s, B_WIDTH), BF16),
                   jax.ShapeDtypeStruct((b * s, SSM_INNER), BF16)],
        scratch_shapes=[pltpu.VMEM((L + SUBLANES, B_WIDTH), F32),
                        pltpu.VMEM((L + SUBLANES, SSM_XBC), F32),
                        pltpu.VMEM((SSM_GROUPS, SSM_STATE, SSM_INNER // SSM_GROUPS), F32)],
        compiler_params=_params(("parallel", "arbitrary")),
        name="conv_ssd_mixers",
    )(bmix, z, xbc, sm, dt_t, *consts)


def _dense_kernel(x_ref, a_ref, b_ref, c_ref, p_ref, g_post_ref, g_pre_ref, g_fpost_ref,
                  woa_ref, wob_ref, woc_ref, wg_ref, wu_ref, wd_ref, wpg_ref, wpp_ref, o_ref):
    mix = _dot(a_ref[...], woa_ref[...]) + _dot(b_ref[...], wob_ref[...]) + _dot(c_ref[...], woc_ref[...])
    x = x_ref[...] + _rms(mix, g_post_ref[...])
    h = _rms(x, g_pre_ref[...]).astype(BF16)
    gate = _dot(h, wg_ref[...])
    up = _dot(h, wu_ref[...])
    act = (gate * _sigmoid(gate) * up).astype(BF16)
    x = x + _rms(_dot(act, wd_ref[...]), g_fpost_ref[...])
    pg = _sigmoid(_dot(x.astype(BF16), wpg_ref[...]))
    o_ref[...] = x + pg * _dot(p_ref[...].astype(BF16), wpp_ref[...])


def _dense_call(x2, a, b, c, p2, g_post, g_pre, g_fpost, woa, wob, woc, wg, wu, wd, wpg, wpp):
    t = x2.shape[0]
    tm = TM_OUT
    row = lambda w: pl.BlockSpec((tm, w), lambda r: (r, 0))
    consts = [g_post, g_pre, g_fpost, woa, wob, woc, wg, wu, wd, wpg, wpp]
    return pl.pallas_call(
        _dense_kernel,
        grid=(t // tm,),
        in_specs=[row(D_MODEL), row(a.shape[1]), row(b.shape[1]), row(c.shape[1]), row(PLE_DIM)]
                 + [_const_spec(w.shape) for w in consts],
        out_specs=row(D_MODEL),
        out_shape=jax.ShapeDtypeStruct((t, D_MODEL), F32),
        compiler_params=_params(("parallel",)),
        name="outproj_ffn_ple",
    )(x2, a, b, c, p2, *consts)


def kernel(x, p, pre_mix_norm, post_mix_norm, pre_ffn_norm, post_ffn_norm, w_in, kv_norm, idx_k_norm_g, idx_k_norm_b, w_uk, w_uv, rel_bias, short_conv_w, ssm_conv_w, ssm_conv_b, ssm_dt_bias, ssm_a_log, ssm_d, ssm_norm, w_out, w_ffn_gate, w_ffn_up, w_ffn_down, w_ple_proj, w_ple_gate):
    bsz, s, d = x.shape
    assert d == D_MODEL and s % KT == 0 and s % SSD_L == 0 and (bsz * s) % TM_OUT == 0
    t = bsz * s
    k_top = min(TOPK_MAX, s // 4)

    bucket_np, near, far_bucket = _bucket_tiles()
    bias_tiles = _bias_tiles_call(rel_bias.astype(F32), jnp.asarray(bucket_np), far_bucket)
    tri = jnp.asarray(np.tril(np.ones((KT, KT), np.float32)), BF16)
    row1 = lambda v: v.reshape(1, -1).astype(F32)
    vone_np = np.zeros((A_HEADS, V_ROWS, 1), np.float32)
    vone_np[:, A_V_DIM] = 1.0
    vone = jnp.asarray(vone_np.reshape(A_HEADS * V_ROWS, 1))

    x2 = x.reshape(t, d)
    for i in range(DEPTH):
        wparts = _wprep_call(w_in[i].astype(F32))
        wuv_t = jnp.transpose(w_uv[i], (1, 2, 0)).astype(BF16)
        wuv_t = jnp.pad(wuv_t, ((0, 0), (0, V_ROWS - A_V_DIM), (0, 0))).reshape(A_HEADS * V_ROWS, A_KV_RANK)
        q, ckv, vt4, iq, ik, sm, bmix, z, xbc, dtp = _inproj_call(
            x2, row1(pre_mix_norm[i]), row1(kv_norm[i]), row1(idx_k_norm_g[i]), row1(idx_k_norm_b[i]),
            *wparts, wuv_t, vone, s // KT)

        a_out = _attn_call(
            q, iq, sm, ckv.reshape(bsz, s, A_KV_RANK), vt4, ik.reshape(bsz, s, IDX_DIM),
            jnp.transpose(w_uk[i], (1, 0, 2)).astype(BF16),
            bias_tiles, tri, k_top=k_top, near=near)

        dt_t = dtp[:, :SSM_HEADS].reshape(bsz, s, SSM_HEADS).transpose(0, 2, 1)
        b_out, c_out = _mixer_call(
            bmix, z, xbc, dtp, dt_t, short_conv_w[i].astype(F32), ssm_conv_w[i].astype(F32),
            ssm_conv_b[i].astype(F32), ssm_dt_bias[i].astype(F32), ssm_a_log[i].astype(F32),
            jnp.repeat(ssm_d[i].astype(F32), SSM_HEAD_DIM), ssm_norm[i].astype(F32), bsz, s)

        wo = w_out[i].astype(BF16)
        na, nbw = A_HEADS * A_V_DIM, A_HEADS * A_V_DIM + B_WIDTH
        x2 = _dense_call(
            x2, a_out, b_out, c_out, p[i].reshape(t, PLE_DIM),
            row1(post_mix_norm[i]), row1(pre_ffn_norm[i]), row1(post_ffn_norm[i]),
            wo[:na], wo[na:nbw], wo[nbw:], w_ffn_gate[i].astype(BF16), w_ffn_up[i].astype(BF16),
            w_ffn_down[i].astype(BF16), w_ple_gate[i].astype(BF16), w_ple_proj[i].astype(BF16))
    return x2.reshape(bsz, s, d)
```

```python
import functools
import math

import numpy as np
import jax
import jax.numpy as jnp
from jax import lax
from jax.experimental import pallas as pl
from jax.experimental.pallas import tpu as pltpu

F32, BF16, I32 = jnp.float32, jnp.bfloat16, jnp.int32

D_MODEL = 1024
DEPTH = 2
CHUNK = 64
A_HEADS = 8
A_QK_DIM = 64
A_V_DIM = 64
A_KV_RANK = 256
IDX_HEADS = 4
IDX_DIM = 64
TOPK_MAX = 256
REL_BUCKETS = 32
REL_MAX_DIST = 1024
B_WIDTH = 512
SHORT_CONV = 3
SSM_HEADS = 16
SSM_HEAD_DIM = 64
SSM_INNER = SSM_HEADS * SSM_HEAD_DIM
SSM_GROUPS = 2
SSM_STATE = 128
SSM_CONV = 4
SSM_XBC = SSM_INNER + 2 * SSM_GROUPS * SSM_STATE
MIX_WIDTH = A_HEADS * A_V_DIM + B_WIDTH + SSM_INNER
D_FF = -(-8 * D_MODEL // (3 * 256)) * 256
PLE_DIM = 256
NORM_EPS = 1e-6
IN_SPLITS = (A_HEADS * A_QK_DIM, A_KV_RANK, IDX_HEADS * IDX_DIM, IDX_DIM, IDX_HEADS,
             B_WIDTH, B_WIDTH, B_WIDTH, SSM_INNER, SSM_XBC, SSM_HEADS)
IN_OFFSETS = tuple(int(v) for v in np.concatenate([[0], np.cumsum(IN_SPLITS)]))

LANES = 128
SUBLANES = 8
VMEM_LIMIT_BYTES = 56 * 1024 * 1024

QB = LANES
KT = 256
SSD_L = 128
TM_SUB = 256
ROW_SUBTILES = 2
TM_OUT = TM_SUB * ROW_SUBTILES
SMALL_W = LANES
IW_OFF = IDX_DIM
INT_MIN = np.int32(-2 ** 31)
LOG2E = math.log2(math.e)
BF16_ROWS = 2 * SUBLANES
V_ROWS = A_V_DIM + BF16_ROWS

def _const_spec(shape):
    nd = len(shape)
    return pl.BlockSpec(shape, lambda *_: (0,) * nd, pipeline_mode=pl.Buffered(1))


def _params(sem):
    return pltpu.CompilerParams(dimension_semantics=sem, vmem_limit_bytes=VMEM_LIMIT_BYTES)


def _rms(x, g):
    return x * lax.rsqrt(jnp.mean(x * x, axis=-1, keepdims=True) + NORM_EPS) * g


def _sigmoid(x):
    return 1.0 / (1.0 + jnp.exp(-x))


def _split3(v):
    hi = v.astype(BF16)
    r1 = v - hi.astype(F32)
    mid = r1.astype(BF16)
    lo = (r1 - mid.astype(F32)).astype(BF16)
    return hi, mid, lo


def _dot(a, b):
    return jnp.dot(a, b, preferred_element_type=F32)


def _t5_bucket_np(rel):
    half = REL_BUCKETS // 2
    max_exact = half // 2
    ret = np.where(rel > 0, half, 0)
    n = np.abs(rel)
    nf = np.maximum(n, 1).astype(np.float32)
    large = max_exact + (np.log(nf / np.float32(max_exact)) / np.float32(math.log(REL_MAX_DIST / max_exact))
                         * np.float32(half - max_exact)).astype(np.int32)
    large = np.minimum(large, half - 1)
    return (ret + np.where(n < max_exact, n, large)).astype(np.int32)


@functools.lru_cache(maxsize=None)
def _bucket_tiles():
    kr = np.arange(KT)[:, None]
    qc = np.arange(QB)[None, :]
    tiles, near = [], None
    for o in range(64):
        t = _t5_bucket_np(kr - qc - o * QB)
        tiles.append(t)
    for o in range(63, -1, -1):
        if not (np.all(tiles[o] == tiles[63][0, 0])):
            near = o + 1
            break
    assert near is not None and 0 < near < 63
    return np.stack(tiles[:near + 1]), near, int(tiles[63][0, 0])


def _bias_kernel(rb_ref, bk_ref, out_ref, *, far_bucket):
    bk = bk_ref[0]
    for h in range(A_HEADS):
        acc = jnp.zeros(bk.shape, F32)
        for b in range(REL_BUCKETS):
            acc = jnp.where(bk == b, (rb_ref[b, h] - rb_ref[far_bucket, h]) * LOG2E, acc)
        out_ref[0, :, h * QB:(h + 1) * QB] = acc


def _bias_tiles_call(rel_bias, bucket_tiles, far_bucket):
    n = bucket_tiles.shape[0]
    return pl.pallas_call(
        functools.partial(_bias_kernel, far_bucket=far_bucket),
        grid=(n,),
        in_specs=[pl.BlockSpec(memory_space=pltpu.SMEM),
                  pl.BlockSpec((1, KT, QB), lambda o: (o, 0, 0))],
        out_specs=pl.BlockSpec((1, KT, A_HEADS * QB), lambda o: (o, 0, 0)),
        out_shape=jax.ShapeDtypeStruct((n, KT, A_HEADS * QB), F32),
        compiler_params=_params(("arbitrary",)),
        name="rel_bias_tiles",
    )(rel_bias, bucket_tiles)


def _wprep_kernel(w_ref, wq_out, wckv_out, wiq_out, wsm_out, wb_out, wz_out, wxbc_out, wdt_out):
    o_q, o_ckv, o_iq, o_ik, _, o_bg, _, _, o_z, o_xbc, o_dt, o_end = IN_OFFSETS
    wq_out[...] = w_ref[:, o_q:o_ckv].astype(BF16)
    wckv_out[...] = w_ref[:, o_ckv:o_iq].astype(BF16)
    wiq_out[...] = w_ref[:, o_iq:o_ik].astype(BF16)
    wsm_out[...] = w_ref[:, o_ik:o_ik + SMALL_W].astype(BF16)
    tail = w_ref[:, o_bg:o_end].astype(BF16)
    wb_out[...] = tail[:, :o_z - o_bg]
    wz_out[...] = tail[:, o_z - o_bg:o_xbc - o_bg]
    wxbc_out[...] = tail[:, o_xbc - o_bg:o_dt - o_bg]
    wdt_out[...] = jnp.zeros(wdt_out.shape, BF16)
    wdt_out[:, :o_end - o_dt] = tail[:, o_dt - o_bg:]


def _wprep_call(w):
    d, n = w.shape
    tr = 256
    widths = [IN_SPLITS[0], IN_SPLITS[1], IN_SPLITS[2], SMALL_W, 3 * B_WIDTH, SSM_INNER, SSM_XBC, SMALL_W]
    return pl.pallas_call(
        _wprep_kernel,
        grid=(d // tr,),
        in_specs=[pl.BlockSpec((tr, n), lambda r: (r, 0))],
        out_specs=[pl.BlockSpec((tr, wd), lambda r: (r, 0)) for wd in widths],
        out_shape=[jax.ShapeDtypeStruct((d, wd), BF16) for wd in widths],
        compiler_params=_params(("parallel",)),
        name="inproj_weight_prep",
    )(w)


def _inproj_kernel(x_ref, g_ref, kvn_ref, ikg_ref, ikb_ref,
                   wq_ref, wckv_ref, wiq_ref, wsm_ref, wb_ref, wz_ref, wxbc_ref, wdt_ref, wuv_ref, vone_ref,
                   q_out, ckv_out, vt_out, iq_out, ik_out, sm_out, b_out, z_out, xbc_out, dt_out):
    h = _rms(x_ref[...], g_ref[...]).astype(BF16)
    q_out[...] = _dot(h, wq_ref[...]).astype(BF16)
    ckv = _rms(_dot(h, wckv_ref[...]), kvn_ref[...]).astype(BF16)
    ckv_out[...] = ckv
    vt_out[0, 0] = (lax.dot_general(wuv_ref[...], ckv, (((1,), (1,)), ((), ())),
                                    preferred_element_type=F32) + vone_ref[...]).astype(BF16)
    iq_out[...] = (_dot(h, wiq_ref[...]) * (IDX_DIM ** -0.5)).astype(BF16)
    sm = _dot(h, wsm_ref[...])
    sm_out[...] = sm
    ik = sm[:, :IDX_DIM]
    mu = jnp.mean(ik, axis=-1, keepdims=True)
    var = jnp.mean(jnp.square(ik - mu), axis=-1, keepdims=True)
    ik_out[...] = ((ik - mu) * lax.rsqrt(var + NORM_EPS) * ikg_ref[...] + ikb_ref[...]).astype(BF16)
    b_out[...] = _dot(h, wb_ref[...])
    z_out[...] = _dot(h, wz_ref[...])
    xbc_out[...] = _dot(h, wxbc_ref[...])
    dt_out[...] = _dot(h, wdt_ref[...])


def _inproj_call(x2, g, kvn, ikg, ikb, wq, wckv, wiq, wsm, wb, wz, wxbc, wdt, wuv, vone, n_kt):
    t = x2.shape[0]
    tm = KT
    row = lambda w: pl.BlockSpec((tm, w), lambda r: (r, 0))
    rows = lambda w, dt: (row(w), jax.ShapeDtypeStruct((t, w), dt))
    hv = wuv.shape[0]
    outs = [rows(wq.shape[1], BF16), rows(wckv.shape[1], BF16),
            (pl.BlockSpec((1, 1, hv, tm), lambda r: (r // n_kt, r % n_kt, 0, 0)),
             jax.ShapeDtypeStruct((t // (n_kt * tm), n_kt, hv, tm), BF16)),
            rows(wiq.shape[1], BF16), rows(IDX_DIM, BF16), rows(SMALL_W, F32),
            rows(wb.shape[1], F32), rows(wz.shape[1], F32), rows(wxbc.shape[1], F32),
            rows(SMALL_W, F32)]
    consts = [g, kvn, ikg, ikb, wq, wckv, wiq, wsm, wb, wz, wxbc, wdt, wuv, vone]
    return pl.pallas_call(
        _inproj_kernel,
        grid=(t // tm,),
        in_specs=[row(D_MODEL)] + [_const_spec(c.shape) for c in consts],
        out_specs=[spec for spec, _ in outs],
        out_shape=[shape for _, shape in outs],
        compiler_params=_params(("parallel",)),
        name="inproj",
    )(x2, *consts)


def _bit_transpose32(words):
    a = list(words)
    m, j = 0x0000FFFF, 16
    while j:
        k = 0
        while k < 32:
            t = (a[k] ^ lax.shift_right_logical(a[k + j], np.int32(j))) & np.int32(m)
            a[k] = a[k] ^ t
            a[k + j] = a[k + j] ^ (t << j)
            k = (k + j + 1) & ~j
        j >>= 1
        m = (m ^ (m << j)) & 0xFFFFFFFF
    return a


def _attn_kernel(q_ref, iq_ref, sm_ref, ckv_ref, vt_ref, ik_ref, wuk_ref, bias_ref, tri_ref,
                 o_ref, keys_ref, planes_ref, qlat_ref, acc_ref, p0_ref, p1_ref, lg0_ref, lg1_ref,
                 *, k_top, near):
    i = pl.program_id(1)
    nt = (i + 2) // 2
    lane = lax.broadcasted_iota(I32, (1, QB), 1)
    qchunk = (i * QB + lane) >> int(math.log2(CHUNK))
    krow = lax.broadcasted_iota(I32, (KT, QB), 0)

    q_t = q_ref[...].astype(F32).T.astype(BF16)
    for h in range(A_HEADS):
        ql = _dot(wuk_ref[h], q_t[h * A_QK_DIM:(h + 1) * A_QK_DIM, :])
        qlat_ref[:, h * QB:(h + 1) * QB] = (ql * (A_QK_DIM ** -0.5 * LOG2E)).astype(BF16)

    iq_t = iq_ref[...].astype(F32).T.astype(BF16)
    iw = sm_ref[...].T[IW_OFF:IW_OFF + IDX_HEADS, :] * (IDX_HEADS ** -0.5)

    def score_tile(j, carry, masked):
        r0 = pl.multiple_of(j * KT, KT)
        ikt = ik_ref[0, pl.ds(r0, KT), :]
        s = jnp.zeros((KT, QB), F32)
        for h in range(IDX_HEADS):
            raw = _dot(ikt, iq_t[h * IDX_DIM:(h + 1) * IDX_DIM, :])
            s = s + jnp.maximum(raw, 0.0) * iw[h:h + 1, :]
        if masked:
            adm = ((r0 + krow) >> int(math.log2(CHUNK))) <= qchunk
            s = jnp.where(adm, s, -jnp.inf)
        bits = lax.bitcast_convert_type(s, I32)
        bits = jnp.where(bits == INT_MIN, 0, bits)
        key = bits ^ ((bits >> 31) & np.int32(0x7FFFFFFF))
        keys_ref[pl.ds(r0, KT), :] = key
        ukey = key ^ INT_MIN
        planes = _bit_transpose32([ukey[v * SUBLANES:(v + 1) * SUBLANES, :] for v in range(32)])
        p0 = pl.multiple_of(j * SUBLANES, SUBLANES)
        for b in range(32):
            planes_ref[b, pl.ds(p0, SUBLANES), :] = planes[31 - b]
        return carry

    @pl.when(i == 0)
    def _():
        planes_ref[...] = jnp.zeros(planes_ref.shape, I32)

    def score_pair(u, carry):
        score_tile(2 * u, 0, masked=False)
        score_tile(2 * u + 1, 0, masked=False)
        return carry

    lax.fori_loop(0, (nt - 1) // 2, score_pair, 0)

    @pl.when((nt - 1) % 2 == 1)
    def _():
        score_tile(nt - 2, 0, masked=False)

    score_tile(nt - 1, 0, masked=True)

    n_prow = planes_ref.shape[1]
    prow = lax.broadcasted_iota(I32, (n_prow, QB), 0)
    live = jnp.where(prow < nt * SUBLANES, np.int32(-1), np.int32(0))
    kf = float(k_top)
    c_gt = jnp.zeros((1, QB), F32)
    thr_u = jnp.zeros((1, QB), I32)
    for b in range(31, -1, -1):
        x = live & planes_ref[b]
        pc = lax.population_count(x)
        parts = [pc[r * SUBLANES:(r + 1) * SUBLANES, :] for r in range(n_prow // SUBLANES)]
        while len(parts) > 1:
            parts = [parts[k] + parts[k + 1] for k in range(0, len(parts) - 1, 2)] + (
                [parts[-1]] if len(parts) % 2 else [])
        cnt = c_gt + jnp.sum(parts[0].astype(F32), axis=0, keepdims=True)
        take = cnt >= kf
        live = jnp.where(take, x, live ^ x)
        c_gt = jnp.where(take, c_gt, cnt)
        thr_u = thr_u | jnp.where(take, np.int32(-2 ** 31 if b == 31 else 2 ** b), np.int32(0))
    thr = thr_u ^ INT_MIN
    need = kf - c_gt

    p_refs, lg_refs = (p0_ref, p1_ref), (lg0_ref, lg1_ref)
    acc_ref[...] = jnp.zeros(acc_ref.shape, F32)
    p1_ref[...] = jnp.zeros(p1_ref.shape, BF16)

    def stage_scores(j, cnt_eq, slot):
        r0 = pl.multiple_of(j * KT, KT)
        kt = keys_ref[pl.ds(r0, KT), :]
        eq = kt == thr
        eqf = jnp.where(eq, 1.0, 0.0)
        rank = _dot(tri_ref[...], eqf.astype(BF16)) + cnt_eq
        sel = (kt > thr) | (eq & (rank <= need))
        sel = sel & (((r0 + krow) >> int(math.log2(CHUNK))) <= qchunk)
        bias = bias_ref[jnp.minimum(i - 2 * j, near)]
        logits = _dot(ckv_ref[0, pl.ds(r0, KT), :], qlat_ref[...])
        tmax = []
        for h in range(A_HEADS):
            sl = slice(h * QB, (h + 1) * QB)
            lh = jnp.where(sel, logits[:, sl] + bias[:, sl], -jnp.inf)
            lg_refs[slot][:, sl] = lh
            tmax.append(jnp.max(lh, axis=0, keepdims=True))
        return tuple(tmax), cnt_eq + jnp.sum(eqf, axis=0, keepdims=True)

    def stage_probs(ms, tmax, slot):
        new_ms, alphas = [], []
        for h in range(A_HEADS):
            sl = slice(h * QB, (h + 1) * QB)
            m_new = jnp.maximum(ms[h], tmax[h])
            m_safe = jnp.where(m_new == -jnp.inf, 0.0, m_new)
            p_refs[slot][:, sl] = jnp.exp2(lg_refs[slot][:, sl] - m_safe).astype(BF16)
            alphas.append(jnp.exp2(ms[h] - m_safe))
            new_ms.append(m_new)
        return tuple(new_ms), tuple(alphas)

    def stage_values(alphas, jp, slot):
        for h in range(A_HEADS):
            vs = slice(h * V_ROWS, (h + 1) * V_ROWS)
            pv = _dot(vt_ref[0, jp, vs, :], p_refs[slot][:, h * QB:(h + 1) * QB])
            acc_ref[vs, :] = acc_ref[vs, :] * alphas[h] + pv

    def trip(t, carry, slot):
        ms, tmax, cnt_eq, alphas = carry
        stage_values(alphas, jnp.maximum(t - 1, 0), 1 - slot)
        ms, alphas = stage_probs(ms, tmax, slot)
        tmax, cnt_eq = stage_scores(t + 1, cnt_eq, 1 - slot)
        return ms, tmax, cnt_eq, alphas

    def finish(t, carry, slot):
        ms, tmax, _, alphas = carry
        stage_values(alphas, jnp.maximum(t - 1, 0), 1 - slot)
        _, alphas = stage_probs(ms, tmax, slot)
        stage_values(alphas, t, slot)

    def pair(u, carry):
        return trip(2 * u + 1, trip(2 * u, carry, 0), 1)

    tmax0, cnt_eq0 = stage_scores(0, jnp.zeros((1, QB), F32), 0)
    init = (tuple(jnp.full((1, QB), -jnp.inf, F32) for _ in range(A_HEADS)), tmax0, cnt_eq0,
            tuple(jnp.ones((1, QB), F32) for _ in range(A_HEADS)))
    n_trips = nt - 1
    carry = lax.fori_loop(0, n_trips // 2, pair, init)

    @pl.when(n_trips % 2 == 0)
    def _():
        finish(nt - 1, carry, 0)

    @pl.when(n_trips % 2 == 1)
    def _():
        finish(nt - 1, trip(nt - 2, carry, 0), 1)

    a_t = jnp.concatenate(
        [acc_ref[h * V_ROWS:h * V_ROWS + A_V_DIM, :]
         * (1.0 / acc_ref[h * V_ROWS + A_V_DIM:h * V_ROWS + A_V_DIM + 1, :]) for h in range(A_HEADS)],
        axis=0)
    o_ref[...] = a_t.T.astype(BF16)


def _attn_call(q, iq, sm, ckv3, vt4, ik3, wuk, bias_tiles, tri, *, k_top, near):
    b, s = ckv3.shape[0], ckv3.shape[1]
    nb = s // QB
    n_kt = s // KT
    hv = A_HEADS * A_V_DIM
    hvr = A_HEADS * V_ROWS
    kern = functools.partial(_attn_kernel, k_top=k_top, near=near)
    row = lambda w: pl.BlockSpec((QB, w), lambda bb, i: (bb * nb + i, 0))
    return pl.pallas_call(
        kern,
        grid=(b, nb),
        in_specs=[
            row(q.shape[1]), row(iq.shape[1]), row(SMALL_W),
            pl.BlockSpec((1, s, A_KV_RANK), lambda bb, i: (bb, 0, 0)),
            pl.BlockSpec((1, n_kt, hvr, KT), lambda bb, i: (bb, 0, 0, 0)),
            pl.BlockSpec((1, s, IDX_DIM), lambda bb, i: (bb, 0, 0)),
            _const_spec(wuk.shape), _const_spec(bias_tiles.shape), _const_spec(tri.shape),
        ],
        out_specs=row(hv),
        out_shape=jax.ShapeDtypeStruct((b * s, hv), BF16),
        scratch_shapes=[pltpu.VMEM((s, QB), I32), pltpu.VMEM((32, s // 32, QB), I32),
                        pltpu.VMEM((A_KV_RANK, A_HEADS * QB), BF16),
                        pltpu.VMEM((hvr, QB), F32),
                        pltpu.VMEM((KT, A_HEADS * QB), BF16), pltpu.VMEM((KT, A_HEADS * QB), BF16),
                        pltpu.VMEM((KT, A_HEADS * QB), F32), pltpu.VMEM((KT, A_HEADS * QB), F32)],
        compiler_params=_params(("parallel", "arbitrary")),
        name="dsa_attention",
    )(q, iq, sm, ckv3, vt4, ik3, wuk, bias_tiles, tri)


def _softplus(x):
    return jnp.maximum(x, 0.0) + jnp.log1p(jnp.exp(-jnp.abs(x)))


def _mixer_kernel(bm_ref, z_ref, xbc_ref, dtp_ref, dtt_ref, scw_ref, cw_ref, cb_ref,
                  dtb_r_ref, dtb_c_ref, alog_r_ref, alog_c_ref, dexp_ref, nw_ref,
                  tril_ref, triu_ref, eh_ref,
                  bo_ref, co_ref, ubuf, xbuf, state):
    L = SSD_L
    P = SUBLANES
    gw = SSM_INNER // SSM_GROUPS

    @pl.when(pl.program_id(1) == 0)
    def _():
        ubuf[0:P, :] = jnp.zeros((P, B_WIDTH), F32)
        xbuf[0:P, :] = jnp.zeros((P, SSM_XBC), F32)
        state[...] = jnp.zeros(state.shape, F32)

    bm = bm_ref[...]
    u = bm[:, B_WIDTH:2 * B_WIDTH] * bm[:, 2 * B_WIDTH:3 * B_WIDTH]
    ubuf[P:P + L, :] = u
    conv = scw_ref[SHORT_CONV - 1:SHORT_CONV, :] * u
    for j in range(SHORT_CONV - 1):
        off = P - (SHORT_CONV - 1) + j
        conv = conv + scw_ref[j:j + 1, :] * ubuf[off:off + L, :]
    bo_ref[...] = (bm[:, :B_WIDTH] * conv).astype(BF16)
    ubuf[0:P, :] = u[L - P:L, :]

    xr = xbc_ref[...]
    xbuf[P:P + L, :] = xr
    xc = cw_ref[SSM_CONV - 1:SSM_CONV, :] * xr + cb_ref[...]
    for j in range(SSM_CONV - 1):
        off = P - (SSM_CONV - 1) + j
        xc = xc + cw_ref[j:j + 1, :] * xbuf[off:off + L, :]
    xbuf[0:P, :] = xr[L - P:L, :]
    xa = xc * _sigmoid(xc)
    xs = xa[:, :SSM_INNER]

    dt = _softplus(dtp_ref[:, :SSM_HEADS] + dtb_r_ref[...])
    a = dt * (-jnp.exp(alog_r_ref[...]))
    a_cs = sum(_dot(tril_ref[...], part) for part in _split3(a))
    dt_t = _softplus(dtt_ref[0] + dtb_c_ref[...])
    a_t = dt_t * (-jnp.exp(alog_c_ref[...]))
    a_cs_t = sum(_dot(part, triu_ref[...]) for part in _split3(a_t))

    def expand(v):
        return sum(_dot(part, eh_ref[...]) for part in _split3(v))

    xdt = xs * expand(dt)
    e_cs = expand(jnp.exp(a_cs))
    dec = expand(jnp.exp(a_cs[L - 1:L, :] - a_cs))
    chunk_dec = e_cs[L - 1:L, :]

    row = lax.broadcasted_iota(I32, (L, L), 0)
    col = lax.broadcasted_iota(I32, (L, L), 1)
    causal = row >= col
    lo_half = lax.broadcasted_iota(I32, (L, LANES), 1) < SSM_HEAD_DIM
    hpg = SSM_HEADS // SSM_GROUPS

    y_parts = []
    for g in range(SSM_GROUPS):
        bg = xa[:, SSM_INNER + g * SSM_STATE:SSM_INNER + (g + 1) * SSM_STATE].astype(BF16)
        cg = xa[:, SSM_INNER + (SSM_GROUPS + g) * SSM_STATE:
                SSM_INNER + (SSM_GROUPS + g + 1) * SSM_STATE].astype(BF16)
        cb = lax.dot_general(cg, bg, (((1,), (1,)), ((), ())), preferred_element_type=F32)
        sl = slice(g * gw, (g + 1) * gw)
        st = state[g]
        y_off = _dot(cg, st.astype(BF16)) * e_cs[:, sl]
        for k in range(hpg // 2):
            xp = xdt[:, g * gw + k * LANES:g * gw + (k + 1) * LANES]
            halves = (jnp.where(lo_half, xp, 0.0).astype(BF16), jnp.where(lo_half, 0.0, xp).astype(BF16))
            y = y_off[:, k * LANES:(k + 1) * LANES]
            for hh in range(2):
                h = g * hpg + 2 * k + hh
                seg = a_cs[:, h:h + 1] - a_cs_t[h:h + 1, :]
                m = (cb * jnp.exp(jnp.where(causal, seg, -jnp.inf))).astype(BF16)
                y = y + _dot(m, halves[hh])
            y_parts.append(y)
        xw = (dec[:, sl] * xdt[:, sl]).astype(BF16)
        upd = lax.dot_general(bg, xw, (((0,), (0,)), ((), ())), preferred_element_type=F32)
        state[g] = chunk_dec[:, sl] * st + upd

    y = jnp.concatenate(y_parts, axis=1) + dexp_ref[...] * xs
    zz = z_ref[...]
    y = y * (zz * _sigmoid(zz))
    outs = []
    for g in range(SSM_GROUPS):
        sl = slice(g * gw, (g + 1) * gw)
        outs.append(_rms(y[:, sl], nw_ref[:, sl]))
    co_ref[...] = jnp.concatenate(outs, axis=1).astype(BF16)


def _mixer_call(bmix, z, xbc, sm, dt_t, scw, cw, cb, dtb, alog, dexp, nw, b, s):
    L = SSD_L
    nc = s // L
    tril = jnp.asarray(np.tril(np.ones((L, L), np.float32)), BF16)
    triu = jnp.asarray(np.triu(np.ones((L, L), np.float32)), BF16)
    eh = jnp.asarray(np.repeat(np.eye(SSM_HEADS, dtype=np.float32), SSM_HEAD_DIM, axis=1), BF16)
    row = lambda w: pl.BlockSpec((L, w), lambda bb, c: (bb * nc + c, 0))
    consts = [scw, cw, cb.reshape(1, -1), dtb.reshape(1, -1), dtb.reshape(-1, 1),
              alog.reshape(1, -1), alog.reshape(-1, 1), dexp.reshape(1, -1), nw.reshape(1, -1),
              tril, triu, eh]
    return pl.pallas_call(
        _mixer_kernel,
        grid=(b, nc),
        in_specs=[row(3 * B_WIDTH), row(SSM_INNER), row(SSM_XBC), row(SMALL_W),
                  pl.BlockSpec((1, SSM_HEADS, L), lambda bb, c: (bb, 0, c))]
                 + [_const_spec(c.shape) for c in consts],
        out_specs=[row(B_WIDTH), row(SSM_INNER)],
        out_shape=[jax.ShapeDtypeStruct((b * s, B_WIDTH), BF16),
                   jax.ShapeDtypeStruct((b * s, SSM_INNER), BF16)],
        scratch_shapes=[pltpu.VMEM((L + SUBLANES, B_WIDTH), F32),
                        pltpu.VMEM((L + SUBLANES, SSM_XBC), F32),
                        pltpu.VMEM((SSM_GROUPS, SSM_STATE, SSM_INNER // SSM_GROUPS), F32)],
        compiler_params=_params(("parallel", "arbitrary")),
        name="conv_ssd_mixers",
    )(bmix, z, xbc, sm, dt_t, *consts)


def _dense_kernel(x_ref, a_ref, b_ref, c_ref, p_ref, g_post_ref, g_pre_ref, g_fpost_ref,
                  woa_ref, wob_ref, woc_ref, wg_ref, wu_ref, wd_ref, wpg_ref, wpp_ref, o_ref):
    subs = [slice(r * TM_SUB, (r + 1) * TM_SUB) for r in range(ROW_SUBTILES)]
    mix = [_dot(a_ref[rows, :], woa_ref[...]) + _dot(b_ref[rows, :], wob_ref[...])
           + _dot(c_ref[rows, :], woc_ref[...]) for rows in subs]
    ple = [_dot(p_ref[rows, :].astype(BF16), wpp_ref[...]) for rows in subs]
    x = [x_ref[rows, :] + _rms(m, g_post_ref[...]) for rows, m in zip(subs, mix)]
    h = [_rms(xx, g_pre_ref[...]).astype(BF16) for xx in x]
    gate = [_dot(hh, wg_ref[...]) for hh in h]
    up = [_dot(hh, wu_ref[...]) for hh in h]
    act = [(g * _sigmoid(g) * u).astype(BF16) for g, u in zip(gate, up)]
    f = [_dot(aa, wd_ref[...]) for aa in act]
    x = [xx + _rms(ff, g_fpost_ref[...]) for xx, ff in zip(x, f)]
    pg = [_sigmoid(_dot(xx.astype(BF16), wpg_ref[...])) for xx in x]
    for rows, xx, g, e in zip(subs, x, pg, ple):
        o_ref[rows, :] = xx + g * e


def _dense_call(x2, a, b, c, p2, g_post, g_pre, g_fpost, woa, wob, woc, wg, wu, wd, wpg, wpp):
    t = x2.shape[0]
    tm = TM_OUT
    row = lambda w: pl.BlockSpec((tm, w), lambda r: (r, 0))
    consts = [g_post, g_pre, g_fpost, woa, wob, woc, wg, wu, wd, wpg, wpp]
    return pl.pallas_call(
        _dense_kernel,
        grid=(t // tm,),
        in_specs=[row(D_MODEL), row(a.shape[1]), row(b.shape[1]), row(c.shape[1]), row(PLE_DIM)]
                 + [_const_spec(w.shape) for w in consts],
        out_specs=row(D_MODEL),
        out_shape=jax.ShapeDtypeStruct((t, D_MODEL), F32),
        compiler_params=_params(("parallel",)),
        name="outproj_ffn_ple",
    )(x2, a, b, c, p2, *consts)


def kernel(x, p, pre_mix_norm, post_mix_norm, pre_ffn_norm, post_ffn_norm, w_in, kv_norm, idx_k_norm_g, idx_k_norm_b, w_uk, w_uv, rel_bias, short_conv_w, ssm_conv_w, ssm_conv_b, ssm_dt_bias, ssm_a_log, ssm_d, ssm_norm, w_out, w_ffn_gate, w_ffn_up, w_ffn_down, w_ple_proj, w_ple_gate):
    bsz, s, d = x.shape
    assert d == D_MODEL and s % KT == 0 and s % SSD_L == 0 and (bsz * s) % TM_OUT == 0
    t = bsz * s
    k_top = min(TOPK_MAX, s // 4)

    bucket_np, near, far_bucket = _bucket_tiles()
    bias_tiles = _bias_tiles_call(rel_bias.astype(F32), jnp.asarray(bucket_np), far_bucket)
    tri = jnp.asarray(np.tril(np.ones((KT, KT), np.float32)), BF16)
    row1 = lambda v: v.reshape(1, -1).astype(F32)
    vone_np = np.zeros((A_HEADS, V_ROWS, 1), np.float32)
    vone_np[:, A_V_DIM] = 1.0
    vone = jnp.asarray(vone_np.reshape(A_HEADS * V_ROWS, 1))

    x2 = x.reshape(t, d)
    for i in range(DEPTH):
        wparts = _wprep_call(w_in[i].astype(F32))
        wuv_t = jnp.transpose(w_uv[i], (1, 2, 0)).astype(BF16)
        wuv_t = jnp.pad(wuv_t, ((0, 0), (0, V_ROWS - A_V_DIM), (0, 0))).reshape(A_HEADS * V_ROWS, A_KV_RANK)
        q, ckv, vt4, iq, ik, sm, bmix, z, xbc, dtp = _inproj_call(
            x2, row1(pre_mix_norm[i]), row1(kv_norm[i]), row1(idx_k_norm_g[i]), row1(idx_k_norm_b[i]),
            *wparts, wuv_t, vone, s // KT)

        a_out = _attn_call(
            q, iq, sm, ckv.reshape(bsz, s, A_KV_RANK), vt4, ik.reshape(bsz, s, IDX_DIM),
            jnp.transpose(w_uk[i], (1, 0, 2)).astype(BF16),
            bias_tiles, tri, k_top=k_top, near=near)

        dt_t = dtp[:, :SSM_HEADS].reshape(bsz, s, SSM_HEADS).transpose(0, 2, 1)
        b_out, c_out = _mixer_call(
            bmix, z, xbc, dtp, dt_t, short_conv_w[i].astype(F32), ssm_conv_w[i].astype(F32),
            ssm_conv_b[i].astype(F32), ssm_dt_bias[i].astype(F32), ssm_a_log[i].astype(F32),
            jnp.repeat(ssm_d[i].astype(F32), SSM_HEAD_DIM), ssm_norm[i].astype(F32), bsz, s)

        wo = w_out[i].astype(BF16)
        na, nbw = A_HEADS * A_V_DIM, A_HEADS * A_V_DIM + B_WIDTH
        x2 = _dense_call(
            x2, a_out, b_out, c_out, p[i].reshape(t, PLE_DIM),
            row1(post_mix_norm[i]), row1(pre_ffn_norm[i]), row1(post_ffn_norm[i]),
            wo[:na], wo[na:nbw], wo[nbw:], w_ffn_gate[i].astype(BF16), w_ffn_up[i].astype(BF16),
            w_ffn_down[i].astype(BF16), w_ple_gate[i].astype(BF16), w_ple_proj[i].astype(BF16))
    return x2.reshape(bsz, s, d)
```

```python
import functools
import math

import numpy as np
import jax
import jax.numpy as jnp
from jax import lax
from jax.experimental import pallas as pl
from jax.experimental.pallas import tpu as pltpu

F32, BF16, I32 = jnp.float32, jnp.bfloat16, jnp.int32

D_MODEL = 1024
DEPTH = 2
CHUNK = 64
A_HEADS = 8
A_QK_DIM = 64
A_V_DIM = 64
A_KV_RANK = 256
IDX_HEADS = 4
IDX_DIM = 64
TOPK_MAX = 256
REL_BUCKETS = 32
REL_MAX_DIST = 1024
B_WIDTH = 512
SHORT_CONV = 3
SSM_HEADS = 16
SSM_HEAD_DIM = 64
SSM_INNER = SSM_HEADS * SSM_HEAD_DIM
SSM_GROUPS = 2
SSM_STATE = 128
SSM_CONV = 4
SSM_XBC = SSM_INNER + 2 * SSM_GROUPS * SSM_STATE
MIX_WIDTH = A_HEADS * A_V_DIM + B_WIDTH + SSM_INNER
D_FF = -(-8 * D_MODEL // (3 * 256)) * 256
PLE_DIM = 256
NORM_EPS = 1e-6
IN_SPLITS = (A_HEADS * A_QK_DIM, A_KV_RANK, IDX_HEADS * IDX_DIM, IDX_DIM, IDX_HEADS,
             B_WIDTH, B_WIDTH, B_WIDTH, SSM_INNER, SSM_XBC, SSM_HEADS)
IN_OFFSETS = tuple(int(v) for v in np.concatenate([[0], np.cumsum(IN_SPLITS)]))

LANES = 128
SUBLANES = 8
VMEM_LIMIT_BYTES = 56 * 1024 * 1024

QB = 2 * LANES
QGROUPS = QB // LANES
KT = 256
SSD_L = 128
TM_SUB = 256
ROW_SUBTILES = 2
TM_OUT = TM_SUB * ROW_SUBTILES
SMALL_W = LANES
IW_OFF = IDX_DIM
INT_MIN = np.int32(-2 ** 31)
LOG2E = math.log2(math.e)
BF16_ROWS = 2 * SUBLANES
V_ROWS = A_V_DIM + BF16_ROWS

def _const_spec(shape):
    nd = len(shape)
    return pl.BlockSpec(shape, lambda *_: (0,) * nd, pipeline_mode=pl.Buffered(1))


def _params(sem):
    return pltpu.CompilerParams(dimension_semantics=sem, vmem_limit_bytes=VMEM_LIMIT_BYTES)


def _rms(x, g):
    return x * lax.rsqrt(jnp.mean(x * x, axis=-1, keepdims=True) + NORM_EPS) * g


def _sigmoid(x):
    return 1.0 / (1.0 + jnp.exp(-x))


def _split3(v):
    hi = v.astype(BF16)
    r1 = v - hi.astype(F32)
    mid = r1.astype(BF16)
    lo = (r1 - mid.astype(F32)).astype(BF16)
    return hi, mid, lo


def _dot(a, b):
    return jnp.dot(a, b, preferred_element_type=F32)


def _t5_bucket_np(rel):
    half = REL_BUCKETS // 2
    max_exact = half // 2
    ret = np.where(rel > 0, half, 0)
    n = np.abs(rel)
    nf = np.maximum(n, 1).astype(np.float32)
    large = max_exact + (np.log(nf / np.float32(max_exact)) / np.float32(math.log(REL_MAX_DIST / max_exact))
                         * np.float32(half - max_exact)).astype(np.int32)
    large = np.minimum(large, half - 1)
    return (ret + np.where(n < max_exact, n, large)).astype(np.int32)


@functools.lru_cache(maxsize=None)
def _bucket_tiles():
    kr = np.arange(KT)[:, None]
    qc = np.arange(QB)[None, :]
    tiles, near = [], None
    for o in range(64):
        t = _t5_bucket_np(kr - qc - o * QB)
        tiles.append(t)
    for o in range(63, -1, -1):
        if not (np.all(tiles[o] == tiles[63][0, 0])):
            near = o + 1
            break
    assert near is not None and 0 < near < 63
    return np.stack(tiles[:near + 1]), near, int(tiles[63][0, 0])


def _bias_kernel(rb_ref, bk_ref, out_ref, *, far_bucket):
    bk = bk_ref[0]
    for h in range(A_HEADS):
        acc = jnp.zeros(bk.shape, F32)
        for b in range(REL_BUCKETS):
            acc = jnp.where(bk == b, (rb_ref[b, h] - rb_ref[far_bucket, h]) * LOG2E, acc)
        out_ref[0, :, h * QB:(h + 1) * QB] = acc


def _bias_tiles_call(rel_bias, bucket_tiles, far_bucket):
    n = bucket_tiles.shape[0]
    return pl.pallas_call(
        functools.partial(_bias_kernel, far_bucket=far_bucket),
        grid=(n,),
        in_specs=[pl.BlockSpec(memory_space=pltpu.SMEM),
                  pl.BlockSpec((1, KT, QB), lambda o: (o, 0, 0))],
        out_specs=pl.BlockSpec((1, KT, A_HEADS * QB), lambda o: (o, 0, 0)),
        out_shape=jax.ShapeDtypeStruct((n, KT, A_HEADS * QB), F32),
        compiler_params=_params(("arbitrary",)),
        name="rel_bias_tiles",
    )(rel_bias, bucket_tiles)


def _wprep_kernel(w_ref, wq_out, wckv_out, wiq_out, wsm_out, wb_out, wz_out, wxbc_out, wdt_out):
    o_q, o_ckv, o_iq, o_ik, _, o_bg, _, _, o_z, o_xbc, o_dt, o_end = IN_OFFSETS
    wq_out[...] = w_ref[:, o_q:o_ckv].astype(BF16)
    wckv_out[...] = w_ref[:, o_ckv:o_iq].astype(BF16)
    wiq_out[...] = w_ref[:, o_iq:o_ik].astype(BF16)
    wsm_out[...] = w_ref[:, o_ik:o_ik + SMALL_W].astype(BF16)
    tail = w_ref[:, o_bg:o_end].astype(BF16)
    wb_out[...] = tail[:, :o_z - o_bg]
    wz_out[...] = tail[:, o_z - o_bg:o_xbc - o_bg]
    wxbc_out[...] = tail[:, o_xbc - o_bg:o_dt - o_bg]
    wdt_out[...] = jnp.zeros(wdt_out.shape, BF16)
    wdt_out[:, :o_end - o_dt] = tail[:, o_dt - o_bg:]


def _wprep_call(w):
    d, n = w.shape
    tr = 256
    widths = [IN_SPLITS[0], IN_SPLITS[1], IN_SPLITS[2], SMALL_W, 3 * B_WIDTH, SSM_INNER, SSM_XBC, SMALL_W]
    return pl.pallas_call(
        _wprep_kernel,
        grid=(d // tr,),
        in_specs=[pl.BlockSpec((tr, n), lambda r: (r, 0))],
        out_specs=[pl.BlockSpec((tr, wd), lambda r: (r, 0)) for wd in widths],
        out_shape=[jax.ShapeDtypeStruct((d, wd), BF16) for wd in widths],
        compiler_params=_params(("parallel",)),
        name="inproj_weight_prep",
    )(w)


def _inproj_kernel(x_ref, g_ref, kvn_ref, ikg_ref, ikb_ref,
                   wq_ref, wckv_ref, wiq_ref, wsm_ref, wb_ref, wz_ref, wxbc_ref, wdt_ref, wuv_ref, vone_ref,
                   q_out, ckv_out, vt_out, iq_out, ik_out, sm_out, b_out, z_out, xbc_out, dt_out):
    h = _rms(x_ref[...], g_ref[...]).astype(BF16)
    q_out[...] = _dot(h, wq_ref[...]).astype(BF16)
    ckv = _rms(_dot(h, wckv_ref[...]), kvn_ref[...]).astype(BF16)
    ckv_out[...] = ckv
    vt_out[0, 0] = (lax.dot_general(wuv_ref[...], ckv, (((1,), (1,)), ((), ())),
                                    preferred_element_type=F32) + vone_ref[...]).astype(BF16)
    iq_out[...] = (_dot(h, wiq_ref[...]) * (IDX_DIM ** -0.5)).astype(BF16)
    sm = _dot(h, wsm_ref[...])
    sm_out[...] = sm
    ik = sm[:, :IDX_DIM]
    mu = jnp.mean(ik, axis=-1, keepdims=True)
    var = jnp.mean(jnp.square(ik - mu), axis=-1, keepdims=True)
    ik_out[...] = ((ik - mu) * lax.rsqrt(var + NORM_EPS) * ikg_ref[...] + ikb_ref[...]).astype(BF16)
    b_out[...] = _dot(h, wb_ref[...])
    z_out[...] = _dot(h, wz_ref[...])
    xbc_out[...] = _dot(h, wxbc_ref[...])
    dt_out[...] = _dot(h, wdt_ref[...])


def _inproj_call(x2, g, kvn, ikg, ikb, wq, wckv, wiq, wsm, wb, wz, wxbc, wdt, wuv, vone, n_kt):
    t = x2.shape[0]
    tm = KT
    row = lambda w: pl.BlockSpec((tm, w), lambda r: (r, 0))
    rows = lambda w, dt: (row(w), jax.ShapeDtypeStruct((t, w), dt))
    hv = wuv.shape[0]
    outs = [rows(wq.shape[1], BF16), rows(wckv.shape[1], BF16),
            (pl.BlockSpec((1, 1, hv, tm), lambda r: (r // n_kt, r % n_kt, 0, 0)),
             jax.ShapeDtypeStruct((t // (n_kt * tm), n_kt, hv, tm), BF16)),
            rows(wiq.shape[1], BF16), rows(IDX_DIM, BF16), rows(SMALL_W, F32),
            rows(wb.shape[1], F32), rows(wz.shape[1], F32), rows(wxbc.shape[1], F32),
            rows(SMALL_W, F32)]
    consts = [g, kvn, ikg, ikb, wq, wckv, wiq, wsm, wb, wz, wxbc, wdt, wuv, vone]
    return pl.pallas_call(
        _inproj_kernel,
        grid=(t // tm,),
        in_specs=[row(D_MODEL)] + [_const_spec(c.shape) for c in consts],
        out_specs=[spec for spec, _ in outs],
        out_shape=[shape for _, shape in outs],
        compiler_params=_params(("parallel",)),
        name="inproj",
    )(x2, *consts)


def _bit_transpose32(words):
    a = list(words)
    m, j = 0x0000FFFF, 16
    while j:
        k = 0
        while k < 32:
            t = (a[k] ^ lax.shift_right_logical(a[k + j], np.int32(j))) & np.int32(m)
            a[k] = a[k] ^ t
            a[k + j] = a[k + j] ^ (t << j)
            k = (k + j + 1) & ~j
        j >>= 1
        m = (m ^ (m << j)) & 0xFFFFFFFF
    return a


def _attn_kernel(q_ref, iq_ref, sm_ref, ckv_ref, vt_ref, ik_ref, wuk_ref, bias_ref, tri_ref,
                 o_ref, keys_ref, planes_ref, qlat_ref, acc_ref, p0_ref, p1_ref, lg0_ref, lg1_ref,
                 *, k_top, near):
    i = pl.program_id(1)
    nt = ((i + 1) * QB + KT - 1) // KT
    lane = lax.broadcasted_iota(I32, (1, QB), 1)
    qchunk = (i * QB + lane) >> int(math.log2(CHUNK))
    krow = lax.broadcasted_iota(I32, (KT, QB), 0)

    q_t = q_ref[...].astype(F32).T.astype(BF16)
    for h in range(A_HEADS):
        ql = _dot(wuk_ref[h], q_t[h * A_QK_DIM:(h + 1) * A_QK_DIM, :])
        qlat_ref[:, h * QB:(h + 1) * QB] = (ql * (A_QK_DIM ** -0.5 * LOG2E)).astype(BF16)

    iq_t = iq_ref[...].astype(F32).T.astype(BF16)
    iw = sm_ref[...].T[IW_OFF:IW_OFF + IDX_HEADS, :] * (IDX_HEADS ** -0.5)

    def score_tile(j, carry, masked):
        r0 = pl.multiple_of(j * KT, KT)
        ikt = ik_ref[0, pl.ds(r0, KT), :]
        s = jnp.zeros((KT, QB), F32)
        for h in range(IDX_HEADS):
            raw = _dot(ikt, iq_t[h * IDX_DIM:(h + 1) * IDX_DIM, :])
            s = s + jnp.maximum(raw, 0.0) * iw[h:h + 1, :]
        if masked:
            adm = ((r0 + krow) >> int(math.log2(CHUNK))) <= qchunk
            s = jnp.where(adm, s, -jnp.inf)
        bits = lax.bitcast_convert_type(s, I32)
        bits = jnp.where(bits == INT_MIN, 0, bits)
        key = bits ^ ((bits >> 31) & np.int32(0x7FFFFFFF))
        keys_ref[pl.ds(r0, KT), :] = key
        ukey = key ^ INT_MIN
        planes = _bit_transpose32([ukey[v * SUBLANES:(v + 1) * SUBLANES, :] for v in range(32)])
        p0 = pl.multiple_of(j * SUBLANES, SUBLANES)
        for b in range(32):
            planes_ref[b, pl.ds(p0, SUBLANES), :] = planes[31 - b]
        return carry

    @pl.when(i == 0)
    def _():
        planes_ref[...] = jnp.zeros(planes_ref.shape, I32)

    def score_pair(u, carry):
        score_tile(2 * u, 0, masked=False)
        score_tile(2 * u + 1, 0, masked=False)
        return carry

    lax.fori_loop(0, (nt - 1) // 2, score_pair, 0)

    @pl.when((nt - 1) % 2 == 1)
    def _():
        score_tile(nt - 2, 0, masked=False)

    score_tile(nt - 1, 0, masked=True)

    n_prow = planes_ref.shape[1]
    prow = lax.broadcasted_iota(I32, (n_prow, QB), 0)
    live = jnp.where(prow < nt * SUBLANES, np.int32(-1), np.int32(0))
    kf = float(k_top)
    c_gt = jnp.zeros((1, QB), F32)
    thr_u = jnp.zeros((1, QB), I32)
    for b in range(31, -1, -1):
        x = live & planes_ref[b]
        pc = lax.population_count(x)
        parts = [pc[r * SUBLANES:(r + 1) * SUBLANES, :] for r in range(n_prow // SUBLANES)]
        while len(parts) > 1:
            parts = [parts[k] + parts[k + 1] for k in range(0, len(parts) - 1, 2)] + (
                [parts[-1]] if len(parts) % 2 else [])
        cnt = c_gt + jnp.sum(parts[0].astype(F32), axis=0, keepdims=True)
        take = cnt >= kf
        live = jnp.where(take, x, live ^ x)
        c_gt = jnp.where(take, c_gt, cnt)
        thr_u = thr_u | jnp.where(take, np.int32(-2 ** 31 if b == 31 else 2 ** b), np.int32(0))
    thr = thr_u ^ INT_MIN
    need = kf - c_gt

    p_refs, lg_refs = (p0_ref, p1_ref), (lg0_ref, lg1_ref)
    n_cols = A_HEADS * QGROUPS
    acc_ref[...] = jnp.zeros(acc_ref.shape, F32)
    p1_ref[...] = jnp.zeros(p1_ref.shape, BF16)

    def stage_scores(j, cnt_eq, slot):
        r0 = pl.multiple_of(j * KT, KT)
        kt = keys_ref[pl.ds(r0, KT), :]
        eq = kt == thr
        eqf = jnp.where(eq, 1.0, 0.0)
        rank = _dot(tri_ref[...], eqf.astype(BF16)) + cnt_eq
        sel = (kt > thr) | (eq & (rank <= need))
        sel = sel & (((r0 + krow) >> int(math.log2(CHUNK))) <= qchunk)
        bias = bias_ref[jnp.minimum(i - j * (KT // QB), near)]
        logits = _dot(ckv_ref[0, pl.ds(r0, KT), :], qlat_ref[...])
        tmax = []
        for c in range(n_cols):
            sl = slice(c * LANES, (c + 1) * LANES)
            qs = slice((c % QGROUPS) * LANES, (c % QGROUPS + 1) * LANES)
            lh = jnp.where(sel[:, qs], logits[:, sl] + bias[:, sl], -jnp.inf)
            lg_refs[slot][:, sl] = lh
            tmax.append(jnp.max(lh, axis=0, keepdims=True))
        return tuple(tmax), cnt_eq + jnp.sum(eqf, axis=0, keepdims=True)

    def stage_probs(ms, tmax, slot):
        new_ms, alphas = [], []
        for c in range(n_cols):
            sl = slice(c * LANES, (c + 1) * LANES)
            m_new = jnp.maximum(ms[c], tmax[c])
            m_safe = jnp.where(m_new == -jnp.inf, 0.0, m_new)
            p_refs[slot][:, sl] = jnp.exp2(lg_refs[slot][:, sl] - m_safe).astype(BF16)
            alphas.append(jnp.exp2(ms[c] - m_safe))
            new_ms.append(m_new)
        return tuple(new_ms), tuple(alphas)

    def stage_values(alphas, jp, slot):
        for h in range(A_HEADS):
            vs = slice(h * V_ROWS, (h + 1) * V_ROWS)
            pv = _dot(vt_ref[0, jp, vs, :], p_refs[slot][:, h * QB:(h + 1) * QB])
            alpha = jnp.concatenate(alphas[h * QGROUPS:(h + 1) * QGROUPS], axis=1)
            acc_ref[vs, :] = acc_ref[vs, :] * alpha + pv

    def trip(t, carry, slot):
        ms, tmax, cnt_eq, alphas = carry
        stage_values(alphas, jnp.maximum(t - 1, 0), 1 - slot)
        ms, alphas = stage_probs(ms, tmax, slot)
        tmax, cnt_eq = stage_scores(t + 1, cnt_eq, 1 - slot)
        return ms, tmax, cnt_eq, alphas

    def finish(t, carry, slot):
        ms, tmax, _, alphas = carry
        stage_values(alphas, jnp.maximum(t - 1, 0), 1 - slot)
        _, alphas = stage_probs(ms, tmax, slot)
        stage_values(alphas, t, slot)

    def pair(u, carry):
        return trip(2 * u + 1, trip(2 * u, carry, 0), 1)

    tmax0, cnt_eq0 = stage_scores(0, jnp.zeros((1, QB), F32), 0)
    init = (tuple(jnp.full((1, LANES), -jnp.inf, F32) for _ in range(n_cols)), tmax0, cnt_eq0,
            tuple(jnp.ones((1, LANES), F32) for _ in range(n_cols)))
    n_trips = nt - 1
    carry = lax.fori_loop(0, n_trips // 2, pair, init)

    @pl.when(n_trips % 2 == 0)
    def _():
        finish(nt - 1, carry, 0)

    @pl.when(n_trips % 2 == 1)
    def _():
        finish(nt - 1, trip(nt - 2, carry, 0), 1)

    a_t = jnp.concatenate(
        [acc_ref[h * V_ROWS:h * V_ROWS + A_V_DIM, :]
         * (1.0 / acc_ref[h * V_ROWS + A_V_DIM:h * V_ROWS + A_V_DIM + 1, :]) for h in range(A_HEADS)],
        axis=0)
    o_ref[...] = a_t.T.astype(BF16)


def _attn_call(q, iq, sm, ckv3, vt4, ik3, wuk, bias_tiles, tri, *, k_top, near):
    b, s = ckv3.shape[0], ckv3.shape[1]
    nb = s // QB
    n_kt = s // KT
    hv = A_HEADS * A_V_DIM
    hvr = A_HEADS * V_ROWS
    kern = functools.partial(_attn_kernel, k_top=k_top, near=near)
    row = lambda w: pl.BlockSpec((QB, w), lambda bb, i: (bb * nb + i, 0))
    return pl.pallas_call(
        kern,
        grid=(b, nb),
        in_specs=[
            row(q.shape[1]), row(iq.shape[1]), row(SMALL_W),
            pl.BlockSpec((1, s, A_KV_RANK), lambda bb, i: (bb, 0, 0)),
            pl.BlockSpec((1, n_kt, hvr, KT), lambda bb, i: (bb, 0, 0, 0)),
            pl.BlockSpec((1, s, IDX_DIM), lambda bb, i: (bb, 0, 0)),
            _const_spec(wuk.shape), _const_spec(bias_tiles.shape), _const_spec(tri.shape),
        ],
        out_specs=row(hv),
        out_shape=jax.ShapeDtypeStruct((b * s, hv), BF16),
        scratch_shapes=[pltpu.VMEM((s, QB), I32), pltpu.VMEM((32, s // 32, QB), I32),
                        pltpu.VMEM((A_KV_RANK, A_HEADS * QB), BF16),
                        pltpu.VMEM((hvr, QB), F32),
                        pltpu.VMEM((KT, A_HEADS * QB), BF16), pltpu.VMEM((KT, A_HEADS * QB), BF16),
                        pltpu.VMEM((KT, A_HEADS * QB), F32), pltpu.VMEM((KT, A_HEADS * QB), F32)],
        compiler_params=_params(("parallel", "arbitrary")),
        name="dsa_attention",
    )(q, iq, sm, ckv3, vt4, ik3, wuk, bias_tiles, tri)


def _softplus(x):
    return jnp.maximum(x, 0.0) + jnp.log1p(jnp.exp(-jnp.abs(x)))


def _mixer_kernel(bm_ref, z_ref, xbc_ref, dtp_ref, dtt_ref, scw_ref, cw_ref, cb_ref,
                  dtb_r_ref, dtb_c_ref, alog_r_ref, alog_c_ref, dexp_ref, nw_ref,
                  tril_ref, triu_ref, eh_ref,
                  bo_ref, co_ref, ubuf, xbuf, state):
    L = SSD_L
    P = SUBLANES
    gw = SSM_INNER // SSM_GROUPS

    @pl.when(pl.program_id(1) == 0)
    def _():
        ubuf[0:P, :] = jnp.zeros((P, B_WIDTH), F32)
        xbuf[0:P, :] = jnp.zeros((P, SSM_XBC), F32)
        state[...] = jnp.zeros(state.shape, F32)

    bm = bm_ref[...]
    u = bm[:, B_WIDTH:2 * B_WIDTH] * bm[:, 2 * B_WIDTH:3 * B_WIDTH]
    ubuf[P:P + L, :] = u
    conv = scw_ref[SHORT_CONV - 1:SHORT_CONV, :] * u
    for j in range(SHORT_CONV - 1):
        off = P - (SHORT_CONV - 1) + j
        conv = conv + scw_ref[j:j + 1, :] * ubuf[off:off + L, :]
    bo_ref[...] = (bm[:, :B_WIDTH] * conv).astype(BF16)
    ubuf[0:P, :] = u[L - P:L, :]

    xr = xbc_ref[...]
    xbuf[P:P + L, :] = xr
    xc = cw_ref[SSM_CONV - 1:SSM_CONV, :] * xr + cb_ref[...]
    for j in range(SSM_CONV - 1):
        off = P - (SSM_CONV - 1) + j
        xc = xc + cw_ref[j:j + 1, :] * xbuf[off:off + L, :]
    xbuf[0:P, :] = xr[L - P:L, :]
    xa = xc * _sigmoid(xc)
    xs = xa[:, :SSM_INNER]

    dt = _softplus(dtp_ref[:, :SSM_HEADS] + dtb_r_ref[...])
    a = dt * (-jnp.exp(alog_r_ref[...]))
    a_cs = sum(_dot(tril_ref[...], part) for part in _split3(a))
    dt_t = _softplus(dtt_ref[0] + dtb_c_ref[...])
    a_t = dt_t * (-jnp.exp(alog_c_ref[...]))
    a_cs_t = sum(_dot(part, triu_ref[...]) for part in _split3(a_t))

    def expand(v):
        return sum(_dot(part, eh_ref[...]) for part in _split3(v))

    xdt = xs * expand(dt)
    e_cs = expand(jnp.exp(a_cs))
    dec = expand(jnp.exp(a_cs[L - 1:L, :] - a_cs))
    chunk_dec = e_cs[L - 1:L, :]

    row = lax.broadcasted_iota(I32, (L, L), 0)
    col = lax.broadcasted_iota(I32, (L, L), 1)
    causal = row >= col
    lo_half = lax.broadcasted_iota(I32, (L, LANES), 1) < SSM_HEAD_DIM
    hpg = SSM_HEADS // SSM_GROUPS

    y_parts = []
    for g in range(SSM_GROUPS):
        bg = xa[:, SSM_INNER + g * SSM_STATE:SSM_INNER + (g + 1) * SSM_STATE].astype(BF16)
        cg = xa[:, SSM_INNER + (SSM_GROUPS + g) * SSM_STATE:
                SSM_INNER + (SSM_GROUPS + g + 1) * SSM_STATE].astype(BF16)
        cb = lax.dot_general(cg, bg, (((1,), (1,)), ((), ())), preferred_element_type=F32)
        sl = slice(g * gw, (g + 1) * gw)
        st = state[g]
        y_off = _dot(cg, st.astype(BF16)) * e_cs[:, sl]
        for k in range(hpg // 2):
            xp = xdt[:, g * gw + k * LANES:g * gw + (k + 1) * LANES]
            halves = (jnp.where(lo_half, xp, 0.0).astype(BF16), jnp.where(lo_half, 0.0, xp).astype(BF16))
            y = y_off[:, k * LANES:(k + 1) * LANES]
            for hh in range(2):
                h = g * hpg + 2 * k + hh
                seg = a_cs[:, h:h + 1] - a_cs_t[h:h + 1, :]
                m = (cb * jnp.exp(jnp.where(causal, seg, -jnp.inf))).astype(BF16)
                y = y + _dot(m, halves[hh])
            y_parts.append(y)
        xw = (dec[:, sl] * xdt[:, sl]).astype(BF16)
        upd = lax.dot_general(bg, xw, (((0,), (0,)), ((), ())), preferred_element_type=F32)
        state[g] = chunk_dec[:, sl] * st + upd

    y = jnp.concatenate(y_parts, axis=1) + dexp_ref[...] * xs
    zz = z_ref[...]
    y = y * (zz * _sigmoid(zz))
    outs = []
    for g in range(SSM_GROUPS):
        sl = slice(g * gw, (g + 1) * gw)
        outs.append(_rms(y[:, sl], nw_ref[:, sl]))
    co_ref[...] = jnp.concatenate(outs, axis=1).astype(BF16)


def _mixer_call(bmix, z, xbc, sm, dt_t, scw, cw, cb, dtb, alog, dexp, nw, b, s):
    L = SSD_L
    nc = s // L
    tril = jnp.asarray(np.tril(np.ones((L, L), np.float32)), BF16)
    triu = jnp.asarray(np.triu(np.ones((L, L), np.float32)), BF16)
    eh = jnp.asarray(np.repeat(np.eye(SSM_HEADS, dtype=np.float32), SSM_HEAD_DIM, axis=1), BF16)
    row = lambda w: pl.BlockSpec((L, w), lambda bb, c: (bb * nc + c, 0))
    consts = [scw, cw, cb.reshape(1, -1), dtb.reshape(1, -1), dtb.reshape(-1, 1),
              alog.reshape(1, -1), alog.reshape(-1, 1), dexp.reshape(1, -1), nw.reshape(1, -1),
              tril, triu, eh]
    return pl.pallas_call(
        _mixer_kernel,
        grid=(b, nc),
        in_specs=[row(3 * B_WIDTH), row(SSM_INNER), row(SSM_XBC), row(SMALL_W),
                  pl.BlockSpec((1, SSM_HEADS, L), lambda bb, c: (bb, 0, c))]
                 + [_const_spec(c.shape) for c in consts],
        out_specs=[row(B_WIDTH), row(SSM_INNER)],
        out_shape=[jax.ShapeDtypeStruct((b * s, B_WIDTH), BF16),
                   jax.ShapeDtypeStruct((b * s, SSM_INNER), BF16)],
        scratch_shapes=[pltpu.VMEM((L + SUBLANES, B_WIDTH), F32),
                        pltpu.VMEM((L + SUBLANES, SSM_XBC), F32),
                        pltpu.VMEM((SSM_GROUPS, SSM_STATE, SSM_INNER // SSM_GROUPS), F32)],
        compiler_params=_params(("parallel", "arbitrary")),
        name="conv_ssd_mixers",
    )(bmix, z, xbc, sm, dt_t, *consts)


def _dense_kernel(x_ref, a_ref, b_ref, c_ref, p_ref, g_post_ref, g_pre_ref, g_fpost_ref,
                  woa_ref, wob_ref, woc_ref, wg_ref, wu_ref, wd_ref, wpg_ref, wpp_ref, o_ref):
    subs = [slice(r * TM_SUB, (r + 1) * TM_SUB) for r in range(ROW_SUBTILES)]
    mix = [_dot(a_ref[rows, :], woa_ref[...]) + _dot(b_ref[rows, :], wob_ref[...])
           + _dot(c_ref[rows, :], woc_ref[...]) for rows in subs]
    ple = [_dot(p_ref[rows, :].astype(BF16), wpp_ref[...]) for rows in subs]
    x = [x_ref[rows, :] + _rms(m, g_post_ref[...]) for rows, m in zip(subs, mix)]
    h = [_rms(xx, g_pre_ref[...]).astype(BF16) for xx in x]
    gate = [_dot(hh, wg_ref[...]) for hh in h]
    up = [_dot(hh, wu_ref[...]) for hh in h]
    act = [(g * _sigmoid(g) * u).astype(BF16) for g, u in zip(gate, up)]
    f = [_dot(aa, wd_ref[...]) for aa in act]
    x = [xx + _rms(ff, g_fpost_ref[...]) for xx, ff in zip(x, f)]
    pg = [_sigmoid(_dot(xx.astype(BF16), wpg_ref[...])) for xx in x]
    for rows, xx, g, e in zip(subs, x, pg, ple):
        o_ref[rows, :] = xx + g * e


def _dense_call(x2, a, b, c, p2, g_post, g_pre, g_fpost, woa, wob, woc, wg, wu, wd, wpg, wpp):
    t = x2.shape[0]
    tm = TM_OUT
    row = lambda w: pl.BlockSpec((tm, w), lambda r: (r, 0))
    consts = [g_post, g_pre, g_fpost, woa, wob, woc, wg, wu, wd, wpg, wpp]
    return pl.pallas_call(
        _dense_kernel,
        grid=(t // tm,),
        in_specs=[row(D_MODEL), row(a.shape[1]), row(b.shape[1]), row(c.shape[1]), row(PLE_DIM)]
                 + [_const_spec(w.shape) for w in consts],
        out_specs=row(D_MODEL),
        out_shape=jax.ShapeDtypeStruct((t, D_MODEL), F32),
        compiler_params=_params(("parallel",)),
        name="outproj_ffn_ple",
    )(x2, a, b, c, p2, *consts)


def kernel(x, p, pre_mix_norm, post_mix_norm, pre_ffn_norm, post_ffn_norm, w_in, kv_norm, idx_k_norm_g, idx_k_norm_b, w_uk, w_uv, rel_bias, short_conv_w, ssm_conv_w, ssm_conv_b, ssm_dt_bias, ssm_a_log, ssm_d, ssm_norm, w_out, w_ffn_gate, w_ffn_up, w_ffn_down, w_ple_proj, w_ple_gate):
    bsz, s, d = x.shape
    assert d == D_MODEL and s % KT == 0 and s % SSD_L == 0 and (bsz * s) % TM_OUT == 0
    t = bsz * s
    k_top = min(TOPK_MAX, s // 4)

    bucket_np, near, far_bucket = _bucket_tiles()
    bias_tiles = _bias_tiles_call(rel_bias.astype(F32), jnp.asarray(bucket_np), far_bucket)
    tri = jnp.asarray(np.tril(np.ones((KT, KT), np.float32)), BF16)
    row1 = lambda v: v.reshape(1, -1).astype(F32)
    vone_np = np.zeros((A_HEADS, V_ROWS, 1), np.float32)
    vone_np[:, A_V_DIM] = 1.0
    vone = jnp.asarray(vone_np.reshape(A_HEADS * V_ROWS, 1))

    x2 = x.reshape(t, d)
    for i in range(DEPTH):
        wparts = _wprep_call(w_in[i].astype(F32))
        wuv_t = jnp.transpose(w_uv[i], (1, 2, 0)).astype(BF16)
        wuv_t = jnp.pad(wuv_t, ((0, 0), (0, V_ROWS - A_V_DIM), (0, 0))).reshape(A_HEADS * V_ROWS, A_KV_RANK)
        q, ckv, vt4, iq, ik, sm, bmix, z, xbc, dtp = _inproj_call(
            x2, row1(pre_mix_norm[i]), row1(kv_norm[i]), row1(idx_k_norm_g[i]), row1(idx_k_norm_b[i]),
            *wparts, wuv_t, vone, s // KT)

        a_out = _attn_call(
            q, iq, sm, ckv.reshape(bsz, s, A_KV_RANK), vt4, ik.reshape(bsz, s, IDX_DIM),
            jnp.transpose(w_uk[i], (1, 0, 2)).astype(BF16),
            bias_tiles, tri, k_top=k_top, near=near)

        dt_t = dtp[:, :SSM_HEADS].reshape(bsz, s, SSM_HEADS).transpose(0, 2, 1)
        b_out, c_out = _mixer_call(
            bmix, z, xbc, dtp, dt_t, short_conv_w[i].astype(F32), ssm_conv_w[i].astype(F32),
            ssm_conv_b[i].astype(F32), ssm_dt_bias[i].astype(F32), ssm_a_log[i].astype(F32),
            jnp.repeat(ssm_d[i].astype(F32), SSM_HEAD_DIM), ssm_norm[i].astype(F32), bsz, s)

        wo = w_out[i].astype(BF16)
        na, nbw = A_HEADS * A_V_DIM, A_HEADS * A_V_DIM + B_WIDTH
        x2 = _dense_call(
            x2, a_out, b_out, c_out, p[i].reshape(t, PLE_DIM),
            row1(post_mix_norm[i]), row1(pre_ffn_norm[i]), row1(post_ffn_norm[i]),
            wo[:na], wo[na:nbw], wo[nbw:], w_ffn_gate[i].astype(BF16), w_ffn_up[i].astype(BF16),
            w_ffn_down[i].astype(BF16), w_ple_gate[i].astype(BF16), w_ple_proj[i].astype(BF16))
    return x2.reshape(bsz, s, d)
```

```python
import functools
import math

import numpy as np
import jax
import jax.numpy as jnp
from jax import lax
from jax.experimental import pallas as pl
from jax.experimental.pallas import tpu as pltpu

F32, BF16, I32 = jnp.float32, jnp.bfloat16, jnp.int32

D_MODEL = 1024
DEPTH = 2
CHUNK = 64
A_HEADS = 8
A_QK_DIM = 64
A_V_DIM = 64
A_KV_RANK = 256
IDX_HEADS = 4
IDX_DIM = 64
TOPK_MAX = 256
REL_BUCKETS = 32
REL_MAX_DIST = 1024
B_WIDTH = 512
SHORT_CONV = 3
SSM_HEADS = 16
SSM_HEAD_DIM = 64
SSM_INNER = SSM_HEADS * SSM_HEAD_DIM
SSM_GROUPS = 2
SSM_STATE = 128
SSM_CONV = 4
SSM_XBC = SSM_INNER + 2 * SSM_GROUPS * SSM_STATE
MIX_WIDTH = A_HEADS * A_V_DIM + B_WIDTH + SSM_INNER
D_FF = -(-8 * D_MODEL // (3 * 256)) * 256
PLE_DIM = 256
NORM_EPS = 1e-6
IN_SPLITS = (A_HEADS * A_QK_DIM, A_KV_RANK, IDX_HEADS * IDX_DIM, IDX_DIM, IDX_HEADS,
             B_WIDTH, B_WIDTH, B_WIDTH, SSM_INNER, SSM_XBC, SSM_HEADS)
IN_OFFSETS = tuple(int(v) for v in np.concatenate([[0], np.cumsum(IN_SPLITS)]))

LANES = 128
SUBLANES = 8
VMEM_LIMIT_BYTES = 56 * 1024 * 1024

QB = 2 * LANES
QGROUPS = QB // LANES
KT = 256
SSD_L = 128
IN_SUBTILES = 2
TM_SUB = 256
ROW_SUBTILES = 2
TM_OUT = TM_SUB * ROW_SUBTILES
SMALL_W = LANES
IW_OFF = IDX_DIM
INT_MIN = np.int32(-2 ** 31)
LOG2E = math.log2(math.e)
BF16_ROWS = 2 * SUBLANES
V_ROWS = A_V_DIM + BF16_ROWS

def _const_spec(shape):
    nd = len(shape)
    return pl.BlockSpec(shape, lambda *_: (0,) * nd, pipeline_mode=pl.Buffered(1))


def _params(sem):
    return pltpu.CompilerParams(dimension_semantics=sem, vmem_limit_bytes=VMEM_LIMIT_BYTES)


def _rms(x, g):
    return x * lax.rsqrt(jnp.mean(x * x, axis=-1, keepdims=True) + NORM_EPS) * g


def _sigmoid(x):
    return 1.0 / (1.0 + jnp.exp(-x))


def _split3(v):
    hi = v.astype(BF16)
    r1 = v - hi.astype(F32)
    mid = r1.astype(BF16)
    lo = (r1 - mid.astype(F32)).astype(BF16)
    return hi, mid, lo


def _dot(a, b):
    return jnp.dot(a, b, preferred_element_type=F32)


def _t5_bucket_np(rel):
    half = REL_BUCKETS // 2
    max_exact = half // 2
    ret = np.where(rel > 0, half, 0)
    n = np.abs(rel)
    nf = np.maximum(n, 1).astype(np.float32)
    large = max_exact + (np.log(nf / np.float32(max_exact)) / np.float32(math.log(REL_MAX_DIST / max_exact))
                         * np.float32(half - max_exact)).astype(np.int32)
    large = np.minimum(large, half - 1)
    return (ret + np.where(n < max_exact, n, large)).astype(np.int32)


@functools.lru_cache(maxsize=None)
def _bucket_tiles():
    kr = np.arange(KT)[:, None]
    qc = np.arange(QB)[None, :]
    tiles, near = [], None
    for o in range(64):
        t = _t5_bucket_np(kr - qc - o * QB)
        tiles.append(t)
    for o in range(63, -1, -1):
        if not (np.all(tiles[o] == tiles[63][0, 0])):
            near = o + 1
            break
    assert near is not None and 0 < near < 63
    return np.stack(tiles[:near + 1]), near, int(tiles[63][0, 0])


def _bias_kernel(rb_ref, bk_ref, out_ref, *, far_bucket):
    bk = bk_ref[0]
    for h in range(A_HEADS):
        acc = jnp.zeros(bk.shape, F32)
        for b in range(REL_BUCKETS):
            acc = jnp.where(bk == b, (rb_ref[b, h] - rb_ref[far_bucket, h]) * LOG2E, acc)
        out_ref[0, :, h * QB:(h + 1) * QB] = acc


def _bias_tiles_call(rel_bias, bucket_tiles, far_bucket):
    n = bucket_tiles.shape[0]
    return pl.pallas_call(
        functools.partial(_bias_kernel, far_bucket=far_bucket),
        grid=(n,),
        in_specs=[pl.BlockSpec(memory_space=pltpu.SMEM),
                  pl.BlockSpec((1, KT, QB), lambda o: (o, 0, 0))],
        out_specs=pl.BlockSpec((1, KT, A_HEADS * QB), lambda o: (o, 0, 0)),
        out_shape=jax.ShapeDtypeStruct((n, KT, A_HEADS * QB), F32),
        compiler_params=_params(("arbitrary",)),
        name="rel_bias_tiles",
    )(rel_bias, bucket_tiles)


def _wprep_kernel(w_ref, wq_out, wckv_out, wiq_out, wsm_out, wb_out, wz_out, wxbc_out, wdt_out):
    o_q, o_ckv, o_iq, o_ik, _, o_bg, _, _, o_z, o_xbc, o_dt, o_end = IN_OFFSETS
    wq_out[...] = w_ref[:, o_q:o_ckv].astype(BF16)
    wckv_out[...] = w_ref[:, o_ckv:o_iq].astype(BF16)
    wiq_out[...] = w_ref[:, o_iq:o_ik].astype(BF16)
    wsm_out[...] = w_ref[:, o_ik:o_ik + SMALL_W].astype(BF16)
    tail = w_ref[:, o_bg:o_end].astype(BF16)
    wb_out[...] = tail[:, :o_z - o_bg]
    wz_out[...] = tail[:, o_z - o_bg:o_xbc - o_bg]
    wxbc_out[...] = tail[:, o_xbc - o_bg:o_dt - o_bg]
    wdt_out[...] = jnp.zeros(wdt_out.shape, BF16)
    wdt_out[:, :o_end - o_dt] = tail[:, o_dt - o_bg:]


def _wprep_call(w):
    d, n = w.shape
    tr = 256
    widths = [IN_SPLITS[0], IN_SPLITS[1], IN_SPLITS[2], SMALL_W, 3 * B_WIDTH, SSM_INNER, SSM_XBC, SMALL_W]
    return pl.pallas_call(
        _wprep_kernel,
        grid=(d // tr,),
        in_specs=[pl.BlockSpec((tr, n), lambda r: (r, 0))],
        out_specs=[pl.BlockSpec((tr, wd), lambda r: (r, 0)) for wd in widths],
        out_shape=[jax.ShapeDtypeStruct((d, wd), BF16) for wd in widths],
        compiler_params=_params(("parallel",)),
        name="inproj_weight_prep",
    )(w)


def _dot_nt(a, b):
    return lax.dot_general(a, b, (((1,), (1,)), ((), ())), preferred_element_type=F32)


def _inproj_kernel(x_ref, g_ref, kvn_ref, ikg_ref, ikb_ref,
                   wqt_ref, wckv_ref, wiqt_ref, wsm_ref, wiwt_ref, wb_ref, wz_ref, wxbc_ref, wdt_ref,
                   wuv_ref, vone_ref,
                   qt_out, ckv_out, vt_out, iqt_out, iwt_out, ik_out, b_out, z_out, xbc_out, dt_out):
    subs = [slice(k * KT, (k + 1) * KT) for k in range(IN_SUBTILES)]
    h = [_rms(x_ref[rows, :], g_ref[...]).astype(BF16) for rows in subs]
    ckv = [_rms(_dot(hh, wckv_ref[...]), kvn_ref[...]).astype(BF16) for hh in h]
    sm = [_dot(hh, wsm_ref[...]) for hh in h]
    for k, rows in enumerate(subs):
        ckv_out[rows, :] = ckv[k]
        qt_out[k] = _dot_nt(wqt_ref[...], h[k]).astype(BF16)
        iqt_out[k] = (_dot_nt(wiqt_ref[...], h[k]) * (IDX_DIM ** -0.5)).astype(BF16)
        iwt_out[k] = _dot_nt(wiwt_ref[...], h[k]) * (IDX_HEADS ** -0.5)
    for k, rows in enumerate(subs):
        vt_out[0, k] = (_dot_nt(wuv_ref[...], ckv[k]) + vone_ref[...]).astype(BF16)
        ik = sm[k][:, :IDX_DIM]
        mu = jnp.mean(ik, axis=-1, keepdims=True)
        var = jnp.mean(jnp.square(ik - mu), axis=-1, keepdims=True)
        ik_out[rows, :] = ((ik - mu) * lax.rsqrt(var + NORM_EPS) * ikg_ref[...] + ikb_ref[...]).astype(BF16)
    for k, rows in enumerate(subs):
        b_out[rows, :] = _dot(h[k], wb_ref[...])
    for k, rows in enumerate(subs):
        z_out[rows, :] = _dot(h[k], wz_ref[...])
    for k, rows in enumerate(subs):
        xbc_out[rows, :] = _dot(h[k], wxbc_ref[...])
        dt_out[rows, :] = _dot(h[k], wdt_ref[...])


def _inproj_call(x2, g, kvn, ikg, ikb, wqt, wckv, wiqt, wsm, wiwt, wb, wz, wxbc, wdt, wuv, vone, n_kt):
    t = x2.shape[0]
    assert KT == QB and n_kt % IN_SUBTILES == 0
    tm = IN_SUBTILES * KT
    steps_per_seq = n_kt // IN_SUBTILES
    row = lambda w: pl.BlockSpec((tm, w), lambda r: (r, 0))
    rows = lambda w, dt: (row(w), jax.ShapeDtypeStruct((t, w), dt))
    qblk = lambda w, dt: (pl.BlockSpec((IN_SUBTILES, w, QB), lambda r: (r, 0, 0)),
                          jax.ShapeDtypeStruct((t // QB, w, QB), dt))
    hv = wuv.shape[0]
    outs = [qblk(wqt.shape[0], BF16), rows(wckv.shape[1], BF16),
            (pl.BlockSpec((1, IN_SUBTILES, hv, KT), lambda r: (r // steps_per_seq, r % steps_per_seq, 0, 0)),
             jax.ShapeDtypeStruct((t // (n_kt * KT), n_kt, hv, KT), BF16)),
            qblk(wiqt.shape[0], BF16), qblk(wiwt.shape[0], F32), rows(IDX_DIM, BF16),
            rows(wb.shape[1], F32), rows(wz.shape[1], F32), rows(wxbc.shape[1], F32),
            rows(SMALL_W, F32)]
    consts = [g, kvn, ikg, ikb, wqt, wckv, wiqt, wsm, wiwt, wb, wz, wxbc, wdt, wuv, vone]
    return pl.pallas_call(
        _inproj_kernel,
        grid=(t // tm,),
        in_specs=[row(D_MODEL)] + [_const_spec(c.shape) for c in consts],
        out_specs=[spec for spec, _ in outs],
        out_shape=[shape for _, shape in outs],
        compiler_params=_params(("parallel",)),
        name="inproj",
    )(x2, *consts)


def _bit_transpose32(words):
    a = list(words)
    m, j = 0x0000FFFF, 16
    while j:
        k = 0
        while k < 32:
            t = (a[k] ^ lax.shift_right_logical(a[k + j], np.int32(j))) & np.int32(m)
            a[k] = a[k] ^ t
            a[k + j] = a[k + j] ^ (t << j)
            k = (k + j + 1) & ~j
        j >>= 1
        m = (m ^ (m << j)) & 0xFFFFFFFF
    return a


def _attn_kernel(q_ref, iq_ref, iw_ref, ckv_ref, vt_ref, ik_ref, wuk_ref, bias_ref, tri_ref,
                 o_ref, keys_ref, planes_ref, qlat_ref, acc_ref, p0_ref, p1_ref, lg0_ref, lg1_ref,
                 *, k_top, near):
    i = pl.program_id(1)
    nt = ((i + 1) * QB + KT - 1) // KT
    lane = lax.broadcasted_iota(I32, (1, QB), 1)
    qchunk = (i * QB + lane) >> int(math.log2(CHUNK))
    krow = lax.broadcasted_iota(I32, (KT, QB), 0)

    q_t = q_ref[0]
    for h in range(A_HEADS):
        ql = _dot(wuk_ref[h], q_t[h * A_QK_DIM:(h + 1) * A_QK_DIM, :])
        qlat_ref[:, h * QB:(h + 1) * QB] = (ql * (A_QK_DIM ** -0.5 * LOG2E)).astype(BF16)

    iq_t = iq_ref[0]
    iw = iw_ref[0][:IDX_HEADS, :]

    def score_tile(j, carry, masked):
        r0 = pl.multiple_of(j * KT, KT)
        ikt = ik_ref[0, pl.ds(r0, KT), :]
        s = jnp.zeros((KT, QB), F32)
        for h in range(IDX_HEADS):
            raw = _dot(ikt, iq_t[h * IDX_DIM:(h + 1) * IDX_DIM, :])
            s = s + jnp.maximum(raw, 0.0) * iw[h:h + 1, :]
        if masked:
            adm = ((r0 + krow) >> int(math.log2(CHUNK))) <= qchunk
            s = jnp.where(adm, s, -jnp.inf)
        bits = lax.bitcast_convert_type(s, I32)
        bits = jnp.where(bits == INT_MIN, 0, bits)
        key = bits ^ ((bits >> 31) & np.int32(0x7FFFFFFF))
        keys_ref[pl.ds(r0, KT), :] = key
        ukey = key ^ INT_MIN
        planes = _bit_transpose32([ukey[v * SUBLANES:(v + 1) * SUBLANES, :] for v in range(32)])
        p0 = pl.multiple_of(j * SUBLANES, SUBLANES)
        for b in range(32):
            planes_ref[b, pl.ds(p0, SUBLANES), :] = planes[31 - b]
        return carry

    @pl.when(i == 0)
    def _():
        planes_ref[...] = jnp.zeros(planes_ref.shape, I32)

    def score_pair(u, carry):
        score_tile(2 * u, 0, masked=False)
        score_tile(2 * u + 1, 0, masked=False)
        return carry

    lax.fori_loop(0, (nt - 1) // 2, score_pair, 0)

    @pl.when((nt - 1) % 2 == 1)
    def _():
        score_tile(nt - 2, 0, masked=False)

    score_tile(nt - 1, 0, masked=True)

    n_prow = planes_ref.shape[1]
    prow = lax.broadcasted_iota(I32, (n_prow, QB), 0)
    live = jnp.where(prow < nt * SUBLANES, np.int32(-1), np.int32(0))
    kf = float(k_top)
    c_gt = jnp.zeros((1, QB), F32)
    thr_u = jnp.zeros((1, QB), I32)
    for b in range(31, -1, -1):
        x = live & planes_ref[b]
        pc = lax.population_count(x)
        parts = [pc[r * SUBLANES:(r + 1) * SUBLANES, :] for r in range(n_prow // SUBLANES)]
        while len(parts) > 1:
            parts = [parts[k] + parts[k + 1] for k in range(0, len(parts) - 1, 2)] + (
                [parts[-1]] if len(parts) % 2 else [])
        cnt = c_gt + jnp.sum(parts[0].astype(F32), axis=0, keepdims=True)
        take = cnt >= kf
        live = jnp.where(take, x, live ^ x)
        c_gt = jnp.where(take, c_gt, cnt)
        thr_u = thr_u | jnp.where(take, np.int32(-2 ** 31 if b == 31 else 2 ** b), np.int32(0))
    thr = thr_u ^ INT_MIN
    need = kf - c_gt

    p_refs, lg_refs = (p0_ref, p1_ref), (lg0_ref, lg1_ref)
    n_cols = A_HEADS * QGROUPS
    acc_ref[...] = jnp.zeros(acc_ref.shape, F32)
    p1_ref[...] = jnp.zeros(p1_ref.shape, BF16)

    def stage_scores(j, cnt_eq, slot, far=False):
        r0 = pl.multiple_of(j * KT, KT)
        kt = keys_ref[pl.ds(r0, KT), :]
        eq = kt == thr
        eqf = jnp.where(eq, 1.0, 0.0)
        rank = _dot(tri_ref[...], eqf.astype(BF16)) + cnt_eq
        sel = (kt > thr) | (eq & (rank <= need))
        if not far:
            sel = sel & (((r0 + krow) >> int(math.log2(CHUNK))) <= qchunk)
            bias = bias_ref[jnp.minimum(i - j * (KT // QB), near)]
        logits = _dot(ckv_ref[0, pl.ds(r0, KT), :], qlat_ref[...])
        tmax = []
        for c in range(n_cols):
            sl = slice(c * LANES, (c + 1) * LANES)
            qs = slice((c % QGROUPS) * LANES, (c % QGROUPS + 1) * LANES)
            lh = logits[:, sl] if far else logits[:, sl] + bias[:, sl]
            lh = jnp.where(sel[:, qs], lh, -jnp.inf)
            lg_refs[slot][:, sl] = lh
            tmax.append(jnp.max(lh, axis=0, keepdims=True))
        return tuple(tmax), cnt_eq + jnp.sum(eqf, axis=0, keepdims=True)

    def stage_probs(ms, tmax, slot):
        new_ms, alphas = [], []
        for c in range(n_cols):
            sl = slice(c * LANES, (c + 1) * LANES)
            m_new = jnp.maximum(ms[c], tmax[c])
            m_safe = jnp.where(m_new == -jnp.inf, 0.0, m_new)
            p_refs[slot][:, sl] = jnp.exp2(lg_refs[slot][:, sl] - m_safe).astype(BF16)
            alphas.append(jnp.exp2(ms[c] - m_safe))
            new_ms.append(m_new)
        return tuple(new_ms), tuple(alphas)

    def stage_values(alphas, jp, slot):
        for h in range(A_HEADS):
            vs = slice(h * V_ROWS, (h + 1) * V_ROWS)
            pv = _dot(vt_ref[0, jp, vs, :], p_refs[slot][:, h * QB:(h + 1) * QB])
            alpha = jnp.concatenate(alphas[h * QGROUPS:(h + 1) * QGROUPS], axis=1)
            acc_ref[vs, :] = acc_ref[vs, :] * alpha + pv

    def trip(t, carry, slot, far=False):
        ms, tmax, cnt_eq, alphas = carry
        stage_values(alphas, jnp.maximum(t - 1, 0), 1 - slot)
        ms, alphas = stage_probs(ms, tmax, slot)
        tmax, cnt_eq = stage_scores(t + 1, cnt_eq, 1 - slot, far)
        return ms, tmax, cnt_eq, alphas

    def finish(t, carry, slot):
        ms, tmax, _, alphas = carry
        stage_values(alphas, jnp.maximum(t - 1, 0), 1 - slot)
        _, alphas = stage_probs(ms, tmax, slot)
        stage_values(alphas, t, slot)

    def pair(u, carry, far=False):
        return trip(2 * u + 1, trip(2 * u, carry, 0, far), 1, far)

    tmax0, cnt_eq0 = stage_scores(0, jnp.zeros((1, QB), F32), 0)
    init = (tuple(jnp.full((1, LANES), -jnp.inf, F32) for _ in range(n_cols)), tmax0, cnt_eq0,
            tuple(jnp.ones((1, LANES), F32) for _ in range(n_cols)))
    n_trips = nt - 1
    n_far_pairs = jnp.maximum(i * (QB // KT) - near, 0) // 2
    carry = lax.fori_loop(0, n_far_pairs, functools.partial(pair, far=True), init)
    carry = lax.fori_loop(n_far_pairs, n_trips // 2, pair, carry)

    @pl.when(n_trips % 2 == 0)
    def _():
        finish(nt - 1, carry, 0)

    @pl.when(n_trips % 2 == 1)
    def _():
        finish(nt - 1, trip(nt - 2, carry, 0), 1)

    a_t = jnp.concatenate(
        [acc_ref[h * V_ROWS:h * V_ROWS + A_V_DIM, :]
         * (1.0 / acc_ref[h * V_ROWS + A_V_DIM:h * V_ROWS + A_V_DIM + 1, :]) for h in range(A_HEADS)],
        axis=0)
    o_ref[...] = a_t.T.astype(BF16)


def _attn_call(q_t, iq_t, iw_t, ckv3, vt4, ik3, wuk, bias_tiles, tri, *, k_top, near):
    b, s = ckv3.shape[0], ckv3.shape[1]
    nb = s // QB
    n_kt = s // KT
    hv = A_HEADS * A_V_DIM
    hvr = A_HEADS * V_ROWS
    kern = functools.partial(_attn_kernel, k_top=k_top, near=near)
    row = lambda w: pl.BlockSpec((QB, w), lambda bb, i: (bb * nb + i, 0))
    qblk = lambda a: pl.BlockSpec((1, a.shape[1], QB), lambda bb, i: (bb * nb + i, 0, 0))
    return pl.pallas_call(
        kern,
        grid=(b, nb),
        in_specs=[
            qblk(q_t), qblk(iq_t), qblk(iw_t),
            pl.BlockSpec((1, s, A_KV_RANK), lambda bb, i: (bb, 0, 0)),
            pl.BlockSpec((1, n_kt, hvr, KT), lambda bb, i: (bb, 0, 0, 0)),
            pl.BlockSpec((1, s, IDX_DIM), lambda bb, i: (bb, 0, 0)),
            _const_spec(wuk.shape), _const_spec(bias_tiles.shape), _const_spec(tri.shape),
        ],
        out_specs=row(hv),
        out_shape=jax.ShapeDtypeStruct((b * s, hv), BF16),
        scratch_shapes=[pltpu.VMEM((s, QB), I32), pltpu.VMEM((32, s // 32, QB), I32),
                        pltpu.VMEM((A_KV_RANK, A_HEADS * QB), BF16),
                        pltpu.VMEM((hvr, QB), F32),
                        pltpu.VMEM((KT, A_HEADS * QB), BF16), pltpu.VMEM((KT, A_HEADS * QB), BF16),
                        pltpu.VMEM((KT, A_HEADS * QB), F32), pltpu.VMEM((KT, A_HEADS * QB), F32)],
        compiler_params=_params(("parallel", "arbitrary")),
        name="dsa_attention",
    )(q_t, iq_t, iw_t, ckv3, vt4, ik3, wuk, bias_tiles, tri)


def _softplus(x):
    return jnp.maximum(x, 0.0) + jnp.log1p(jnp.exp(-jnp.abs(x)))


def _mixer_kernel(bm_ref, z_ref, xbc_ref, dtp_ref, dtt_ref, scw_ref, cw_ref, cb_ref,
                  dtb_r_ref, dtb_c_ref, alog_r_ref, alog_c_ref, dexp_ref, nw_ref,
                  tril_ref, triu_ref, eh_ref,
                  bo_ref, co_ref, ubuf, xbuf, state):
    L = SSD_L
    P = SUBLANES
    gw = SSM_INNER // SSM_GROUPS

    @pl.when(pl.program_id(1) == 0)
    def _():
        ubuf[0:P, :] = jnp.zeros((P, B_WIDTH), F32)
        xbuf[0:P, :] = jnp.zeros((P, SSM_XBC), F32)
        state[...] = jnp.zeros(state.shape, F32)

    bm = bm_ref[...]
    u = bm[:, B_WIDTH:2 * B_WIDTH] * bm[:, 2 * B_WIDTH:3 * B_WIDTH]
    ubuf[P:P + L, :] = u
    conv = scw_ref[SHORT_CONV - 1:SHORT_CONV, :] * u
    for j in range(SHORT_CONV - 1):
        off = P - (SHORT_CONV - 1) + j
        conv = conv + scw_ref[j:j + 1, :] * ubuf[off:off + L, :]
    bo_ref[...] = (bm[:, :B_WIDTH] * conv).astype(BF16)
    ubuf[0:P, :] = u[L - P:L, :]

    xr = xbc_ref[...]
    xbuf[P:P + L, :] = xr
    xc = cw_ref[SSM_CONV - 1:SSM_CONV, :] * xr + cb_ref[...]
    for j in range(SSM_CONV - 1):
        off = P - (SSM_CONV - 1) + j
        xc = xc + cw_ref[j:j + 1, :] * xbuf[off:off + L, :]
    xbuf[0:P, :] = xr[L - P:L, :]
    xa = xc * _sigmoid(xc)
    xs = xa[:, :SSM_INNER]

    dt = _softplus(dtp_ref[:, :SSM_HEADS] + dtb_r_ref[...])
    a = dt * (-jnp.exp(alog_r_ref[...]))
    a_cs = sum(_dot(tril_ref[...], part) for part in _split3(a))
    dt_t = _softplus(dtt_ref[0] + dtb_c_ref[...])
    a_t = dt_t * (-jnp.exp(alog_c_ref[...]))
    a_cs_t = sum(_dot(part, triu_ref[...]) for part in _split3(a_t))

    def expand(v):
        return sum(_dot(part, eh_ref[...]) for part in _split3(v))

    xdt = xs * expand(dt)
    e_cs = expand(jnp.exp(a_cs))
    dec = expand(jnp.exp(a_cs[L - 1:L, :] - a_cs))
    chunk_dec = e_cs[L - 1:L, :]

    row = lax.broadcasted_iota(I32, (L, L), 0)
    col = lax.broadcasted_iota(I32, (L, L), 1)
    causal = row >= col
    lo_half = lax.broadcasted_iota(I32, (L, LANES), 1) < SSM_HEAD_DIM
    hpg = SSM_HEADS // SSM_GROUPS

    y_parts = []
    for g in range(SSM_GROUPS):
        bg = xa[:, SSM_INNER + g * SSM_STATE:SSM_INNER + (g + 1) * SSM_STATE].astype(BF16)
        cg = xa[:, SSM_INNER + (SSM_GROUPS + g) * SSM_STATE:
                SSM_INNER + (SSM_GROUPS + g + 1) * SSM_STATE].astype(BF16)
        cb = lax.dot_general(cg, bg, (((1,), (1,)), ((), ())), preferred_element_type=F32)
        sl = slice(g * gw, (g + 1) * gw)
        st = state[g]
        y_off = _dot(cg, st.astype(BF16)) * e_cs[:, sl]
        for k in range(hpg // 2):
            xp = xdt[:, g * gw + k * LANES:g * gw + (k + 1) * LANES]
            halves = (jnp.where(lo_half, xp, 0.0).astype(BF16), jnp.where(lo_half, 0.0, xp).astype(BF16))
            y = y_off[:, k * LANES:(k + 1) * LANES]
            for hh in range(2):
                h = g * hpg + 2 * k + hh
                seg = a_cs[:, h:h + 1] - a_cs_t[h:h + 1, :]
                m = (cb * jnp.exp(jnp.where(causal, seg, -jnp.inf))).astype(BF16)
                y = y + _dot(m, halves[hh])
            y_parts.append(y)
        xw = (dec[:, sl] * xdt[:, sl]).astype(BF16)
        upd = lax.dot_general(bg, xw, (((0,), (0,)), ((), ())), preferred_element_type=F32)
        state[g] = chunk_dec[:, sl] * st + upd

    y = jnp.concatenate(y_parts, axis=1) + dexp_ref[...] * xs
    zz = z_ref[...]
    y = y * (zz * _sigmoid(zz))
    outs = []
    for g in range(SSM_GROUPS):
        sl = slice(g * gw, (g + 1) * gw)
        outs.append(_rms(y[:, sl], nw_ref[:, sl]))
    co_ref[...] = jnp.concatenate(outs, axis=1).astype(BF16)


def _mixer_call(bmix, z, xbc, sm, dt_t, scw, cw, cb, dtb, alog, dexp, nw, b, s):
    L = SSD_L
    nc = s // L
    tril = jnp.asarray(np.tril(np.ones((L, L), np.float32)), BF16)
    triu = jnp.asarray(np.triu(np.ones((L, L), np.float32)), BF16)
    eh = jnp.asarray(np.repeat(np.eye(SSM_HEADS, dtype=np.float32), SSM_HEAD_DIM, axis=1), BF16)
    row = lambda w: pl.BlockSpec((L, w), lambda bb, c: (bb * nc + c, 0))
    consts = [scw, cw, cb.reshape(1, -1), dtb.reshape(1, -1), dtb.reshape(-1, 1),
              alog.reshape(1, -1), alog.reshape(-1, 1), dexp.reshape(1, -1), nw.reshape(1, -1),
              tril, triu, eh]
    return pl.pallas_call(
        _mixer_kernel,
        grid=(b, nc),
        in_specs=[row(3 * B_WIDTH), row(SSM_INNER), row(SSM_XBC), row(SMALL_W),
                  pl.BlockSpec((1, SSM_HEADS, L), lambda bb, c: (bb, 0, c))]
                 + [_const_spec(c.shape) for c in consts],
        out_specs=[row(B_WIDTH), row(SSM_INNER)],
        out_shape=[jax.ShapeDtypeStruct((b * s, B_WIDTH), BF16),
                   jax.ShapeDtypeStruct((b * s, SSM_INNER), BF16)],
        scratch_shapes=[pltpu.VMEM((L + SUBLANES, B_WIDTH), F32),
                        pltpu.VMEM((L + SUBLANES, SSM_XBC), F32),
                        pltpu.VMEM((SSM_GROUPS, SSM_STATE, SSM_INNER // SSM_GROUPS), F32)],
        compiler_params=_params(("parallel", "arbitrary")),
        name="conv_ssd_mixers",
    )(bmix, z, xbc, sm, dt_t, *consts)


def _dense_kernel(x_ref, a_ref, b_ref, c_ref, p_ref, g_post_ref, g_pre_ref, g_fpost_ref,
                  woa_ref, wob_ref, woc_ref, wg_ref, wu_ref, wd_ref, wpg_ref, wpp_ref, o_ref):
    subs = [slice(r * TM_SUB, (r + 1) * TM_SUB) for r in range(ROW_SUBTILES)]
    mix = [_dot(a_ref[rows, :], woa_ref[...]) + _dot(b_ref[rows, :], wob_ref[...])
           + _dot(c_ref[rows, :], woc_ref[...]) for rows in subs]
    ple = [_dot(p_ref[rows, :].astype(BF16), wpp_ref[...]) for rows in subs]
    x = [x_ref[rows, :] + _rms(m, g_post_ref[...]) for rows, m in zip(subs, mix)]
    h = [_rms(xx, g_pre_ref[...]).astype(BF16) for xx in x]
    gate = [_dot(hh, wg_ref[...]) for hh in h]
    up = [_dot(hh, wu_ref[...]) for hh in h]
    act = [(g * _sigmoid(g) * u).astype(BF16) for g, u in zip(gate, up)]
    f = [_dot(aa, wd_ref[...]) for aa in act]
    x = [xx + _rms(ff, g_fpost_ref[...]) for xx, ff in zip(x, f)]
    pg = [_sigmoid(_dot(xx.astype(BF16), wpg_ref[...])) for xx in x]
    for rows, xx, g, e in zip(subs, x, pg, ple):
        o_ref[rows, :] = xx + g * e


def _dense_call(x2, a, b, c, p2, g_post, g_pre, g_fpost, woa, wob, woc, wg, wu, wd, wpg, wpp):
    t = x2.shape[0]
    tm = TM_OUT
    row = lambda w: pl.BlockSpec((tm, w), lambda r: (r, 0))
    consts = [g_post, g_pre, g_fpost, woa, wob, woc, wg, wu, wd, wpg, wpp]
    return pl.pallas_call(
        _dense_kernel,
        grid=(t // tm,),
        in_specs=[row(D_MODEL), row(a.shape[1]), row(b.shape[1]), row(c.shape[1]), row(PLE_DIM)]
                 + [_const_spec(w.shape) for w in consts],
        out_specs=row(D_MODEL),
        out_shape=jax.ShapeDtypeStruct((t, D_MODEL), F32),
        compiler_params=_params(("parallel",)),
        name="outproj_ffn_ple",
    )(x2, a, b, c, p2, *consts)


def kernel(x, p, pre_mix_norm, post_mix_norm, pre_ffn_norm, post_ffn_norm, w_in, kv_norm, idx_k_norm_g, idx_k_norm_b, w_uk, w_uv, rel_bias, short_conv_w, ssm_conv_w, ssm_conv_b, ssm_dt_bias, ssm_a_log, ssm_d, ssm_norm, w_out, w_ffn_gate, w_ffn_up, w_ffn_down, w_ple_proj, w_ple_gate):
    bsz, s, d = x.shape
    assert d == D_MODEL and s % (IN_SUBTILES * KT) == 0 and s % QB == 0 and s % SSD_L == 0
    assert (bsz * s) % TM_OUT == 0
    t = bsz * s
    k_top = min(TOPK_MAX, s // 4)

    bucket_np, near, far_bucket = _bucket_tiles()
    bias_tiles = _bias_tiles_call(rel_bias.astype(F32), jnp.asarray(bucket_np), far_bucket)
    tri = jnp.asarray(np.tril(np.ones((KT, KT), np.float32)), BF16)
    row1 = lambda v: v.reshape(1, -1).astype(F32)
    vone_np = np.zeros((A_HEADS, V_ROWS, 1), np.float32)
    vone_np[:, A_V_DIM] = 1.0
    vone = jnp.asarray(vone_np.reshape(A_HEADS * V_ROWS, 1))

    x2 = x.reshape(t, d)
    for i in range(DEPTH):
        wparts = _wprep_call(w_in[i].astype(F32))
        wuv_t = jnp.transpose(w_uv[i], (1, 2, 0)).astype(BF16)
        wuv_t = jnp.pad(wuv_t, ((0, 0), (0, V_ROWS - A_V_DIM), (0, 0))).reshape(A_HEADS * V_ROWS, A_KV_RANK)
        wq, wckv, wiq, wsm, wb, wz, wxbc, wdt = wparts
        wiw_t = wsm[:, IW_OFF:IW_OFF + SUBLANES].T
        q_t, ckv, vt4, iq_t, iw_t, ik, bmix, z, xbc, dtp = _inproj_call(
            x2, row1(pre_mix_norm[i]), row1(kv_norm[i]), row1(idx_k_norm_g[i]), row1(idx_k_norm_b[i]),
            wq.T, wckv, wiq.T, wsm, wiw_t, wb, wz, wxbc, wdt, wuv_t, vone, s // KT)

        a_out = _attn_call(
            q_t, iq_t, iw_t, ckv.reshape(bsz, s, A_KV_RANK), vt4, ik.reshape(bsz, s, IDX_DIM),
            jnp.transpose(w_uk[i], (1, 0, 2)).astype(BF16),
            bias_tiles, tri, k_top=k_top, near=near)

        dt_t = dtp[:, :SSM_HEADS].reshape(bsz, s, SSM_HEADS).transpose(0, 2, 1)
        b_out, c_out = _mixer_call(
            bmix, z, xbc, dtp, dt_t, short_conv_w[i].astype(F32), ssm_conv_w[i].astype(F32),
            ssm_conv_b[i].astype(F32), ssm_dt_bias[i].astype(F32), ssm_a_log[i].astype(F32),
            jnp.repeat(ssm_d[i].astype(F32), SSM_HEAD_DIM), ssm_norm[i].astype(F32), bsz, s)

        wo = w_out[i].astype(BF16)
        na, nbw = A_HEADS * A_V_DIM, A_HEADS * A_V_DIM + B_WIDTH
        x2 = _dense_call(
            x2, a_out, b_out, c_out, p[i].reshape(t, PLE_DIM),
            row1(post_mix_norm[i]), row1(pre_ffn_norm[i]), row1(post_ffn_norm[i]),
            wo[:na], wo[na:nbw], wo[nbw:], w_ffn_gate[i].astype(BF16), w_ffn_up[i].astype(BF16),
            w_ffn_down[i].astype(BF16), w_ple_gate[i].astype(BF16), w_ple_proj[i].astype(BF16))
    return x2.reshape(bsz, s, d)
```

```python
import functools
import math

import numpy as np
import jax
import jax.numpy as jnp
from jax import lax
from jax.experimental import pallas as pl
from jax.experimental.pallas import tpu as pltpu

F32, BF16, I32 = jnp.float32, jnp.bfloat16, jnp.int32

D_MODEL = 1024
DEPTH = 2
CHUNK = 64
A_HEADS = 8
A_QK_DIM = 64
A_V_DIM = 64
A_KV_RANK = 256
IDX_HEADS = 4
IDX_DIM = 64
TOPK_MAX = 256
REL_BUCKETS = 32
REL_MAX_DIST = 1024
B_WIDTH = 512
SHORT_CONV = 3
SSM_HEADS = 16
SSM_HEAD_DIM = 64
SSM_INNER = SSM_HEADS * SSM_HEAD_DIM
SSM_GROUPS = 2
SSM_STATE = 128
SSM_CONV = 4
SSM_XBC = SSM_INNER + 2 * SSM_GROUPS * SSM_STATE
MIX_WIDTH = A_HEADS * A_V_DIM + B_WIDTH + SSM_INNER
D_FF = -(-8 * D_MODEL // (3 * 256)) * 256
PLE_DIM = 256
NORM_EPS = 1e-6
IN_SPLITS = (A_HEADS * A_QK_DIM, A_KV_RANK, IDX_HEADS * IDX_DIM, IDX_DIM, IDX_HEADS,
             B_WIDTH, B_WIDTH, B_WIDTH, SSM_INNER, SSM_XBC, SSM_HEADS)
IN_OFFSETS = tuple(int(v) for v in np.concatenate([[0], np.cumsum(IN_SPLITS)]))

LANES = 128
SUBLANES = 8
VMEM_LIMIT_BYTES = 56 * 1024 * 1024

QB = 2 * LANES
QGROUPS = QB // LANES
KT = 256
SSD_L = 128
IN_SUBTILES = 2
TM_SUB = 256
ROW_SUBTILES = 2
TM_OUT = TM_SUB * ROW_SUBTILES
SMALL_W = LANES
IW_OFF = IDX_DIM
INT_MIN = np.int32(-2 ** 31)
LOG2E = math.log2(math.e)
BF16_ROWS = 2 * SUBLANES
V_ROWS = A_V_DIM + BF16_ROWS

def _const_spec(shape):
    nd = len(shape)
    return pl.BlockSpec(shape, lambda *_: (0,) * nd, pipeline_mode=pl.Buffered(1))


def _params(sem):
    return pltpu.CompilerParams(dimension_semantics=sem, vmem_limit_bytes=VMEM_LIMIT_BYTES)


def _rms(x, g):
    return x * lax.rsqrt(jnp.mean(x * x, axis=-1, keepdims=True) + NORM_EPS) * g


def _sigmoid(x):
    return 1.0 / (1.0 + jnp.exp(-x))


def _split3(v):
    hi = v.astype(BF16)
    r1 = v - hi.astype(F32)
    mid = r1.astype(BF16)
    lo = (r1 - mid.astype(F32)).astype(BF16)
    return hi, mid, lo


def _dot(a, b):
    return jnp.dot(a, b, preferred_element_type=F32)


def _t5_bucket_np(rel):
    half = REL_BUCKETS // 2
    max_exact = half // 2
    ret = np.where(rel > 0, half, 0)
    n = np.abs(rel)
    nf = np.maximum(n, 1).astype(np.float32)
    large = max_exact + (np.log(nf / np.float32(max_exact)) / np.float32(math.log(REL_MAX_DIST / max_exact))
                         * np.float32(half - max_exact)).astype(np.int32)
    large = np.minimum(large, half - 1)
    return (ret + np.where(n < max_exact, n, large)).astype(np.int32)


@functools.lru_cache(maxsize=None)
def _bucket_tiles():
    kr = np.arange(KT)[:, None]
    qc = np.arange(QB)[None, :]
    tiles, near = [], None
    for o in range(64):
        t = _t5_bucket_np(kr - qc - o * QB)
        tiles.append(t)
    for o in range(63, -1, -1):
        if not (np.all(tiles[o] == tiles[63][0, 0])):
            near = o + 1
            break
    assert near is not None and 0 < near < 63
    return np.stack(tiles[:near + 1]), near, int(tiles[63][0, 0])


def _bias_kernel(rb_ref, bk_ref, out_ref, *, far_bucket):
    bk = bk_ref[0]
    for h in range(A_HEADS):
        acc = jnp.zeros(bk.shape, F32)
        for b in range(REL_BUCKETS):
            acc = jnp.where(bk == b, (rb_ref[b, h] - rb_ref[far_bucket, h]) * LOG2E, acc)
        out_ref[0, :, h * QB:(h + 1) * QB] = acc


def _bias_tiles_call(rel_bias, bucket_tiles, far_bucket):
    n = bucket_tiles.shape[0]
    return pl.pallas_call(
        functools.partial(_bias_kernel, far_bucket=far_bucket),
        grid=(n,),
        in_specs=[pl.BlockSpec(memory_space=pltpu.SMEM),
                  pl.BlockSpec((1, KT, QB), lambda o: (o, 0, 0))],
        out_specs=pl.BlockSpec((1, KT, A_HEADS * QB), lambda o: (o, 0, 0)),
        out_shape=jax.ShapeDtypeStruct((n, KT, A_HEADS * QB), F32),
        compiler_params=_params(("arbitrary",)),
        name="rel_bias_tiles",
    )(rel_bias, bucket_tiles)


def _wprep_kernel(w_ref, wq_out, wckv_out, wiq_out, wsm_out, wb_out, wz_out, wxbc_out, wdt_out):
    o_q, o_ckv, o_iq, o_ik, _, o_bg, _, _, o_z, o_xbc, o_dt, o_end = IN_OFFSETS
    wq_out[...] = w_ref[:, o_q:o_ckv].astype(BF16)
    wckv_out[...] = w_ref[:, o_ckv:o_iq].astype(BF16)
    wiq_out[...] = w_ref[:, o_iq:o_ik].astype(BF16)
    wsm_out[...] = w_ref[:, o_ik:o_ik + SMALL_W].astype(BF16)
    tail = w_ref[:, o_bg:o_end].astype(BF16)
    wb_out[...] = tail[:, :o_z - o_bg]
    wz_out[...] = tail[:, o_z - o_bg:o_xbc - o_bg]
    wxbc_out[...] = tail[:, o_xbc - o_bg:o_dt - o_bg]
    wdt_out[...] = jnp.zeros(wdt_out.shape, BF16)
    wdt_out[:, :o_end - o_dt] = tail[:, o_dt - o_bg:]


def _wprep_call(w):
    d, n = w.shape
    tr = 256
    widths = [IN_SPLITS[0], IN_SPLITS[1], IN_SPLITS[2], SMALL_W, 3 * B_WIDTH, SSM_INNER, SSM_XBC, SMALL_W]
    return pl.pallas_call(
        _wprep_kernel,
        grid=(d // tr,),
        in_specs=[pl.BlockSpec((tr, n), lambda r: (r, 0))],
        out_specs=[pl.BlockSpec((tr, wd), lambda r: (r, 0)) for wd in widths],
        out_shape=[jax.ShapeDtypeStruct((d, wd), BF16) for wd in widths],
        compiler_params=_params(("parallel",)),
        name="inproj_weight_prep",
    )(w)


def _dot_nt(a, b):
    return lax.dot_general(a, b, (((1,), (1,)), ((), ())), preferred_element_type=F32)


def _inproj_kernel(x_ref, g_ref, kvn_ref, ikg_ref, ikb_ref,
                   wqt_ref, wckv_ref, wiqt_ref, wsm_ref, wiwt_ref, wb_ref, wz_ref, wxbc_ref, wdt_ref,
                   wuv_ref, vone_ref,
                   qt_out, ckv_out, vt_out, iqt_out, iwt_out, ik_out, b_out, z_out, xbc_out, dt_out):
    subs = [slice(k * KT, (k + 1) * KT) for k in range(IN_SUBTILES)]
    h = [_rms(x_ref[rows, :], g_ref[...]).astype(BF16) for rows in subs]
    ckv = [_rms(_dot(hh, wckv_ref[...]), kvn_ref[...]).astype(BF16) for hh in h]
    sm = [_dot(hh, wsm_ref[...]) for hh in h]
    for k, rows in enumerate(subs):
        ckv_out[rows, :] = ckv[k]
        qt_out[k] = _dot_nt(wqt_ref[...], h[k]).astype(BF16)
        iqt_out[k] = (_dot_nt(wiqt_ref[...], h[k]) * (IDX_DIM ** -0.5)).astype(BF16)
        iwt_out[k] = _dot_nt(wiwt_ref[...], h[k]) * (IDX_HEADS ** -0.5)
    for k, rows in enumerate(subs):
        vt_out[0, k] = (_dot_nt(wuv_ref[...], ckv[k]) + vone_ref[...]).astype(BF16)
        ik = sm[k][:, :IDX_DIM]
        mu = jnp.mean(ik, axis=-1, keepdims=True)
        var = jnp.mean(jnp.square(ik - mu), axis=-1, keepdims=True)
        ik_out[rows, :] = ((ik - mu) * lax.rsqrt(var + NORM_EPS) * ikg_ref[...] + ikb_ref[...]).astype(BF16)
    for k, rows in enumerate(subs):
        b_out[rows, :] = _dot(h[k], wb_ref[...])
    for k, rows in enumerate(subs):
        z_out[rows, :] = _dot(h[k], wz_ref[...])
    for k, rows in enumerate(subs):
        xbc_out[rows, :] = _dot(h[k], wxbc_ref[...])
        dt_out[rows, :] = _dot(h[k], wdt_ref[...])


def _inproj_call(x2, g, kvn, ikg, ikb, wqt, wckv, wiqt, wsm, wiwt, wb, wz, wxbc, wdt, wuv, vone, n_kt):
    t = x2.shape[0]
    assert KT == QB and n_kt % IN_SUBTILES == 0
    tm = IN_SUBTILES * KT
    steps_per_seq = n_kt // IN_SUBTILES
    row = lambda w: pl.BlockSpec((tm, w), lambda r: (r, 0))
    rows = lambda w, dt: (row(w), jax.ShapeDtypeStruct((t, w), dt))
    qblk = lambda w, dt: (pl.BlockSpec((IN_SUBTILES, w, QB), lambda r: (r, 0, 0)),
                          jax.ShapeDtypeStruct((t // QB, w, QB), dt))
    hv = wuv.shape[0]
    outs = [qblk(wqt.shape[0], BF16), rows(wckv.shape[1], BF16),
            (pl.BlockSpec((1, IN_SUBTILES, hv, KT), lambda r: (r // steps_per_seq, r % steps_per_seq, 0, 0)),
             jax.ShapeDtypeStruct((t // (n_kt * KT), n_kt, hv, KT), BF16)),
            qblk(wiqt.shape[0], BF16), qblk(wiwt.shape[0], F32), rows(IDX_DIM, BF16),
            rows(wb.shape[1], F32), rows(wz.shape[1], F32), rows(wxbc.shape[1], F32),
            rows(SMALL_W, F32)]
    consts = [g, kvn, ikg, ikb, wqt, wckv, wiqt, wsm, wiwt, wb, wz, wxbc, wdt, wuv, vone]
    return pl.pallas_call(
        _inproj_kernel,
        grid=(t // tm,),
        in_specs=[row(D_MODEL)] + [_const_spec(c.shape) for c in consts],
        out_specs=[spec for spec, _ in outs],
        out_shape=[shape for _, shape in outs],
        compiler_params=_params(("parallel",)),
        name="inproj",
    )(x2, *consts)


def _bit_transpose32(words):
    a = list(words)
    m, j = 0x0000FFFF, 16
    while j:
        k = 0
        while k < 32:
            t = (a[k] ^ lax.shift_right_logical(a[k + j], np.int32(j))) & np.int32(m)
            a[k] = a[k] ^ t
            a[k + j] = a[k + j] ^ (t << j)
            k = (k + j + 1) & ~j
        j >>= 1
        m = (m ^ (m << j)) & 0xFFFFFFFF
    return a


def _attn_kernel(q_ref, iq_ref, iw_ref, ckv_ref, vt_ref, ik_ref, wuk_ref, bias_ref, tri_ref,
                 o_ref, keys_ref, planes_ref, qlat_ref, acc_ref, p0_ref, p1_ref, lg0_ref, lg1_ref,
                 *, k_top, near):
    i = pl.program_id(1)
    nt = ((i + 1) * QB + KT - 1) // KT
    lane = lax.broadcasted_iota(I32, (1, QB), 1)
    qchunk = (i * QB + lane) >> int(math.log2(CHUNK))
    krow = lax.broadcasted_iota(I32, (KT, QB), 0)

    q_t = q_ref[0]
    for h in range(A_HEADS):
        ql = _dot(wuk_ref[h], q_t[h * A_QK_DIM:(h + 1) * A_QK_DIM, :])
        qlat_ref[:, h * QB:(h + 1) * QB] = (ql * (A_QK_DIM ** -0.5 * LOG2E)).astype(BF16)

    iq_t = iq_ref[0]
    iw = iw_ref[0][:IDX_HEADS, :]

    def score_tile(j, carry, masked):
        r0 = pl.multiple_of(j * KT, KT)
        ikt = ik_ref[0, pl.ds(r0, KT), :]
        s = jnp.zeros((KT, QB), F32)
        for h in range(IDX_HEADS):
            raw = _dot(ikt, iq_t[h * IDX_DIM:(h + 1) * IDX_DIM, :])
            s = s + jnp.maximum(raw, 0.0) * iw[h:h + 1, :]
        if masked:
            adm = ((r0 + krow) >> int(math.log2(CHUNK))) <= qchunk
            s = jnp.where(adm, s, -jnp.inf)
        bits = lax.bitcast_convert_type(s, I32)
        bits = jnp.where(bits == INT_MIN, 0, bits)
        key = bits ^ ((bits >> 31) & np.int32(0x7FFFFFFF))
        keys_ref[pl.ds(r0, KT), :] = key
        ukey = key ^ INT_MIN
        planes = _bit_transpose32([ukey[v * SUBLANES:(v + 1) * SUBLANES, :] for v in range(32)])
        p0 = pl.multiple_of(j * SUBLANES, SUBLANES)
        for b in range(32):
            planes_ref[b, pl.ds(p0, SUBLANES), :] = planes[31 - b]
        return carry

    @pl.when(i == 0)
    def _():
        planes_ref[...] = jnp.zeros(planes_ref.shape, I32)

    def score_pair(u, carry):
        score_tile(2 * u, 0, masked=False)
        score_tile(2 * u + 1, 0, masked=False)
        return carry

    lax.fori_loop(0, (nt - 1) // 2, score_pair, 0)

    @pl.when((nt - 1) % 2 == 1)
    def _():
        score_tile(nt - 2, 0, masked=False)

    score_tile(nt - 1, 0, masked=True)

    n_prow = planes_ref.shape[1]
    prow = lax.broadcasted_iota(I32, (n_prow, QB), 0)
    kf = float(k_top)

    def search_step(k, carry):
        live, c_gt, thr_u = carry
        b = 31 - k
        x = live & planes_ref[b]
        pc = lax.population_count(x)
        parts = [pc[r * SUBLANES:(r + 1) * SUBLANES, :] for r in range(n_prow // SUBLANES)]
        while len(parts) > 1:
            parts = [parts[k2] + parts[k2 + 1] for k2 in range(0, len(parts) - 1, 2)] + (
                [parts[-1]] if len(parts) % 2 else [])
        cnt = c_gt + jnp.sum(parts[0].astype(F32), axis=0, keepdims=True)
        take = cnt >= kf
        live = jnp.where(take, x, live ^ x)
        c_gt = jnp.where(take, c_gt, cnt)
        thr_u = thr_u | jnp.where(take, jnp.left_shift(jnp.int32(1), b), np.int32(0))
        return live, c_gt, thr_u

    _, c_gt, thr_u = lax.fori_loop(
        0, 32, search_step,
        (jnp.where(prow < nt * SUBLANES, np.int32(-1), np.int32(0)),
         jnp.zeros((1, QB), F32), jnp.zeros((1, QB), I32)))
    thr = thr_u ^ INT_MIN
    need = kf - c_gt

    p_refs, lg_refs = (p0_ref, p1_ref), (lg0_ref, lg1_ref)
    n_cols = A_HEADS * QGROUPS
    acc_ref[...] = jnp.zeros(acc_ref.shape, F32)
    p1_ref[...] = jnp.zeros(p1_ref.shape, BF16)

    def stage_scores(j, cnt_eq, slot, far=False):
        r0 = pl.multiple_of(j * KT, KT)
        kt = keys_ref[pl.ds(r0, KT), :]
        eq = kt == thr
        eqf = jnp.where(eq, 1.0, 0.0)
        rank = _dot(tri_ref[...], eqf.astype(BF16)) + cnt_eq
        sel = (kt > thr) | (eq & (rank <= need))
        if not far:
            sel = sel & (((r0 + krow) >> int(math.log2(CHUNK))) <= qchunk)
            bias = bias_ref[jnp.minimum(i - j * (KT // QB), near)]
        logits = _dot(ckv_ref[0, pl.ds(r0, KT), :], qlat_ref[...])
        tmax = []
        for c in range(n_cols):
            sl = slice(c * LANES, (c + 1) * LANES)
            qs = slice((c % QGROUPS) * LANES, (c % QGROUPS + 1) * LANES)
            lh = logits[:, sl] if far else logits[:, sl] + bias[:, sl]
            lh = jnp.where(sel[:, qs], lh, -jnp.inf)
            lg_refs[slot][:, sl] = lh
            tmax.append(jnp.max(lh, axis=0, keepdims=True))
        return tuple(tmax), cnt_eq + jnp.sum(eqf, axis=0, keepdims=True)

    def stage_probs(ms, tmax, slot):
        new_ms, alphas = [], []
        for c in range(n_cols):
            sl = slice(c * LANES, (c + 1) * LANES)
            m_new = jnp.maximum(ms[c], tmax[c])
            m_safe = jnp.where(m_new == -jnp.inf, 0.0, m_new)
            p_refs[slot][:, sl] = jnp.exp2(lg_refs[slot][:, sl] - m_safe).astype(BF16)
            alphas.append(jnp.exp2(ms[c] - m_safe))
            new_ms.append(m_new)
        return tuple(new_ms), tuple(alphas)

    def stage_values(alphas, jp, slot):
        for h in range(A_HEADS):
            vs = slice(h * V_ROWS, (h + 1) * V_ROWS)
            pv = _dot(vt_ref[0, jp, vs, :], p_refs[slot][:, h * QB:(h + 1) * QB])
            alpha = jnp.concatenate(alphas[h * QGROUPS:(h + 1) * QGROUPS], axis=1)
            acc_ref[vs, :] = acc_ref[vs, :] * alpha + pv

    def trip(t, carry, slot, far=False):
        ms, tmax, cnt_eq, alphas = carry
        stage_values(alphas, jnp.maximum(t - 1, 0), 1 - slot)
        ms, alphas = stage_probs(ms, tmax, slot)
        tmax, cnt_eq = stage_scores(t + 1, cnt_eq, 1 - slot, far)
        return ms, tmax, cnt_eq, alphas

    def finish(t, carry, slot):
        ms, tmax, _, alphas = carry
        stage_values(alphas, jnp.maximum(t - 1, 0), 1 - slot)
        _, alphas = stage_probs(ms, tmax, slot)
        stage_values(alphas, t, slot)

    def pair(u, carry, far=False):
        return trip(2 * u + 1, trip(2 * u, carry, 0, far), 1, far)

    tmax0, cnt_eq0 = stage_scores(0, jnp.zeros((1, QB), F32), 0)
    init = (tuple(jnp.full((1, LANES), -jnp.inf, F32) for _ in range(n_cols)), tmax0, cnt_eq0,
            tuple(jnp.ones((1, LANES), F32) for _ in range(n_cols)))
    n_trips = nt - 1
    n_far_pairs = jnp.maximum(i * (QB // KT) - near, 0) // 2
    carry = lax.fori_loop(0, n_far_pairs, functools.partial(pair, far=True), init)
    carry = lax.fori_loop(n_far_pairs, n_trips // 2, pair, carry)

    @pl.when(n_trips % 2 == 0)
    def _():
        finish(nt - 1, carry, 0)

    @pl.when(n_trips % 2 == 1)
    def _():
        finish(nt - 1, trip(nt - 2, carry, 0), 1)

    a_t = jnp.concatenate(
        [acc_ref[h * V_ROWS:h * V_ROWS + A_V_DIM, :]
         * (1.0 / acc_ref[h * V_ROWS + A_V_DIM:h * V_ROWS + A_V_DIM + 1, :]) for h in range(A_HEADS)],
        axis=0)
    o_ref[...] = a_t.T.astype(BF16)


def _attn_call(q_t, iq_t, iw_t, ckv3, vt4, ik3, wuk, bias_tiles, tri, *, k_top, near):
    b, s = ckv3.shape[0], ckv3.shape[1]
    nb = s // QB
    n_kt = s // KT
    hv = A_HEADS * A_V_DIM
    hvr = A_HEADS * V_ROWS
    kern = functools.partial(_attn_kernel, k_top=k_top, near=near)
    row = lambda w: pl.BlockSpec((QB, w), lambda bb, i: (bb * nb + i, 0))
    qblk = lambda a: pl.BlockSpec((1, a.shape[1], QB), lambda bb, i: (bb * nb + i, 0, 0))
    return pl.pallas_call(
        kern,
        grid=(b, nb),
        in_specs=[
            qblk(q_t), qblk(iq_t), qblk(iw_t),
            pl.BlockSpec((1, s, A_KV_RANK), lambda bb, i: (bb, 0, 0)),
            pl.BlockSpec((1, n_kt, hvr, KT), lambda bb, i: (bb, 0, 0, 0)),
            pl.BlockSpec((1, s, IDX_DIM), lambda bb, i: (bb, 0, 0)),
            _const_spec(wuk.shape), _const_spec(bias_tiles.shape), _const_spec(tri.shape),
        ],
        out_specs=row(hv),
        out_shape=jax.ShapeDtypeStruct((b * s, hv), BF16),
        scratch_shapes=[pltpu.VMEM((s, QB), I32), pltpu.VMEM((32, s // 32, QB), I32),
                        pltpu.VMEM((A_KV_RANK, A_HEADS * QB), BF16),
                        pltpu.VMEM((hvr, QB), F32),
                        pltpu.VMEM((KT, A_HEADS * QB), BF16), pltpu.VMEM((KT, A_HEADS * QB), BF16),
                        pltpu.VMEM((KT, A_HEADS * QB), F32), pltpu.VMEM((KT, A_HEADS * QB), F32)],
        compiler_params=_params(("parallel", "arbitrary")),
        name="dsa_attention",
    )(q_t, iq_t, iw_t, ckv3, vt4, ik3, wuk, bias_tiles, tri)


def _softplus(x):
    return jnp.maximum(x, 0.0) + jnp.log1p(jnp.exp(-jnp.abs(x)))


def _mixer_kernel(bm_ref, z_ref, xbc_ref, dtp_ref, dtt_ref, scw_ref, cw_ref, cb_ref,
                  dtb_r_ref, dtb_c_ref, alog_r_ref, alog_c_ref, dexp_ref, nw_ref,
                  tril_ref, triu_ref, eh_ref,
                  bo_ref, co_ref, ubuf, xbuf, state):
    L = SSD_L
    P = SUBLANES
    gw = SSM_INNER // SSM_GROUPS

    @pl.when(pl.program_id(1) == 0)
    def _():
        ubuf[0:P, :] = jnp.zeros((P, B_WIDTH), F32)
        xbuf[0:P, :] = jnp.zeros((P, SSM_XBC), F32)
        state[...] = jnp.zeros(state.shape, F32)

    bm = bm_ref[...]
    u = bm[:, B_WIDTH:2 * B_WIDTH] * bm[:, 2 * B_WIDTH:3 * B_WIDTH]
    ubuf[P:P + L, :] = u
    conv = scw_ref[SHORT_CONV - 1:SHORT_CONV, :] * u
    for j in range(SHORT_CONV - 1):
        off = P - (SHORT_CONV - 1) + j
        conv = conv + scw_ref[j:j + 1, :] * ubuf[off:off + L, :]
    bo_ref[...] = (bm[:, :B_WIDTH] * conv).astype(BF16)
    ubuf[0:P, :] = u[L - P:L, :]

    xr = xbc_ref[...]
    xbuf[P:P + L, :] = xr
    xc = cw_ref[SSM_CONV - 1:SSM_CONV, :] * xr + cb_ref[...]
    for j in range(SSM_CONV - 1):
        off = P - (SSM_CONV - 1) + j
        xc = xc + cw_ref[j:j + 1, :] * xbuf[off:off + L, :]
    xbuf[0:P, :] = xr[L - P:L, :]
    xa = xc * _sigmoid(xc)
    xs = xa[:, :SSM_INNER]

    dt = _softplus(dtp_ref[:, :SSM_HEADS] + dtb_r_ref[...])
    a = dt * (-jnp.exp(alog_r_ref[...]))
    a_cs = sum(_dot(tril_ref[...], part) for part in _split3(a))
    dt_t = _softplus(dtt_ref[0] + dtb_c_ref[...])
    a_t = dt_t * (-jnp.exp(alog_c_ref[...]))
    a_cs_t = sum(_dot(part, triu_ref[...]) for part in _split3(a_t))

    def expand(v):
        return sum(_dot(part, eh_ref[...]) for part in _split3(v))

    xdt = xs * expand(dt)
    e_cs = expand(jnp.exp(a_cs))
    dec = expand(jnp.exp(a_cs[L - 1:L, :] - a_cs))
    chunk_dec = e_cs[L - 1:L, :]

    row = lax.broadcasted_iota(I32, (L, L), 0)
    col = lax.broadcasted_iota(I32, (L, L), 1)
    causal = row >= col
    lo_half = lax.broadcasted_iota(I32, (L, LANES), 1) < SSM_HEAD_DIM
    hpg = SSM_HEADS // SSM_GROUPS

    y_parts = []
    for g in range(SSM_GROUPS):
        bg = xa[:, SSM_INNER + g * SSM_STATE:SSM_INNER + (g + 1) * SSM_STATE].astype(BF16)
        cg = xa[:, SSM_INNER + (SSM_GROUPS + g) * SSM_STATE:
                SSM_INNER + (SSM_GROUPS + g + 1) * SSM_STATE].astype(BF16)
        cb = lax.dot_general(cg, bg, (((1,), (1,)), ((), ())), preferred_element_type=F32)
        sl = slice(g * gw, (g + 1) * gw)
        st = state[g]
        y_off = _dot(cg, st.astype(BF16)) * e_cs[:, sl]
        for k in range(hpg // 2):
            xp = xdt[:, g * gw + k * LANES:g * gw + (k + 1) * LANES]
            halves = (jnp.where(lo_half, xp, 0.0).astype(BF16), jnp.where(lo_half, 0.0, xp).astype(BF16))
            y = y_off[:, k * LANES:(k + 1) * LANES]
            for hh in range(2):
                h = g * hpg + 2 * k + hh
                seg = a_cs[:, h:h + 1] - a_cs_t[h:h + 1, :]
                m = (cb * jnp.exp(jnp.where(causal, seg, -jnp.inf))).astype(BF16)
                y = y + _dot(m, halves[hh])
            y_parts.append(y)
        xw = (dec[:, sl] * xdt[:, sl]).astype(BF16)
        upd = lax.dot_general(bg, xw, (((0,), (0,)), ((), ())), preferred_element_type=F32)
        state[g] = chunk_dec[:, sl] * st + upd

    y = jnp.concatenate(y_parts, axis=1) + dexp_ref[...] * xs
    zz = z_ref[...]
    y = y * (zz * _sigmoid(zz))
    outs = []
    for g in range(SSM_GROUPS):
        sl = slice(g * gw, (g + 1) * gw)
        outs.append(_rms(y[:, sl], nw_ref[:, sl]))
    co_ref[...] = jnp.concatenate(outs, axis=1).astype(BF16)


def _mixer_call(bmix, z, xbc, sm, dt_t, scw, cw, cb, dtb, alog, dexp, nw, b, s):
    L = SSD_L
    nc = s // L
    tril = jnp.asarray(np.tril(np.ones((L, L), np.float32)), BF16)
    triu = jnp.asarray(np.triu(np.ones((L, L), np.float32)), BF16)
    eh = jnp.asarray(np.repeat(np.eye(SSM_HEADS, dtype=np.float32), SSM_HEAD_DIM, axis=1), BF16)
    row = lambda w: pl.BlockSpec((L, w), lambda bb, c: (bb * nc + c, 0))
    consts = [scw, cw, cb.reshape(1, -1), dtb.reshape(1, -1), dtb.reshape(-1, 1),
              alog.reshape(1, -1), alog.reshape(-1, 1), dexp.reshape(1, -1), nw.reshape(1, -1),
              tril, triu, eh]
    return pl.pallas_call(
        _mixer_kernel,
        grid=(b, nc),
        in_specs=[row(3 * B_WIDTH), row(SSM_INNER), row(SSM_XBC), row(SMALL_W),
                  pl.BlockSpec((1, SSM_HEADS, L), lambda bb, c: (bb, 0, c))]
                 + [_const_spec(c.shape) for c in consts],
        out_specs=[row(B_WIDTH), row(SSM_INNER)],
        out_shape=[jax.ShapeDtypeStruct((b * s, B_WIDTH), BF16),
                   jax.ShapeDtypeStruct((b * s, SSM_INNER), BF16)],
        scratch_shapes=[pltpu.VMEM((L + SUBLANES, B_WIDTH), F32),
                        pltpu.VMEM((L + SUBLANES, SSM_XBC), F32),
                        pltpu.VMEM((SSM_GROUPS, SSM_STATE, SSM_INNER // SSM_GROUPS), F32)],
        compiler_params=_params(("parallel", "arbitrary")),
        name="conv_ssd_mixers",
    )(bmix, z, xbc, sm, dt_t, *consts)


def _dense_kernel(x_ref, a_ref, b_ref, c_ref, p_ref, g_post_ref, g_pre_ref, g_fpost_ref,
                  woa_ref, wob_ref, woc_ref, wg_ref, wu_ref, wd_ref, wpg_ref, wpp_ref, o_ref):
    subs = [slice(r * TM_SUB, (r + 1) * TM_SUB) for r in range(ROW_SUBTILES)]
    mix = [_dot(a_ref[rows, :], woa_ref[...]) + _dot(b_ref[rows, :], wob_ref[...])
           + _dot(c_ref[rows, :], woc_ref[...]) for rows in subs]
    ple = [_dot(p_ref[rows, :].astype(BF16), wpp_ref[...]) for rows in subs]
    x = [x_ref[rows, :] + _rms(m, g_post_ref[...]) for rows, m in zip(subs, mix)]
    h = [_rms(xx, g_pre_ref[...]).astype(BF16) for xx in x]
    gate = [_dot(hh, wg_ref[...]) for hh in h]
    up = [_dot(hh, wu_ref[...]) for hh in h]
    act = [(g * _sigmoid(g) * u).astype(BF16) for g, u in zip(gate, up)]
    f = [_dot(aa, wd_ref[...]) for aa in act]
    x = [xx + _rms(ff, g_fpost_ref[...]) for xx, ff in zip(x, f)]
    pg = [_sigmoid(_dot(xx.astype(BF16), wpg_ref[...])) for xx in x]
    for rows, xx, g, e in zip(subs, x, pg, ple):
        o_ref[rows, :] = xx + g * e


def _dense_call(x2, a, b, c, p2, g_post, g_pre, g_fpost, woa, wob, woc, wg, wu, wd, wpg, wpp):
    t = x2.shape[0]
    tm = TM_OUT
    row = lambda w: pl.BlockSpec((tm, w), lambda r: (r, 0))
    consts = [g_post, g_pre, g_fpost, woa, wob, woc, wg, wu, wd, wpg, wpp]
    return pl.pallas_call(
        _dense_kernel,
        grid=(t // tm,),
        in_specs=[row(D_MODEL), row(a.shape[1]), row(b.shape[1]), row(c.shape[1]), row(PLE_DIM)]
                 + [_const_spec(w.shape) for w in consts],
        out_specs=row(D_MODEL),
        out_shape=jax.ShapeDtypeStruct((t, D_MODEL), F32),
        compiler_params=_params(("parallel",)),
        name="outproj_ffn_ple",
    )(x2, a, b, c, p2, *consts)


def kernel(x, p, pre_mix_norm, post_mix_norm, pre_ffn_norm, post_ffn_norm, w_in, kv_norm, idx_k_norm_g, idx_k_norm_b, w_uk, w_uv, rel_bias, short_conv_w, ssm_conv_w, ssm_conv_b, ssm_dt_bias, ssm_a_log, ssm_d, ssm_norm, w_out, w_ffn_gate, w_ffn_up, w_ffn_down, w_ple_proj, w_ple_gate):
    bsz, s, d = x.shape
    assert d == D_MODEL and s % (IN_SUBTILES * KT) == 0 and s % QB == 0 and s % SSD_L == 0
    assert (bsz * s) % TM_OUT == 0
    t = bsz * s
    k_top = min(TOPK_MAX, s // 4)

    bucket_np, near, far_bucket = _bucket_tiles()
    bias_tiles = _bias_tiles_call(rel_bias.astype(F32), jnp.asarray(bucket_np), far_bucket)
    tri = jnp.asarray(np.tril(np.ones((KT, KT), np.float32)), BF16)
    row1 = lambda v: v.reshape(1, -1).astype(F32)
    vone_np = np.zeros((A_HEADS, V_ROWS, 1), np.float32)
    vone_np[:, A_V_DIM] = 1.0
    vone = jnp.asarray(vone_np.reshape(A_HEADS * V_ROWS, 1))

    x2 = x.reshape(t, d)
    for i in range(DEPTH):
        wparts = _wprep_call(w_in[i].astype(F32))
        wuv_t = jnp.transpose(w_uv[i], (1, 2, 0)).astype(BF16)
        wuv_t = jnp.pad(wuv_t, ((0, 0), (0, V_ROWS - A_V_DIM), (0, 0))).reshape(A_HEADS * V_ROWS, A_KV_RANK)
        wq, wckv, wiq, wsm, wb, wz, wxbc, wdt = wparts
        wiw_t = wsm[:, IW_OFF:IW_OFF + SUBLANES].T
        q_t, ckv, vt4, iq_t, iw_t, ik, bmix, z, xbc, dtp = _inproj_call(
            x2, row1(pre_mix_norm[i]), row1(kv_norm[i]), row1(idx_k_norm_g[i]), row1(idx_k_norm_b[i]),
            wq.T, wckv, wiq.T, wsm, wiw_t, wb, wz, wxbc, wdt, wuv_t, vone, s // KT)

        a_out = _attn_call(
            q_t, iq_t, iw_t, ckv.reshape(bsz, s, A_KV_RANK), vt4, ik.reshape(bsz, s, IDX_DIM),
            jnp.transpose(w_uk[i], (1, 0, 2)).astype(BF16),
            bias_tiles, tri, k_top=k_top, near=near)

        dt_t = dtp[:, :SSM_HEADS].reshape(bsz, s, SSM_HEADS).transpose(0, 2, 1)
        b_out, c_out = _mixer_call(
            bmix, z, xbc, dtp, dt_t, short_conv_w[i].astype(F32), ssm_conv_w[i].astype(F32),
            ssm_conv_b[i].astype(F32), ssm_dt_bias[i].astype(F32), ssm_a_log[i].astype(F32),
            jnp.repeat(ssm_d[i].astype(F32), SSM_HEAD_DIM), ssm_norm[i].astype(F32), bsz, s)

        wo = w_out[i].astype(BF16)
        na, nbw = A_HEADS * A_V_DIM, A_HEADS * A_V_DIM + B_WIDTH
        x2 = _dense_call(
            x2, a_out, b_out, c_out, p[i].reshape(t, PLE_DIM),
            row1(post_mix_norm[i]), row1(pre_ffn_norm[i]), row1(post_ffn_norm[i]),
            wo[:na], wo[na:nbw], wo[nbw:], w_ffn_gate[i].astype(BF16), w_ffn_up[i].astype(BF16),
            w_ffn_down[i].astype(BF16), w_ple_gate[i].astype(BF16), w_ple_proj[i].astype(BF16))
    return x2.reshape(bsz, s, d)
```

```python
import functools
import math

import numpy as np
import jax
import jax.numpy as jnp
from jax import lax
from jax.experimental import pallas as pl
from jax.experimental.pallas import tpu as pltpu

F32, BF16, I32 = jnp.float32, jnp.bfloat16, jnp.int32

D_MODEL = 1024
DEPTH = 2
CHUNK = 64
A_HEADS = 8
A_QK_DIM = 64
A_V_DIM = 64
A_KV_RANK = 256
IDX_HEADS = 4
IDX_DIM = 64
TOPK_MAX = 256
REL_BUCKETS = 32
REL_MAX_DIST = 1024
B_WIDTH = 512
SHORT_CONV = 3
SSM_HEADS = 16
SSM_HEAD_DIM = 64
SSM_INNER = SSM_HEADS * SSM_HEAD_DIM
SSM_GROUPS = 2
SSM_STATE = 128
SSM_CONV = 4
SSM_XBC = SSM_INNER + 2 * SSM_GROUPS * SSM_STATE
MIX_WIDTH = A_HEADS * A_V_DIM + B_WIDTH + SSM_INNER
D_FF = -(-8 * D_MODEL // (3 * 256)) * 256
PLE_DIM = 256
NORM_EPS = 1e-6
IN_SPLITS = (A_HEADS * A_QK_DIM, A_KV_RANK, IDX_HEADS * IDX_DIM, IDX_DIM, IDX_HEADS,
             B_WIDTH, B_WIDTH, B_WIDTH, SSM_INNER, SSM_XBC, SSM_HEADS)
IN_OFFSETS = tuple(int(v) for v in np.concatenate([[0], np.cumsum(IN_SPLITS)]))

LANES = 128
SUBLANES = 8
VMEM_LIMIT_BYTES = 56 * 1024 * 1024

QB = 2 * LANES
QGROUPS = QB // LANES
KT = 256
SSD_L = 128
IN_SUBTILES = 2
TM_SUB = 256
ROW_SUBTILES = 2
TM_OUT = TM_SUB * ROW_SUBTILES
SMALL_W = LANES
IW_OFF = IDX_DIM
INT_MIN = np.int32(-2 ** 31)
LOG2E = math.log2(math.e)
BF16_ROWS = 2 * SUBLANES
V_ROWS = A_V_DIM + BF16_ROWS

def _const_spec(shape):
    nd = len(shape)
    return pl.BlockSpec(shape, lambda *_: (0,) * nd, pipeline_mode=pl.Buffered(1))


def _params(sem):
    return pltpu.CompilerParams(dimension_semantics=sem, vmem_limit_bytes=VMEM_LIMIT_BYTES)


def _rms(x, g):
    return x * lax.rsqrt(jnp.mean(x * x, axis=-1, keepdims=True) + NORM_EPS) * g


def _sigmoid(x):
    return 1.0 / (1.0 + jnp.exp(-x))


def _split3(v):
    hi = v.astype(BF16)
    r1 = v - hi.astype(F32)
    mid = r1.astype(BF16)
    lo = (r1 - mid.astype(F32)).astype(BF16)
    return hi, mid, lo


def _dot(a, b):
    return jnp.dot(a, b, preferred_element_type=F32)


def _t5_bucket_np(rel):
    half = REL_BUCKETS // 2
    max_exact = half // 2
    ret = np.where(rel > 0, half, 0)
    n = np.abs(rel)
    nf = np.maximum(n, 1).astype(np.float32)
    large = max_exact + (np.log(nf / np.float32(max_exact)) / np.float32(math.log(REL_MAX_DIST / max_exact))
                         * np.float32(half - max_exact)).astype(np.int32)
    large = np.minimum(large, half - 1)
    return (ret + np.where(n < max_exact, n, large)).astype(np.int32)


@functools.lru_cache(maxsize=None)
def _bucket_tiles():
    kr = np.arange(KT)[:, None]
    qc = np.arange(QB)[None, :]
    tiles, near = [], None
    for o in range(64):
        t = _t5_bucket_np(kr - qc - o * QB)
        tiles.append(t)
    for o in range(63, -1, -1):
        if not (np.all(tiles[o] == tiles[63][0, 0])):
            near = o + 1
            break
    assert near is not None and 0 < near < 63
    return np.stack(tiles[:near + 1]), near, int(tiles[63][0, 0])


def _bias_kernel(rb_ref, bk_ref, out_ref, *, far_bucket):
    bk = bk_ref[0]
    for h in range(A_HEADS):
        acc = jnp.zeros(bk.shape, F32)
        for b in range(REL_BUCKETS):
            acc = jnp.where(bk == b, (rb_ref[b, h] - rb_ref[far_bucket, h]) * LOG2E, acc)
        out_ref[0, :, h * QB:(h + 1) * QB] = acc


def _bias_tiles_call(rel_bias, bucket_tiles, far_bucket):
    n = bucket_tiles.shape[0]
    return pl.pallas_call(
        functools.partial(_bias_kernel, far_bucket=far_bucket),
        grid=(n,),
        in_specs=[pl.BlockSpec(memory_space=pltpu.SMEM),
                  pl.BlockSpec((1, KT, QB), lambda o: (o, 0, 0))],
        out_specs=pl.BlockSpec((1, KT, A_HEADS * QB), lambda o: (o, 0, 0)),
        out_shape=jax.ShapeDtypeStruct((n, KT, A_HEADS * QB), F32),
        compiler_params=_params(("arbitrary",)),
        name="rel_bias_tiles",
    )(rel_bias, bucket_tiles)


def _wprep_kernel(w_ref, wq_out, wckv_out, wiq_out, wsm_out, wb_out, wz_out, wxbc_out, wdt_out):
    o_q, o_ckv, o_iq, o_ik, _, o_bg, _, _, o_z, o_xbc, o_dt, o_end = IN_OFFSETS
    wq_out[...] = w_ref[:, o_q:o_ckv].astype(BF16)
    wckv_out[...] = w_ref[:, o_ckv:o_iq].astype(BF16)
    wiq_out[...] = w_ref[:, o_iq:o_ik].astype(BF16)
    wsm_out[...] = w_ref[:, o_ik:o_ik + SMALL_W].astype(BF16)
    tail = w_ref[:, o_bg:o_end].astype(BF16)
    wb_out[...] = tail[:, :o_z - o_bg]
    wz_out[...] = tail[:, o_z - o_bg:o_xbc - o_bg]
    wxbc_out[...] = tail[:, o_xbc - o_bg:o_dt - o_bg]
    wdt_out[...] = jnp.zeros(wdt_out.shape, BF16)
    wdt_out[:, :o_end - o_dt] = tail[:, o_dt - o_bg:]


def _wprep_call(w):
    d, n = w.shape
    tr = 256
    widths = [IN_SPLITS[0], IN_SPLITS[1], IN_SPLITS[2], SMALL_W, 3 * B_WIDTH, SSM_INNER, SSM_XBC, SMALL_W]
    return pl.pallas_call(
        _wprep_kernel,
        grid=(d // tr,),
        in_specs=[pl.BlockSpec((tr, n), lambda r: (r, 0))],
        out_specs=[pl.BlockSpec((tr, wd), lambda r: (r, 0)) for wd in widths],
        out_shape=[jax.ShapeDtypeStruct((d, wd), BF16) for wd in widths],
        compiler_params=_params(("parallel",)),
        name="inproj_weight_prep",
    )(w)


def _dot_nt(a, b):
    return lax.dot_general(a, b, (((1,), (1,)), ((), ())), preferred_element_type=F32)


def _inproj_kernel(x_ref, g_ref, kvn_ref, ikg_ref, ikb_ref,
                   wqt_ref, wckv_ref, wiqt_ref, wsm_ref, wiwt_ref, wb_ref, wz_ref, wxbc_ref, wdt_ref,
                   wuv_ref, vone_ref,
                   qt_out, ckv_out, vt_out, iqt_out, iwt_out, ik_out, b_out, z_out, xbc_out, dt_out):
    subs = [slice(k * KT, (k + 1) * KT) for k in range(IN_SUBTILES)]
    h = [_rms(x_ref[rows, :], g_ref[...]).astype(BF16) for rows in subs]
    ckv = [_rms(_dot(hh, wckv_ref[...]), kvn_ref[...]).astype(BF16) for hh in h]
    sm = [_dot(hh, wsm_ref[...]) for hh in h]
    for k, rows in enumerate(subs):
        ckv_out[rows, :] = ckv[k]
        qt_out[k] = _dot_nt(wqt_ref[...], h[k]).astype(BF16)
        iqt_out[k] = (_dot_nt(wiqt_ref[...], h[k]) * (IDX_DIM ** -0.5)).astype(BF16)
        iwt_out[k] = _dot_nt(wiwt_ref[...], h[k]) * (IDX_HEADS ** -0.5)
    for k, rows in enumerate(subs):
        vt_out[0, k] = (_dot_nt(wuv_ref[...], ckv[k]) + vone_ref[...]).astype(BF16)
        ik = sm[k][:, :IDX_DIM]
        mu = jnp.mean(ik, axis=-1, keepdims=True)
        var = jnp.mean(jnp.square(ik - mu), axis=-1, keepdims=True)
        ik_out[rows, :] = ((ik - mu) * lax.rsqrt(var + NORM_EPS) * ikg_ref[...] + ikb_ref[...]).astype(BF16)
    for k, rows in enumerate(subs):
        b_out[rows, :] = _dot(h[k], wb_ref[...])
    for k, rows in enumerate(subs):
        z_out[rows, :] = _dot(h[k], wz_ref[...])
    for k, rows in enumerate(subs):
        xbc_out[rows, :] = _dot(h[k], wxbc_ref[...])
        dt_out[rows, :] = _dot(h[k], wdt_ref[...])


def _inproj_call(x2, g, kvn, ikg, ikb, wqt, wckv, wiqt, wsm, wiwt, wb, wz, wxbc, wdt, wuv, vone, n_kt):
    t = x2.shape[0]
    assert KT == QB and n_kt % IN_SUBTILES == 0
    tm = IN_SUBTILES * KT
    steps_per_seq = n_kt // IN_SUBTILES
    row = lambda w: pl.BlockSpec((tm, w), lambda r: (r, 0))
    rows = lambda w, dt: (row(w), jax.ShapeDtypeStruct((t, w), dt))
    qblk = lambda w, dt: (pl.BlockSpec((IN_SUBTILES, w, QB), lambda r: (r, 0, 0)),
                          jax.ShapeDtypeStruct((t // QB, w, QB), dt))
    hv = wuv.shape[0]
    outs = [qblk(wqt.shape[0], BF16), rows(wckv.shape[1], BF16),
            (pl.BlockSpec((1, IN_SUBTILES, hv, KT), lambda r: (r // steps_per_seq, r % steps_per_seq, 0, 0)),
             jax.ShapeDtypeStruct((t // (n_kt * KT), n_kt, hv, KT), BF16)),
            qblk(wiqt.shape[0], BF16), qblk(wiwt.shape[0], F32), rows(IDX_DIM, BF16),
            rows(wb.shape[1], F32), rows(wz.shape[1], F32), rows(wxbc.shape[1], F32),
            rows(SMALL_W, F32)]
    consts = [g, kvn, ikg, ikb, wqt, wckv, wiqt, wsm, wiwt, wb, wz, wxbc, wdt, wuv, vone]
    return pl.pallas_call(
        _inproj_kernel,
        grid=(t // tm,),
        in_specs=[row(D_MODEL)] + [_const_spec(c.shape) for c in consts],
        out_specs=[spec for spec, _ in outs],
        out_shape=[shape for _, shape in outs],
        compiler_params=_params(("parallel",)),
        name="inproj",
    )(x2, *consts)


def _bit_transpose32(words):
    a = list(words)
    m, j = 0x0000FFFF, 16
    while j:
        k = 0
        while k < 32:
            t = (a[k] ^ lax.shift_right_logical(a[k + j], np.int32(j))) & np.int32(m)
            a[k] = a[k] ^ t
            a[k + j] = a[k + j] ^ (t << j)
            k = (k + j + 1) & ~j
        j >>= 1
        m = (m ^ (m << j)) & 0xFFFFFFFF
    return a


def _attn_kernel(q_ref, iq_ref, iw_ref, ckv_ref, vt_ref, ik_ref, wuk_ref, bias_ref, tri_ref,
                 o_ref, keys_ref, planes_ref, qlat_ref, acc_ref, p0_ref, p1_ref, lg0_ref, lg1_ref,
                 *, k_top, near):
    i = pl.program_id(1)
    nt = ((i + 1) * QB + KT - 1) // KT
    lane = lax.broadcasted_iota(I32, (1, QB), 1)
    qchunk = (i * QB + lane) >> int(math.log2(CHUNK))
    krow = lax.broadcasted_iota(I32, (KT, QB), 0)

    q_t = q_ref[0]
    for h in range(A_HEADS):
        ql = _dot(wuk_ref[h], q_t[h * A_QK_DIM:(h + 1) * A_QK_DIM, :])
        qlat_ref[:, h * QB:(h + 1) * QB] = (ql * (A_QK_DIM ** -0.5 * LOG2E)).astype(BF16)

    iq_t = iq_ref[0]
    iw = iw_ref[0][:IDX_HEADS, :]

    def score_tile(j, carry, masked):
        r0 = pl.multiple_of(j * KT, KT)
        ikt = ik_ref[0, pl.ds(r0, KT), :]
        s = jnp.zeros((KT, QB), F32)
        for h in range(IDX_HEADS):
            raw = _dot(ikt, iq_t[h * IDX_DIM:(h + 1) * IDX_DIM, :])
            s = s + jnp.maximum(raw, 0.0) * iw[h:h + 1, :]
        if masked:
            adm = ((r0 + krow) >> int(math.log2(CHUNK))) <= qchunk
            s = jnp.where(adm, s, -jnp.inf)
        bits = lax.bitcast_convert_type(s, I32)
        bits = jnp.where(bits == INT_MIN, 0, bits)
        key = bits ^ ((bits >> 31) & np.int32(0x7FFFFFFF))
        keys_ref[pl.ds(r0, KT), :] = key
        ukey = key ^ INT_MIN
        planes = _bit_transpose32([ukey[v * SUBLANES:(v + 1) * SUBLANES, :] for v in range(32)])
        p0 = pl.multiple_of(j * SUBLANES, SUBLANES)
        for b in range(32):
            planes_ref[b, pl.ds(p0, SUBLANES), :] = planes[31 - b]
        return carry

    @pl.when(i == 0)
    def _():
        planes_ref[...] = jnp.zeros(planes_ref.shape, I32)

    def score_pair(u, carry):
        score_tile(2 * u, 0, masked=False)
        score_tile(2 * u + 1, 0, masked=False)
        return carry

    lax.fori_loop(0, (nt - 1) // 2, score_pair, 0)

    @pl.when((nt - 1) % 2 == 1)
    def _():
        score_tile(nt - 2, 0, masked=False)

    score_tile(nt - 1, 0, masked=True)

    n_prow = planes_ref.shape[1]
    prow = lax.broadcasted_iota(I32, (n_prow, QB), 0)
    kf = float(k_top)

    def search_step(k, carry):
        live, c_gt, thr_u = carry
        b = 31 - k
        x = live & planes_ref[b]
        pc = lax.population_count(x)
        parts = [pc[r * SUBLANES:(r + 1) * SUBLANES, :] for r in range(n_prow // SUBLANES)]
        while len(parts) > 1:
            parts = [parts[k2] + parts[k2 + 1] for k2 in range(0, len(parts) - 1, 2)] + (
                [parts[-1]] if len(parts) % 2 else [])
        cnt = c_gt + jnp.sum(parts[0].astype(F32), axis=0, keepdims=True)
        take = cnt >= kf
        live = jnp.where(take, x, live ^ x)
        c_gt = jnp.where(take, c_gt, cnt)
        thr_u = thr_u | jnp.where(take, jnp.left_shift(jnp.int32(1), b), np.int32(0))
        return live, c_gt, thr_u

    _, c_gt, thr_u = lax.fori_loop(
        0, 32, search_step,
        (jnp.where(prow < nt * SUBLANES, np.int32(-1), np.int32(0)),
         jnp.zeros((1, QB), F32), jnp.zeros((1, QB), I32)))
    thr = thr_u ^ INT_MIN
    need = kf - c_gt

    p_refs, lg_refs = (p0_ref, p1_ref), (lg0_ref, lg1_ref)
    n_cols = A_HEADS * QGROUPS
    acc_ref[...] = jnp.zeros(acc_ref.shape, F32)
    p1_ref[...] = jnp.zeros(p1_ref.shape, BF16)

    def stage_scores(j, cnt_eq, slot, far=False):
        r0 = pl.multiple_of(j * KT, KT)
        kt = keys_ref[pl.ds(r0, KT), :]
        eq = kt == thr
        eqf = jnp.where(eq, 1.0, 0.0)
        rank = _dot(tri_ref[...], eqf.astype(BF16)) + cnt_eq
        sel = (kt > thr) | (eq & (rank <= need))
        if not far:
            sel = sel & (((r0 + krow) >> int(math.log2(CHUNK))) <= qchunk)
            bias = bias_ref[jnp.minimum(i - j * (KT // QB), near)]
        logits = _dot(ckv_ref[0, pl.ds(r0, KT), :], qlat_ref[...])
        tmax = [[] for _ in range(QGROUPS)]
        for h in range(A_HEADS):
            for g in range(QGROUPS):
                sl = slice(h * QB + g * LANES, h * QB + (g + 1) * LANES)
                lh = logits[:, sl] if far else logits[:, sl] + bias[:, sl]
                lh = jnp.where(sel[:, g * LANES:(g + 1) * LANES], lh, -jnp.inf)
                lg_refs[slot][:, sl] = lh
                tmax[g].append(jnp.max(lh, axis=0, keepdims=True))
        tmax = tuple(jnp.concatenate(rows, axis=0) for rows in tmax)
        return tmax, cnt_eq + jnp.sum(eqf, axis=0, keepdims=True)

    def stage_probs(ms, tmax, slot):
        new_ms, alphas = [], []
        for g in range(QGROUPS):
            m_new = jnp.maximum(ms[g], tmax[g])
            m_safe = jnp.where(m_new == -jnp.inf, 0.0, m_new)
            alphas.append(jnp.exp2(ms[g] - m_safe))
            new_ms.append(m_new)
            for h in range(A_HEADS):
                sl = slice(h * QB + g * LANES, h * QB + (g + 1) * LANES)
                p_refs[slot][:, sl] = jnp.exp2(lg_refs[slot][:, sl] - m_safe[h:h + 1, :]).astype(BF16)
        return tuple(new_ms), tuple(alphas)

    def stage_values(alphas, jp, slot):
        for h in range(A_HEADS):
            vs = slice(h * V_ROWS, (h + 1) * V_ROWS)
            pv = _dot(vt_ref[0, jp, vs, :], p_refs[slot][:, h * QB:(h + 1) * QB])
            alpha = jnp.concatenate([a[h:h + 1, :] for a in alphas], axis=1)
            acc_ref[vs, :] = acc_ref[vs, :] * alpha + pv

    def trip(t, carry, slot, far=False):
        ms, tmax, cnt_eq, alphas = carry
        stage_values(alphas, jnp.maximum(t - 1, 0), 1 - slot)
        ms, alphas = stage_probs(ms, tmax, slot)
        tmax, cnt_eq = stage_scores(t + 1, cnt_eq, 1 - slot, far)
        return ms, tmax, cnt_eq, alphas

    def finish(t, carry, slot):
        ms, tmax, _, alphas = carry
        stage_values(alphas, jnp.maximum(t - 1, 0), 1 - slot)
        _, alphas = stage_probs(ms, tmax, slot)
        stage_values(alphas, t, slot)

    def pair(u, carry, far=False):
        return trip(2 * u + 1, trip(2 * u, carry, 0, far), 1, far)

    tmax0, cnt_eq0 = stage_scores(0, jnp.zeros((1, QB), F32), 0)
    init = (tuple(jnp.full((A_HEADS, LANES), -jnp.inf, F32) for _ in range(QGROUPS)), tmax0, cnt_eq0,
            tuple(jnp.ones((A_HEADS, LANES), F32) for _ in range(QGROUPS)))
    n_trips = nt - 1
    n_far_pairs = jnp.maximum(i * (QB // KT) - near, 0) // 2
    carry = lax.fori_loop(0, n_far_pairs, functools.partial(pair, far=True), init)
    carry = lax.fori_loop(n_far_pairs, n_trips // 2, pair, carry)

    @pl.when(n_trips % 2 == 0)
    def _():
        finish(nt - 1, carry, 0)

    @pl.when(n_trips % 2 == 1)
    def _():
        finish(nt - 1, trip(nt - 2, carry, 0), 1)

    a_t = jnp.concatenate(
        [acc_ref[h * V_ROWS:h * V_ROWS + A_V_DIM, :]
         * (1.0 / acc_ref[h * V_ROWS + A_V_DIM:h * V_ROWS + A_V_DIM + 1, :]) for h in range(A_HEADS)],
        axis=0)
    o_ref[...] = a_t.T.astype(BF16)


def _attn_call(q_t, iq_t, iw_t, ckv3, vt4, ik3, wuk, bias_tiles, tri, *, k_top, near):
    b, s = ckv3.shape[0], ckv3.shape[1]
    nb = s // QB
    n_kt = s // KT
    hv = A_HEADS * A_V_DIM
    hvr = A_HEADS * V_ROWS
    kern = functools.partial(_attn_kernel, k_top=k_top, near=near)
    row = lambda w: pl.BlockSpec((QB, w), lambda bb, i: (bb * nb + i, 0))
    qblk = lambda a: pl.BlockSpec((1, a.shape[1], QB), lambda bb, i: (bb * nb + i, 0, 0))
    return pl.pallas_call(
        kern,
        grid=(b, nb),
        in_specs=[
            qblk(q_t), qblk(iq_t), qblk(iw_t),
            pl.BlockSpec((1, s, A_KV_RANK), lambda bb, i: (bb, 0, 0)),
            pl.BlockSpec((1, n_kt, hvr, KT), lambda bb, i: (bb, 0, 0, 0)),
            pl.BlockSpec((1, s, IDX_DIM), lambda bb, i: (bb, 0, 0)),
            _const_spec(wuk.shape), _const_spec(bias_tiles.shape), _const_spec(tri.shape),
        ],
        out_specs=row(hv),
        out_shape=jax.ShapeDtypeStruct((b * s, hv), BF16),
        scratch_shapes=[pltpu.VMEM((s, QB), I32), pltpu.VMEM((32, s // 32, QB), I32),
                        pltpu.VMEM((A_KV_RANK, A_HEADS * QB), BF16),
                        pltpu.VMEM((hvr, QB), F32),
                        pltpu.VMEM((KT, A_HEADS * QB), BF16), pltpu.VMEM((KT, A_HEADS * QB), BF16),
                        pltpu.VMEM((KT, A_HEADS * QB), F32), pltpu.VMEM((KT, A_HEADS * QB), F32)],
        compiler_params=_params(("parallel", "arbitrary")),
        name="dsa_attention",
    )(q_t, iq_t, iw_t, ckv3, vt4, ik3, wuk, bias_tiles, tri)


def _softplus(x):
    return jnp.maximum(x, 0.0) + jnp.log1p(jnp.exp(-jnp.abs(x)))


def _mixer_kernel(bm_ref, z_ref, xbc_ref, dtp_ref, dtt_ref, scw_ref, cw_ref, cb_ref,
                  dtb_r_ref, dtb_c_ref, alog_r_ref, alog_c_ref, dexp_ref, nw_ref,
                  tril_ref, triu_ref, eh_ref,
                  bo_ref, co_ref, ubuf, xbuf, state):
    L = SSD_L
    P = SUBLANES
    gw = SSM_INNER // SSM_GROUPS

    @pl.when(pl.program_id(1) == 0)
    def _():
        ubuf[0:P, :] = jnp.zeros((P, B_WIDTH), F32)
        xbuf[0:P, :] = jnp.zeros((P, SSM_XBC), F32)
        state[...] = jnp.zeros(state.shape, F32)

    bm = bm_ref[...]
    u = bm[:, B_WIDTH:2 * B_WIDTH] * bm[:, 2 * B_WIDTH:3 * B_WIDTH]
    ubuf[P:P + L, :] = u
    conv = scw_ref[SHORT_CONV - 1:SHORT_CONV, :] * u
    for j in range(SHORT_CONV - 1):
        off = P - (SHORT_CONV - 1) + j
        conv = conv + scw_ref[j:j + 1, :] * ubuf[off:off + L, :]
    bo_ref[...] = (bm[:, :B_WIDTH] * conv).astype(BF16)
    ubuf[0:P, :] = u[L - P:L, :]

    xr = xbc_ref[...]
    xbuf[P:P + L, :] = xr
    xc = cw_ref[SSM_CONV - 1:SSM_CONV, :] * xr + cb_ref[...]
    for j in range(SSM_CONV - 1):
        off = P - (SSM_CONV - 1) + j
        xc = xc + cw_ref[j:j + 1, :] * xbuf[off:off + L, :]
    xbuf[0:P, :] = xr[L - P:L, :]
    xa = xc * _sigmoid(xc)
    xs = xa[:, :SSM_INNER]

    dt = _softplus(dtp_ref[:, :SSM_HEADS] + dtb_r_ref[...])
    a = dt * (-jnp.exp(alog_r_ref[...]))
    a_cs = sum(_dot(tril_ref[...], part) for part in _split3(a))
    dt_t = _softplus(dtt_ref[0] + dtb_c_ref[...])
    a_t = dt_t * (-jnp.exp(alog_c_ref[...]))
    a_cs_t = sum(_dot(part, triu_ref[...]) for part in _split3(a_t))

    def expand(v):
        return sum(_dot(part, eh_ref[...]) for part in _split3(v))

    xdt = xs * expand(dt)
    e_cs = expand(jnp.exp(a_cs))
    dec = expand(jnp.exp(a_cs[L - 1:L, :] - a_cs))
    chunk_dec = e_cs[L - 1:L, :]

    row = lax.broadcasted_iota(I32, (L, L), 0)
    col = lax.broadcasted_iota(I32, (L, L), 1)
    causal = row >= col
    lo_half = lax.broadcasted_iota(I32, (L, LANES), 1) < SSM_HEAD_DIM
    hpg = SSM_HEADS // SSM_GROUPS

    y_parts = []
    for g in range(SSM_GROUPS):
        bg = xa[:, SSM_INNER + g * SSM_STATE:SSM_INNER + (g + 1) * SSM_STATE].astype(BF16)
        cg = xa[:, SSM_INNER + (SSM_GROUPS + g) * SSM_STATE:
                SSM_INNER + (SSM_GROUPS + g + 1) * SSM_STATE].astype(BF16)
        cb = lax.dot_general(cg, bg, (((1,), (1,)), ((), ())), preferred_element_type=F32)
        sl = slice(g * gw, (g + 1) * gw)
        st = state[g]
        y_off = _dot(cg, st.astype(BF16)) * e_cs[:, sl]
        for k in range(hpg // 2):
            xp = xdt[:, g * gw + k * LANES:g * gw + (k + 1) * LANES]
            halves = (jnp.where(lo_half, xp, 0.0).astype(BF16), jnp.where(lo_half, 0.0, xp).astype(BF16))
            y = y_off[:, k * LANES:(k + 1) * LANES]
            for hh in range(2):
                h = g * hpg + 2 * k + hh
                seg = a_cs[:, h:h + 1] - a_cs_t[h:h + 1, :]
                m = (cb * jnp.exp(jnp.where(causal, seg, -jnp.inf))).astype(BF16)
                y = y + _dot(m, halves[hh])
            y_parts.append(y)
        xw = (dec[:, sl] * xdt[:, sl]).astype(BF16)
        upd = lax.dot_general(bg, xw, (((0,), (0,)), ((), ())), preferred_element_type=F32)
        state[g] = chunk_dec[:, sl] * st + upd

    y = jnp.concatenate(y_parts, axis=1) + dexp_ref[...] * xs
    zz = z_ref[...]
    y = y * (zz * _sigmoid(zz))
    outs = []
    for g in range(SSM_GROUPS):
        sl = slice(g * gw, (g + 1) * gw)
        outs.append(_rms(y[:, sl], nw_ref[:, sl]))
    co_ref[...] = jnp.concatenate(outs, axis=1).astype(BF16)


def _mixer_call(bmix, z, xbc, sm, dt_t, scw, cw, cb, dtb, alog, dexp, nw, b, s):
    L = SSD_L
    nc = s // L
    tril = jnp.asarray(np.tril(np.ones((L, L), np.float32)), BF16)
    triu = jnp.asarray(np.triu(np.ones((L, L), np.float32)), BF16)
    eh = jnp.asarray(np.repeat(np.eye(SSM_HEADS, dtype=np.float32), SSM_HEAD_DIM, axis=1), BF16)
    row = lambda w: pl.BlockSpec((L, w), lambda bb, c: (bb * nc + c, 0))
    consts = [scw, cw, cb.reshape(1, -1), dtb.reshape(1, -1), dtb.reshape(-1, 1),
              alog.reshape(1, -1), alog.reshape(-1, 1), dexp.reshape(1, -1), nw.reshape(1, -1),
              tril, triu, eh]
    return pl.pallas_call(
        _mixer_kernel,
        grid=(b, nc),
        in_specs=[row(3 * B_WIDTH), row(SSM_INNER), row(SSM_XBC), row(SMALL_W),
                  pl.BlockSpec((1, SSM_HEADS, L), lambda bb, c: (bb, 0, c))]
                 + [_const_spec(c.shape) for c in consts],
        out_specs=[row(B_WIDTH), row(SSM_INNER)],
        out_shape=[jax.ShapeDtypeStruct((b * s, B_WIDTH), BF16),
                   jax.ShapeDtypeStruct((b * s, SSM_INNER), BF16)],
        scratch_shapes=[pltpu.VMEM((L + SUBLANES, B_WIDTH), F32),
                        pltpu.VMEM((L + SUBLANES, SSM_XBC), F32),
                        pltpu.VMEM((SSM_GROUPS, SSM_STATE, SSM_INNER // SSM_GROUPS), F32)],
        compiler_params=_params(("parallel", "arbitrary")),
        name="conv_ssd_mixers",
    )(bmix, z, xbc, sm, dt_t, *consts)


def _dense_kernel(x_ref, a_ref, b_ref, c_ref, p_ref, g_post_ref, g_pre_ref, g_fpost_ref,
                  woa_ref, wob_ref, woc_ref, wg_ref, wu_ref, wd_ref, wpg_ref, wpp_ref, o_ref):
    subs = [slice(r * TM_SUB, (r + 1) * TM_SUB) for r in range(ROW_SUBTILES)]
    mix = [_dot(a_ref[rows, :], woa_ref[...]) + _dot(b_ref[rows, :], wob_ref[...])
           + _dot(c_ref[rows, :], woc_ref[...]) for rows in subs]
    ple = [_dot(p_ref[rows, :].astype(BF16), wpp_ref[...]) for rows in subs]
    x = [x_ref[rows, :] + _rms(m, g_post_ref[...]) for rows, m in zip(subs, mix)]
    h = [_rms(xx, g_pre_ref[...]).astype(BF16) for xx in x]
    gate = [_dot(hh, wg_ref[...]) for hh in h]
    up = [_dot(hh, wu_ref[...]) for hh in h]
    act = [(g * _sigmoid(g) * u).astype(BF16) for g, u in zip(gate, up)]
    f = [_dot(aa, wd_ref[...]) for aa in act]
    x = [xx + _rms(ff, g_fpost_ref[...]) for xx, ff in zip(x, f)]
    pg = [_sigmoid(_dot(xx.astype(BF16), wpg_ref[...])) for xx in x]
    for rows, xx, g, e in zip(subs, x, pg, ple):
        o_ref[rows, :] = xx + g * e


def _dense_call(x2, a, b, c, p2, g_post, g_pre, g_fpost, woa, wob, woc, wg, wu, wd, wpg, wpp):
    t = x2.shape[0]
    tm = TM_OUT
    row = lambda w: pl.BlockSpec((tm, w), lambda r: (r, 0))
    consts = [g_post, g_pre, g_fpost, woa, wob, woc, wg, wu, wd, wpg, wpp]
    return pl.pallas_call(
        _dense_kernel,
        grid=(t // tm,),
        in_specs=[row(D_MODEL), row(a.shape[1]), row(b.shape[1]), row(c.shape[1]), row(PLE_DIM)]
                 + [_const_spec(w.shape) for w in consts],
        out_specs=row(D_MODEL),
        out_shape=jax.ShapeDtypeStruct((t, D_MODEL), F32),
        compiler_params=_params(("parallel",)),
        name="outproj_ffn_ple",
    )(x2, a, b, c, p2, *consts)


def kernel(x, p, pre_mix_norm, post_mix_norm, pre_ffn_norm, post_ffn_norm, w_in, kv_norm, idx_k_norm_g, idx_k_norm_b, w_uk, w_uv, rel_bias, short_conv_w, ssm_conv_w, ssm_conv_b, ssm_dt_bias, ssm_a_log, ssm_d, ssm_norm, w_out, w_ffn_gate, w_ffn_up, w_ffn_down, w_ple_proj, w_ple_gate):
    bsz, s, d = x.shape
    assert d == D_MODEL and s % (IN_SUBTILES * KT) == 0 and s % QB == 0 and s % SSD_L == 0
    assert (bsz * s) % TM_OUT == 0
    t = bsz * s
    k_top = min(TOPK_MAX, s // 4)

    bucket_np, near, far_bucket = _bucket_tiles()
    bias_tiles = _bias_tiles_call(rel_bias.astype(F32), jnp.asarray(bucket_np), far_bucket)
    tri = jnp.asarray(np.tril(np.ones((KT, KT), np.float32)), BF16)
    row1 = lambda v: v.reshape(1, -1).astype(F32)
    vone_np = np.zeros((A_HEADS, V_ROWS, 1), np.float32)
    vone_np[:, A_V_DIM] = 1.0
    vone = jnp.asarray(vone_np.reshape(A_HEADS * V_ROWS, 1))

    x2 = x.reshape(t, d)
    for i in range(DEPTH):
        wparts = _wprep_call(w_in[i].astype(F32))
        wuv_t = jnp.transpose(w_uv[i], (1, 2, 0)).astype(BF16)
        wuv_t = jnp.pad(wuv_t, ((0, 0), (0, V_ROWS - A_V_DIM), (0, 0))).reshape(A_HEADS * V_ROWS, A_KV_RANK)
        wq, wckv, wiq, wsm, wb, wz, wxbc, wdt = wparts
        wiw_t = wsm[:, IW_OFF:IW_OFF + SUBLANES].T
        q_t, ckv, vt4, iq_t, iw_t, ik, bmix, z, xbc, dtp = _inproj_call(
            x2, row1(pre_mix_norm[i]), row1(kv_norm[i]), row1(idx_k_norm_g[i]), row1(idx_k_norm_b[i]),
            wq.T, wckv, wiq.T, wsm, wiw_t, wb, wz, wxbc, wdt, wuv_t, vone, s // KT)

        a_out = _attn_call(
            q_t, iq_t, iw_t, ckv.reshape(bsz, s, A_KV_RANK), vt4, ik.reshape(bsz, s, IDX_DIM),
            jnp.transpose(w_uk[i], (1, 0, 2)).astype(BF16),
            bias_tiles, tri, k_top=k_top, near=near)

        dt_t = dtp[:, :SSM_HEADS].reshape(bsz, s, SSM_HEADS).transpose(0, 2, 1)
        b_out, c_out = _mixer_call(
            bmix, z, xbc, dtp, dt_t, short_conv_w[i].astype(F32), ssm_conv_w[i].astype(F32),
            ssm_conv_b[i].astype(F32), ssm_dt_bias[i].astype(F32), ssm_a_log[i].astype(F32),
            jnp.repeat(ssm_d[i].astype(F32), SSM_HEAD_DIM), ssm_norm[i].astype(F32), bsz, s)

        wo = w_out[i].astype(BF16)
        na, nbw = A_HEADS * A_V_DIM, A_HEADS * A_V_DIM + B_WIDTH
        x2 = _dense_call(
            x2, a_out, b_out, c_out, p[i].reshape(t, PLE_DIM),
            row1(post_mix_norm[i]), row1(pre_ffn_norm[i]), row1(post_ffn_norm[i]),
            wo[:na], wo[na:nbw], wo[nbw:], w_ffn_gate[i].astype(BF16), w_ffn_up[i].astype(BF16),
            w_ffn_down[i].astype(BF16), w_ple_gate[i].astype(BF16), w_ple_proj[i].astype(BF16))
    return x2.reshape(bsz, s, d)
```

```python
import functools
import math

import numpy as np
import jax
import jax.numpy as jnp
from jax import lax
from jax.experimental import pallas as pl
from jax.experimental.pallas import tpu as pltpu

F32, BF16, I32 = jnp.float32, jnp.bfloat16, jnp.int32

D_MODEL = 1024
DEPTH = 2
CHUNK = 64
A_HEADS = 8
A_QK_DIM = 64
A_V_DIM = 64
A_KV_RANK = 256
IDX_HEADS = 4
IDX_DIM = 64
TOPK_MAX = 256
REL_BUCKETS = 32
REL_MAX_DIST = 1024
B_WIDTH = 512
SHORT_CONV = 3
SSM_HEADS = 16
SSM_HEAD_DIM = 64
SSM_INNER = SSM_HEADS * SSM_HEAD_DIM
SSM_GROUPS = 2
SSM_STATE = 128
SSM_CONV = 4
SSM_XBC = SSM_INNER + 2 * SSM_GROUPS * SSM_STATE
MIX_WIDTH = A_HEADS * A_V_DIM + B_WIDTH + SSM_INNER
D_FF = -(-8 * D_MODEL // (3 * 256)) * 256
PLE_DIM = 256
NORM_EPS = 1e-6
IN_SPLITS = (A_HEADS * A_QK_DIM, A_KV_RANK, IDX_HEADS * IDX_DIM, IDX_DIM, IDX_HEADS,
             B_WIDTH, B_WIDTH, B_WIDTH, SSM_INNER, SSM_XBC, SSM_HEADS)
IN_OFFSETS = tuple(int(v) for v in np.concatenate([[0], np.cumsum(IN_SPLITS)]))

LANES = 128
SUBLANES = 8
VMEM_LIMIT_BYTES = 56 * 1024 * 1024

QB = 2 * LANES
QGROUPS = QB // LANES
KT = 256
SSD_L = 128
IN_SUBTILES = 2
TM_SUB = 256
ROW_SUBTILES = 2
TM_OUT = TM_SUB * ROW_SUBTILES
SMALL_W = LANES
IW_OFF = IDX_DIM
INT_MIN = np.int32(-2 ** 31)
LOG2E = math.log2(math.e)
BF16_ROWS = 2 * SUBLANES
V_ROWS = A_V_DIM + BF16_ROWS

def _const_spec(shape):
    nd = len(shape)
    return pl.BlockSpec(shape, lambda *_: (0,) * nd, pipeline_mode=pl.Buffered(1))


def _params(sem):
    return pltpu.CompilerParams(dimension_semantics=sem, vmem_limit_bytes=VMEM_LIMIT_BYTES)


def _rms(x, g):
    return x * lax.rsqrt(jnp.mean(x * x, axis=-1, keepdims=True) + NORM_EPS) * g


def _sigmoid(x):
    return 1.0 / (1.0 + jnp.exp(-x))


def _split3(v):
    hi = v.astype(BF16)
    r1 = v - hi.astype(F32)
    mid = r1.astype(BF16)
    lo = (r1 - mid.astype(F32)).astype(BF16)
    return hi, mid, lo


def _dot(a, b):
    return jnp.dot(a, b, preferred_element_type=F32)


def _t5_bucket_np(rel):
    half = REL_BUCKETS // 2
    max_exact = half // 2
    ret = np.where(rel > 0, half, 0)
    n = np.abs(rel)
    nf = np.maximum(n, 1).astype(np.float32)
    large = max_exact + (np.log(nf / np.float32(max_exact)) / np.float32(math.log(REL_MAX_DIST / max_exact))
                         * np.float32(half - max_exact)).astype(np.int32)
    large = np.minimum(large, half - 1)
    return (ret + np.where(n < max_exact, n, large)).astype(np.int32)


@functools.lru_cache(maxsize=None)
def _bucket_tiles():
    kr = np.arange(KT)[:, None]
    qc = np.arange(QB)[None, :]
    tiles, near = [], None
    for o in range(64):
        t = _t5_bucket_np(kr - qc - o * QB)
        tiles.append(t)
    for o in range(63, -1, -1):
        if not (np.all(tiles[o] == tiles[63][0, 0])):
            near = o + 1
            break
    assert near is not None and 0 < near < 63
    return np.stack(tiles[:near + 1]), near, int(tiles[63][0, 0])


def _bias_kernel(rb_ref, bk_ref, out_ref, *, far_bucket):
    bk = bk_ref[0]
    for h in range(A_HEADS):
        acc = jnp.zeros(bk.shape, F32)
        for b in range(REL_BUCKETS):
            acc = jnp.where(bk == b, (rb_ref[b, h] - rb_ref[far_bucket, h]) * LOG2E, acc)
        out_ref[0, :, h * QB:(h + 1) * QB] = acc


def _bias_tiles_call(rel_bias, bucket_tiles, far_bucket):
    n = bucket_tiles.shape[0]
    return pl.pallas_call(
        functools.partial(_bias_kernel, far_bucket=far_bucket),
        grid=(n,),
        in_specs=[pl.BlockSpec(memory_space=pltpu.SMEM),
                  pl.BlockSpec((1, KT, QB), lambda o: (o, 0, 0))],
        out_specs=pl.BlockSpec((1, KT, A_HEADS * QB), lambda o: (o, 0, 0)),
        out_shape=jax.ShapeDtypeStruct((n, KT, A_HEADS * QB), F32),
        compiler_params=_params(("arbitrary",)),
        name="rel_bias_tiles",
    )(rel_bias, bucket_tiles)


def _wprep_kernel(w_ref, wq_out, wckv_out, wiq_out, wsm_out, wb_out, wz_out, wxbc_out, wdt_out):
    o_q, o_ckv, o_iq, o_ik, _, o_bg, _, _, o_z, o_xbc, o_dt, o_end = IN_OFFSETS
    wq_out[...] = w_ref[:, o_q:o_ckv].astype(BF16)
    wckv_out[...] = w_ref[:, o_ckv:o_iq].astype(BF16)
    wiq_out[...] = w_ref[:, o_iq:o_ik].astype(BF16)
    wsm_out[...] = w_ref[:, o_ik:o_ik + SMALL_W].astype(BF16)
    tail = w_ref[:, o_bg:o_end].astype(BF16)
    wb_out[...] = tail[:, :o_z - o_bg]
    wz_out[...] = tail[:, o_z - o_bg:o_xbc - o_bg]
    wxbc_out[...] = tail[:, o_xbc - o_bg:o_dt - o_bg]
    wdt_out[...] = jnp.zeros(wdt_out.shape, BF16)
    wdt_out[:, :o_end - o_dt] = tail[:, o_dt - o_bg:]


def _wprep_call(w):
    d, n = w.shape
    tr = 256
    widths = [IN_SPLITS[0], IN_SPLITS[1], IN_SPLITS[2], SMALL_W, 3 * B_WIDTH, SSM_INNER, SSM_XBC, SMALL_W]
    return pl.pallas_call(
        _wprep_kernel,
        grid=(d // tr,),
        in_specs=[pl.BlockSpec((tr, n), lambda r: (r, 0))],
        out_specs=[pl.BlockSpec((tr, wd), lambda r: (r, 0)) for wd in widths],
        out_shape=[jax.ShapeDtypeStruct((d, wd), BF16) for wd in widths],
        compiler_params=_params(("parallel",)),
        name="inproj_weight_prep",
    )(w)


def _dot_nt(a, b):
    return lax.dot_general(a, b, (((1,), (1,)), ((), ())), preferred_element_type=F32)


def _inproj_kernel(x_ref, g_ref, kvn_ref, ikg_ref, ikb_ref,
                   wqt_ref, wckv_ref, wiqt_ref, wsm_ref, wiwt_ref, wb_ref, wz_ref, wxbc_ref, wdt_ref,
                   wuv_ref, vone_ref,
                   qt_out, ckv_out, vt_out, iqt_out, iwt_out, ik_out, b_out, z_out, xbc_out, dt_out):
    subs = [slice(k * KT, (k + 1) * KT) for k in range(IN_SUBTILES)]
    h = [_rms(x_ref[rows, :], g_ref[...]).astype(BF16) for rows in subs]
    ckv = [_rms(_dot(hh, wckv_ref[...]), kvn_ref[...]).astype(BF16) for hh in h]
    sm = [_dot(hh, wsm_ref[...]) for hh in h]
    for k, rows in enumerate(subs):
        ckv_out[rows, :] = ckv[k]
        qt_out[k] = _dot_nt(wqt_ref[...], h[k]).astype(BF16)
        iqt_out[k] = (_dot_nt(wiqt_ref[...], h[k]) * (IDX_DIM ** -0.5)).astype(BF16)
        iwt_out[k] = _dot_nt(wiwt_ref[...], h[k]) * (IDX_HEADS ** -0.5)
    for k, rows in enumerate(subs):
        vt_out[0, k] = (_dot_nt(wuv_ref[...], ckv[k]) + vone_ref[...]).astype(BF16)
        ik = sm[k][:, :IDX_DIM]
        mu = jnp.mean(ik, axis=-1, keepdims=True)
        var = jnp.mean(jnp.square(ik - mu), axis=-1, keepdims=True)
        ik_out[rows, :] = ((ik - mu) * lax.rsqrt(var + NORM_EPS) * ikg_ref[...] + ikb_ref[...]).astype(BF16)
    for k, rows in enumerate(subs):
        b_out[rows, :] = _dot(h[k], wb_ref[...])
    for k, rows in enumerate(subs):
        z_out[rows, :] = _dot(h[k], wz_ref[...])
    for k, rows in enumerate(subs):
        xbc_out[rows, :] = _dot(h[k], wxbc_ref[...])
        dt_out[rows, :] = _dot(h[k], wdt_ref[...])


def _inproj_call(x2, g, kvn, ikg, ikb, wqt, wckv, wiqt, wsm, wiwt, wb, wz, wxbc, wdt, wuv, vone, n_kt):
    t = x2.shape[0]
    assert KT == QB and n_kt % IN_SUBTILES == 0
    tm = IN_SUBTILES * KT
    steps_per_seq = n_kt // IN_SUBTILES
    row = lambda w: pl.BlockSpec((tm, w), lambda r: (r, 0))
    rows = lambda w, dt: (row(w), jax.ShapeDtypeStruct((t, w), dt))
    qblk = lambda w, dt: (pl.BlockSpec((IN_SUBTILES, w, QB), lambda r: (r, 0, 0)),
                          jax.ShapeDtypeStruct((t // QB, w, QB), dt))
    hv = wuv.shape[0]
    outs = [qblk(wqt.shape[0], BF16), rows(wckv.shape[1], BF16),
            (pl.BlockSpec((1, IN_SUBTILES, hv, KT), lambda r: (r // steps_per_seq, r % steps_per_seq, 0, 0)),
             jax.ShapeDtypeStruct((t // (n_kt * KT), n_kt, hv, KT), BF16)),
            qblk(wiqt.shape[0], BF16), qblk(wiwt.shape[0], F32), rows(IDX_DIM, BF16),
            rows(wb.shape[1], F32), rows(wz.shape[1], F32), rows(wxbc.shape[1], F32),
            rows(SMALL_W, F32)]
    consts = [g, kvn, ikg, ikb, wqt, wckv, wiqt, wsm, wiwt, wb, wz, wxbc, wdt, wuv, vone]
    return pl.pallas_call(
        _inproj_kernel,
        grid=(t // tm,),
        in_specs=[row(D_MODEL)] + [_const_spec(c.shape) for c in consts],
        out_specs=[spec for spec, _ in outs],
        out_shape=[shape for _, shape in outs],
        compiler_params=_params(("parallel",)),
        name="inproj",
    )(x2, *consts)


def _bit_transpose32(words):
    a = list(words)
    m, j = 0x0000FFFF, 16
    while j:
        k = 0
        while k < 32:
            t = (a[k] ^ lax.shift_right_logical(a[k + j], np.int32(j))) & np.int32(m)
            a[k] = a[k] ^ t
            a[k + j] = a[k + j] ^ (t << j)
            k = (k + j + 1) & ~j
        j >>= 1
        m = (m ^ (m << j)) & 0xFFFFFFFF
    return a


def _attn_kernel(q_ref, iq_ref, iw_ref, ckv_ref, vt_ref, ik_ref, wuk_ref, bias_ref, tri_ref,
                 o_ref, keys_ref, planes_ref, qlat_ref, acc_ref, p0_ref, p1_ref, lg0_ref, lg1_ref,
                 *, k_top, near):
    i = pl.program_id(1)
    nt = ((i + 1) * QB + KT - 1) // KT
    lane = lax.broadcasted_iota(I32, (1, QB), 1)
    qchunk = (i * QB + lane) >> int(math.log2(CHUNK))
    krow = lax.broadcasted_iota(I32, (KT, QB), 0)

    q_t = q_ref[0]
    for h in range(A_HEADS):
        ql = _dot(wuk_ref[h], q_t[h * A_QK_DIM:(h + 1) * A_QK_DIM, :])
        qlat_ref[:, h * QB:(h + 1) * QB] = (ql * (A_QK_DIM ** -0.5 * LOG2E)).astype(BF16)

    iq_t = iq_ref[0]
    iw = iw_ref[0][:IDX_HEADS, :]

    def score_tile(j, carry, masked):
        r0 = pl.multiple_of(j * KT, KT)
        ikt = ik_ref[0, pl.ds(r0, KT), :]
        s = jnp.zeros((KT, QB), F32)
        for h in range(IDX_HEADS):
            raw = _dot(ikt, iq_t[h * IDX_DIM:(h + 1) * IDX_DIM, :])
            s = s + jnp.maximum(raw, 0.0) * iw[h:h + 1, :]
        if masked:
            adm = ((r0 + krow) >> int(math.log2(CHUNK))) <= qchunk
            s = jnp.where(adm, s, -jnp.inf)
        bits = lax.bitcast_convert_type(s, I32)
        bits = jnp.where(bits == INT_MIN, 0, bits)
        key = bits ^ ((bits >> 31) & np.int32(0x7FFFFFFF))
        keys_ref[pl.ds(r0, KT), :] = key
        ukey = key ^ INT_MIN
        planes = _bit_transpose32([ukey[v * SUBLANES:(v + 1) * SUBLANES, :] for v in range(32)])
        p0 = pl.multiple_of(j * SUBLANES, SUBLANES)
        for b in range(32):
            planes_ref[b, pl.ds(p0, SUBLANES), :] = planes[31 - b]
        return carry

    @pl.when(i == 0)
    def _():
        planes_ref[...] = jnp.zeros(planes_ref.shape, I32)

    def score_pair(u, carry):
        score_tile(2 * u, 0, masked=False)
        score_tile(2 * u + 1, 0, masked=False)
        return carry

    lax.fori_loop(0, (nt - 1) // 2, score_pair, 0)

    @pl.when((nt - 1) % 2 == 1)
    def _():
        score_tile(nt - 2, 0, masked=False)

    score_tile(nt - 1, 0, masked=True)

    n_prow = planes_ref.shape[1]
    prow = lax.broadcasted_iota(I32, (n_prow, QB), 0)
    kf = float(k_top)

    def search_step(k, carry):
        live, c_gt, thr_u = carry
        b = 31 - k
        x = live & planes_ref[b]
        pc = lax.population_count(x)
        parts = [pc[r * SUBLANES:(r + 1) * SUBLANES, :] for r in range(n_prow // SUBLANES)]
        while len(parts) > 1:
            parts = [parts[k2] + parts[k2 + 1] for k2 in range(0, len(parts) - 1, 2)] + (
                [parts[-1]] if len(parts) % 2 else [])
        cnt = c_gt + jnp.sum(parts[0].astype(F32), axis=0, keepdims=True)
        take = cnt >= kf
        live = jnp.where(take, x, live ^ x)
        c_gt = jnp.where(take, c_gt, cnt)
        thr_u = thr_u | jnp.where(take, jnp.left_shift(jnp.int32(1), b), np.int32(0))
        return live, c_gt, thr_u

    _, c_gt, thr_u = lax.fori_loop(
        0, 32, search_step,
        (jnp.where(prow < nt * SUBLANES, np.int32(-1), np.int32(0)),
         jnp.zeros((1, QB), F32), jnp.zeros((1, QB), I32)))
    thr = thr_u ^ INT_MIN
    need = kf - c_gt

    p_refs, lg_refs = (p0_ref, p1_ref), (lg0_ref, lg1_ref)
    n_cols = A_HEADS * QGROUPS
    acc_ref[...] = jnp.zeros(acc_ref.shape, F32)
    p1_ref[...] = jnp.zeros(p1_ref.shape, BF16)

    def stage_scores(j, cnt_eq, slot, far=False):
        r0 = pl.multiple_of(j * KT, KT)
        kt = keys_ref[pl.ds(r0, KT), :]
        eq = kt == thr
        eqf = jnp.where(eq, 1.0, 0.0)
        rank = _dot(tri_ref[...], eqf.astype(BF16)) + cnt_eq
        sel = (kt > thr) | (eq & (rank <= need))
        if not far:
            sel = sel & (((r0 + krow) >> int(math.log2(CHUNK))) <= qchunk)
            bias = bias_ref[jnp.minimum(i - j * (KT // QB), near)]
        logits = _dot(ckv_ref[0, pl.ds(r0, KT), :], qlat_ref[...])
        tmax = [[] for _ in range(QGROUPS)]
        for h in range(A_HEADS):
            for g in range(QGROUPS):
                sl = slice(h * QB + g * LANES, h * QB + (g + 1) * LANES)
                lh = logits[:, sl] if far else logits[:, sl] + bias[:, sl]
                lh = jnp.where(sel[:, g * LANES:(g + 1) * LANES], lh, -jnp.inf)
                lg_refs[slot][:, sl] = lh
                tmax[g].append(jnp.max(lh, axis=0, keepdims=True))
        tmax = tuple(jnp.concatenate(rows, axis=0) for rows in tmax)
        return tmax, cnt_eq + jnp.sum(eqf, axis=0, keepdims=True)

    def stage_probs(ms, tmax, slot):
        new_ms, alphas = [], []
        for g in range(QGROUPS):
            m_new = jnp.maximum(ms[g], tmax[g])
            m_safe = jnp.where(m_new == -jnp.inf, 0.0, m_new)
            alphas.append(jnp.exp2(ms[g] - m_safe))
            new_ms.append(m_new)
            for h in range(A_HEADS):
                sl = slice(h * QB + g * LANES, h * QB + (g + 1) * LANES)
                p_refs[slot][:, sl] = jnp.exp2(lg_refs[slot][:, sl] - m_safe[h:h + 1, :]).astype(BF16)
        return tuple(new_ms), tuple(alphas)

    def stage_values(alphas, jp, slot):
        for h in range(A_HEADS):
            vs = slice(h * V_ROWS, (h + 1) * V_ROWS)
            pv = _dot(vt_ref[0, jp, vs, :], p_refs[slot][:, h * QB:(h + 1) * QB])
            alpha = jnp.concatenate([a[h:h + 1, :] for a in alphas], axis=1)
            acc_ref[vs, :] = acc_ref[vs, :] * alpha + pv

    def trip(t, carry, slot, far=False):
        ms, tmax, cnt_eq, alphas = carry
        stage_values(alphas, jnp.maximum(t - 1, 0), 1 - slot)
        ms, alphas = stage_probs(ms, tmax, slot)
        tmax, cnt_eq = stage_scores(t + 1, cnt_eq, 1 - slot, far)
        return ms, tmax, cnt_eq, alphas

    def finish(t, carry, slot):
        ms, tmax, _, alphas = carry
        stage_values(alphas, jnp.maximum(t - 1, 0), 1 - slot)
        _, alphas = stage_probs(ms, tmax, slot)
        stage_values(alphas, t, slot)

    def pair(u, carry, far=False):
        return trip(2 * u + 1, trip(2 * u, carry, 0, far), 1, far)

    tmax0, cnt_eq0 = stage_scores(0, jnp.zeros((1, QB), F32), 0)
    init = (tuple(jnp.full((A_HEADS, LANES), -jnp.inf, F32) for _ in range(QGROUPS)), tmax0, cnt_eq0,
            tuple(jnp.ones((A_HEADS, LANES), F32) for _ in range(QGROUPS)))
    n_trips = nt - 1
    n_far_pairs = jnp.maximum(i * (QB // KT) - near, 0) // 2
    carry = lax.fori_loop(0, n_far_pairs, functools.partial(pair, far=True), init)
    carry = lax.fori_loop(n_far_pairs, n_trips // 2, pair, carry)

    @pl.when(n_trips % 2 == 0)
    def _():
        finish(nt - 1, carry, 0)

    @pl.when(n_trips % 2 == 1)
    def _():
        finish(nt - 1, trip(nt - 2, carry, 0), 1)

    a_t = jnp.concatenate(
        [acc_ref[h * V_ROWS:h * V_ROWS + A_V_DIM, :]
         * (1.0 / acc_ref[h * V_ROWS + A_V_DIM:h * V_ROWS + A_V_DIM + 1, :]) for h in range(A_HEADS)],
        axis=0)
    o_ref[...] = a_t.T.astype(BF16)


def _attn_call(q_t, iq_t, iw_t, ckv3, vt4, ik3, wuk, bias_tiles, tri, *, k_top, near):
    b, s = ckv3.shape[0], ckv3.shape[1]
    nb = s // QB
    n_kt = s // KT
    hv = A_HEADS * A_V_DIM
    hvr = A_HEADS * V_ROWS
    kern = functools.partial(_attn_kernel, k_top=k_top, near=near)
    row = lambda w: pl.BlockSpec((QB, w), lambda bb, i: (bb * nb + i, 0))
    qblk = lambda a: pl.BlockSpec((1, a.shape[1], QB), lambda bb, i: (bb * nb + i, 0, 0))
    return pl.pallas_call(
        kern,
        grid=(b, nb),
        in_specs=[
            qblk(q_t), qblk(iq_t), qblk(iw_t),
            pl.BlockSpec((1, s, A_KV_RANK), lambda bb, i: (bb, 0, 0)),
            pl.BlockSpec((1, n_kt, hvr, KT), lambda bb, i: (bb, 0, 0, 0)),
            pl.BlockSpec((1, s, IDX_DIM), lambda bb, i: (bb, 0, 0)),
            _const_spec(wuk.shape), _const_spec(bias_tiles.shape), _const_spec(tri.shape),
        ],
        out_specs=row(hv),
        out_shape=jax.ShapeDtypeStruct((b * s, hv), BF16),
        scratch_shapes=[pltpu.VMEM((s, QB), I32), pltpu.VMEM((32, s // 32, QB), I32),
                        pltpu.VMEM((A_KV_RANK, A_HEADS * QB), BF16),
                        pltpu.VMEM((hvr, QB), F32),
                        pltpu.VMEM((KT, A_HEADS * QB), BF16), pltpu.VMEM((KT, A_HEADS * QB), BF16),
                        pltpu.VMEM((KT, A_HEADS * QB), F32), pltpu.VMEM((KT, A_HEADS * QB), F32)],
        compiler_params=_params(("parallel", "arbitrary")),
        name="dsa_attention",
    )(q_t, iq_t, iw_t, ckv3, vt4, ik3, wuk, bias_tiles, tri)


def _softplus(x):
    return jnp.maximum(x, 0.0) + jnp.log1p(jnp.exp(-jnp.abs(x)))


def _mixer_kernel(bm_ref, z_ref, xbc_ref, dtp_ref, dtt_ref, scw_ref, cw_ref, cb_ref,
                  dtb_r_ref, dtb_c_ref, alog_r_ref, alog_c_ref, dexp_ref, nw_ref,
                  tril_ref, triu_ref, eh_ref,
                  bo_ref, co_ref, ubuf, xbuf, state):
    L = SSD_L
    P = SUBLANES
    gw = SSM_INNER // SSM_GROUPS

    @pl.when(pl.program_id(1) == 0)
    def _():
        ubuf[0:P, :] = jnp.zeros((P, B_WIDTH), F32)
        xbuf[0:P, :] = jnp.zeros((P, SSM_XBC), F32)
        state[...] = jnp.zeros(state.shape, F32)

    bm = bm_ref[...]
    u = bm[:, B_WIDTH:2 * B_WIDTH] * bm[:, 2 * B_WIDTH:3 * B_WIDTH]
    ubuf[P:P + L, :] = u
    conv = scw_ref[SHORT_CONV - 1:SHORT_CONV, :] * u
    for j in range(SHORT_CONV - 1):
        off = P - (SHORT_CONV - 1) + j
        conv = conv + scw_ref[j:j + 1, :] * ubuf[off:off + L, :]
    bo_ref[...] = (bm[:, :B_WIDTH] * conv).astype(BF16)
    ubuf[0:P, :] = u[L - P:L, :]

    xr = xbc_ref[...]
    xbuf[P:P + L, :] = xr
    xc = cw_ref[SSM_CONV - 1:SSM_CONV, :] * xr + cb_ref[...]
    for j in range(SSM_CONV - 1):
        off = P - (SSM_CONV - 1) + j
        xc = xc + cw_ref[j:j + 1, :] * xbuf[off:off + L, :]
    xbuf[0:P, :] = xr[L - P:L, :]
    xa = xc * _sigmoid(xc)
    xs = xa[:, :SSM_INNER]

    dt = _softplus(dtp_ref[:, :SSM_HEADS] + dtb_r_ref[...])
    a = dt * (-jnp.exp(alog_r_ref[...]))
    a_cs = sum(_dot(tril_ref[...], part) for part in _split3(a))
    dt_t = _softplus(dtt_ref[0] + dtb_c_ref[...])
    a_t = dt_t * (-jnp.exp(alog_c_ref[...]))
    a_cs_t = sum(_dot(part, triu_ref[...]) for part in _split3(a_t))

    def expand(v):
        return sum(_dot(part, eh_ref[...]) for part in _split3(v))

    xdt = xs * expand(dt)
    e_cs = expand(jnp.exp(a_cs))
    dec = expand(jnp.exp(a_cs[L - 1:L, :] - a_cs))
    chunk_dec = e_cs[L - 1:L, :]

    row = lax.broadcasted_iota(I32, (L, L), 0)
    col = lax.broadcasted_iota(I32, (L, L), 1)
    causal = row >= col
    lo_half = lax.broadcasted_iota(I32, (L, LANES), 1) < SSM_HEAD_DIM
    hpg = SSM_HEADS // SSM_GROUPS

    y_parts = []
    for g in range(SSM_GROUPS):
        bg = xa[:, SSM_INNER + g * SSM_STATE:SSM_INNER + (g + 1) * SSM_STATE].astype(BF16)
        cg = xa[:, SSM_INNER + (SSM_GROUPS + g) * SSM_STATE:
                SSM_INNER + (SSM_GROUPS + g + 1) * SSM_STATE].astype(BF16)
        cb = lax.dot_general(cg, bg, (((1,), (1,)), ((), ())), preferred_element_type=F32)
        sl = slice(g * gw, (g + 1) * gw)
        st = state[g]
        y_off = _dot(cg, st.astype(BF16)) * e_cs[:, sl]
        for k in range(hpg // 2):
            xp = xdt[:, g * gw + k * LANES:g * gw + (k + 1) * LANES]
            halves = (jnp.where(lo_half, xp, 0.0).astype(BF16), jnp.where(lo_half, 0.0, xp).astype(BF16))
            y = y_off[:, k * LANES:(k + 1) * LANES]
            for hh in range(2):
                h = g * hpg + 2 * k + hh
                seg = a_cs[:, h:h + 1] - a_cs_t[h:h + 1, :]
                m = (cb * jnp.exp(jnp.where(causal, seg, -jnp.inf))).astype(BF16)
                y = y + _dot(m, halves[hh])
            y_parts.append(y)
        xw = (dec[:, sl] * xdt[:, sl]).astype(BF16)
        upd = lax.dot_general(bg, xw, (((0,), (0,)), ((), ())), preferred_element_type=F32)
        state[g] = chunk_dec[:, sl] * st + upd

    y = jnp.concatenate(y_parts, axis=1) + dexp_ref[...] * xs
    zz = z_ref[...]
    y = y * (zz * _sigmoid(zz))
    outs = []
    for g in range(SSM_GROUPS):
        sl = slice(g * gw, (g + 1) * gw)
        outs.append(_rms(y[:, sl], nw_ref[:, sl]))
    co_ref[...] = jnp.concatenate(outs, axis=1).astype(BF16)


def _mixer_call(bmix, z, xbc, sm, dt_t, scw, cw, cb, dtb, alog, dexp, nw, b, s):
    L = SSD_L
    nc = s // L
    tril = jnp.asarray(np.tril(np.ones((L, L), np.float32)), BF16)
    triu = jnp.asarray(np.triu(np.ones((L, L), np.float32)), BF16)
    eh = jnp.asarray(np.repeat(np.eye(SSM_HEADS, dtype=np.float32), SSM_HEAD_DIM, axis=1), BF16)
    row = lambda w: pl.BlockSpec((L, w), lambda bb, c: (bb * nc + c, 0))
    consts = [scw, cw, cb.reshape(1, -1), dtb.reshape(1, -1), dtb.reshape(-1, 1),
              alog.reshape(1, -1), alog.reshape(-1, 1), dexp.reshape(1, -1), nw.reshape(1, -1),
              tril, triu, eh]
    return pl.pallas_call(
        _mixer_kernel,
        grid=(b, nc),
        in_specs=[row(3 * B_WIDTH), row(SSM_INNER), row(SSM_XBC), row(SMALL_W),
                  pl.BlockSpec((1, SSM_HEADS, L), lambda bb, c: (bb, 0, c))]
                 + [_const_spec(c.shape) for c in consts],
        out_specs=[row(B_WIDTH), row(SSM_INNER)],
        out_shape=[jax.ShapeDtypeStruct((b * s, B_WIDTH), BF16),
                   jax.ShapeDtypeStruct((b * s, SSM_INNER), BF16)],
        scratch_shapes=[pltpu.VMEM((L + SUBLANES, B_WIDTH), F32),
                        pltpu.VMEM((L + SUBLANES, SSM_XBC), F32),
                        pltpu.VMEM((SSM_GROUPS, SSM_STATE, SSM_INNER // SSM_GROUPS), F32)],
        compiler_params=_params(("parallel", "arbitrary")),
        name="conv_ssd_mixers",
    )(bmix, z, xbc, sm, dt_t, *consts)


def _ssd_chunk(xa, zz, dt_raw, dt_t_raw, state, dtb_r, dtb_c, alog_r, alog_c, dexp, nw, tril, triu, eh):
    L = SSD_L
    gw = SSM_INNER // SSM_GROUPS
    xs = xa[:, :SSM_INNER]
    dt = _softplus(dt_raw + dtb_r)
    a = dt * (-jnp.exp(alog_r))
    a_cs = sum(_dot(tril, part) for part in _split3(a))
    dt_t = _softplus(dt_t_raw + dtb_c)
    a_t = dt_t * (-jnp.exp(alog_c))
    a_cs_t = sum(_dot(part, triu) for part in _split3(a_t))

    def expand(v):
        return sum(_dot(part, eh) for part in _split3(v))

    xdt = xs * expand(dt)
    e_cs = expand(jnp.exp(a_cs))
    dec = expand(jnp.exp(a_cs[L - 1:L, :] - a_cs))
    chunk_dec = e_cs[L - 1:L, :]

    row = lax.broadcasted_iota(I32, (L, L), 0)
    col = lax.broadcasted_iota(I32, (L, L), 1)
    causal = row >= col
    lo_half = lax.broadcasted_iota(I32, (L, LANES), 1) < SSM_HEAD_DIM
    hpg = SSM_HEADS // SSM_GROUPS

    y_parts = []
    for g in range(SSM_GROUPS):
        bg = xa[:, SSM_INNER + g * SSM_STATE:SSM_INNER + (g + 1) * SSM_STATE].astype(BF16)
        cg = xa[:, SSM_INNER + (SSM_GROUPS + g) * SSM_STATE:
                SSM_INNER + (SSM_GROUPS + g + 1) * SSM_STATE].astype(BF16)
        cb = _dot_nt(cg, bg)
        sl = slice(g * gw, (g + 1) * gw)
        st = state[g]
        y_off = _dot(cg, st.astype(BF16)) * e_cs[:, sl]
        for k in range(hpg // 2):
            xp = xdt[:, g * gw + k * LANES:g * gw + (k + 1) * LANES]
            halves = (jnp.where(lo_half, xp, 0.0).astype(BF16), jnp.where(lo_half, 0.0, xp).astype(BF16))
            y = y_off[:, k * LANES:(k + 1) * LANES]
            for hh in range(2):
                h = g * hpg + 2 * k + hh
                seg = a_cs[:, h:h + 1] - a_cs_t[h:h + 1, :]
                m = (cb * jnp.exp(jnp.where(causal, seg, -jnp.inf))).astype(BF16)
                y = y + _dot(m, halves[hh])
            y_parts.append(y)
        xw = (dec[:, sl] * xdt[:, sl]).astype(BF16)
        upd = lax.dot_general(bg, xw, (((0,), (0,)), ((), ())), preferred_element_type=F32)
        state[g] = chunk_dec[:, sl] * st + upd

    y = jnp.concatenate(y_parts, axis=1) + dexp * xs
    y = y * (zz * _sigmoid(zz))
    return jnp.concatenate(
        [_rms(y[:, g * gw:(g + 1) * gw], nw[:, g * gw:(g + 1) * gw]) for g in range(SSM_GROUPS)], axis=1)


def _inproj_mixer_kernel(x_ref, g_ref, kvn_ref, ikg_ref, ikb_ref,
                         wqt_ref, wckv_ref, wiqt_ref, wsm_ref, wiwt_ref, wb_ref, wz_ref, wxbc_ref,
                         wdt_ref, wdtt_ref, wuv_ref, vone_ref,
                         scw_ref, cw_ref, cb_ref, dtb_r_ref, dtb_c_ref, alog_r_ref, alog_c_ref,
                         dexp_ref, nw_ref, tril_ref, triu_ref, eh_ref,
                         qt_out, ckv_out, vt_out, iqt_out, iwt_out, ik_out, bo_out, co_out,
                         ubuf, xbuf, state, *, steps_per_seq):
    P = SUBLANES
    tm = IN_SUBTILES * KT

    @pl.when(pl.program_id(0) % steps_per_seq == 0)
    def _():
        ubuf[0:P, :] = jnp.zeros((P, B_WIDTH), F32)
        xbuf[0:P, :] = jnp.zeros((P, SSM_XBC), F32)
        state[...] = jnp.zeros(state.shape, F32)

    subs = [slice(k * KT, (k + 1) * KT) for k in range(IN_SUBTILES)]
    h = [_rms(x_ref[rows, :], g_ref[...]).astype(BF16) for rows in subs]
    ssd_consts = (dtb_r_ref[...], dtb_c_ref[...], alog_r_ref[...], alog_c_ref[...], dexp_ref[...],
                  nw_ref[...], tril_ref[...], triu_ref[...], eh_ref[...])

    for k, rows in enumerate(subs):
        base = P + k * KT
        bm = _dot(h[k], wb_ref[...])
        xr = _dot(h[k], wxbc_ref[...])
        zz = _dot(h[k], wz_ref[...])
        dt_raw = _dot(h[k], wdt_ref[...])[:, :SSM_HEADS]
        dt_t_raw = _dot_nt(wdtt_ref[...], h[k])

        u = bm[:, B_WIDTH:2 * B_WIDTH] * bm[:, 2 * B_WIDTH:3 * B_WIDTH]
        ubuf[base:base + KT, :] = u
        conv = scw_ref[SHORT_CONV - 1:SHORT_CONV, :] * u
        for j in range(SHORT_CONV - 1):
            off = base - (SHORT_CONV - 1) + j
            conv = conv + scw_ref[j:j + 1, :] * ubuf[off:off + KT, :]
        bo_out[rows, :] = (bm[:, :B_WIDTH] * conv).astype(BF16)

        xbuf[base:base + KT, :] = xr
        xc = cw_ref[SSM_CONV - 1:SSM_CONV, :] * xr + cb_ref[...]
        for j in range(SSM_CONV - 1):
            off = base - (SSM_CONV - 1) + j
            xc = xc + cw_ref[j:j + 1, :] * xbuf[off:off + KT, :]
        xa = xc * _sigmoid(xc)
        for c in range(KT // SSD_L):
            cs = slice(c * SSD_L, (c + 1) * SSD_L)
            y = _ssd_chunk(xa[cs, :], zz[cs, :], dt_raw[cs, :], dt_t_raw[:, cs], state, *ssd_consts)
            co_out[rows.start + c * SSD_L:rows.start + (c + 1) * SSD_L, :] = y.astype(BF16)

    ubuf[0:P, :] = ubuf[tm:tm + P, :]
    xbuf[0:P, :] = xbuf[tm:tm + P, :]

    ckv = [_rms(_dot(hh, wckv_ref[...]), kvn_ref[...]).astype(BF16) for hh in h]
    sm = [_dot(hh, wsm_ref[...]) for hh in h]
    for k, rows in enumerate(subs):
        ckv_out[rows, :] = ckv[k]
        qt_out[k] = _dot_nt(wqt_ref[...], h[k]).astype(BF16)
        iqt_out[k] = (_dot_nt(wiqt_ref[...], h[k]) * (IDX_DIM ** -0.5)).astype(BF16)
        iwt_out[k] = _dot_nt(wiwt_ref[...], h[k]) * (IDX_HEADS ** -0.5)
        vt_out[0, k] = (_dot_nt(wuv_ref[...], ckv[k]) + vone_ref[...]).astype(BF16)
        ik = sm[k][:, :IDX_DIM]
        mu = jnp.mean(ik, axis=-1, keepdims=True)
        var = jnp.mean(jnp.square(ik - mu), axis=-1, keepdims=True)
        ik_out[rows, :] = ((ik - mu) * lax.rsqrt(var + NORM_EPS) * ikg_ref[...] + ikb_ref[...]).astype(BF16)


def _inproj_mixer_call(x2, norm_consts, proj_weights, wuv, vone, mixer_params, n_kt):
    t = x2.shape[0]
    assert KT == QB and n_kt % IN_SUBTILES == 0 and KT % SSD_L == 0
    tm = IN_SUBTILES * KT
    steps_per_seq = n_kt // IN_SUBTILES
    scw, cw, cb, dtb, alog, dexp, nw = mixer_params
    L = SSD_L
    tril = jnp.asarray(np.tril(np.ones((L, L), np.float32)), BF16)
    triu = jnp.asarray(np.triu(np.ones((L, L), np.float32)), BF16)
    eh = jnp.asarray(np.repeat(np.eye(SSM_HEADS, dtype=np.float32), SSM_HEAD_DIM, axis=1), BF16)
    mixer_consts = [scw, cw, cb.reshape(1, -1), dtb.reshape(1, -1), dtb.reshape(-1, 1),
                    alog.reshape(1, -1), alog.reshape(-1, 1), dexp.reshape(1, -1), nw.reshape(1, -1),
                    tril, triu, eh]
    wqt, wckv, wiqt, wiwt = proj_weights[0], proj_weights[1], proj_weights[2], proj_weights[4]
    row = lambda w: pl.BlockSpec((tm, w), lambda r: (r, 0))
    rows = lambda w, dt: (row(w), jax.ShapeDtypeStruct((t, w), dt))
    qblk = lambda w, dt: (pl.BlockSpec((IN_SUBTILES, w, QB), lambda r: (r, 0, 0)),
                          jax.ShapeDtypeStruct((t // QB, w, QB), dt))
    hv = wuv.shape[0]
    outs = [qblk(wqt.shape[0], BF16), rows(wckv.shape[1], BF16),
            (pl.BlockSpec((1, IN_SUBTILES, hv, KT), lambda r: (r // steps_per_seq, r % steps_per_seq, 0, 0)),
             jax.ShapeDtypeStruct((t // (n_kt * KT), n_kt, hv, KT), BF16)),
            qblk(wiqt.shape[0], BF16), qblk(wiwt.shape[0], F32), rows(IDX_DIM, BF16),
            rows(B_WIDTH, BF16), rows(SSM_INNER, BF16)]
    consts = [*norm_consts, *proj_weights, wuv, vone, *mixer_consts]
    return pl.pallas_call(
        functools.partial(_inproj_mixer_kernel, steps_per_seq=steps_per_seq),
        grid=(t // tm,),
        in_specs=[row(D_MODEL)] + [_const_spec(c.shape) for c in consts],
        out_specs=[spec for spec, _ in outs],
        out_shape=[shape for _, shape in outs],
        scratch_shapes=[pltpu.VMEM((tm + SUBLANES, B_WIDTH), F32),
                        pltpu.VMEM((tm + SUBLANES, SSM_XBC), F32),
                        pltpu.VMEM((SSM_GROUPS, SSM_STATE, SSM_INNER // SSM_GROUPS), F32)],
        compiler_params=_params(("arbitrary",)),
        name="inproj_conv_ssd",
    )(x2, *consts)


def _dense_kernel(x_ref, a_ref, b_ref, c_ref, p_ref, g_post_ref, g_pre_ref, g_fpost_ref,
                  woa_ref, wob_ref, woc_ref, wg_ref, wu_ref, wd_ref, wpg_ref, wpp_ref, o_ref):
    subs = [slice(r * TM_SUB, (r + 1) * TM_SUB) for r in range(ROW_SUBTILES)]
    mix = [_dot(a_ref[rows, :], woa_ref[...]) + _dot(b_ref[rows, :], wob_ref[...])
           + _dot(c_ref[rows, :], woc_ref[...]) for rows in subs]
    ple = [_dot(p_ref[rows, :].astype(BF16), wpp_ref[...]) for rows in subs]
    x = [x_ref[rows, :] + _rms(m, g_post_ref[...]) for rows, m in zip(subs, mix)]
    h = [_rms(xx, g_pre_ref[...]).astype(BF16) for xx in x]
    gate = [_dot(hh, wg_ref[...]) for hh in h]
    up = [_dot(hh, wu_ref[...]) for hh in h]
    act = [(g * _sigmoid(g) * u).astype(BF16) for g, u in zip(gate, up)]
    f = [_dot(aa, wd_ref[...]) for aa in act]
    x = [xx + _rms(ff, g_fpost_ref[...]) for xx, ff in zip(x, f)]
    pg = [_sigmoid(_dot(xx.astype(BF16), wpg_ref[...])) for xx in x]
    for rows, xx, g, e in zip(subs, x, pg, ple):
        o_ref[rows, :] = xx + g * e


def _dense_call(x2, a, b, c, p2, g_post, g_pre, g_fpost, woa, wob, woc, wg, wu, wd, wpg, wpp):
    t = x2.shape[0]
    tm = TM_OUT
    row = lambda w: pl.BlockSpec((tm, w), lambda r: (r, 0))
    consts = [g_post, g_pre, g_fpost, woa, wob, woc, wg, wu, wd, wpg, wpp]
    return pl.pallas_call(
        _dense_kernel,
        grid=(t // tm,),
        in_specs=[row(D_MODEL), row(a.shape[1]), row(b.shape[1]), row(c.shape[1]), row(PLE_DIM)]
                 + [_const_spec(w.shape) for w in consts],
        out_specs=row(D_MODEL),
        out_shape=jax.ShapeDtypeStruct((t, D_MODEL), F32),
        compiler_params=_params(("parallel",)),
        name="outproj_ffn_ple",
    )(x2, a, b, c, p2, *consts)


def kernel(x, p, pre_mix_norm, post_mix_norm, pre_ffn_norm, post_ffn_norm, w_in, kv_norm, idx_k_norm_g, idx_k_norm_b, w_uk, w_uv, rel_bias, short_conv_w, ssm_conv_w, ssm_conv_b, ssm_dt_bias, ssm_a_log, ssm_d, ssm_norm, w_out, w_ffn_gate, w_ffn_up, w_ffn_down, w_ple_proj, w_ple_gate):
    bsz, s, d = x.shape
    assert d == D_MODEL and s % (IN_SUBTILES * KT) == 0 and s % QB == 0 and s % SSD_L == 0
    assert (bsz * s) % TM_OUT == 0
    t = bsz * s
    k_top = min(TOPK_MAX, s // 4)

    bucket_np, near, far_bucket = _bucket_tiles()
    bias_tiles = _bias_tiles_call(rel_bias.astype(F32), jnp.asarray(bucket_np), far_bucket)
    tri = jnp.asarray(np.tril(np.ones((KT, KT), np.float32)), BF16)
    row1 = lambda v: v.reshape(1, -1).astype(F32)
    vone_np = np.zeros((A_HEADS, V_ROWS, 1), np.float32)
    vone_np[:, A_V_DIM] = 1.0
    vone = jnp.asarray(vone_np.reshape(A_HEADS * V_ROWS, 1))

    x2 = x.reshape(t, d)
    for i in range(DEPTH):
        wparts = _wprep_call(w_in[i].astype(F32))
        wuv_t = jnp.transpose(w_uv[i], (1, 2, 0)).astype(BF16)
        wuv_t = jnp.pad(wuv_t, ((0, 0), (0, V_ROWS - A_V_DIM), (0, 0))).reshape(A_HEADS * V_ROWS, A_KV_RANK)
        wq, wckv, wiq, wsm, wb, wz, wxbc, wdt = wparts
        wiw_t = wsm[:, IW_OFF:IW_OFF + SUBLANES].T
        wdt_t = wdt[:, :SSM_HEADS].T
        q_t, ckv, vt4, iq_t, iw_t, ik, b_out, c_out = _inproj_mixer_call(
            x2, (row1(pre_mix_norm[i]), row1(kv_norm[i]), row1(idx_k_norm_g[i]), row1(idx_k_norm_b[i])),
            (wq.T, wckv, wiq.T, wsm, wiw_t, wb, wz, wxbc, wdt, wdt_t), wuv_t, vone,
            (short_conv_w[i].astype(F32), ssm_conv_w[i].astype(F32), ssm_conv_b[i].astype(F32),
             ssm_dt_bias[i].astype(F32), ssm_a_log[i].astype(F32),
             jnp.repeat(ssm_d[i].astype(F32), SSM_HEAD_DIM), ssm_norm[i].astype(F32)), s // KT)

        a_out = _attn_call(
            q_t, iq_t, iw_t, ckv.reshape(bsz, s, A_KV_RANK), vt4, ik.reshape(bsz, s, IDX_DIM),
            jnp.transpose(w_uk[i], (1, 0, 2)).astype(BF16),
            bias_tiles, tri, k_top=k_top, near=near)

        wo = w_out[i].astype(BF16)
        na, nbw = A_HEADS * A_V_DIM, A_HEADS * A_V_DIM + B_WIDTH
        x2 = _dense_call(
            x2, a_out, b_out, c_out, p[i].reshape(t, PLE_DIM),
            row1(post_mix_norm[i]), row1(pre_ffn_norm[i]), row1(post_ffn_norm[i]),
            wo[:na], wo[na:nbw], wo[nbw:], w_ffn_gate[i].astype(BF16), w_ffn_up[i].astype(BF16),
            w_ffn_down[i].astype(BF16), w_ple_gate[i].astype(BF16), w_ple_proj[i].astype(BF16))
    return x2.reshape(bsz, s, d)
```

```python
import functools
import math

import numpy as np
import jax
import jax.numpy as jnp
from jax import lax
from jax.experimental import pallas as pl
from jax.experimental.pallas import tpu as pltpu

F32, BF16, I32 = jnp.float32, jnp.bfloat16, jnp.int32

D_MODEL = 1024
DEPTH = 2
CHUNK = 64
A_HEADS = 8
A_QK_DIM = 64
A_V_DIM = 64
A_KV_RANK = 256
IDX_HEADS = 4
IDX_DIM = 64
TOPK_MAX = 256
REL_BUCKETS = 32
REL_MAX_DIST = 1024
B_WIDTH = 512
SHORT_CONV = 3
SSM_HEADS = 16
SSM_HEAD_DIM = 64
SSM_INNER = SSM_HEADS * SSM_HEAD_DIM
SSM_GROUPS = 2
SSM_STATE = 128
SSM_CONV = 4
SSM_XBC = SSM_INNER + 2 * SSM_GROUPS * SSM_STATE
MIX_WIDTH = A_HEADS * A_V_DIM + B_WIDTH + SSM_INNER
D_FF = -(-8 * D_MODEL // (3 * 256)) * 256
PLE_DIM = 256
NORM_EPS = 1e-6
IN_SPLITS = (A_HEADS * A_QK_DIM, A_KV_RANK, IDX_HEADS * IDX_DIM, IDX_DIM, IDX_HEADS,
             B_WIDTH, B_WIDTH, B_WIDTH, SSM_INNER, SSM_XBC, SSM_HEADS)
IN_OFFSETS = tuple(int(v) for v in np.concatenate([[0], np.cumsum(IN_SPLITS)]))

LANES = 128
SUBLANES = 8
VMEM_LIMIT_BYTES = 56 * 1024 * 1024

QB = 2 * LANES
QGROUPS = QB // LANES
KT = 256
SSD_L = 128
IN_SUBTILES = 2
TM_SUB = 256
ROW_SUBTILES = 2
TM_OUT = TM_SUB * ROW_SUBTILES
SMALL_W = LANES
IW_OFF = IDX_DIM
INT_MIN = np.int32(-2 ** 31)
LOG2E = math.log2(math.e)
BF16_ROWS = 2 * SUBLANES
V_ROWS = A_V_DIM + BF16_ROWS

def _const_spec(shape):
    nd = len(shape)
    return pl.BlockSpec(shape, lambda *_: (0,) * nd, pipeline_mode=pl.Buffered(1))


def _params(sem):
    return pltpu.CompilerParams(dimension_semantics=sem, vmem_limit_bytes=VMEM_LIMIT_BYTES)


def _rms(x, g):
    return x * lax.rsqrt(jnp.mean(x * x, axis=-1, keepdims=True) + NORM_EPS) * g


def _sigmoid(x):
    return 1.0 / (1.0 + jnp.exp(-x))


def _split3(v):
    hi = v.astype(BF16)
    r1 = v - hi.astype(F32)
    mid = r1.astype(BF16)
    lo = (r1 - mid.astype(F32)).astype(BF16)
    return hi, mid, lo


def _dot(a, b):
    return jnp.dot(a, b, preferred_element_type=F32)


def _t5_bucket_np(rel):
    half = REL_BUCKETS // 2
    max_exact = half // 2
    ret = np.where(rel > 0, half, 0)
    n = np.abs(rel)
    nf = np.maximum(n, 1).astype(np.float32)
    large = max_exact + (np.log(nf / np.float32(max_exact)) / np.float32(math.log(REL_MAX_DIST / max_exact))
                         * np.float32(half - max_exact)).astype(np.int32)
    large = np.minimum(large, half - 1)
    return (ret + np.where(n < max_exact, n, large)).astype(np.int32)


@functools.lru_cache(maxsize=None)
def _bucket_tiles():
    kr = np.arange(KT)[:, None]
    qc = np.arange(QB)[None, :]
    tiles, near = [], None
    for o in range(64):
        t = _t5_bucket_np(kr - qc - o * QB)
        tiles.append(t)
    for o in range(63, -1, -1):
        if not (np.all(tiles[o] == tiles[63][0, 0])):
            near = o + 1
            break
    assert near is not None and 0 < near < 63
    return np.stack(tiles[:near + 1]), near, int(tiles[63][0, 0])


def _bias_kernel(rb_ref, bk_ref, out_ref, *, far_bucket):
    bk = bk_ref[0]
    for h in range(A_HEADS):
        acc = jnp.zeros(bk.shape, F32)
        for b in range(REL_BUCKETS):
            acc = jnp.where(bk == b, (rb_ref[b, h] - rb_ref[far_bucket, h]) * LOG2E, acc)
        out_ref[0, :, h * QB:(h + 1) * QB] = acc


def _bias_tiles_call(rel_bias, bucket_tiles, far_bucket):
    n = bucket_tiles.shape[0]
    return pl.pallas_call(
        functools.partial(_bias_kernel, far_bucket=far_bucket),
        grid=(n,),
        in_specs=[pl.BlockSpec(memory_space=pltpu.SMEM),
                  pl.BlockSpec((1, KT, QB), lambda o: (o, 0, 0))],
        out_specs=pl.BlockSpec((1, KT, A_HEADS * QB), lambda o: (o, 0, 0)),
        out_shape=jax.ShapeDtypeStruct((n, KT, A_HEADS * QB), F32),
        compiler_params=_params(("arbitrary",)),
        name="rel_bias_tiles",
    )(rel_bias, bucket_tiles)


def _wprep_kernel(w_ref, wq_out, wckv_out, wiq_out, wsm_out, wb_out, wz_out, wxbc_out, wdt_out):
    o_q, o_ckv, o_iq, o_ik, _, o_bg, _, _, o_z, o_xbc, o_dt, o_end = IN_OFFSETS
    wq_out[...] = w_ref[:, o_q:o_ckv].astype(BF16)
    wckv_out[...] = w_ref[:, o_ckv:o_iq].astype(BF16)
    wiq_out[...] = w_ref[:, o_iq:o_ik].astype(BF16)
    wsm_out[...] = w_ref[:, o_ik:o_ik + SMALL_W].astype(BF16)
    tail = w_ref[:, o_bg:o_end].astype(BF16)
    wb_out[...] = tail[:, :o_z - o_bg]
    wz_out[...] = tail[:, o_z - o_bg:o_xbc - o_bg]
    wxbc_out[...] = tail[:, o_xbc - o_bg:o_dt - o_bg]
    wdt_out[...] = jnp.zeros(wdt_out.shape, BF16)
    wdt_out[:, :o_end - o_dt] = tail[:, o_dt - o_bg:]


def _wprep_call(w):
    d, n = w.shape
    tr = 256
    widths = [IN_SPLITS[0], IN_SPLITS[1], IN_SPLITS[2], SMALL_W, 3 * B_WIDTH, SSM_INNER, SSM_XBC, SMALL_W]
    return pl.pallas_call(
        _wprep_kernel,
        grid=(d // tr,),
        in_specs=[pl.BlockSpec((tr, n), lambda r: (r, 0))],
        out_specs=[pl.BlockSpec((tr, wd), lambda r: (r, 0)) for wd in widths],
        out_shape=[jax.ShapeDtypeStruct((d, wd), BF16) for wd in widths],
        compiler_params=_params(("parallel",)),
        name="inproj_weight_prep",
    )(w)


def _dot_nt(a, b):
    return lax.dot_general(a, b, (((1,), (1,)), ((), ())), preferred_element_type=F32)


def _inproj_kernel(x_ref, g_ref, kvn_ref, ikg_ref, ikb_ref,
                   wqt_ref, wckv_ref, wiqt_ref, wsm_ref, wiwt_ref, wb_ref, wz_ref, wxbc_ref, wdt_ref,
                   wuv_ref, vone_ref,
                   qt_out, ckv_out, vt_out, iqt_out, iwt_out, ik_out, b_out, z_out, xbc_out, dt_out):
    subs = [slice(k * KT, (k + 1) * KT) for k in range(IN_SUBTILES)]
    h = [_rms(x_ref[rows, :], g_ref[...]).astype(BF16) for rows in subs]
    ckv = [_rms(_dot(hh, wckv_ref[...]), kvn_ref[...]).astype(BF16) for hh in h]
    sm = [_dot(hh, wsm_ref[...]) for hh in h]
    for k, rows in enumerate(subs):
        ckv_out[rows, :] = ckv[k]
        qt_out[k] = _dot_nt(wqt_ref[...], h[k]).astype(BF16)
        iqt_out[k] = (_dot_nt(wiqt_ref[...], h[k]) * (IDX_DIM ** -0.5)).astype(BF16)
        iwt_out[k] = _dot_nt(wiwt_ref[...], h[k]) * (IDX_HEADS ** -0.5)
    for k, rows in enumerate(subs):
        vt_out[0, k] = (_dot_nt(wuv_ref[...], ckv[k]) + vone_ref[...]).astype(BF16)
        ik = sm[k][:, :IDX_DIM]
        mu = jnp.mean(ik, axis=-1, keepdims=True)
        var = jnp.mean(jnp.square(ik - mu), axis=-1, keepdims=True)
        ik_out[rows, :] = ((ik - mu) * lax.rsqrt(var + NORM_EPS) * ikg_ref[...] + ikb_ref[...]).astype(BF16)
    for k, rows in enumerate(subs):
        b_out[rows, :] = _dot(h[k], wb_ref[...])
    for k, rows in enumerate(subs):
        z_out[rows, :] = _dot(h[k], wz_ref[...])
    for k, rows in enumerate(subs):
        xbc_out[rows, :] = _dot(h[k], wxbc_ref[...])
        dt_out[rows, :] = _dot(h[k], wdt_ref[...])


def _inproj_call(x2, g, kvn, ikg, ikb, wqt, wckv, wiqt, wsm, wiwt, wb, wz, wxbc, wdt, wuv, vone, n_kt):
    t = x2.shape[0]
    assert KT == QB and n_kt % IN_SUBTILES == 0
    tm = IN_SUBTILES * KT
    steps_per_seq = n_kt // IN_SUBTILES
    row = lambda w: pl.BlockSpec((tm, w), lambda r: (r, 0))
    rows = lambda w, dt: (row(w), jax.ShapeDtypeStruct((t, w), dt))
    qblk = lambda w, dt: (pl.BlockSpec((IN_SUBTILES, w, QB), lambda r: (r, 0, 0)),
                          jax.ShapeDtypeStruct((t // QB, w, QB), dt))
    hv = wuv.shape[0]
    outs = [qblk(wqt.shape[0], BF16), rows(wckv.shape[1], BF16),
            (pl.BlockSpec((1, IN_SUBTILES, hv, KT), lambda r: (r // steps_per_seq, r % steps_per_seq, 0, 0)),
             jax.ShapeDtypeStruct((t // (n_kt * KT), n_kt, hv, KT), BF16)),
            qblk(wiqt.shape[0], BF16), qblk(wiwt.shape[0], F32), rows(IDX_DIM, BF16),
            rows(wb.shape[1], F32), rows(wz.shape[1], F32), rows(wxbc.shape[1], F32),
            rows(SMALL_W, F32)]
    consts = [g, kvn, ikg, ikb, wqt, wckv, wiqt, wsm, wiwt, wb, wz, wxbc, wdt, wuv, vone]
    return pl.pallas_call(
        _inproj_kernel,
        grid=(t // tm,),
        in_specs=[row(D_MODEL)] + [_const_spec(c.shape) for c in consts],
        out_specs=[spec for spec, _ in outs],
        out_shape=[shape for _, shape in outs],
        compiler_params=_params(("parallel",)),
        name="inproj",
    )(x2, *consts)


def _bit_transpose32(words):
    a = list(words)
    m, j = 0x0000FFFF, 16
    while j:
        k = 0
        while k < 32:
            t = (a[k] ^ lax.shift_right_logical(a[k + j], np.int32(j))) & np.int32(m)
            a[k] = a[k] ^ t
            a[k + j] = a[k + j] ^ (t << j)
            k = (k + j + 1) & ~j
        j >>= 1
        m = (m ^ (m << j)) & 0xFFFFFFFF
    return a


def _attn_kernel(q_ref, iq_ref, iw_ref, ckv_ref, vt_ref, ik_ref, wuk_ref, bias_ref, tri_ref,
                 o_ref, keys_ref, planes_ref, qlat_ref, acc_ref, p0_ref, p1_ref, lg0_ref, lg1_ref,
                 *, k_top, near):
    i = pl.program_id(1)
    nt = ((i + 1) * QB + KT - 1) // KT
    lane = lax.broadcasted_iota(I32, (1, QB), 1)
    qchunk = (i * QB + lane) >> int(math.log2(CHUNK))
    krow = lax.broadcasted_iota(I32, (KT, QB), 0)

    q_t = q_ref[0]
    for h in range(A_HEADS):
        ql = _dot(wuk_ref[h], q_t[h * A_QK_DIM:(h + 1) * A_QK_DIM, :])
        qlat_ref[:, h * QB:(h + 1) * QB] = (ql * (A_QK_DIM ** -0.5 * LOG2E)).astype(BF16)

    iq_t = iq_ref[0]
    iw = iw_ref[0][:IDX_HEADS, :]

    def score_tile(j, carry, masked):
        r0 = pl.multiple_of(j * KT, KT)
        ikt = ik_ref[0, pl.ds(r0, KT), :]
        s = jnp.zeros((KT, QB), F32)
        for h in range(IDX_HEADS):
            raw = _dot(ikt, iq_t[h * IDX_DIM:(h + 1) * IDX_DIM, :])
            s = s + jnp.maximum(raw, 0.0) * iw[h:h + 1, :]
        if masked:
            adm = ((r0 + krow) >> int(math.log2(CHUNK))) <= qchunk
            s = jnp.where(adm, s, -jnp.inf)
        bits = lax.bitcast_convert_type(s, I32)
        bits = jnp.where(bits == INT_MIN, 0, bits)
        key = bits ^ ((bits >> 31) & np.int32(0x7FFFFFFF))
        keys_ref[pl.ds(r0, KT), :] = key
        ukey = key ^ INT_MIN
        planes = _bit_transpose32([ukey[v * SUBLANES:(v + 1) * SUBLANES, :] for v in range(32)])
        p0 = pl.multiple_of(j * SUBLANES, SUBLANES)
        for b in range(32):
            planes_ref[b, pl.ds(p0, SUBLANES), :] = planes[31 - b]
        return carry

    @pl.when(i == 0)
    def _():
        planes_ref[...] = jnp.zeros(planes_ref.shape, I32)

    def score_pair(u, carry):
        score_tile(2 * u, 0, masked=False)
        score_tile(2 * u + 1, 0, masked=False)
        return carry

    lax.fori_loop(0, (nt - 1) // 2, score_pair, 0)

    @pl.when((nt - 1) % 2 == 1)
    def _():
        score_tile(nt - 2, 0, masked=False)

    score_tile(nt - 1, 0, masked=True)

    n_prow = planes_ref.shape[1]
    prow = lax.broadcasted_iota(I32, (n_prow, QB), 0)
    kf = float(k_top)

    def search_step(k, carry):
        live, c_gt, thr_u = carry
        b = 31 - k
        x = live & planes_ref[b]
        pc = lax.population_count(x)
        parts = [pc[r * SUBLANES:(r + 1) * SUBLANES, :] for r in range(n_prow // SUBLANES)]
        while len(parts) > 1:
            parts = [parts[k2] + parts[k2 + 1] for k2 in range(0, len(parts) - 1, 2)] + (
                [parts[-1]] if len(parts) % 2 else [])
        cnt = c_gt + jnp.sum(parts[0].astype(F32), axis=0, keepdims=True)
        take = cnt >= kf
        live = jnp.where(take, x, live ^ x)
        c_gt = jnp.where(take, c_gt, cnt)
        thr_u = thr_u | jnp.where(take, jnp.left_shift(jnp.int32(1), b), np.int32(0))
        return live, c_gt, thr_u

    _, c_gt, thr_u = lax.fori_loop(
        0, 32, search_step,
        (jnp.where(prow < nt * SUBLANES, np.int32(-1), np.int32(0)),
         jnp.zeros((1, QB), F32), jnp.zeros((1, QB), I32)))
    thr = thr_u ^ INT_MIN
    need = kf - c_gt

    p_refs, lg_refs = (p0_ref, p1_ref), (lg0_ref, lg1_ref)
    n_cols = A_HEADS * QGROUPS
    acc_ref[...] = jnp.zeros(acc_ref.shape, F32)
    p1_ref[...] = jnp.zeros(p1_ref.shape, BF16)

    def stage_scores(j, cnt_eq, slot, far=False):
        r0 = pl.multiple_of(j * KT, KT)
        kt = keys_ref[pl.ds(r0, KT), :]
        eq = kt == thr
        eqf = jnp.where(eq, 1.0, 0.0)
        rank = _dot(tri_ref[...], eqf.astype(BF16)) + cnt_eq
        sel = (kt > thr) | (eq & (rank <= need))
        if not far:
            sel = sel & (((r0 + krow) >> int(math.log2(CHUNK))) <= qchunk)
            bias = bias_ref[jnp.minimum(i - j * (KT // QB), near)]
        logits = _dot(ckv_ref[0, pl.ds(r0, KT), :], qlat_ref[...])
        tmax = [[] for _ in range(QGROUPS)]
        for h in range(A_HEADS):
            for g in range(QGROUPS):
                sl = slice(h * QB + g * LANES, h * QB + (g + 1) * LANES)
                lh = logits[:, sl] if far else logits[:, sl] + bias[:, sl]
                lh = jnp.where(sel[:, g * LANES:(g + 1) * LANES], lh, -jnp.inf)
                lg_refs[slot][:, sl] = lh
                tmax[g].append(jnp.max(lh, axis=0, keepdims=True))
        tmax = tuple(jnp.concatenate(rows, axis=0) for rows in tmax)
        return tmax, cnt_eq + jnp.sum(eqf, axis=0, keepdims=True)

    def stage_probs(ms, tmax, slot):
        new_ms, alphas = [], []
        for g in range(QGROUPS):
            m_new = jnp.maximum(ms[g], tmax[g])
            m_safe = jnp.where(m_new == -jnp.inf, 0.0, m_new)
            alphas.append(jnp.exp2(ms[g] - m_safe))
            new_ms.append(m_new)
            for h in range(A_HEADS):
                sl = slice(h * QB + g * LANES, h * QB + (g + 1) * LANES)
                p_refs[slot][:, sl] = jnp.exp2(lg_refs[slot][:, sl] - m_safe[h:h + 1, :]).astype(BF16)
        return tuple(new_ms), tuple(alphas)

    def stage_values(alphas, jp, slot):
        for h in range(A_HEADS):
            vs = slice(h * V_ROWS, (h + 1) * V_ROWS)
            pv = _dot(vt_ref[0, jp, vs, :], p_refs[slot][:, h * QB:(h + 1) * QB])
            alpha = jnp.concatenate([a[h:h + 1, :] for a in alphas], axis=1)
            acc_ref[vs, :] = acc_ref[vs, :] * alpha + pv

    def trip(t, carry, slot, far=False):
        ms, tmax, cnt_eq, alphas = carry
        stage_values(alphas, jnp.maximum(t - 1, 0), 1 - slot)
        ms, alphas = stage_probs(ms, tmax, slot)
        tmax, cnt_eq = stage_scores(t + 1, cnt_eq, 1 - slot, far)
        return ms, tmax, cnt_eq, alphas

    def finish(t, carry, slot):
        ms, tmax, _, alphas = carry
        stage_values(alphas, jnp.maximum(t - 1, 0), 1 - slot)
        _, alphas = stage_probs(ms, tmax, slot)
        stage_values(alphas, t, slot)

    def pair(u, carry, far=False):
        return trip(2 * u + 1, trip(2 * u, carry, 0, far), 1, far)

    tmax0, cnt_eq0 = stage_scores(0, jnp.zeros((1, QB), F32), 0)
    init = (tuple(jnp.full((A_HEADS, LANES), -jnp.inf, F32) for _ in range(QGROUPS)), tmax0, cnt_eq0,
            tuple(jnp.ones((A_HEADS, LANES), F32) for _ in range(QGROUPS)))
    n_trips = nt - 1
    n_far_pairs = jnp.maximum(i * (QB // KT) - near, 0) // 2
    carry = lax.fori_loop(0, n_far_pairs, functools.partial(pair, far=True), init)
    carry = lax.fori_loop(n_far_pairs, n_trips // 2, pair, carry)

    @pl.when(n_trips % 2 == 0)
    def _():
        finish(nt - 1, carry, 0)

    @pl.when(n_trips % 2 == 1)
    def _():
        finish(nt - 1, trip(nt - 2, carry, 0), 1)

    a_t = jnp.concatenate(
        [acc_ref[h * V_ROWS:h * V_ROWS + A_V_DIM, :]
         * (1.0 / acc_ref[h * V_ROWS + A_V_DIM:h * V_ROWS + A_V_DIM + 1, :]) for h in range(A_HEADS)],
        axis=0)
    o_ref[...] = a_t.T.astype(BF16)


def _attn_call(q_t, iq_t, iw_t, ckv3, vt4, ik3, wuk, bias_tiles, tri, *, k_top, near):
    b, s = ckv3.shape[0], ckv3.shape[1]
    nb = s // QB
    n_kt = s // KT
    hv = A_HEADS * A_V_DIM
    hvr = A_HEADS * V_ROWS
    kern = functools.partial(_attn_kernel, k_top=k_top, near=near)
    row = lambda w: pl.BlockSpec((QB, w), lambda bb, i: (bb * nb + i, 0))
    qblk = lambda a: pl.BlockSpec((1, a.shape[1], QB), lambda bb, i: (bb * nb + i, 0, 0))
    return pl.pallas_call(
        kern,
        grid=(b, nb),
        in_specs=[
            qblk(q_t), qblk(iq_t), qblk(iw_t),
            pl.BlockSpec((1, s, A_KV_RANK), lambda bb, i: (bb, 0, 0)),
            pl.BlockSpec((1, n_kt, hvr, KT), lambda bb, i: (bb, 0, 0, 0)),
            pl.BlockSpec((1, s, IDX_DIM), lambda bb, i: (bb, 0, 0)),
            _const_spec(wuk.shape), _const_spec(bias_tiles.shape), _const_spec(tri.shape),
        ],
        out_specs=row(hv),
        out_shape=jax.ShapeDtypeStruct((b * s, hv), BF16),
        scratch_shapes=[pltpu.VMEM((s, QB), I32), pltpu.VMEM((32, s // 32, QB), I32),
                        pltpu.VMEM((A_KV_RANK, A_HEADS * QB), BF16),
                        pltpu.VMEM((hvr, QB), F32),
                        pltpu.VMEM((KT, A_HEADS * QB), BF16), pltpu.VMEM((KT, A_HEADS * QB), BF16),
                        pltpu.VMEM((KT, A_HEADS * QB), F32), pltpu.VMEM((KT, A_HEADS * QB), F32)],
        compiler_params=_params(("parallel", "arbitrary")),
        name="dsa_attention",
    )(q_t, iq_t, iw_t, ckv3, vt4, ik3, wuk, bias_tiles, tri)


def _softplus(x):
    return jnp.maximum(x, 0.0) + jnp.log1p(jnp.exp(-jnp.abs(x)))


def _mixer_kernel(bm_ref, z_ref, xbc_ref, dtp_ref, dtt_ref, scw_ref, cw_ref, cb_ref,
                  dtb_r_ref, dtb_c_ref, alog_r_ref, alog_c_ref, dexp_ref, nw_ref,
                  tril_ref, triu_ref, eh_ref,
                  bo_ref, co_ref, ubuf, xbuf, state):
    L = SSD_L
    P = SUBLANES
    gw = SSM_INNER // SSM_GROUPS

    @pl.when(pl.program_id(1) == 0)
    def _():
        ubuf[0:P, :] = jnp.zeros((P, B_WIDTH), F32)
        xbuf[0:P, :] = jnp.zeros((P, SSM_XBC), F32)
        state[...] = jnp.zeros(state.shape, F32)

    bm = bm_ref[...]
    u = bm[:, B_WIDTH:2 * B_WIDTH] * bm[:, 2 * B_WIDTH:3 * B_WIDTH]
    ubuf[P:P + L, :] = u
    conv = scw_ref[SHORT_CONV - 1:SHORT_CONV, :] * u
    for j in range(SHORT_CONV - 1):
        off = P - (SHORT_CONV - 1) + j
        conv = conv + scw_ref[j:j + 1, :] * ubuf[off:off + L, :]
    bo_ref[...] = (bm[:, :B_WIDTH] * conv).astype(BF16)
    ubuf[0:P, :] = u[L - P:L, :]

    xr = xbc_ref[...]
    xbuf[P:P + L, :] = xr
    xc = cw_ref[SSM_CONV - 1:SSM_CONV, :] * xr + cb_ref[...]
    for j in range(SSM_CONV - 1):
        off = P - (SSM_CONV - 1) + j
        xc = xc + cw_ref[j:j + 1, :] * xbuf[off:off + L, :]
    xbuf[0:P, :] = xr[L - P:L, :]
    xa = xc * _sigmoid(xc)
    xs = xa[:, :SSM_INNER]

    dt = _softplus(dtp_ref[:, :SSM_HEADS] + dtb_r_ref[...])
    a = dt * (-jnp.exp(alog_r_ref[...]))
    a_cs = sum(_dot(tril_ref[...], part) for part in _split3(a))
    dt_t = _softplus(dtt_ref[0] + dtb_c_ref[...])
    a_t = dt_t * (-jnp.exp(alog_c_ref[...]))
    a_cs_t = sum(_dot(part, triu_ref[...]) for part in _split3(a_t))

    def expand(v):
        return sum(_dot(part, eh_ref[...]) for part in _split3(v))

    xdt = xs * expand(dt)
    e_cs = expand(jnp.exp(a_cs))
    dec = expand(jnp.exp(a_cs[L - 1:L, :] - a_cs))
    chunk_dec = e_cs[L - 1:L, :]

    row = lax.broadcasted_iota(I32, (L, L), 0)
    col = lax.broadcasted_iota(I32, (L, L), 1)
    causal = row >= col
    lo_half = lax.broadcasted_iota(I32, (L, LANES), 1) < SSM_HEAD_DIM
    hpg = SSM_HEADS // SSM_GROUPS

    y_parts = []
    for g in range(SSM_GROUPS):
        bg = xa[:, SSM_INNER + g * SSM_STATE:SSM_INNER + (g + 1) * SSM_STATE].astype(BF16)
        cg = xa[:, SSM_INNER + (SSM_GROUPS + g) * SSM_STATE:
                SSM_INNER + (SSM_GROUPS + g + 1) * SSM_STATE].astype(BF16)
        cb = lax.dot_general(cg, bg, (((1,), (1,)), ((), ())), preferred_element_type=F32)
        sl = slice(g * gw, (g + 1) * gw)
        st = state[g]
        y_off = _dot(cg, st.astype(BF16)) * e_cs[:, sl]
        for k in range(hpg // 2):
            xp = xdt[:, g * gw + k * LANES:g * gw + (k + 1) * LANES]
            halves = (jnp.where(lo_half, xp, 0.0).astype(BF16), jnp.where(lo_half, 0.0, xp).astype(BF16))
            y = y_off[:, k * LANES:(k + 1) * LANES]
            for hh in range(2):
                h = g * hpg + 2 * k + hh
                seg = a_cs[:, h:h + 1] - a_cs_t[h:h + 1, :]
                m = (cb * jnp.exp(jnp.where(causal, seg, -jnp.inf))).astype(BF16)
                y = y + _dot(m, halves[hh])
            y_parts.append(y)
        xw = (dec[:, sl] * xdt[:, sl]).astype(BF16)
        upd = lax.dot_general(bg, xw, (((0,), (0,)), ((), ())), preferred_element_type=F32)
        state[g] = chunk_dec[:, sl] * st + upd

    y = jnp.concatenate(y_parts, axis=1) + dexp_ref[...] * xs
    zz = z_ref[...]
    y = y * (zz * _sigmoid(zz))
    outs = []
    for g in range(SSM_GROUPS):
        sl = slice(g * gw, (g + 1) * gw)
        outs.append(_rms(y[:, sl], nw_ref[:, sl]))
    co_ref[...] = jnp.concatenate(outs, axis=1).astype(BF16)


def _mixer_call(bmix, z, xbc, sm, dt_t, scw, cw, cb, dtb, alog, dexp, nw, b, s):
    L = SSD_L
    nc = s // L
    tril = jnp.asarray(np.tril(np.ones((L, L), np.float32)), BF16)
    triu = jnp.asarray(np.triu(np.ones((L, L), np.float32)), BF16)
    eh = jnp.asarray(np.repeat(np.eye(SSM_HEADS, dtype=np.float32), SSM_HEAD_DIM, axis=1), BF16)
    row = lambda w: pl.BlockSpec((L, w), lambda bb, c: (bb * nc + c, 0))
    consts = [scw, cw, cb.reshape(1, -1), dtb.reshape(1, -1), dtb.reshape(-1, 1),
              alog.reshape(1, -1), alog.reshape(-1, 1), dexp.reshape(1, -1), nw.reshape(1, -1),
              tril, triu, eh]
    return pl.pallas_call(
        _mixer_kernel,
        grid=(b, nc),
        in_specs=[row(3 * B_WIDTH), row(SSM_INNER), row(SSM_XBC), row(SMALL_W),
                  pl.BlockSpec((1, SSM_HEADS, L), lambda bb, c: (bb, 0, c))]
                 + [_const_spec(c.shape) for c in consts],
        out_specs=[row(B_WIDTH), row(SSM_INNER)],
        out_shape=[jax.ShapeDtypeStruct((b * s, B_WIDTH), BF16),
                   jax.ShapeDtypeStruct((b * s, SSM_INNER), BF16)],
        scratch_shapes=[pltpu.VMEM((L + SUBLANES, B_WIDTH), F32),
                        pltpu.VMEM((L + SUBLANES, SSM_XBC), F32),
                        pltpu.VMEM((SSM_GROUPS, SSM_STATE, SSM_INNER // SSM_GROUPS), F32)],
        compiler_params=_params(("parallel", "arbitrary")),
        name="conv_ssd_mixers",
    )(bmix, z, xbc, sm, dt_t, *consts)


def _ssd_chunk(xa, zz, dt_raw, dt_t_raw, state, dtb_r, dtb_c, alog_r, alog_c, dexp, nw, tril, triu, eh):
    L = SSD_L
    gw = SSM_INNER // SSM_GROUPS
    xs = xa[:, :SSM_INNER]
    dt = _softplus(dt_raw + dtb_r)
    a = dt * (-jnp.exp(alog_r))
    a_cs = sum(_dot(tril, part) for part in _split3(a))
    dt_t = _softplus(dt_t_raw + dtb_c)
    a_t = dt_t * (-jnp.exp(alog_c))
    a_cs_t = sum(_dot(part, triu) for part in _split3(a_t))

    def expand(v):
        v3 = jnp.concatenate([v, v, v], axis=1)
        hi, mid, lo = _split3(v3)
        term = lax.broadcasted_iota(I32, v3.shape, 1) // SSM_HEADS
        return _dot(jnp.where(term == 0, hi, jnp.where(term == 1, mid, lo)), eh)

    xdt = xs * expand(dt)
    e_cs = expand(jnp.exp(a_cs))
    dec = expand(jnp.exp(a_cs[L - 1:L, :] - a_cs))
    chunk_dec = e_cs[L - 1:L, :]

    row = lax.broadcasted_iota(I32, (L, L), 0)
    col = lax.broadcasted_iota(I32, (L, L), 1)
    causal = row >= col
    lo_half = lax.broadcasted_iota(I32, (L, LANES), 1) < SSM_HEAD_DIM
    hpg = SSM_HEADS // SSM_GROUPS

    y_parts = []
    for g in range(SSM_GROUPS):
        bg = xa[:, SSM_INNER + g * SSM_STATE:SSM_INNER + (g + 1) * SSM_STATE].astype(BF16)
        cg = xa[:, SSM_INNER + (SSM_GROUPS + g) * SSM_STATE:
                SSM_INNER + (SSM_GROUPS + g + 1) * SSM_STATE].astype(BF16)
        cb = _dot_nt(cg, bg)
        sl = slice(g * gw, (g + 1) * gw)
        st = state[g]
        y_off = _dot(cg, st.astype(BF16)) * e_cs[:, sl]
        for k in range(hpg // 2):
            xp = xdt[:, g * gw + k * LANES:g * gw + (k + 1) * LANES]
            x2 = jnp.concatenate([jnp.where(lo_half, xp, 0.0), jnp.where(lo_half, 0.0, xp)], axis=0)
            ms = []
            for hh in range(2):
                h = g * hpg + 2 * k + hh
                seg = a_cs[:, h:h + 1] - a_cs_t[h:h + 1, :]
                ms.append((cb * jnp.exp(jnp.where(causal, seg, -jnp.inf))).astype(BF16))
            y_parts.append(y_off[:, k * LANES:(k + 1) * LANES]
                           + _dot(jnp.concatenate(ms, axis=1), x2.astype(BF16)))
        xw = (dec[:, sl] * xdt[:, sl]).astype(BF16)
        upd = lax.dot_general(bg, xw, (((0,), (0,)), ((), ())), preferred_element_type=F32)
        state[g] = chunk_dec[:, sl] * st + upd

    y = jnp.concatenate(y_parts, axis=1) + dexp * xs
    y = y * (zz * _sigmoid(zz))
    return jnp.concatenate(
        [_rms(y[:, g * gw:(g + 1) * gw], nw[:, g * gw:(g + 1) * gw]) for g in range(SSM_GROUPS)], axis=1)


def _inproj_mixer_kernel(x_ref, g_ref, kvn_ref, ikg_ref, ikb_ref,
                         wqt_ref, wckv_ref, wiqt_ref, wsm_ref, wiwt_ref, wb_ref, wz_ref, wxbc_ref,
                         wdt_ref, wdtt_ref, wuv_ref, vone_ref,
                         scw_ref, cw_ref, cb_ref, dtb_r_ref, dtb_c_ref, alog_r_ref, alog_c_ref,
                         dexp_ref, nw_ref, tril_ref, triu_ref, eh_ref,
                         qt_out, ckv_out, vt_out, iqt_out, iwt_out, ik_out, bo_out, co_out,
                         ubuf, xbuf, state, *, steps_per_seq):
    P = SUBLANES
    tm = IN_SUBTILES * KT

    @pl.when(pl.program_id(0) % steps_per_seq == 0)
    def _():
        ubuf[0:P, :] = jnp.zeros((P, B_WIDTH), F32)
        xbuf[0:P, :] = jnp.zeros((P, SSM_XBC), F32)
        state[...] = jnp.zeros(state.shape, F32)

    subs = [slice(k * KT, (k + 1) * KT) for k in range(IN_SUBTILES)]
    h = [_rms(x_ref[rows, :], g_ref[...]).astype(BF16) for rows in subs]
    ssd_consts = (dtb_r_ref[...], dtb_c_ref[...], alog_r_ref[...], alog_c_ref[...], dexp_ref[...],
                  nw_ref[...], tril_ref[...], triu_ref[...], eh_ref[...])

    def proj_mixer(k):
        return dict(bm=_dot(h[k], wb_ref[...]),
                    xr=_dot(h[k], wxbc_ref[...]), zz=_dot(h[k], wz_ref[...]),
                    dt=_dot(h[k], wdt_ref[...])[:, :SSM_HEADS],
                    dt_t=_dot_nt(wdtt_ref[...], h[k]))

    def short_conv(k, bm):
        base = P + k * KT
        u = bm[:, B_WIDTH:2 * B_WIDTH] * bm[:, 2 * B_WIDTH:3 * B_WIDTH]
        ubuf[base:base + KT, :] = u
        conv = scw_ref[SHORT_CONV - 1:SHORT_CONV, :] * u
        for j in range(SHORT_CONV - 1):
            off = base - (SHORT_CONV - 1) + j
            conv = conv + scw_ref[j:j + 1, :] * ubuf[off:off + KT, :]
        bo_out[subs[k], :] = (bm[:, :B_WIDTH] * conv).astype(BF16)

    def ssm_conv(k, xr):
        base = P + k * KT
        xbuf[base:base + KT, :] = xr
        xc = cw_ref[SSM_CONV - 1:SSM_CONV, :] * xr + cb_ref[...]
        for j in range(SSM_CONV - 1):
            off = base - (SSM_CONV - 1) + j
            xc = xc + cw_ref[j:j + 1, :] * xbuf[off:off + KT, :]
        return xc * _sigmoid(xc)

    def scan_chunk(k, c, xa, pm):
        cs = slice(c * SSD_L, (c + 1) * SSD_L)
        y = _ssd_chunk(xa[cs, :], pm["zz"][cs, :], pm["dt"][cs, :], pm["dt_t"][:, cs], state, *ssd_consts)
        r0 = k * KT + c * SSD_L
        co_out[r0:r0 + SSD_L, :] = y.astype(BF16)

    def attn_keys(k):
        ckv = _rms(_dot(h[k], wckv_ref[...]), kvn_ref[...]).astype(BF16)
        ckv_out[subs[k], :] = ckv
        ik = _dot(h[k], wsm_ref[...])[:, :IDX_DIM]
        mu = jnp.mean(ik, axis=-1, keepdims=True)
        var = jnp.mean(jnp.square(ik - mu), axis=-1, keepdims=True)
        ik_out[subs[k], :] = ((ik - mu) * lax.rsqrt(var + NORM_EPS) * ikg_ref[...] + ikb_ref[...]).astype(BF16)
        return ckv

    def attn_queries(k):
        qt_out[k] = _dot_nt(wqt_ref[...], h[k]).astype(BF16)
        iqt_out[k] = (_dot_nt(wiqt_ref[...], h[k]) * (IDX_DIM ** -0.5)).astype(BF16)
        iwt_out[k] = _dot_nt(wiwt_ref[...], h[k]) * (IDX_HEADS ** -0.5)

    def attn_values(k, ckv):
        vt_out[0, k] = (_dot_nt(wuv_ref[...], ckv) + vone_ref[...]).astype(BF16)

    assert IN_SUBTILES == 2 and KT // SSD_L == 2
    pm0 = proj_mixer(0)
    short_conv(0, pm0["bm"])
    pm1 = proj_mixer(1)
    xa0 = ssm_conv(0, pm0["xr"])
    ckv0 = attn_keys(0)
    scan_chunk(0, 0, xa0, pm0)
    attn_queries(0)
    scan_chunk(0, 1, xa0, pm0)
    ckv1 = attn_keys(1)
    attn_values(0, ckv0)
    short_conv(1, pm1["bm"])
    xa1 = ssm_conv(1, pm1["xr"])
    attn_queries(1)
    scan_chunk(1, 0, xa1, pm1)
    attn_values(1, ckv1)
    scan_chunk(1, 1, xa1, pm1)

    ubuf[0:P, :] = ubuf[tm:tm + P, :]
    xbuf[0:P, :] = xbuf[tm:tm + P, :]


def _inproj_mixer_call(x2, norm_consts, proj_weights, wuv, vone, mixer_params, n_kt):
    t = x2.shape[0]
    assert KT == QB and n_kt % IN_SUBTILES == 0 and KT % SSD_L == 0
    tm = IN_SUBTILES * KT
    steps_per_seq = n_kt // IN_SUBTILES
    scw, cw, cb, dtb, alog, dexp, nw = mixer_params
    L = SSD_L
    tril = jnp.asarray(np.tril(np.ones((L, L), np.float32)), BF16)
    triu = jnp.asarray(np.triu(np.ones((L, L), np.float32)), BF16)
    eh = jnp.asarray(np.tile(np.repeat(np.eye(SSM_HEADS, dtype=np.float32), SSM_HEAD_DIM, axis=1), (3, 1)), BF16)
    mixer_consts = [scw, cw, cb.reshape(1, -1), dtb.reshape(1, -1), dtb.reshape(-1, 1),
                    alog.reshape(1, -1), alog.reshape(-1, 1), dexp.reshape(1, -1), nw.reshape(1, -1),
                    tril, triu, eh]
    wqt, wckv, wiqt, wiwt = proj_weights[0], proj_weights[1], proj_weights[2], proj_weights[4]
    row = lambda w: pl.BlockSpec((tm, w), lambda r: (r, 0))
    rows = lambda w, dt: (row(w), jax.ShapeDtypeStruct((t, w), dt))
    qblk = lambda w, dt: (pl.BlockSpec((IN_SUBTILES, w, QB), lambda r: (r, 0, 0)),
                          jax.ShapeDtypeStruct((t // QB, w, QB), dt))
    hv = wuv.shape[0]
    outs = [qblk(wqt.shape[0], BF16), rows(wckv.shape[1], BF16),
            (pl.BlockSpec((1, IN_SUBTILES, hv, KT), lambda r: (r // steps_per_seq, r % steps_per_seq, 0, 0)),
             jax.ShapeDtypeStruct((t // (n_kt * KT), n_kt, hv, KT), BF16)),
            qblk(wiqt.shape[0], BF16), qblk(wiwt.shape[0], F32), rows(IDX_DIM, BF16),
            rows(B_WIDTH, BF16), rows(SSM_INNER, BF16)]
    consts = [*norm_consts, *proj_weights, wuv, vone, *mixer_consts]
    return pl.pallas_call(
        functools.partial(_inproj_mixer_kernel, steps_per_seq=steps_per_seq),
        grid=(t // tm,),
        in_specs=[row(D_MODEL)] + [_const_spec(c.shape) for c in consts],
        out_specs=[spec for spec, _ in outs],
        out_shape=[shape for _, shape in outs],
        scratch_shapes=[pltpu.VMEM((tm + SUBLANES, B_WIDTH), F32),
                        pltpu.VMEM((tm + SUBLANES, SSM_XBC), F32),
                        pltpu.VMEM((SSM_GROUPS, SSM_STATE, SSM_INNER // SSM_GROUPS), F32)],
        compiler_params=_params(("arbitrary",)),
        name="inproj_conv_ssd",
    )(x2, *consts)


def _dense_kernel(x_ref, a_ref, b_ref, c_ref, p_ref, g_post_ref, g_pre_ref, g_fpost_ref,
                  woa_ref, wob_ref, woc_ref, wg_ref, wu_ref, wd_ref, wpg_ref, wpp_ref, o_ref):
    subs = [slice(r * TM_SUB, (r + 1) * TM_SUB) for r in range(ROW_SUBTILES)]
    mix = [_dot(a_ref[rows, :], woa_ref[...]) + _dot(b_ref[rows, :], wob_ref[...])
           + _dot(c_ref[rows, :], woc_ref[...]) for rows in subs]
    ple = [_dot(p_ref[rows, :].astype(BF16), wpp_ref[...]) for rows in subs]
    x = [x_ref[rows, :] + _rms(m, g_post_ref[...]) for rows, m in zip(subs, mix)]
    h = [_rms(xx, g_pre_ref[...]).astype(BF16) for xx in x]
    gate = [_dot(hh, wg_ref[...]) for hh in h]
    up = [_dot(hh, wu_ref[...]) for hh in h]
    act = [(g * _sigmoid(g) * u).astype(BF16) for g, u in zip(gate, up)]
    f = [_dot(aa, wd_ref[...]) for aa in act]
    x = [xx + _rms(ff, g_fpost_ref[...]) for xx, ff in zip(x, f)]
    pg = [_sigmoid(_dot(xx.astype(BF16), wpg_ref[...])) for xx in x]
    for rows, xx, g, e in zip(subs, x, pg, ple):
        o_ref[rows, :] = xx + g * e


def _dense_call(x2, a, b, c, p2, g_post, g_pre, g_fpost, woa, wob, woc, wg, wu, wd, wpg, wpp):
    t = x2.shape[0]
    tm = TM_OUT
    row = lambda w: pl.BlockSpec((tm, w), lambda r: (r, 0))
    consts = [g_post, g_pre, g_fpost, woa, wob, woc, wg, wu, wd, wpg, wpp]
    return pl.pallas_call(
        _dense_kernel,
        grid=(t // tm,),
        in_specs=[row(D_MODEL), row(a.shape[1]), row(b.shape[1]), row(c.shape[1]), row(PLE_DIM)]
                 + [_const_spec(w.shape) for w in consts],
        out_specs=row(D_MODEL),
        out_shape=jax.ShapeDtypeStruct((t, D_MODEL), F32),
        compiler_params=_params(("parallel",)),
        name="outproj_ffn_ple",
    )(x2, a, b, c, p2, *consts)


def kernel(x, p, pre_mix_norm, post_mix_norm, pre_ffn_norm, post_ffn_norm, w_in, kv_norm, idx_k_norm_g, idx_k_norm_b, w_uk, w_uv, rel_bias, short_conv_w, ssm_conv_w, ssm_conv_b, ssm_dt_bias, ssm_a_log, ssm_d, ssm_norm, w_out, w_ffn_gate, w_ffn_up, w_ffn_down, w_ple_proj, w_ple_gate):
    bsz, s, d = x.shape
    assert d == D_MODEL and s % (IN_SUBTILES * KT) == 0 and s % QB == 0 and s % SSD_L == 0
    assert (bsz * s) % TM_OUT == 0
    t = bsz * s
    k_top = min(TOPK_MAX, s // 4)

    bucket_np, near, far_bucket = _bucket_tiles()
    bias_tiles = _bias_tiles_call(rel_bias.astype(F32), jnp.asarray(bucket_np), far_bucket)
    tri = jnp.asarray(np.tril(np.ones((KT, KT), np.float32)), BF16)
    row1 = lambda v: v.reshape(1, -1).astype(F32)
    vone_np = np.zeros((A_HEADS, V_ROWS, 1), np.float32)
    vone_np[:, A_V_DIM] = 1.0
    vone = jnp.asarray(vone_np.reshape(A_HEADS * V_ROWS, 1))

    x2 = x.reshape(t, d)
    for i in range(DEPTH):
        wparts = _wprep_call(w_in[i].astype(F32))
        wuv_t = jnp.transpose(w_uv[i], (1, 2, 0)).astype(BF16)
        wuv_t = jnp.pad(wuv_t, ((0, 0), (0, V_ROWS - A_V_DIM), (0, 0))).reshape(A_HEADS * V_ROWS, A_KV_RANK)
        wq, wckv, wiq, wsm, wb, wz, wxbc, wdt = wparts
        wiw_t = wsm[:, IW_OFF:IW_OFF + SUBLANES].T
        wdt_t = wdt[:, :SSM_HEADS].T
        q_t, ckv, vt4, iq_t, iw_t, ik, b_out, c_out = _inproj_mixer_call(
            x2, (row1(pre_mix_norm[i]), row1(kv_norm[i]), row1(idx_k_norm_g[i]), row1(idx_k_norm_b[i])),
            (wq.T, wckv, wiq.T, wsm, wiw_t, wb, wz, wxbc, wdt, wdt_t), wuv_t, vone,
            (short_conv_w[i].astype(F32), ssm_conv_w[i].astype(F32), ssm_conv_b[i].astype(F32),
             ssm_dt_bias[i].astype(F32), ssm_a_log[i].astype(F32),
             jnp.repeat(ssm_d[i].astype(F32), SSM_HEAD_DIM), ssm_norm[i].astype(F32)), s // KT)

        a_out = _attn_call(
            q_t, iq_t, iw_t, ckv.reshape(bsz, s, A_KV_RANK), vt4, ik.reshape(bsz, s, IDX_DIM),
            jnp.transpose(w_uk[i], (1, 0, 2)).astype(BF16),
            bias_tiles, tri, k_top=k_top, near=near)

        wo = w_out[i].astype(BF16)
        na, nbw = A_HEADS * A_V_DIM, A_HEADS * A_V_DIM + B_WIDTH
        x2 = _dense_call(
            x2, a_out, b_out, c_out, p[i].reshape(t, PLE_DIM),
            row1(post_mix_norm[i]), row1(pre_ffn_norm[i]), row1(post_ffn_norm[i]),
            wo[:na], wo[na:nbw], wo[nbw:], w_ffn_gate[i].astype(BF16), w_ffn_up[i].astype(BF16),
            w_ffn_down[i].astype(BF16), w_ple_gate[i].astype(BF16), w_ple_proj[i].astype(BF16))
    return x2.reshape(bsz, s, d)
```

```python
import functools
import math

import numpy as np
import jax
import jax.numpy as jnp
from jax import lax
from jax.experimental import pallas as pl
from jax.experimental.pallas import tpu as pltpu

F32, BF16, I32 = jnp.float32, jnp.bfloat16, jnp.int32

D_MODEL = 1024
DEPTH = 2
CHUNK = 64
A_HEADS = 8
A_QK_DIM = 64
A_V_DIM = 64
A_KV_RANK = 256
IDX_HEADS = 4
IDX_DIM = 64
TOPK_MAX = 256
REL_BUCKETS = 32
REL_MAX_DIST = 1024
B_WIDTH = 512
SHORT_CONV = 3
SSM_HEADS = 16
SSM_HEAD_DIM = 64
SSM_INNER = SSM_HEADS * SSM_HEAD_DIM
SSM_GROUPS = 2
SSM_STATE = 128
SSM_CONV = 4
SSM_XBC = SSM_INNER + 2 * SSM_GROUPS * SSM_STATE
MIX_WIDTH = A_HEADS * A_V_DIM + B_WIDTH + SSM_INNER
D_FF = -(-8 * D_MODEL // (3 * 256)) * 256
PLE_DIM = 256
NORM_EPS = 1e-6
IN_SPLITS = (A_HEADS * A_QK_DIM, A_KV_RANK, IDX_HEADS * IDX_DIM, IDX_DIM, IDX_HEADS,
             B_WIDTH, B_WIDTH, B_WIDTH, SSM_INNER, SSM_XBC, SSM_HEADS)
IN_OFFSETS = tuple(int(v) for v in np.concatenate([[0], np.cumsum(IN_SPLITS)]))

LANES = 128
SUBLANES = 8
VMEM_LIMIT_BYTES = 56 * 1024 * 1024

QB = 2 * LANES
QGROUPS = QB // LANES
KT = 256
SSD_L = 128
IN_SUBTILES = 2
TM_SUB = 256
ROW_SUBTILES = 2
TM_OUT = TM_SUB * ROW_SUBTILES
SMALL_W = LANES
IW_OFF = IDX_DIM
INT_MIN = np.int32(-2 ** 31)
LOG2E = math.log2(math.e)
BF16_ROWS = 2 * SUBLANES
V_ROWS = A_V_DIM + BF16_ROWS

def _const_spec(shape):
    nd = len(shape)
    return pl.BlockSpec(shape, lambda *_: (0,) * nd, pipeline_mode=pl.Buffered(1))


def _params(sem):
    return pltpu.CompilerParams(dimension_semantics=sem, vmem_limit_bytes=VMEM_LIMIT_BYTES)


def _rms(x, g):
    return x * lax.rsqrt(jnp.mean(x * x, axis=-1, keepdims=True) + NORM_EPS) * g


def _sigmoid(x):
    return 1.0 / (1.0 + jnp.exp(-x))


def _split3(v):
    hi = v.astype(BF16)
    r1 = v - hi.astype(F32)
    mid = r1.astype(BF16)
    lo = (r1 - mid.astype(F32)).astype(BF16)
    return hi, mid, lo


def _dot(a, b):
    return jnp.dot(a, b, preferred_element_type=F32)


def _t5_bucket_np(rel):
    half = REL_BUCKETS // 2
    max_exact = half // 2
    ret = np.where(rel > 0, half, 0)
    n = np.abs(rel)
    nf = np.maximum(n, 1).astype(np.float32)
    large = max_exact + (np.log(nf / np.float32(max_exact)) / np.float32(math.log(REL_MAX_DIST / max_exact))
                         * np.float32(half - max_exact)).astype(np.int32)
    large = np.minimum(large, half - 1)
    return (ret + np.where(n < max_exact, n, large)).astype(np.int32)


@functools.lru_cache(maxsize=None)
def _bucket_tiles():
    kr = np.arange(KT)[:, None]
    qc = np.arange(QB)[None, :]
    tiles, near = [], None
    for o in range(64):
        t = _t5_bucket_np(kr - qc - o * QB)
        tiles.append(t)
    for o in range(63, -1, -1):
        if not (np.all(tiles[o] == tiles[63][0, 0])):
            near = o + 1
            break
    assert near is not None and 0 < near < 63
    return np.stack(tiles[:near + 1]), near, int(tiles[63][0, 0])


def _bias_kernel(rb_ref, bk_ref, out_ref, *, far_bucket):
    bk = bk_ref[0]
    for h in range(A_HEADS):
        acc = jnp.zeros(bk.shape, F32)
        for b in range(REL_BUCKETS):
            acc = jnp.where(bk == b, (rb_ref[b, h] - rb_ref[far_bucket, h]) * LOG2E, acc)
        out_ref[0, :, h * QB:(h + 1) * QB] = acc


def _bias_tiles_call(rel_bias, bucket_tiles, far_bucket):
    n = bucket_tiles.shape[0]
    return pl.pallas_call(
        functools.partial(_bias_kernel, far_bucket=far_bucket),
        grid=(n,),
        in_specs=[pl.BlockSpec(memory_space=pltpu.SMEM),
                  pl.BlockSpec((1, KT, QB), lambda o: (o, 0, 0))],
        out_specs=pl.BlockSpec((1, KT, A_HEADS * QB), lambda o: (o, 0, 0)),
        out_shape=jax.ShapeDtypeStruct((n, KT, A_HEADS * QB), F32),
        compiler_params=_params(("arbitrary",)),
        name="rel_bias_tiles",
    )(rel_bias, bucket_tiles)


def _wprep_kernel(w_ref, wq_out, wckv_out, wiq_out, wsm_out, wb_out, wz_out, wxbc_out, wdt_out):
    o_q, o_ckv, o_iq, o_ik, _, o_bg, _, _, o_z, o_xbc, o_dt, o_end = IN_OFFSETS
    wq_out[...] = w_ref[:, o_q:o_ckv].astype(BF16)
    wckv_out[...] = w_ref[:, o_ckv:o_iq].astype(BF16)
    wiq_out[...] = w_ref[:, o_iq:o_ik].astype(BF16)
    wsm_out[...] = w_ref[:, o_ik:o_ik + SMALL_W].astype(BF16)
    tail = w_ref[:, o_bg:o_end].astype(BF16)
    wb_out[...] = tail[:, :o_z - o_bg]
    wz_out[...] = tail[:, o_z - o_bg:o_xbc - o_bg]
    wxbc_out[...] = tail[:, o_xbc - o_bg:o_dt - o_bg]
    wdt_out[...] = jnp.zeros(wdt_out.shape, BF16)
    wdt_out[:, :o_end - o_dt] = tail[:, o_dt - o_bg:]


def _wprep_call(w):
    d, n = w.shape
    tr = 256
    widths = [IN_SPLITS[0], IN_SPLITS[1], IN_SPLITS[2], SMALL_W, 3 * B_WIDTH, SSM_INNER, SSM_XBC, SMALL_W]
    return pl.pallas_call(
        _wprep_kernel,
        grid=(d // tr,),
        in_specs=[pl.BlockSpec((tr, n), lambda r: (r, 0))],
        out_specs=[pl.BlockSpec((tr, wd), lambda r: (r, 0)) for wd in widths],
        out_shape=[jax.ShapeDtypeStruct((d, wd), BF16) for wd in widths],
        compiler_params=_params(("parallel",)),
        name="inproj_weight_prep",
    )(w)


def _dot_nt(a, b):
    return lax.dot_general(a, b, (((1,), (1,)), ((), ())), preferred_element_type=F32)


def _bit_transpose32(words):
    a = list(words)
    m, j = 0x0000FFFF, 16
    while j:
        k = 0
        while k < 32:
            t = (a[k] ^ lax.shift_right_logical(a[k + j], np.int32(j))) & np.int32(m)
            a[k] = a[k] ^ t
            a[k + j] = a[k + j] ^ (t << j)
            k = (k + j + 1) & ~j
        j >>= 1
        m = (m ^ (m << j)) & 0xFFFFFFFF
    return a


def _attn_kernel(q_ref, iq_ref, iw_ref, ckv_ref, vt_ref, ik_ref, wuk_ref, bias_ref, tri_ref,
                 o_ref, keys_ref, planes_ref, qlat_ref, acc_ref, p0_ref, p1_ref, lg0_ref, lg1_ref,
                 *, k_top, near):
    i = pl.program_id(1)
    nt = ((i + 1) * QB + KT - 1) // KT
    lane = lax.broadcasted_iota(I32, (1, QB), 1)
    qchunk = (i * QB + lane) >> int(math.log2(CHUNK))
    krow = lax.broadcasted_iota(I32, (KT, QB), 0)

    q_t = q_ref[0]
    for h in range(A_HEADS):
        ql = _dot(wuk_ref[h], q_t[h * A_QK_DIM:(h + 1) * A_QK_DIM, :])
        qlat_ref[:, h * QB:(h + 1) * QB] = (ql * (A_QK_DIM ** -0.5 * LOG2E)).astype(BF16)

    iq_t = iq_ref[0]
    iw = iw_ref[0][:IDX_HEADS, :]

    def score_tile(j, carry, masked):
        r0 = pl.multiple_of(j * KT, KT)
        ikt = ik_ref[0, pl.ds(r0, KT), :]
        s = jnp.zeros((KT, QB), F32)
        for h in range(IDX_HEADS):
            raw = _dot(ikt, iq_t[h * IDX_DIM:(h + 1) * IDX_DIM, :])
            s = s + jnp.maximum(raw, 0.0) * iw[h:h + 1, :]
        if masked:
            adm = ((r0 + krow) >> int(math.log2(CHUNK))) <= qchunk
            s = jnp.where(adm, s, -jnp.inf)
        bits = lax.bitcast_convert_type(s, I32)
        bits = jnp.where(bits == INT_MIN, 0, bits)
        key = bits ^ ((bits >> 31) & np.int32(0x7FFFFFFF))
        keys_ref[pl.ds(r0, KT), :] = key
        ukey = key ^ INT_MIN
        planes = _bit_transpose32([ukey[v * SUBLANES:(v + 1) * SUBLANES, :] for v in range(32)])
        p0 = pl.multiple_of(j * SUBLANES, SUBLANES)
        for b in range(32):
            planes_ref[b, pl.ds(p0, SUBLANES), :] = planes[31 - b]
        return carry

    @pl.when(i == 0)
    def _():
        planes_ref[...] = jnp.zeros(planes_ref.shape, I32)

    def score_pair(u, carry):
        score_tile(2 * u, 0, masked=False)
        score_tile(2 * u + 1, 0, masked=False)
        return carry

    lax.fori_loop(0, (nt - 1) // 2, score_pair, 0)

    @pl.when((nt - 1) % 2 == 1)
    def _():
        score_tile(nt - 2, 0, masked=False)

    score_tile(nt - 1, 0, masked=True)

    n_prow = planes_ref.shape[1]
    prow = lax.broadcasted_iota(I32, (n_prow, QB), 0)
    kf = float(k_top)

    def search_step(k, carry):
        live, c_gt, thr_u = carry
        b = 31 - k
        x = live & planes_ref[b]
        pc = lax.population_count(x)
        parts = [pc[r * SUBLANES:(r + 1) * SUBLANES, :] for r in range(n_prow // SUBLANES)]
        while len(parts) > 1:
            parts = [parts[k2] + parts[k2 + 1] for k2 in range(0, len(parts) - 1, 2)] + (
                [parts[-1]] if len(parts) % 2 else [])
        cnt = c_gt + jnp.sum(parts[0].astype(F32), axis=0, keepdims=True)
        take = cnt >= kf
        live = jnp.where(take, x, live ^ x)
        c_gt = jnp.where(take, c_gt, cnt)
        thr_u = thr_u | jnp.where(take, jnp.left_shift(jnp.int32(1), b), np.int32(0))
        return live, c_gt, thr_u

    _, c_gt, thr_u = lax.fori_loop(
        0, 32, search_step,
        (jnp.where(prow < nt * SUBLANES, np.int32(-1), np.int32(0)),
         jnp.zeros((1, QB), F32), jnp.zeros((1, QB), I32)))
    thr = thr_u ^ INT_MIN
    need = kf - c_gt

    p_refs, lg_refs = (p0_ref, p1_ref), (lg0_ref, lg1_ref)
    n_cols = A_HEADS * QGROUPS
    acc_ref[...] = jnp.zeros(acc_ref.shape, F32)
    p1_ref[...] = jnp.zeros(p1_ref.shape, BF16)

    def stage_scores(j, cnt_eq, slot, far=False):
        r0 = pl.multiple_of(j * KT, KT)
        kt = keys_ref[pl.ds(r0, KT), :]
        eq = kt == thr
        eqf = jnp.where(eq, 1.0, 0.0)
        rank = _dot(tri_ref[...], eqf.astype(BF16)) + cnt_eq
        sel = (kt > thr) | (eq & (rank <= need))
        if not far:
            sel = sel & (((r0 + krow) >> int(math.log2(CHUNK))) <= qchunk)
            bias = bias_ref[jnp.minimum(i - j * (KT // QB), near)]
        logits = _dot(ckv_ref[0, pl.ds(r0, KT), :], qlat_ref[...])
        tmax = [[] for _ in range(QGROUPS)]
        for h in range(A_HEADS):
            for g in range(QGROUPS):
                sl = slice(h * QB + g * LANES, h * QB + (g + 1) * LANES)
                lh = logits[:, sl] if far else logits[:, sl] + bias[:, sl]
                lh = jnp.where(sel[:, g * LANES:(g + 1) * LANES], lh, -jnp.inf)
                lg_refs[slot][:, sl] = lh
                tmax[g].append(jnp.max(lh, axis=0, keepdims=True))
        tmax = tuple(jnp.concatenate(rows, axis=0) for rows in tmax)
        return tmax, cnt_eq + jnp.sum(eqf, axis=0, keepdims=True)

    def stage_probs(ms, tmax, slot):
        new_ms, alphas = [], []
        for g in range(QGROUPS):
            m_new = jnp.maximum(ms[g], tmax[g])
            m_safe = jnp.where(m_new == -jnp.inf, 0.0, m_new)
            alphas.append(jnp.exp2(ms[g] - m_safe))
            new_ms.append(m_new)
            for h in range(A_HEADS):
                sl = slice(h * QB + g * LANES, h * QB + (g + 1) * LANES)
                p_refs[slot][:, sl] = jnp.exp2(lg_refs[slot][:, sl] - m_safe[h:h + 1, :]).astype(BF16)
        return tuple(new_ms), tuple(alphas)

    def stage_values(alphas, jp, slot):
        for h in range(A_HEADS):
            vs = slice(h * V_ROWS, (h + 1) * V_ROWS)
            pv = _dot(vt_ref[0, jp, vs, :], p_refs[slot][:, h * QB:(h + 1) * QB])
            alpha = jnp.concatenate([a[h:h + 1, :] for a in alphas], axis=1)
            acc_ref[vs, :] = acc_ref[vs, :] * alpha + pv

    def trip(t, carry, slot, far=False):
        ms, tmax, cnt_eq, alphas = carry
        stage_values(alphas, jnp.maximum(t - 1, 0), 1 - slot)
        ms, alphas = stage_probs(ms, tmax, slot)
        tmax, cnt_eq = stage_scores(t + 1, cnt_eq, 1 - slot, far)
        return ms, tmax, cnt_eq, alphas

    def finish(t, carry, slot):
        ms, tmax, _, alphas = carry
        stage_values(alphas, jnp.maximum(t - 1, 0), 1 - slot)
        _, alphas = stage_probs(ms, tmax, slot)
        stage_values(alphas, t, slot)

    def pair(u, carry, far=False):
        return trip(2 * u + 1, trip(2 * u, carry, 0, far), 1, far)

    tmax0, cnt_eq0 = stage_scores(0, jnp.zeros((1, QB), F32), 0)
    init = (tuple(jnp.full((A_HEADS, LANES), -jnp.inf, F32) for _ in range(QGROUPS)), tmax0, cnt_eq0,
            tuple(jnp.ones((A_HEADS, LANES), F32) for _ in range(QGROUPS)))
    n_trips = nt - 1
    n_far_pairs = jnp.maximum(i * (QB // KT) - near, 0) // 2
    carry = lax.fori_loop(0, n_far_pairs, functools.partial(pair, far=True), init)
    carry = lax.fori_loop(n_far_pairs, n_trips // 2, pair, carry)

    @pl.when(n_trips % 2 == 0)
    def _():
        finish(nt - 1, carry, 0)

    @pl.when(n_trips % 2 == 1)
    def _():
        finish(nt - 1, trip(nt - 2, carry, 0), 1)

    a_t = jnp.concatenate(
        [acc_ref[h * V_ROWS:h * V_ROWS + A_V_DIM, :]
         * (1.0 / acc_ref[h * V_ROWS + A_V_DIM:h * V_ROWS + A_V_DIM + 1, :]) for h in range(A_HEADS)],
        axis=0)
    o_ref[...] = a_t.T.astype(BF16)


def _attn_call(q_t, iq_t, iw_t, ckv3, vt4, ik3, wuk, bias_tiles, tri, *, k_top, near):
    b, s = ckv3.shape[0], ckv3.shape[1]
    nb = s // QB
    n_kt = s // KT
    hv = A_HEADS * A_V_DIM
    hvr = A_HEADS * V_ROWS
    kern = functools.partial(_attn_kernel, k_top=k_top, near=near)
    row = lambda w: pl.BlockSpec((QB, w), lambda bb, i: (bb * nb + i, 0))
    qblk = lambda a: pl.BlockSpec((1, a.shape[1], QB), lambda bb, i: (bb * nb + i, 0, 0))
    return pl.pallas_call(
        kern,
        grid=(b, nb),
        in_specs=[
            qblk(q_t), qblk(iq_t), qblk(iw_t),
            pl.BlockSpec((1, s, A_KV_RANK), lambda bb, i: (bb, 0, 0)),
            pl.BlockSpec((1, n_kt, hvr, KT), lambda bb, i: (bb, 0, 0, 0)),
            pl.BlockSpec((1, s, IDX_DIM), lambda bb, i: (bb, 0, 0)),
            _const_spec(wuk.shape), _const_spec(bias_tiles.shape), _const_spec(tri.shape),
        ],
        out_specs=row(hv),
        out_shape=jax.ShapeDtypeStruct((b * s, hv), BF16),
        scratch_shapes=[pltpu.VMEM((s, QB), I32), pltpu.VMEM((32, s // 32, QB), I32),
                        pltpu.VMEM((A_KV_RANK, A_HEADS * QB), BF16),
                        pltpu.VMEM((hvr, QB), F32),
                        pltpu.VMEM((KT, A_HEADS * QB), BF16), pltpu.VMEM((KT, A_HEADS * QB), BF16),
                        pltpu.VMEM((KT, A_HEADS * QB), F32), pltpu.VMEM((KT, A_HEADS * QB), F32)],
        compiler_params=_params(("parallel", "arbitrary")),
        name="dsa_attention",
    )(q_t, iq_t, iw_t, ckv3, vt4, ik3, wuk, bias_tiles, tri)


def _softplus(x):
    return jnp.maximum(x, 0.0) + jnp.log1p(jnp.exp(-jnp.abs(x)))


def _ssd_chunk(xa, zz, dt_raw, dt_t_raw, state, dtb_r, dtb_c, alog_r, alog_c, dexp, nw, tril, triu, eh,
               fill=lambda: None):
    L = SSD_L
    gw = SSM_INNER // SSM_GROUPS
    xs = xa[:, :SSM_INNER]
    dt = _softplus(dt_raw + dtb_r)
    a = dt * (-jnp.exp(alog_r))
    a_cs = sum(_dot(tril, part) for part in _split3(a))
    fill()
    dt_t = _softplus(dt_t_raw + dtb_c)
    a_t = dt_t * (-jnp.exp(alog_c))
    a_cs_t = sum(_dot(part, triu) for part in _split3(a_t))

    def expand(v):
        v3 = jnp.concatenate([v, v, v], axis=1)
        hi, mid, lo = _split3(v3)
        term = lax.broadcasted_iota(I32, v3.shape, 1) // SSM_HEADS
        return _dot(jnp.where(term == 0, hi, jnp.where(term == 1, mid, lo)), eh)

    xdt = xs * expand(dt)
    fill()
    e_cs = expand(jnp.exp(a_cs))
    dec = expand(jnp.exp(a_cs[L - 1:L, :] - a_cs))
    fill()
    chunk_dec = e_cs[L - 1:L, :]

    row = lax.broadcasted_iota(I32, (L, L), 0)
    col = lax.broadcasted_iota(I32, (L, L), 1)
    causal = row >= col
    lo_half = lax.broadcasted_iota(I32, (L, LANES), 1) < SSM_HEAD_DIM
    hpg = SSM_HEADS // SSM_GROUPS

    y_parts = []
    for g in range(SSM_GROUPS):
        bg = xa[:, SSM_INNER + g * SSM_STATE:SSM_INNER + (g + 1) * SSM_STATE].astype(BF16)
        cg = xa[:, SSM_INNER + (SSM_GROUPS + g) * SSM_STATE:
                SSM_INNER + (SSM_GROUPS + g + 1) * SSM_STATE].astype(BF16)
        cb = _dot_nt(cg, bg)
        fill()
        sl = slice(g * gw, (g + 1) * gw)
        st = state[g]
        y_off = _dot(cg, st.astype(BF16)) * e_cs[:, sl]
        for k in range(hpg // 2):
            xp = xdt[:, g * gw + k * LANES:g * gw + (k + 1) * LANES]
            x2 = jnp.concatenate([jnp.where(lo_half, xp, 0.0), jnp.where(lo_half, 0.0, xp)], axis=0)
            ms = []
            for hh in range(2):
                h = g * hpg + 2 * k + hh
                seg = a_cs[:, h:h + 1] - a_cs_t[h:h + 1, :]
                ms.append((cb * jnp.exp(jnp.where(causal, seg, -jnp.inf))).astype(BF16))
            y_parts.append(y_off[:, k * LANES:(k + 1) * LANES]
                           + _dot(jnp.concatenate(ms, axis=1), x2.astype(BF16)))
            if k % 2 == 1:
                fill()
        xw = (dec[:, sl] * xdt[:, sl]).astype(BF16)
        upd = lax.dot_general(bg, xw, (((0,), (0,)), ((), ())), preferred_element_type=F32)
        state[g] = chunk_dec[:, sl] * st + upd

    y = jnp.concatenate(y_parts, axis=1) + dexp * xs
    y = y * (zz * _sigmoid(zz))
    return jnp.concatenate(
        [_rms(y[:, g * gw:(g + 1) * gw], nw[:, g * gw:(g + 1) * gw]) for g in range(SSM_GROUPS)], axis=1)


def _inproj_mixer_kernel(x_ref, g_ref, kvn_ref, ikg_ref, ikb_ref,
                         wqt_ref, wckv_ref, wiqt_ref, wsm_ref, wiwt_ref, wb_ref, wz_ref, wxbc_ref,
                         wdt_ref, wdtt_ref, wuv_ref, vone_ref,
                         scw_ref, cw_ref, cb_ref, dtb_r_ref, dtb_c_ref, alog_r_ref, alog_c_ref,
                         dexp_ref, nw_ref, tril_ref, triu_ref, eh_ref,
                         qt_out, ckv_out, vt_out, iqt_out, iwt_out, ik_out, bo_out, co_out,
                         ubuf, xbuf, state, *, steps_per_seq):
    P = SUBLANES
    tm = IN_SUBTILES * KT

    @pl.when(pl.program_id(0) % steps_per_seq == 0)
    def _():
        ubuf[0:P, :] = jnp.zeros((P, B_WIDTH), F32)
        xbuf[0:P, :] = jnp.zeros((P, SSM_XBC), F32)
        state[...] = jnp.zeros(state.shape, F32)

    subs = [slice(k * KT, (k + 1) * KT) for k in range(IN_SUBTILES)]
    h = [_rms(x_ref[rows, :], g_ref[...]).astype(BF16) for rows in subs]
    ssd_consts = (dtb_r_ref[...], dtb_c_ref[...], alog_r_ref[...], alog_c_ref[...], dexp_ref[...],
                  nw_ref[...], tril_ref[...], triu_ref[...], eh_ref[...])

    def proj_mixer(k):
        return dict(bm=_dot(h[k], wb_ref[...]),
                    xr=_dot(h[k], wxbc_ref[...]), zz=_dot(h[k], wz_ref[...]),
                    dt=_dot(h[k], wdt_ref[...])[:, :SSM_HEADS],
                    dt_t=_dot_nt(wdtt_ref[...], h[k]))

    def short_conv(k, bm):
        base = P + k * KT
        u = bm[:, B_WIDTH:2 * B_WIDTH] * bm[:, 2 * B_WIDTH:3 * B_WIDTH]
        ubuf[base:base + KT, :] = u
        conv = scw_ref[SHORT_CONV - 1:SHORT_CONV, :] * u
        for j in range(SHORT_CONV - 1):
            off = base - (SHORT_CONV - 1) + j
            conv = conv + scw_ref[j:j + 1, :] * ubuf[off:off + KT, :]
        bo_out[subs[k], :] = (bm[:, :B_WIDTH] * conv).astype(BF16)

    def ssm_conv(k, xr):
        base = P + k * KT
        xbuf[base:base + KT, :] = xr
        xc = cw_ref[SSM_CONV - 1:SSM_CONV, :] * xr + cb_ref[...]
        for j in range(SSM_CONV - 1):
            off = base - (SSM_CONV - 1) + j
            xc = xc + cw_ref[j:j + 1, :] * xbuf[off:off + KT, :]
        return xc * _sigmoid(xc)

    def scan_chunk(k, c, xa, pm, fill):
        cs = slice(c * SSD_L, (c + 1) * SSD_L)
        y = _ssd_chunk(xa[cs, :], pm["zz"][cs, :], pm["dt"][cs, :], pm["dt_t"][:, cs], state, *ssd_consts,
                       fill=fill)
        r0 = k * KT + c * SSD_L
        co_out[r0:r0 + SSD_L, :] = y.astype(BF16)

    def attn_keys(k):
        ckv = _rms(_dot(h[k], wckv_ref[...]), kvn_ref[...]).astype(BF16)
        ckv_out[subs[k], :] = ckv
        ik = _dot(h[k], wsm_ref[...])[:, :IDX_DIM]
        mu = jnp.mean(ik, axis=-1, keepdims=True)
        var = jnp.mean(jnp.square(ik - mu), axis=-1, keepdims=True)
        ik_out[subs[k], :] = ((ik - mu) * lax.rsqrt(var + NORM_EPS) * ikg_ref[...] + ikb_ref[...]).astype(BF16)
        return ckv

    def attn_queries(k):
        qt_out[k] = _dot_nt(wqt_ref[...], h[k]).astype(BF16)
        iqt_out[k] = (_dot_nt(wiqt_ref[...], h[k]) * (IDX_DIM ** -0.5)).astype(BF16)
        iwt_out[k] = _dot_nt(wiwt_ref[...], h[k]) * (IDX_HEADS ** -0.5)

    def attn_values(k, ckv):
        vt_out[0, k] = (_dot_nt(wuv_ref[...], ckv) + vone_ref[...]).astype(BF16)

    assert IN_SUBTILES == 2 and KT // SSD_L == 2
    blk = KT
    tasks = []

    def fill():
        if tasks:
            tasks.pop(0)()

    def drain():
        while tasks:
            fill()

    def col_blocks(k, got, name, w_ref):
        for c0 in range(0, w_ref.shape[1], blk):
            tasks.append(lambda c0=c0: got.setdefault(name, []).append(_dot(h[k], w_ref[:, c0:c0 + blk])))

    def dt_task(k, got):
        tasks.append(lambda: got.update(dt=_dot(h[k], wdt_ref[...])[:, :SSM_HEADS],
                                        dt_t=_dot_nt(wdtt_ref[...], h[k])))

    def joined(got, name):
        return jnp.concatenate(got[name], axis=1)

    pm0 = proj_mixer(0)
    got1 = {}
    col_blocks(1, got1, "bm", wb_ref)
    col_blocks(1, got1, "xr", wxbc_ref)
    col_blocks(1, got1, "zz", wz_ref)
    dt_task(1, got1)
    short_conv(0, pm0["bm"])
    fill()
    xa0 = ssm_conv(0, pm0["xr"])
    fill()
    scan_chunk(0, 0, xa0, pm0, fill)
    scan_chunk(0, 1, xa0, pm0, fill)
    drain()
    pm1 = dict(bm=joined(got1, "bm"), xr=joined(got1, "xr"), zz=joined(got1, "zz"),
               dt=got1["dt"], dt_t=got1["dt_t"])

    ckvs = {}
    tasks.extend([lambda: ckvs.update({0: attn_keys(0)}), lambda: attn_queries(0),
                  lambda: ckvs.update({1: attn_keys(1)}), lambda: attn_values(0, ckvs[0]),
                  lambda: attn_queries(1), lambda: attn_values(1, ckvs[1])])
    short_conv(1, pm1["bm"])
    fill()
    xa1 = ssm_conv(1, pm1["xr"])
    fill()
    scan_chunk(1, 0, xa1, pm1, fill)
    scan_chunk(1, 1, xa1, pm1, fill)
    while tasks:
        fill()

    ubuf[0:P, :] = ubuf[tm:tm + P, :]
    xbuf[0:P, :] = xbuf[tm:tm + P, :]


def _inproj_mixer_call(x2, norm_consts, proj_weights, wuv, vone, mixer_params, n_kt):
    t = x2.shape[0]
    assert KT == QB and n_kt % IN_SUBTILES == 0 and KT % SSD_L == 0
    tm = IN_SUBTILES * KT
    steps_per_seq = n_kt // IN_SUBTILES
    scw, cw, cb, dtb, alog, dexp, nw = mixer_params
    L = SSD_L
    tril = jnp.asarray(np.tril(np.ones((L, L), np.float32)), BF16)
    triu = jnp.asarray(np.triu(np.ones((L, L), np.float32)), BF16)
    eh = jnp.asarray(np.tile(np.repeat(np.eye(SSM_HEADS, dtype=np.float32), SSM_HEAD_DIM, axis=1), (3, 1)), BF16)
    mixer_consts = [scw, cw, cb.reshape(1, -1), dtb.reshape(1, -1), dtb.reshape(-1, 1),
                    alog.reshape(1, -1), alog.reshape(-1, 1), dexp.reshape(1, -1), nw.reshape(1, -1),
                    tril, triu, eh]
    wqt, wckv, wiqt, wiwt = proj_weights[0], proj_weights[1], proj_weights[2], proj_weights[4]
    row = lambda w: pl.BlockSpec((tm, w), lambda r: (r, 0))
    rows = lambda w, dt: (row(w), jax.ShapeDtypeStruct((t, w), dt))
    qblk = lambda w, dt: (pl.BlockSpec((IN_SUBTILES, w, QB), lambda r: (r, 0, 0)),
                          jax.ShapeDtypeStruct((t // QB, w, QB), dt))
    hv = wuv.shape[0]
    outs = [qblk(wqt.shape[0], BF16), rows(wckv.shape[1], BF16),
            (pl.BlockSpec((1, IN_SUBTILES, hv, KT), lambda r: (r // steps_per_seq, r % steps_per_seq, 0, 0)),
             jax.ShapeDtypeStruct((t // (n_kt * KT), n_kt, hv, KT), BF16)),
            qblk(wiqt.shape[0], BF16), qblk(wiwt.shape[0], F32), rows(IDX_DIM, BF16),
            rows(B_WIDTH, BF16), rows(SSM_INNER, BF16)]
    consts = [*norm_consts, *proj_weights, wuv, vone, *mixer_consts]
    return pl.pallas_call(
        functools.partial(_inproj_mixer_kernel, steps_per_seq=steps_per_seq),
        grid=(t // tm,),
        in_specs=[row(D_MODEL)] + [_const_spec(c.shape) for c in consts],
        out_specs=[spec for spec, _ in outs],
        out_shape=[shape for _, shape in outs],
        scratch_shapes=[pltpu.VMEM((tm + SUBLANES, B_WIDTH), F32),
                        pltpu.VMEM((tm + SUBLANES, SSM_XBC), F32),
                        pltpu.VMEM((SSM_GROUPS, SSM_STATE, SSM_INNER // SSM_GROUPS), F32)],
        compiler_params=_params(("arbitrary",)),
        name="inproj_conv_ssd",
    )(x2, *consts)


def _dense_kernel(x_ref, a_ref, b_ref, c_ref, p_ref, g_post_ref, g_pre_ref, g_fpost_ref,
                  woa_ref, wob_ref, woc_ref, wg_ref, wu_ref, wd_ref, wpg_ref, wpp_ref, o_ref):
    subs = [slice(r * TM_SUB, (r + 1) * TM_SUB) for r in range(ROW_SUBTILES)]
    mix = [_dot(a_ref[rows, :], woa_ref[...]) + _dot(b_ref[rows, :], wob_ref[...])
           + _dot(c_ref[rows, :], woc_ref[...]) for rows in subs]
    ple = [_dot(p_ref[rows, :].astype(BF16), wpp_ref[...]) for rows in subs]
    x = [x_ref[rows, :] + _rms(m, g_post_ref[...]) for rows, m in zip(subs, mix)]
    h = [_rms(xx, g_pre_ref[...]).astype(BF16) for xx in x]
    gate = [_dot(hh, wg_ref[...]) for hh in h]
    up = [_dot(hh, wu_ref[...]) for hh in h]
    act = [(g * _sigmoid(g) * u).astype(BF16) for g, u in zip(gate, up)]
    f = [_dot(aa, wd_ref[...]) for aa in act]
    x = [xx + _rms(ff, g_fpost_ref[...]) for xx, ff in zip(x, f)]
    pg = [_sigmoid(_dot(xx.astype(BF16), wpg_ref[...])) for xx in x]
    for rows, xx, g, e in zip(subs, x, pg, ple):
        o_ref[rows, :] = xx + g * e


def _dense_call(x2, a, b, c, p2, g_post, g_pre, g_fpost, woa, wob, woc, wg, wu, wd, wpg, wpp):
    t = x2.shape[0]
    tm = TM_OUT
    row = lambda w: pl.BlockSpec((tm, w), lambda r: (r, 0))
    consts = [g_post, g_pre, g_fpost, woa, wob, woc, wg, wu, wd, wpg, wpp]
    return pl.pallas_call(
        _dense_kernel,
        grid=(t // tm,),
        in_specs=[row(D_MODEL), row(a.shape[1]), row(b.shape[1]), row(c.shape[1]), row(PLE_DIM)]
                 + [_const_spec(w.shape) for w in consts],
        out_specs=row(D_MODEL),
        out_shape=jax.ShapeDtypeStruct((t, D_MODEL), F32),
        compiler_params=_params(("parallel",)),
        name="outproj_ffn_ple",
    )(x2, a, b, c, p2, *consts)


def kernel(x, p, pre_mix_norm, post_mix_norm, pre_ffn_norm, post_ffn_norm, w_in, kv_norm, idx_k_norm_g, idx_k_norm_b, w_uk, w_uv, rel_bias, short_conv_w, ssm_conv_w, ssm_conv_b, ssm_dt_bias, ssm_a_log, ssm_d, ssm_norm, w_out, w_ffn_gate, w_ffn_up, w_ffn_down, w_ple_proj, w_ple_gate):
    bsz, s, d = x.shape
    assert d == D_MODEL and s % (IN_SUBTILES * KT) == 0 and s % QB == 0 and s % SSD_L == 0
    assert (bsz * s) % TM_OUT == 0
    t = bsz * s
    k_top = min(TOPK_MAX, s // 4)

    bucket_np, near, far_bucket = _bucket_tiles()
    bias_tiles = _bias_tiles_call(rel_bias.astype(F32), jnp.asarray(bucket_np), far_bucket)
    tri = jnp.asarray(np.tril(np.ones((KT, KT), np.float32)), BF16)
    row1 = lambda v: v.reshape(1, -1).astype(F32)
    vone_np = np.zeros((A_HEADS, V_ROWS, 1), np.float32)
    vone_np[:, A_V_DIM] = 1.0
    vone = jnp.asarray(vone_np.reshape(A_HEADS * V_ROWS, 1))

    x2 = x.reshape(t, d)
    for i in range(DEPTH):
        wparts = _wprep_call(w_in[i].astype(F32))
        wuv_t = jnp.transpose(w_uv[i], (1, 2, 0)).astype(BF16)
        wuv_t = jnp.pad(wuv_t, ((0, 0), (0, V_ROWS - A_V_DIM), (0, 0))).reshape(A_HEADS * V_ROWS, A_KV_RANK)
        wq, wckv, wiq, wsm, wb, wz, wxbc, wdt = wparts
        wiw_t = wsm[:, IW_OFF:IW_OFF + SUBLANES].T
        wdt_t = wdt[:, :SSM_HEADS].T
        q_t, ckv, vt4, iq_t, iw_t, ik, b_out, c_out = _inproj_mixer_call(
            x2, (row1(pre_mix_norm[i]), row1(kv_norm[i]), row1(idx_k_norm_g[i]), row1(idx_k_norm_b[i])),
            (wq.T, wckv, wiq.T, wsm, wiw_t, wb, wz, wxbc, wdt, wdt_t), wuv_t, vone,
            (short_conv_w[i].astype(F32), ssm_conv_w[i].astype(F32), ssm_conv_b[i].astype(F32),
             ssm_dt_bias[i].astype(F32), ssm_a_log[i].astype(F32),
             jnp.repeat(ssm_d[i].astype(F32), SSM_HEAD_DIM), ssm_norm[i].astype(F32)), s // KT)

        a_out = _attn_call(
            q_t, iq_t, iw_t, ckv.reshape(bsz, s, A_KV_RANK), vt4, ik.reshape(bsz, s, IDX_DIM),
            jnp.transpose(w_uk[i], (1, 0, 2)).astype(BF16),
            bias_tiles, tri, k_top=k_top, near=near)

        wo = w_out[i].astype(BF16)
        na, nbw = A_HEADS * A_V_DIM, A_HEADS * A_V_DIM + B_WIDTH
        x2 = _dense_call(
            x2, a_out, b_out, c_out, p[i].reshape(t, PLE_DIM),
            row1(post_mix_norm[i]), row1(pre_ffn_norm[i]), row1(post_ffn_norm[i]),
            wo[:na], wo[na:nbw], wo[nbw:], w_ffn_gate[i].astype(BF16), w_ffn_up[i].astype(BF16),
            w_ffn_down[i].astype(BF16), w_ple_gate[i].astype(BF16), w_ple_proj[i].astype(BF16))
    return x2.reshape(bsz, s, d)
```

```python
import functools
import math

import numpy as np
import jax
import jax.numpy as jnp
from jax import lax
from jax.experimental import pallas as pl
from jax.experimental.pallas import tpu as pltpu

F32, BF16, I32 = jnp.float32, jnp.bfloat16, jnp.int32

D_MODEL = 1024
DEPTH = 2
CHUNK = 64
A_HEADS = 8
A_QK_DIM = 64
A_V_DIM = 64
A_KV_RANK = 256
IDX_HEADS = 4
IDX_DIM = 64
TOPK_MAX = 256
REL_BUCKETS = 32
REL_MAX_DIST = 1024
B_WIDTH = 512
SHORT_CONV = 3
SSM_HEADS = 16
SSM_HEAD_DIM = 64
SSM_INNER = SSM_HEADS * SSM_HEAD_DIM
SSM_GROUPS = 2
SSM_STATE = 128
SSM_CONV = 4
SSM_XBC = SSM_INNER + 2 * SSM_GROUPS * SSM_STATE
MIX_WIDTH = A_HEADS * A_V_DIM + B_WIDTH + SSM_INNER
D_FF = -(-8 * D_MODEL // (3 * 256)) * 256
PLE_DIM = 256
NORM_EPS = 1e-6
IN_SPLITS = (A_HEADS * A_QK_DIM, A_KV_RANK, IDX_HEADS * IDX_DIM, IDX_DIM, IDX_HEADS,
             B_WIDTH, B_WIDTH, B_WIDTH, SSM_INNER, SSM_XBC, SSM_HEADS)
IN_OFFSETS = tuple(int(v) for v in np.concatenate([[0], np.cumsum(IN_SPLITS)]))

LANES = 128
SUBLANES = 8
VMEM_LIMIT_BYTES = 56 * 1024 * 1024

QB = 2 * LANES
QGROUPS = QB // LANES
KT = 256
SSD_L = 128
IN_SUBTILES = 2
TM_SUB = 256
ROW_SUBTILES = 2
TM_OUT = TM_SUB * ROW_SUBTILES
SMALL_W = LANES
IW_OFF = IDX_DIM
INT_MIN = np.int32(-2 ** 31)
LOG2E = math.log2(math.e)
BF16_ROWS = 2 * SUBLANES
V_ROWS = A_V_DIM + BF16_ROWS

def _const_spec(shape):
    nd = len(shape)
    return pl.BlockSpec(shape, lambda *_: (0,) * nd, pipeline_mode=pl.Buffered(1))


def _params(sem):
    return pltpu.CompilerParams(dimension_semantics=sem, vmem_limit_bytes=VMEM_LIMIT_BYTES)


def _rms(x, g):
    return x * lax.rsqrt(jnp.mean(x * x, axis=-1, keepdims=True) + NORM_EPS) * g


def _sigmoid(x):
    return 1.0 / (1.0 + jnp.exp(-x))


def _split3(v):
    hi = v.astype(BF16)
    r1 = v - hi.astype(F32)
    mid = r1.astype(BF16)
    lo = (r1 - mid.astype(F32)).astype(BF16)
    return hi, mid, lo


def _dot(a, b):
    return jnp.dot(a, b, preferred_element_type=F32)


def _t5_bucket_np(rel):
    half = REL_BUCKETS // 2
    max_exact = half // 2
    ret = np.where(rel > 0, half, 0)
    n = np.abs(rel)
    nf = np.maximum(n, 1).astype(np.float32)
    large = max_exact + (np.log(nf / np.float32(max_exact)) / np.float32(math.log(REL_MAX_DIST / max_exact))
                         * np.float32(half - max_exact)).astype(np.int32)
    large = np.minimum(large, half - 1)
    return (ret + np.where(n < max_exact, n, large)).astype(np.int32)


@functools.lru_cache(maxsize=None)
def _bucket_tiles():
    kr = np.arange(KT)[:, None]
    qc = np.arange(QB)[None, :]
    tiles, near = [], None
    for o in range(64):
        t = _t5_bucket_np(kr - qc - o * QB)
        tiles.append(t)
    for o in range(63, -1, -1):
        if not (np.all(tiles[o] == tiles[63][0, 0])):
            near = o + 1
            break
    assert near is not None and 0 < near < 63
    return np.stack(tiles[:near + 1]), near, int(tiles[63][0, 0])


def _bias_kernel(rb_ref, bk_ref, out_ref, *, far_bucket):
    bk = bk_ref[0]
    for h in range(A_HEADS):
        acc = jnp.zeros(bk.shape, F32)
        for b in range(REL_BUCKETS):
            acc = jnp.where(bk == b, (rb_ref[b, h] - rb_ref[far_bucket, h]) * LOG2E, acc)
        out_ref[0, :, h * QB:(h + 1) * QB] = acc


def _bias_tiles_call(rel_bias, bucket_tiles, far_bucket):
    n = bucket_tiles.shape[0]
    return pl.pallas_call(
        functools.partial(_bias_kernel, far_bucket=far_bucket),
        grid=(n,),
        in_specs=[pl.BlockSpec(memory_space=pltpu.SMEM),
                  pl.BlockSpec((1, KT, QB), lambda o: (o, 0, 0))],
        out_specs=pl.BlockSpec((1, KT, A_HEADS * QB), lambda o: (o, 0, 0)),
        out_shape=jax.ShapeDtypeStruct((n, KT, A_HEADS * QB), F32),
        compiler_params=_params(("arbitrary",)),
        name="rel_bias_tiles",
    )(rel_bias, bucket_tiles)


def _wprep_kernel(w_ref, wq_out, wckv_out, wiq_out, wsm_out, wb_out, wz_out, wxbc_out, wdt_out):
    o_q, o_ckv, o_iq, o_ik, _, o_bg, _, _, o_z, o_xbc, o_dt, o_end = IN_OFFSETS
    wq_out[...] = w_ref[:, o_q:o_ckv].astype(BF16)
    wckv_out[...] = w_ref[:, o_ckv:o_iq].astype(BF16)
    wiq_out[...] = w_ref[:, o_iq:o_ik].astype(BF16)
    wsm_out[...] = w_ref[:, o_ik:o_ik + SMALL_W].astype(BF16)
    tail = w_ref[:, o_bg:o_end].astype(BF16)
    wb_out[...] = tail[:, :o_z - o_bg]
    wz_out[...] = tail[:, o_z - o_bg:o_xbc - o_bg]
    wxbc_out[...] = tail[:, o_xbc - o_bg:o_dt - o_bg]
    wdt_out[...] = jnp.zeros(wdt_out.shape, BF16)
    wdt_out[:, :o_end - o_dt] = tail[:, o_dt - o_bg:]


def _wprep_call(w_all, layer):
    _, d, n = w_all.shape
    tr = 256
    widths = [IN_SPLITS[0], IN_SPLITS[1], IN_SPLITS[2], SMALL_W, 3 * B_WIDTH, SSM_INNER, SSM_XBC, SMALL_W]
    return pl.pallas_call(
        _wprep_kernel,
        grid=(d // tr,),
        in_specs=[pl.BlockSpec((None, tr, n), lambda r: (layer, r, 0))],
        out_specs=[pl.BlockSpec((tr, wd), lambda r: (r, 0)) for wd in widths],
        out_shape=[jax.ShapeDtypeStruct((d, wd), BF16) for wd in widths],
        compiler_params=_params(("parallel",)),
        name="inproj_weight_prep",
    )(w_all)


def _dot_nt(a, b):
    return lax.dot_general(a, b, (((1,), (1,)), ((), ())), preferred_element_type=F32)


def _bit_transpose32(words):
    a = list(words)
    m, j = 0x0000FFFF, 16
    while j:
        k = 0
        while k < 32:
            t = (a[k] ^ lax.shift_right_logical(a[k + j], np.int32(j))) & np.int32(m)
            a[k] = a[k] ^ t
            a[k + j] = a[k + j] ^ (t << j)
            k = (k + j + 1) & ~j
        j >>= 1
        m = (m ^ (m << j)) & 0xFFFFFFFF
    return a


def _attn_kernel(q_ref, iq_ref, iw_ref, ckv_ref, vt_ref, ik_ref, wuk_ref, bias_ref, tri_ref,
                 o_ref, keys_ref, planes_ref, qlat_ref, acc_ref, p0_ref, p1_ref, lg0_ref, lg1_ref,
                 *, k_top, near):
    i = pl.program_id(1)
    nt = ((i + 1) * QB + KT - 1) // KT
    lane = lax.broadcasted_iota(I32, (1, QB), 1)
    qchunk = (i * QB + lane) >> int(math.log2(CHUNK))
    krow = lax.broadcasted_iota(I32, (KT, QB), 0)

    q_t = q_ref[0]
    for h in range(A_HEADS):
        ql = _dot(wuk_ref[h], q_t[h * A_QK_DIM:(h + 1) * A_QK_DIM, :])
        qlat_ref[:, h * QB:(h + 1) * QB] = (ql * (A_QK_DIM ** -0.5 * LOG2E)).astype(BF16)

    iq_t = iq_ref[0]
    iw = iw_ref[0][:IDX_HEADS, :]

    def score_tile(j, carry, masked):
        r0 = pl.multiple_of(j * KT, KT)
        ikt = ik_ref[0, pl.ds(r0, KT), :]
        s = jnp.zeros((KT, QB), F32)
        for h in range(IDX_HEADS):
            raw = _dot(ikt, iq_t[h * IDX_DIM:(h + 1) * IDX_DIM, :])
            s = s + jnp.maximum(raw, 0.0) * iw[h:h + 1, :]
        if masked:
            adm = ((r0 + krow) >> int(math.log2(CHUNK))) <= qchunk
            s = jnp.where(adm, s, -jnp.inf)
        bits = lax.bitcast_convert_type(s, I32)
        bits = jnp.where(bits == INT_MIN, 0, bits)
        key = bits ^ ((bits >> 31) & np.int32(0x7FFFFFFF))
        keys_ref[pl.ds(r0, KT), :] = key
        ukey = key ^ INT_MIN
        planes = _bit_transpose32([ukey[v * SUBLANES:(v + 1) * SUBLANES, :] for v in range(32)])
        p0 = pl.multiple_of(j * SUBLANES, SUBLANES)
        for b in range(32):
            planes_ref[b, pl.ds(p0, SUBLANES), :] = planes[31 - b]
        return carry

    @pl.when(i == 0)
    def _():
        planes_ref[...] = jnp.zeros(planes_ref.shape, I32)

    def score_pair(u, carry):
        score_tile(2 * u, 0, masked=False)
        score_tile(2 * u + 1, 0, masked=False)
        return carry

    lax.fori_loop(0, (nt - 1) // 2, score_pair, 0)

    @pl.when((nt - 1) % 2 == 1)
    def _():
        score_tile(nt - 2, 0, masked=False)

    score_tile(nt - 1, 0, masked=True)

    kf = float(k_top)

    def search(n_prow):
        prow = lax.broadcasted_iota(I32, (n_prow, QB), 0)

        def search_step(k, carry):
            live, c_gt, thr_u = carry
            b = 31 - k
            x = live & planes_ref[b, 0:n_prow, :]
            pc = lax.population_count(x)
            parts = [pc[r * SUBLANES:(r + 1) * SUBLANES, :] for r in range(n_prow // SUBLANES)]
            while len(parts) > 1:
                parts = [parts[k2] + parts[k2 + 1] for k2 in range(0, len(parts) - 1, 2)] + (
                    [parts[-1]] if len(parts) % 2 else [])
            cnt = c_gt + jnp.sum(parts[0].astype(F32), axis=0, keepdims=True)
            take = cnt >= kf
            live = jnp.where(take, x, live ^ x)
            c_gt = jnp.where(take, c_gt, cnt)
            thr_u = thr_u | jnp.where(take, jnp.left_shift(jnp.int32(1), b), np.int32(0))
            return live, c_gt, thr_u

        _, c_gt, thr_u = lax.fori_loop(
            0, 32, search_step,
            (jnp.where(prow < nt * SUBLANES, np.int32(-1), np.int32(0)),
             jnp.zeros((1, QB), F32), jnp.zeros((1, QB), I32)))
        return c_gt, thr_u

    n_all = planes_ref.shape[1]
    if n_all % (4 * SUBLANES) == 0:
        c_gt, thr_u = lax.cond(
            nt * SUBLANES <= n_all // 4, lambda: search(n_all // 4),
            lambda: lax.cond(nt * SUBLANES <= n_all // 2, lambda: search(n_all // 2), lambda: search(n_all)))
    else:
        c_gt, thr_u = search(n_all)
    thr = thr_u ^ INT_MIN
    need = kf - c_gt

    p_refs, lg_refs = (p0_ref, p1_ref), (lg0_ref, lg1_ref)
    n_cols = A_HEADS * QGROUPS
    acc_ref[...] = jnp.zeros(acc_ref.shape, F32)
    p1_ref[...] = jnp.zeros(p1_ref.shape, BF16)

    def stage_scores(j, cnt_eq, slot, far=False):
        r0 = pl.multiple_of(j * KT, KT)
        kt = keys_ref[pl.ds(r0, KT), :]
        eq = kt == thr
        eqf = jnp.where(eq, 1.0, 0.0)
        rank = _dot(tri_ref[...], eqf.astype(BF16)) + cnt_eq
        sel = (kt > thr) | (eq & (rank <= need))
        if not far:
            sel = sel & (((r0 + krow) >> int(math.log2(CHUNK))) <= qchunk)
            bias = bias_ref[jnp.minimum(i - j * (KT // QB), near)]
        logits = _dot(ckv_ref[0, pl.ds(r0, KT), :], qlat_ref[...])
        tmax = [[] for _ in range(QGROUPS)]
        for h in range(A_HEADS):
            for g in range(QGROUPS):
                sl = slice(h * QB + g * LANES, h * QB + (g + 1) * LANES)
                lh = logits[:, sl] if far else logits[:, sl] + bias[:, sl]
                lh = jnp.where(sel[:, g * LANES:(g + 1) * LANES], lh, -jnp.inf)
                lg_refs[slot][:, sl] = lh
                tmax[g].append(jnp.max(lh, axis=0, keepdims=True))
        tmax = tuple(jnp.concatenate(rows, axis=0) for rows in tmax)
        return tmax, cnt_eq + jnp.sum(eqf, axis=0, keepdims=True)

    def stage_probs(ms, tmax, slot):
        new_ms, alphas = [], []
        for g in range(QGROUPS):
            m_new = jnp.maximum(ms[g], tmax[g])
            m_safe = jnp.where(m_new == -jnp.inf, 0.0, m_new)
            alphas.append(jnp.exp2(ms[g] - m_safe))
            new_ms.append(m_new)
            for h in range(A_HEADS):
                sl = slice(h * QB + g * LANES, h * QB + (g + 1) * LANES)
                p_refs[slot][:, sl] = jnp.exp2(lg_refs[slot][:, sl] - m_safe[h:h + 1, :]).astype(BF16)
        return tuple(new_ms), tuple(alphas)

    def stage_values(alphas, jp, slot):
        for h in range(A_HEADS):
            vs = slice(h * V_ROWS, (h + 1) * V_ROWS)
            pv = _dot(vt_ref[0, jp, vs, :], p_refs[slot][:, h * QB:(h + 1) * QB])
            alpha = jnp.concatenate([a[h:h + 1, :] for a in alphas], axis=1)
            acc_ref[vs, :] = acc_ref[vs, :] * alpha + pv

    def trip(t, carry, slot, far=False):
        ms, tmax, cnt_eq, alphas = carry
        stage_values(alphas, jnp.maximum(t - 1, 0), 1 - slot)
        ms, alphas = stage_probs(ms, tmax, slot)
        tmax, cnt_eq = stage_scores(t + 1, cnt_eq, 1 - slot, far)
        return ms, tmax, cnt_eq, alphas

    def finish(t, carry, slot):
        ms, tmax, _, alphas = carry
        stage_values(alphas, jnp.maximum(t - 1, 0), 1 - slot)
        _, alphas = stage_probs(ms, tmax, slot)
        stage_values(alphas, t, slot)

    def pair(u, carry, far=False):
        return trip(2 * u + 1, trip(2 * u, carry, 0, far), 1, far)

    tmax0, cnt_eq0 = stage_scores(0, jnp.zeros((1, QB), F32), 0)
    init = (tuple(jnp.full((A_HEADS, LANES), -jnp.inf, F32) for _ in range(QGROUPS)), tmax0, cnt_eq0,
            tuple(jnp.ones((A_HEADS, LANES), F32) for _ in range(QGROUPS)))
    n_trips = nt - 1
    n_far_pairs = jnp.maximum(i * (QB // KT) - near, 0) // 2
    carry = lax.fori_loop(0, n_far_pairs, functools.partial(pair, far=True), init)
    carry = lax.fori_loop(n_far_pairs, n_trips // 2, pair, carry)

    @pl.when(n_trips % 2 == 0)
    def _():
        finish(nt - 1, carry, 0)

    @pl.when(n_trips % 2 == 1)
    def _():
        finish(nt - 1, trip(nt - 2, carry, 0), 1)

    a_t = jnp.concatenate(
        [acc_ref[h * V_ROWS:h * V_ROWS + A_V_DIM, :]
         * (1.0 / acc_ref[h * V_ROWS + A_V_DIM:h * V_ROWS + A_V_DIM + 1, :]) for h in range(A_HEADS)],
        axis=0)
    o_ref[...] = a_t.T.astype(BF16)


def _attn_call(q_t, iq_t, iw_t, ckv3, vt4, ik3, wuk, bias_tiles, tri, *, k_top, near):
    b, s = ckv3.shape[0], ckv3.shape[1]
    nb = s // QB
    n_kt = s // KT
    hv = A_HEADS * A_V_DIM
    hvr = A_HEADS * V_ROWS
    kern = functools.partial(_attn_kernel, k_top=k_top, near=near)
    row = lambda w: pl.BlockSpec((QB, w), lambda bb, i: (bb * nb + i, 0))
    qblk = lambda a: pl.BlockSpec((1, a.shape[1], QB), lambda bb, i: (bb * nb + i, 0, 0))
    return pl.pallas_call(
        kern,
        grid=(b, nb),
        in_specs=[
            qblk(q_t), qblk(iq_t), qblk(iw_t),
            pl.BlockSpec((1, s, A_KV_RANK), lambda bb, i: (bb, 0, 0)),
            pl.BlockSpec((1, n_kt, hvr, KT), lambda bb, i: (bb, 0, 0, 0)),
            pl.BlockSpec((1, s, IDX_DIM), lambda bb, i: (bb, 0, 0)),
            _const_spec(wuk.shape), _const_spec(bias_tiles.shape), _const_spec(tri.shape),
        ],
        out_specs=row(hv),
        out_shape=jax.ShapeDtypeStruct((b * s, hv), BF16),
        scratch_shapes=[pltpu.VMEM((s, QB), I32), pltpu.VMEM((32, s // 32, QB), I32),
                        pltpu.VMEM((A_KV_RANK, A_HEADS * QB), BF16),
                        pltpu.VMEM((hvr, QB), F32),
                        pltpu.VMEM((KT, A_HEADS * QB), BF16), pltpu.VMEM((KT, A_HEADS * QB), BF16),
                        pltpu.VMEM((KT, A_HEADS * QB), F32), pltpu.VMEM((KT, A_HEADS * QB), F32)],
        compiler_params=_params(("parallel", "arbitrary")),
        name="dsa_attention",
    )(q_t, iq_t, iw_t, ckv3, vt4, ik3, wuk, bias_tiles, tri)


def _softplus(x):
    return jnp.maximum(x, 0.0) + jnp.log1p(jnp.exp(-jnp.abs(x)))


def _ssd_chunk(xa, zz, dt_raw, dt_t_raw, state, dtb_r, dtb_c, alog_r, alog_c, dexp, nw, tril, triu, eh,
               fill=lambda: None):
    L = SSD_L
    gw = SSM_INNER // SSM_GROUPS
    xs = xa[:, :SSM_INNER]
    dt = _softplus(dt_raw + dtb_r)
    a = dt * (-jnp.exp(alog_r))
    a_cs = sum(_dot(tril, part) for part in _split3(a))
    fill()
    dt_t = _softplus(dt_t_raw + dtb_c)
    a_t = dt_t * (-jnp.exp(alog_c))
    a_cs_t = sum(_dot(part, triu) for part in _split3(a_t))

    def expand(v):
        v3 = jnp.concatenate([v, v, v], axis=1)
        hi, mid, lo = _split3(v3)
        term = lax.broadcasted_iota(I32, v3.shape, 1) // SSM_HEADS
        return _dot(jnp.where(term == 0, hi, jnp.where(term == 1, mid, lo)), eh)

    xdt = xs * expand(dt)
    fill()
    e_cs = expand(jnp.exp(a_cs))
    dec = expand(jnp.exp(a_cs[L - 1:L, :] - a_cs))
    fill()
    chunk_dec = e_cs[L - 1:L, :]

    row = lax.broadcasted_iota(I32, (L, L), 0)
    col = lax.broadcasted_iota(I32, (L, L), 1)
    causal = row >= col
    lo_half = lax.broadcasted_iota(I32, (L, LANES), 1) < SSM_HEAD_DIM
    hpg = SSM_HEADS // SSM_GROUPS

    y_parts = []
    for g in range(SSM_GROUPS):
        bg = xa[:, SSM_INNER + g * SSM_STATE:SSM_INNER + (g + 1) * SSM_STATE].astype(BF16)
        cg = xa[:, SSM_INNER + (SSM_GROUPS + g) * SSM_STATE:
                SSM_INNER + (SSM_GROUPS + g + 1) * SSM_STATE].astype(BF16)
        cb = _dot_nt(cg, bg)
        fill()
        sl = slice(g * gw, (g + 1) * gw)
        st = state[g]
        y_off = _dot(cg, st.astype(BF16)) * e_cs[:, sl]
        for k in range(hpg // 2):
            xp = xdt[:, g * gw + k * LANES:g * gw + (k + 1) * LANES]
            x2 = jnp.concatenate([jnp.where(lo_half, xp, 0.0), jnp.where(lo_half, 0.0, xp)], axis=0)
            ms = []
            for hh in range(2):
                h = g * hpg + 2 * k + hh
                seg = a_cs[:, h:h + 1] - a_cs_t[h:h + 1, :]
                ms.append((cb * jnp.exp(jnp.where(causal, seg, -jnp.inf))).astype(BF16))
            y_parts.append(y_off[:, k * LANES:(k + 1) * LANES]
                           + _dot(jnp.concatenate(ms, axis=1), x2.astype(BF16)))
            if k % 2 == 1:
                fill()
        xw = (dec[:, sl] * xdt[:, sl]).astype(BF16)
        upd = lax.dot_general(bg, xw, (((0,), (0,)), ((), ())), preferred_element_type=F32)
        state[g] = chunk_dec[:, sl] * st + upd

    y = jnp.concatenate(y_parts, axis=1) + dexp * xs
    y = y * (zz * _sigmoid(zz))
    return jnp.concatenate(
        [_rms(y[:, g * gw:(g + 1) * gw], nw[:, g * gw:(g + 1) * gw]) for g in range(SSM_GROUPS)], axis=1)


def _inproj_mixer_kernel(x_ref, g_ref, kvn_ref, ikg_ref, ikb_ref,
                         wqt_ref, wckv_ref, wiqt_ref, wsm_ref, wiwt_ref, wb_ref, wz_ref, wxbc_ref,
                         wdt_ref, wdtt_ref, wuv_ref, vone_ref,
                         scw_ref, cw_ref, cb_ref, dtb_r_ref, dtb_c_ref, alog_r_ref, alog_c_ref,
                         dexp_ref, nw_ref, tril_ref, triu_ref, eh_ref,
                         qt_out, ckv_out, vt_out, iqt_out, iwt_out, ik_out, bo_out, co_out,
                         ubuf, xbuf, state, *, steps_per_seq):
    P = SUBLANES
    tm = IN_SUBTILES * KT

    @pl.when(pl.program_id(0) % steps_per_seq == 0)
    def _():
        ubuf[0:P, :] = jnp.zeros((P, B_WIDTH), F32)
        xbuf[0:P, :] = jnp.zeros((P, SSM_XBC), F32)
        state[...] = jnp.zeros(state.shape, F32)

    subs = [slice(k * KT, (k + 1) * KT) for k in range(IN_SUBTILES)]
    h = [_rms(x_ref[rows, :], g_ref[...]).astype(BF16) for rows in subs]
    ssd_consts = (dtb_r_ref[...], dtb_c_ref[...], alog_r_ref[...], alog_c_ref[...], dexp_ref[...],
                  nw_ref[...], tril_ref[...], triu_ref[...], eh_ref[...])

    def proj_mixer(k):
        return dict(bm=_dot(h[k], wb_ref[...]),
                    xr=_dot(h[k], wxbc_ref[...]), zz=_dot(h[k], wz_ref[...]),
                    dt=_dot(h[k], wdt_ref[...])[:, :SSM_HEADS],
                    dt_t=_dot_nt(wdtt_ref[...], h[k]))

    def short_conv(k, bm):
        base = P + k * KT
        u = bm[:, B_WIDTH:2 * B_WIDTH] * bm[:, 2 * B_WIDTH:3 * B_WIDTH]
        ubuf[base:base + KT, :] = u
        conv = scw_ref[SHORT_CONV - 1:SHORT_CONV, :] * u
        for j in range(SHORT_CONV - 1):
            off = base - (SHORT_CONV - 1) + j
            conv = conv + scw_ref[j:j + 1, :] * ubuf[off:off + KT, :]
        bo_out[subs[k], :] = (bm[:, :B_WIDTH] * conv).astype(BF16)

    def ssm_conv(k, xr):
        base = P + k * KT
        xbuf[base:base + KT, :] = xr
        xc = cw_ref[SSM_CONV - 1:SSM_CONV, :] * xr + cb_ref[...]
        for j in range(SSM_CONV - 1):
            off = base - (SSM_CONV - 1) + j
            xc = xc + cw_ref[j:j + 1, :] * xbuf[off:off + KT, :]
        return xc * _sigmoid(xc)

    def scan_chunk(k, c, xa, pm, fill):
        cs = slice(c * SSD_L, (c + 1) * SSD_L)
        y = _ssd_chunk(xa[cs, :], pm["zz"][cs, :], pm["dt"][cs, :], pm["dt_t"][:, cs], state, *ssd_consts,
                       fill=fill)
        r0 = k * KT + c * SSD_L
        co_out[r0:r0 + SSD_L, :] = y.astype(BF16)

    def attn_keys(k):
        ckv = _rms(_dot(h[k], wckv_ref[...]), kvn_ref[...]).astype(BF16)
        ckv_out[subs[k], :] = ckv
        ik = _dot(h[k], wsm_ref[...])[:, :IDX_DIM]
        mu = jnp.mean(ik, axis=-1, keepdims=True)
        var = jnp.mean(jnp.square(ik - mu), axis=-1, keepdims=True)
        ik_out[subs[k], :] = ((ik - mu) * lax.rsqrt(var + NORM_EPS) * ikg_ref[...] + ikb_ref[...]).astype(BF16)
        return ckv

    def attn_queries(k):
        qt_out[k] = _dot_nt(wqt_ref[...], h[k]).astype(BF16)
        iqt_out[k] = (_dot_nt(wiqt_ref[...], h[k]) * (IDX_DIM ** -0.5)).astype(BF16)
        iwt_out[k] = _dot_nt(wiwt_ref[...], h[k]) * (IDX_HEADS ** -0.5)

    def attn_values(k, ckv):
        vt_out[0, k] = (_dot_nt(wuv_ref[...], ckv) + vone_ref[...]).astype(BF16)

    assert IN_SUBTILES == 2 and KT // SSD_L == 2
    blk = KT
    tasks = []

    def fill():
        if tasks:
            tasks.pop(0)()

    def drain():
        while tasks:
            fill()

    def col_blocks(k, got, name, w_ref):
        for c0 in range(0, w_ref.shape[1], blk):
            tasks.append(lambda c0=c0: got.setdefault(name, []).append(_dot(h[k], w_ref[:, c0:c0 + blk])))

    pm0 = proj_mixer(0)
    got1 = {}
    col_blocks(1, got1, "bm", wb_ref)
    col_blocks(1, got1, "xr", wxbc_ref)
    col_blocks(1, got1, "zz", wz_ref)
    tasks.append(lambda: got1.update(dt=_dot(h[1], wdt_ref[...])[:, :SSM_HEADS],
                                     dt_t=_dot_nt(wdtt_ref[...], h[1])))
    short_conv(0, pm0["bm"])
    fill()
    xa0 = ssm_conv(0, pm0["xr"])
    fill()
    scan_chunk(0, 0, xa0, pm0, fill)
    scan_chunk(0, 1, xa0, pm0, fill)
    drain()
    pm1 = dict(bm=jnp.concatenate(got1["bm"], axis=1), xr=jnp.concatenate(got1["xr"], axis=1),
               zz=jnp.concatenate(got1["zz"], axis=1), dt=got1["dt"], dt_t=got1["dt_t"])

    ckvs = {}
    tasks.extend([lambda: ckvs.update({0: attn_keys(0)}), lambda: attn_queries(0),
                  lambda: ckvs.update({1: attn_keys(1)}), lambda: attn_values(0, ckvs[0]),
                  lambda: attn_queries(1), lambda: attn_values(1, ckvs[1])])
    short_conv(1, pm1["bm"])
    fill()
    xa1 = ssm_conv(1, pm1["xr"])
    fill()
    scan_chunk(1, 0, xa1, pm1, fill)
    scan_chunk(1, 1, xa1, pm1, fill)
    drain()

    ubuf[0:P, :] = ubuf[tm:tm + P, :]
    xbuf[0:P, :] = xbuf[tm:tm + P, :]


def _inproj_mixer_call(x2, norm_consts, proj_weights, wuv, vone, mixer_params, n_kt):
    t = x2.shape[0]
    assert KT == QB and n_kt % IN_SUBTILES == 0 and KT % SSD_L == 0
    tm = IN_SUBTILES * KT
    steps_per_seq = n_kt // IN_SUBTILES
    scw, cw, cb, dtb, alog, dexp, nw = mixer_params
    L = SSD_L
    tril = jnp.asarray(np.tril(np.ones((L, L), np.float32)), BF16)
    triu = jnp.asarray(np.triu(np.ones((L, L), np.float32)), BF16)
    eh = jnp.asarray(np.tile(np.repeat(np.eye(SSM_HEADS, dtype=np.float32), SSM_HEAD_DIM, axis=1), (3, 1)), BF16)
    mixer_consts = [scw, cw, cb.reshape(1, -1), dtb.reshape(1, -1), dtb.reshape(-1, 1),
                    alog.reshape(1, -1), alog.reshape(-1, 1), dexp.reshape(1, -1), nw.reshape(1, -1),
                    tril, triu, eh]
    wqt, wckv, wiqt, wiwt = proj_weights[0], proj_weights[1], proj_weights[2], proj_weights[4]
    row = lambda w: pl.BlockSpec((tm, w), lambda r: (r, 0))
    rows = lambda w, dt: (row(w), jax.ShapeDtypeStruct((t, w), dt))
    qblk = lambda w, dt: (pl.BlockSpec((IN_SUBTILES, w, QB), lambda r: (r, 0, 0)),
                          jax.ShapeDtypeStruct((t // QB, w, QB), dt))
    hv = wuv.shape[0]
    outs = [qblk(wqt.shape[0], BF16), rows(wckv.shape[1], BF16),
            (pl.BlockSpec((1, IN_SUBTILES, hv, KT), lambda r: (r // steps_per_seq, r % steps_per_seq, 0, 0)),
             jax.ShapeDtypeStruct((t // (n_kt * KT), n_kt, hv, KT), BF16)),
            qblk(wiqt.shape[0], BF16), qblk(wiwt.shape[0], F32), rows(IDX_DIM, BF16),
            rows(B_WIDTH, BF16), rows(SSM_INNER, BF16)]
    consts = [*norm_consts, *proj_weights, wuv, vone, *mixer_consts]
    return pl.pallas_call(
        functools.partial(_inproj_mixer_kernel, steps_per_seq=steps_per_seq),
        grid=(t // tm,),
        in_specs=[row(D_MODEL)] + [_const_spec(c.shape) for c in consts],
        out_specs=[spec for spec, _ in outs],
        out_shape=[shape for _, shape in outs],
        scratch_shapes=[pltpu.VMEM((tm + SUBLANES, B_WIDTH), F32),
                        pltpu.VMEM((tm + SUBLANES, SSM_XBC), F32),
                        pltpu.VMEM((SSM_GROUPS, SSM_STATE, SSM_INNER // SSM_GROUPS), F32)],
        compiler_params=_params(("arbitrary",)),
        name="inproj_conv_ssd",
    )(x2, *consts)


def _dense_kernel(x_ref, a_ref, b_ref, c_ref, p_ref, g_post_ref, g_pre_ref, g_fpost_ref,
                  woa_ref, wob_ref, woc_ref, wg_ref, wu_ref, wd_ref, wpg_ref, wpp_ref, o_ref):
    subs = [slice(r * TM_SUB, (r + 1) * TM_SUB) for r in range(ROW_SUBTILES)]
    mix = [_dot(a_ref[rows, :], woa_ref[...]) + _dot(b_ref[rows, :], wob_ref[...])
           + _dot(c_ref[rows, :], woc_ref[...]) for rows in subs]
    ple = [_dot(p_ref[rows, :].astype(BF16), wpp_ref[...]) for rows in subs]
    x = [x_ref[rows, :] + _rms(m, g_post_ref[...]) for rows, m in zip(subs, mix)]
    h = [_rms(xx, g_pre_ref[...]).astype(BF16) for xx in x]
    gate = [_dot(hh, wg_ref[...]) for hh in h]
    up = [_dot(hh, wu_ref[...]) for hh in h]
    act = [(g * _sigmoid(g) * u).astype(BF16) for g, u in zip(gate, up)]
    f = [_dot(aa, wd_ref[...]) for aa in act]
    x = [xx + _rms(ff, g_fpost_ref[...]) for xx, ff in zip(x, f)]
    pg = [_sigmoid(_dot(xx.astype(BF16), wpg_ref[...])) for xx in x]
    for rows, xx, g, e in zip(subs, x, pg, ple):
        o_ref[rows, :] = xx + g * e


def _dense_call(x2, a, b, c, p_all, layer, g_post, g_pre, g_fpost, woa, wob, woc, wg, wu, wd, wpg, wpp):
    t = x2.shape[0]
    tm = TM_OUT
    row = lambda w: pl.BlockSpec((tm, w), lambda r: (r, 0))
    p_spec = pl.BlockSpec((None, tm, PLE_DIM), lambda r: (layer, r, 0))
    consts = [g_post, g_pre, g_fpost, woa, wob, woc, wg, wu, wd, wpg, wpp]
    return pl.pallas_call(
        _dense_kernel,
        grid=(t // tm,),
        in_specs=[row(D_MODEL), row(a.shape[1]), row(b.shape[1]), row(c.shape[1]), p_spec]
                 + [_const_spec(w.shape) for w in consts],
        out_specs=row(D_MODEL),
        out_shape=jax.ShapeDtypeStruct((t, D_MODEL), F32),
        compiler_params=_params(("parallel",)),
        name="outproj_ffn_ple",
    )(x2, a, b, c, p_all, *consts)


def kernel(x, p, pre_mix_norm, post_mix_norm, pre_ffn_norm, post_ffn_norm, w_in, kv_norm, idx_k_norm_g, idx_k_norm_b, w_uk, w_uv, rel_bias, short_conv_w, ssm_conv_w, ssm_conv_b, ssm_dt_bias, ssm_a_log, ssm_d, ssm_norm, w_out, w_ffn_gate, w_ffn_up, w_ffn_down, w_ple_proj, w_ple_gate):
    bsz, s, d = x.shape
    assert d == D_MODEL and s % (IN_SUBTILES * KT) == 0 and s % QB == 0 and s % SSD_L == 0
    assert (bsz * s) % TM_OUT == 0
    t = bsz * s
    k_top = min(TOPK_MAX, s // 4)

    bucket_np, near, far_bucket = _bucket_tiles()
    bias_tiles = _bias_tiles_call(rel_bias.astype(F32), jnp.asarray(bucket_np), far_bucket)
    tri = jnp.asarray(np.tril(np.ones((KT, KT), np.float32)), BF16)
    row1 = lambda v: v.reshape(1, -1).astype(F32)
    vone_np = np.zeros((A_HEADS, V_ROWS, 1), np.float32)
    vone_np[:, A_V_DIM] = 1.0
    vone = jnp.asarray(vone_np.reshape(A_HEADS * V_ROWS, 1))

    x2 = x.reshape(t, d)
    p_all = p.reshape(DEPTH, t, PLE_DIM).astype(F32)
    w_in_all = w_in.astype(F32)
    for i in range(DEPTH):
        wparts = _wprep_call(w_in_all, i)
        wuv_t = jnp.transpose(w_uv[i], (1, 2, 0)).astype(BF16)
        wuv_t = jnp.pad(wuv_t, ((0, 0), (0, V_ROWS - A_V_DIM), (0, 0))).reshape(A_HEADS * V_ROWS, A_KV_RANK)
        wq, wckv, wiq, wsm, wb, wz, wxbc, wdt = wparts
        wiw_t = wsm[:, IW_OFF:IW_OFF + SUBLANES].T
        wdt_t = wdt[:, :SSM_HEADS].T
        q_t, ckv, vt4, iq_t, iw_t, ik, b_out, c_out = _inproj_mixer_call(
            x2, (row1(pre_mix_norm[i]), row1(kv_norm[i]), row1(idx_k_norm_g[i]), row1(idx_k_norm_b[i])),
            (wq.T, wckv, wiq.T, wsm, wiw_t, wb, wz, wxbc, wdt, wdt_t), wuv_t, vone,
            (short_conv_w[i].astype(F32), ssm_conv_w[i].astype(F32), ssm_conv_b[i].astype(F32),
             ssm_dt_bias[i].astype(F32), ssm_a_log[i].astype(F32),
             jnp.repeat(ssm_d[i].astype(F32), SSM_HEAD_DIM), ssm_norm[i].astype(F32)), s // KT)

        a_out = _attn_call(
            q_t, iq_t, iw_t, ckv.reshape(bsz, s, A_KV_RANK), vt4, ik.reshape(bsz, s, IDX_DIM),
            jnp.transpose(w_uk[i], (1, 0, 2)).astype(BF16),
            bias_tiles, tri, k_top=k_top, near=near)

        wo = w_out[i].astype(BF16)
        na, nbw = A_HEADS * A_V_DIM, A_HEADS * A_V_DIM + B_WIDTH
        x2 = _dense_call(
            x2, a_out, b_out, c_out, p_all, i,
            row1(post_mix_norm[i]), row1(pre_ffn_norm[i]), row1(post_ffn_norm[i]),
            wo[:na], wo[na:nbw], wo[nbw:], w_ffn_gate[i].astype(BF16), w_ffn_up[i].astype(BF16),
            w_ffn_down[i].astype(BF16), w_ple_gate[i].astype(BF16), w_ple_proj[i].astype(BF16))
    return x2.reshape(bsz, s, d)
```

```python
import functools
import math

import numpy as np
import jax
import jax.numpy as jnp
from jax import lax
from jax.experimental import pallas as pl
from jax.experimental.pallas import tpu as pltpu

F32, BF16, I32 = jnp.float32, jnp.bfloat16, jnp.int32

D_MODEL = 1024
DEPTH = 2
CHUNK = 64
A_HEADS = 8
A_QK_DIM = 64
A_V_DIM = 64
A_KV_RANK = 256
IDX_HEADS = 4
IDX_DIM = 64
TOPK_MAX = 256
REL_BUCKETS = 32
REL_MAX_DIST = 1024
B_WIDTH = 512
SHORT_CONV = 3
SSM_HEADS = 16
SSM_HEAD_DIM = 64
SSM_INNER = SSM_HEADS * SSM_HEAD_DIM
SSM_GROUPS = 2
SSM_STATE = 128
SSM_CONV = 4
SSM_XBC = SSM_INNER + 2 * SSM_GROUPS * SSM_STATE
MIX_WIDTH = A_HEADS * A_V_DIM + B_WIDTH + SSM_INNER
D_FF = -(-8 * D_MODEL // (3 * 256)) * 256
PLE_DIM = 256
NORM_EPS = 1e-6
IN_SPLITS = (A_HEADS * A_QK_DIM, A_KV_RANK, IDX_HEADS * IDX_DIM, IDX_DIM, IDX_HEADS,
             B_WIDTH, B_WIDTH, B_WIDTH, SSM_INNER, SSM_XBC, SSM_HEADS)
IN_OFFSETS = tuple(int(v) for v in np.concatenate([[0], np.cumsum(IN_SPLITS)]))

LANES = 128
SUBLANES = 8
VMEM_LIMIT_BYTES = 56 * 1024 * 1024

QB = 2 * LANES
QGROUPS = QB // LANES
KT = 256
SSD_L = 128
IN_SUBTILES = 2
TM_SUB = 256
ROW_SUBTILES = 2
TM_OUT = TM_SUB * ROW_SUBTILES
SMALL_W = LANES
IW_OFF = IDX_DIM
INT_MIN = np.int32(-2 ** 31)
LOG2E = math.log2(math.e)
BF16_ROWS = 2 * SUBLANES
V_ROWS = A_V_DIM + BF16_ROWS

def _const_spec(shape):
    nd = len(shape)
    return pl.BlockSpec(shape, lambda *_: (0,) * nd, pipeline_mode=pl.Buffered(1))


def _params(sem):
    return pltpu.CompilerParams(dimension_semantics=sem, vmem_limit_bytes=VMEM_LIMIT_BYTES)


def _rms(x, g):
    return x * lax.rsqrt(jnp.mean(x * x, axis=-1, keepdims=True) + NORM_EPS) * g


def _sigmoid(x):
    return 1.0 / (1.0 + jnp.exp(-x))


def _split3(v):
    hi = v.astype(BF16)
    r1 = v - hi.astype(F32)
    mid = r1.astype(BF16)
    lo = (r1 - mid.astype(F32)).astype(BF16)
    return hi, mid, lo


def _dot(a, b):
    return jnp.dot(a, b, preferred_element_type=F32)


def _t5_bucket_np(rel):
    half = REL_BUCKETS // 2
    max_exact = half // 2
    ret = np.where(rel > 0, half, 0)
    n = np.abs(rel)
    nf = np.maximum(n, 1).astype(np.float32)
    large = max_exact + (np.log(nf / np.float32(max_exact)) / np.float32(math.log(REL_MAX_DIST / max_exact))
                         * np.float32(half - max_exact)).astype(np.int32)
    large = np.minimum(large, half - 1)
    return (ret + np.where(n < max_exact, n, large)).astype(np.int32)


@functools.lru_cache(maxsize=None)
def _bucket_tiles():
    kr = np.arange(KT)[:, None]
    qc = np.arange(QB)[None, :]
    tiles, near = [], None
    for o in range(64):
        t = _t5_bucket_np(kr - qc - o * QB)
        tiles.append(t)
    for o in range(63, -1, -1):
        if not (np.all(tiles[o] == tiles[63][0, 0])):
            near = o + 1
            break
    assert near is not None and 0 < near < 63
    return np.stack(tiles[:near + 1]), near, int(tiles[63][0, 0])


def _bias_kernel(rb_ref, bk_ref, out_ref, *, far_bucket):
    bk = bk_ref[0]
    for h in range(A_HEADS):
        acc = jnp.zeros(bk.shape, F32)
        for b in range(REL_BUCKETS):
            acc = jnp.where(bk == b, (rb_ref[b, h] - rb_ref[far_bucket, h]) * LOG2E, acc)
        out_ref[0, :, h * QB:(h + 1) * QB] = acc


def _bias_tiles_call(rel_bias, bucket_tiles, far_bucket):
    n = bucket_tiles.shape[0]
    return pl.pallas_call(
        functools.partial(_bias_kernel, far_bucket=far_bucket),
        grid=(n,),
        in_specs=[pl.BlockSpec(memory_space=pltpu.SMEM),
                  pl.BlockSpec((1, KT, QB), lambda o: (o, 0, 0))],
        out_specs=pl.BlockSpec((1, KT, A_HEADS * QB), lambda o: (o, 0, 0)),
        out_shape=jax.ShapeDtypeStruct((n, KT, A_HEADS * QB), F32),
        compiler_params=_params(("arbitrary",)),
        name="rel_bias_tiles",
    )(rel_bias, bucket_tiles)


def _wprep_kernel(w_ref, wq_out, wckv_out, wiq_out, wsm_out, wb_out, wz_out, wxbc_out, wdt_out):
    o_q, o_ckv, o_iq, o_ik, _, o_bg, _, _, o_z, o_xbc, o_dt, o_end = IN_OFFSETS
    wq_out[...] = w_ref[:, o_q:o_ckv].astype(BF16)
    wckv_out[...] = w_ref[:, o_ckv:o_iq].astype(BF16)
    wiq_out[...] = w_ref[:, o_iq:o_ik].astype(BF16)
    wsm_out[...] = w_ref[:, o_ik:o_ik + SMALL_W].astype(BF16)
    tail = w_ref[:, o_bg:o_end].astype(BF16)
    wb_out[...] = tail[:, :o_z - o_bg]
    wz_out[...] = tail[:, o_z - o_bg:o_xbc - o_bg]
    wxbc_out[...] = tail[:, o_xbc - o_bg:o_dt - o_bg]
    wdt_out[...] = jnp.zeros(wdt_out.shape, BF16)
    wdt_out[:, :o_end - o_dt] = tail[:, o_dt - o_bg:]


def _wprep_call(w_all, layer):
    _, d, n = w_all.shape
    tr = 256
    widths = [IN_SPLITS[0], IN_SPLITS[1], IN_SPLITS[2], SMALL_W, 3 * B_WIDTH, SSM_INNER, SSM_XBC, SMALL_W]
    return pl.pallas_call(
        _wprep_kernel,
        grid=(d // tr,),
        in_specs=[pl.BlockSpec((None, tr, n), lambda r: (layer, r, 0))],
        out_specs=[pl.BlockSpec((tr, wd), lambda r: (r, 0)) for wd in widths],
        out_shape=[jax.ShapeDtypeStruct((d, wd), BF16) for wd in widths],
        compiler_params=_params(("parallel",)),
        name="inproj_weight_prep",
    )(w_all)


def _dot_nt(a, b):
    return lax.dot_general(a, b, (((1,), (1,)), ((), ())), preferred_element_type=F32)


def _bit_transpose32(words):
    a = list(words)
    m, j = 0x0000FFFF, 16
    while j:
        k = 0
        while k < 32:
            t = (a[k] ^ lax.shift_right_logical(a[k + j], np.int32(j))) & np.int32(m)
            a[k] = a[k] ^ t
            a[k + j] = a[k + j] ^ (t << j)
            k = (k + j + 1) & ~j
        j >>= 1
        m = (m ^ (m << j)) & 0xFFFFFFFF
    return a


def _attn_kernel(q_ref, iq_ref, iw_ref, ckv_ref, vt_ref, ik_ref, wuk_ref, bias_ref, tri_ref,
                 o_ref, keys_ref, planes_ref, qlat_ref, acc_ref, p0_ref, p1_ref, lg0_ref, lg1_ref,
                 *, k_top, near):
    i = pl.program_id(1)
    nt = ((i + 1) * QB + KT - 1) // KT
    lane = lax.broadcasted_iota(I32, (1, QB), 1)
    qchunk = (i * QB + lane) >> int(math.log2(CHUNK))
    krow = lax.broadcasted_iota(I32, (KT, QB), 0)

    q_t = q_ref[0]
    for h in range(A_HEADS):
        ql = _dot(wuk_ref[h], q_t[h * A_QK_DIM:(h + 1) * A_QK_DIM, :])
        qlat_ref[:, h * QB:(h + 1) * QB] = (ql * (A_QK_DIM ** -0.5 * LOG2E)).astype(BF16)

    iq_t = iq_ref[0]
    iw = iw_ref[0][:IDX_HEADS, :]

    def score_tile(j, carry, masked):
        r0 = pl.multiple_of(j * KT, KT)
        ikt = ik_ref[0, pl.ds(r0, KT), :]
        s = jnp.zeros((KT, QB), F32)
        for h in range(IDX_HEADS):
            raw = _dot(ikt, iq_t[h * IDX_DIM:(h + 1) * IDX_DIM, :])
            s = s + jnp.maximum(raw, 0.0) * iw[h:h + 1, :]
        if masked:
            adm = ((r0 + krow) >> int(math.log2(CHUNK))) <= qchunk
            s = jnp.where(adm, s, -jnp.inf)
        bits = lax.bitcast_convert_type(s, I32)
        bits = jnp.where(bits == INT_MIN, 0, bits)
        key = bits ^ ((bits >> 31) & np.int32(0x7FFFFFFF))
        keys_ref[pl.ds(r0, KT), :] = key
        ukey = key ^ INT_MIN
        planes = _bit_transpose32([ukey[v * SUBLANES:(v + 1) * SUBLANES, :] for v in range(32)])
        p0 = pl.multiple_of(j * SUBLANES, SUBLANES)
        for b in range(32):
            planes_ref[b, pl.ds(p0, SUBLANES), :] = planes[31 - b]
        return carry

    @pl.when(i == 0)
    def _():
        planes_ref[...] = jnp.zeros(planes_ref.shape, I32)

    def score_pair(u, carry):
        score_tile(2 * u, 0, masked=False)
        score_tile(2 * u + 1, 0, masked=False)
        return carry

    lax.fori_loop(0, (nt - 1) // 2, score_pair, 0)

    @pl.when((nt - 1) % 2 == 1)
    def _():
        score_tile(nt - 2, 0, masked=False)

    score_tile(nt - 1, 0, masked=True)

    kf = float(k_top)

    def search(n_prow):
        prow = lax.broadcasted_iota(I32, (n_prow, QB), 0)

        def search_step(k, carry):
            live, c_gt, thr_u = carry
            b = 31 - k
            x = live & planes_ref[b, 0:n_prow, :]
            pc = lax.population_count(x)
            parts = [pc[r * SUBLANES:(r + 1) * SUBLANES, :] for r in range(n_prow // SUBLANES)]
            while len(parts) > 1:
                parts = [parts[k2] + parts[k2 + 1] for k2 in range(0, len(parts) - 1, 2)] + (
                    [parts[-1]] if len(parts) % 2 else [])
            cnt = c_gt + jnp.sum(parts[0].astype(F32), axis=0, keepdims=True)
            take = cnt >= kf
            live = jnp.where(take, x, live ^ x)
            c_gt = jnp.where(take, c_gt, cnt)
            thr_u = thr_u | jnp.where(take, jnp.left_shift(jnp.int32(1), b), np.int32(0))
            return live, c_gt, thr_u

        _, c_gt, thr_u = lax.fori_loop(
            0, 32, search_step,
            (jnp.where(prow < nt * SUBLANES, np.int32(-1), np.int32(0)),
             jnp.zeros((1, QB), F32), jnp.zeros((1, QB), I32)))
        return c_gt, thr_u

    n_all = planes_ref.shape[1]
    if n_all % (4 * SUBLANES) == 0:
        c_gt, thr_u = lax.cond(
            nt * SUBLANES <= n_all // 4, lambda: search(n_all // 4),
            lambda: lax.cond(nt * SUBLANES <= n_all // 2, lambda: search(n_all // 2), lambda: search(n_all)))
    else:
        c_gt, thr_u = search(n_all)
    thr = thr_u ^ INT_MIN
    need = kf - c_gt

    p_refs, lg_refs = (p0_ref, p1_ref), (lg0_ref, lg1_ref)
    n_cols = A_HEADS * QGROUPS
    acc_ref[...] = jnp.zeros(acc_ref.shape, F32)
    p1_ref[...] = jnp.zeros(p1_ref.shape, BF16)

    def stage_scores(j, cnt_eq, slot, far=False):
        r0 = pl.multiple_of(j * KT, KT)
        kt = keys_ref[pl.ds(r0, KT), :]
        eq = kt == thr
        eqf = jnp.where(eq, 1.0, 0.0)
        rank = _dot(tri_ref[...], eqf.astype(BF16)) + cnt_eq
        sel = (kt > thr) | (eq & (rank <= need))
        if not far:
            sel = sel & (((r0 + krow) >> int(math.log2(CHUNK))) <= qchunk)
            bias = bias_ref[jnp.minimum(i - j * (KT // QB), near)]
        logits = _dot(ckv_ref[0, pl.ds(r0, KT), :], qlat_ref[...])
        tmax = [[] for _ in range(QGROUPS)]
        for h in range(A_HEADS):
            for g in range(QGROUPS):
                sl = slice(h * QB + g * LANES, h * QB + (g + 1) * LANES)
                lh = logits[:, sl] if far else logits[:, sl] + bias[:, sl]
                lh = jnp.where(sel[:, g * LANES:(g + 1) * LANES], lh, -jnp.inf)
                lg_refs[slot][:, sl] = lh
                tmax[g].append(jnp.max(lh, axis=0, keepdims=True))
        tmax = tuple(jnp.concatenate(rows, axis=0) for rows in tmax)
        return tmax, cnt_eq + jnp.sum(eqf, axis=0, keepdims=True)

    def stage_probs(ms, tmax, slot):
        new_ms, alphas = [], []
        for g in range(QGROUPS):
            m_new = jnp.maximum(ms[g], tmax[g])
            m_safe = jnp.where(m_new == -jnp.inf, 0.0, m_new)
            alphas.append(jnp.exp2(ms[g] - m_safe))
            new_ms.append(m_new)
            for h in range(A_HEADS):
                sl = slice(h * QB + g * LANES, h * QB + (g + 1) * LANES)
                p_refs[slot][:, sl] = jnp.exp2(lg_refs[slot][:, sl] - m_safe[h:h + 1, :]).astype(BF16)
        return tuple(new_ms), tuple(alphas)

    def stage_values(alphas, jp, slot):
        for h in range(A_HEADS):
            vs = slice(h * V_ROWS, (h + 1) * V_ROWS)
            pv = _dot(vt_ref[0, jp, vs, :], p_refs[slot][:, h * QB:(h + 1) * QB])
            alpha = jnp.concatenate([a[h:h + 1, :] for a in alphas], axis=1)
            acc_ref[vs, :] = acc_ref[vs, :] * alpha + pv

    def trip(t, carry, slot, far=False):
        ms, tmax, cnt_eq, alphas = carry
        stage_values(alphas, jnp.maximum(t - 1, 0), 1 - slot)
        ms, alphas = stage_probs(ms, tmax, slot)
        tmax, cnt_eq = stage_scores(t + 1, cnt_eq, 1 - slot, far)
        return ms, tmax, cnt_eq, alphas

    def finish(t, carry, slot):
        ms, tmax, _, alphas = carry
        stage_values(alphas, jnp.maximum(t - 1, 0), 1 - slot)
        _, alphas = stage_probs(ms, tmax, slot)
        stage_values(alphas, t, slot)

    def pair(u, carry, far=False):
        return trip(2 * u + 1, trip(2 * u, carry, 0, far), 1, far)

    tmax0, cnt_eq0 = stage_scores(0, jnp.zeros((1, QB), F32), 0)
    init = (tuple(jnp.full((A_HEADS, LANES), -jnp.inf, F32) for _ in range(QGROUPS)), tmax0, cnt_eq0,
            tuple(jnp.ones((A_HEADS, LANES), F32) for _ in range(QGROUPS)))
    n_trips = nt - 1
    n_far_pairs = jnp.maximum(i * (QB // KT) - near, 0) // 2
    carry = lax.fori_loop(0, n_far_pairs, functools.partial(pair, far=True), init)
    carry = lax.fori_loop(n_far_pairs, n_trips // 2, pair, carry)

    @pl.when(n_trips % 2 == 0)
    def _():
        finish(nt - 1, carry, 0)

    @pl.when(n_trips % 2 == 1)
    def _():
        finish(nt - 1, trip(nt - 2, carry, 0), 1)

    a_t = jnp.concatenate(
        [acc_ref[h * V_ROWS:h * V_ROWS + A_V_DIM, :]
         * (1.0 / acc_ref[h * V_ROWS + A_V_DIM:h * V_ROWS + A_V_DIM + 1, :]) for h in range(A_HEADS)],
        axis=0)
    o_ref[...] = a_t.T.astype(BF16)


def _attn_call(q_t, iq_t, iw_t, ckv3, vt4, ik3, wuk, bias_tiles, tri, *, k_top, near):
    b, s = ckv3.shape[0], ckv3.shape[1]
    nb = s // QB
    n_kt = s // KT
    hv = A_HEADS * A_V_DIM
    hvr = A_HEADS * V_ROWS
    kern = functools.partial(_attn_kernel, k_top=k_top, near=near)
    row = lambda w: pl.BlockSpec((QB, w), lambda bb, i: (bb * nb + i, 0))
    qblk = lambda a: pl.BlockSpec((1, a.shape[1], QB), lambda bb, i: (bb * nb + i, 0, 0))
    return pl.pallas_call(
        kern,
        grid=(b, nb),
        in_specs=[
            qblk(q_t), qblk(iq_t), qblk(iw_t),
            pl.BlockSpec((1, s, A_KV_RANK), lambda bb, i: (bb, 0, 0)),
            pl.BlockSpec((1, n_kt, hvr, KT), lambda bb, i: (bb, 0, 0, 0)),
            pl.BlockSpec((1, s, IDX_DIM), lambda bb, i: (bb, 0, 0)),
            _const_spec(wuk.shape), _const_spec(bias_tiles.shape), _const_spec(tri.shape),
        ],
        out_specs=row(hv),
        out_shape=jax.ShapeDtypeStruct((b * s, hv), BF16),
        scratch_shapes=[pltpu.VMEM((s, QB), I32), pltpu.VMEM((32, s // 32, QB), I32),
                        pltpu.VMEM((A_KV_RANK, A_HEADS * QB), BF16),
                        pltpu.VMEM((hvr, QB), F32),
                        pltpu.VMEM((KT, A_HEADS * QB), BF16), pltpu.VMEM((KT, A_HEADS * QB), BF16),
                        pltpu.VMEM((KT, A_HEADS * QB), F32), pltpu.VMEM((KT, A_HEADS * QB), F32)],
        compiler_params=_params(("parallel", "arbitrary")),
        name="dsa_attention",
    )(q_t, iq_t, iw_t, ckv3, vt4, ik3, wuk, bias_tiles, tri)


def _softplus(x):
    return jnp.maximum(x, 0.0) + jnp.log1p(jnp.exp(-jnp.abs(x)))


def _ssd_chunk(xa, zz, dt_raw, dt_t_raw, state, dtb_r, dtb_c, alog_r, alog_c, dexp, nw, tril, triu, eh,
               fill=lambda: None):
    L = SSD_L
    gw = SSM_INNER // SSM_GROUPS
    xs = xa[:, :SSM_INNER]
    dt = _softplus(dt_raw + dtb_r)
    a = dt * (-jnp.exp(alog_r))
    a_cs = sum(_dot(tril, part) for part in _split3(a))
    fill()
    dt_t = _softplus(dt_t_raw + dtb_c)
    a_t = dt_t * (-jnp.exp(alog_c))
    a_cs_t = sum(_dot(part, triu) for part in _split3(a_t))

    def expand(v):
        v3 = jnp.concatenate([v, v, v], axis=1)
        hi, mid, lo = _split3(v3)
        term = lax.broadcasted_iota(I32, v3.shape, 1) // SSM_HEADS
        return _dot(jnp.where(term == 0, hi, jnp.where(term == 1, mid, lo)), eh)

    xdt = xs * expand(dt)
    fill()
    e_cs = expand(jnp.exp(a_cs))
    dec = expand(jnp.exp(a_cs[L - 1:L, :] - a_cs))
    fill()
    chunk_dec = e_cs[L - 1:L, :]

    row = lax.broadcasted_iota(I32, (L, L), 0)
    col = lax.broadcasted_iota(I32, (L, L), 1)
    causal = row >= col
    lo_half = lax.broadcasted_iota(I32, (L, LANES), 1) < SSM_HEAD_DIM
    hpg = SSM_HEADS // SSM_GROUPS

    y_parts = []
    for g in range(SSM_GROUPS):
        bg = xa[:, SSM_INNER + g * SSM_STATE:SSM_INNER + (g + 1) * SSM_STATE].astype(BF16)
        cg = xa[:, SSM_INNER + (SSM_GROUPS + g) * SSM_STATE:
                SSM_INNER + (SSM_GROUPS + g + 1) * SSM_STATE].astype(BF16)
        cb = _dot_nt(cg, bg)
        fill()
        sl = slice(g * gw, (g + 1) * gw)
        st = state[g]
        y_off = _dot(cg, st.astype(BF16)) * e_cs[:, sl]
        for k in range(hpg // 2):
            xp = xdt[:, g * gw + k * LANES:g * gw + (k + 1) * LANES]
            x2 = jnp.concatenate([jnp.where(lo_half, xp, 0.0), jnp.where(lo_half, 0.0, xp)], axis=0)
            ms = []
            for hh in range(2):
                h = g * hpg + 2 * k + hh
                seg = a_cs[:, h:h + 1] - a_cs_t[h:h + 1, :]
                ms.append((cb * jnp.exp(jnp.where(causal, seg, -jnp.inf))).astype(BF16))
            y_parts.append(y_off[:, k * LANES:(k + 1) * LANES]
                           + _dot(jnp.concatenate(ms, axis=1), x2.astype(BF16)))
            if k % 2 == 1:
                fill()
        xw = (dec[:, sl] * xdt[:, sl]).astype(BF16)
        upd = lax.dot_general(bg, xw, (((0,), (0,)), ((), ())), preferred_element_type=F32)
        state[g] = chunk_dec[:, sl] * st + upd

    y = jnp.concatenate(y_parts, axis=1) + dexp * xs
    y = y * (zz * _sigmoid(zz))
    return jnp.concatenate(
        [_rms(y[:, g * gw:(g + 1) * gw], nw[:, g * gw:(g + 1) * gw]) for g in range(SSM_GROUPS)], axis=1)


def _inproj_mixer_kernel(x_ref, g_ref, kvn_ref, ikg_ref, ikb_ref,
                         wqt_ref, wckv_ref, wiqt_ref, wsm_ref, wiwt_ref, wb_ref, wz_ref, wxbc_ref,
                         wdt_ref, wdtt_ref, wuv_ref, vone_ref,
                         scw_ref, cw_ref, cb_ref, dtb_r_ref, dtb_c_ref, alog_r_ref, alog_c_ref,
                         dexp_ref, nw_ref, tril_ref, triu_ref, eh_ref,
                         qt_out, ckv_out, vt_out, iqt_out, iwt_out, ik_out, bo_out, co_out,
                         ubuf, xbuf, state, *, steps_per_seq):
    P = SUBLANES
    tm = IN_SUBTILES * KT

    @pl.when(pl.program_id(0) % steps_per_seq == 0)
    def _():
        ubuf[0:P, :] = jnp.zeros((P, B_WIDTH), F32)
        xbuf[0:P, :] = jnp.zeros((P, SSM_XBC), F32)
        state[...] = jnp.zeros(state.shape, F32)

    subs = [slice(k * KT, (k + 1) * KT) for k in range(IN_SUBTILES)]
    h = [_rms(x_ref[rows, :], g_ref[...]).astype(BF16) for rows in subs]
    ssd_consts = (dtb_r_ref[...], dtb_c_ref[...], alog_r_ref[...], alog_c_ref[...], dexp_ref[...],
                  nw_ref[...], tril_ref[...], triu_ref[...], eh_ref[...])

    def proj_mixer(k):
        return dict(bm=_dot(h[k], wb_ref[...]),
                    xr=_dot(h[k], wxbc_ref[...]), zz=_dot(h[k], wz_ref[...]),
                    dt=_dot(h[k], wdt_ref[...])[:, :SSM_HEADS],
                    dt_t=_dot_nt(wdtt_ref[...], h[k]))

    def short_conv(k, bm):
        base = P + k * KT
        u = bm[:, B_WIDTH:2 * B_WIDTH] * bm[:, 2 * B_WIDTH:3 * B_WIDTH]
        ubuf[base:base + KT, :] = u
        conv = scw_ref[SHORT_CONV - 1:SHORT_CONV, :] * u
        for j in range(SHORT_CONV - 1):
            off = base - (SHORT_CONV - 1) + j
            conv = conv + scw_ref[j:j + 1, :] * ubuf[off:off + KT, :]
        bo_out[subs[k], :] = (bm[:, :B_WIDTH] * conv).astype(BF16)

    def ssm_conv(k, xr):
        base = P + k * KT
        xbuf[base:base + KT, :] = xr
        xc = cw_ref[SSM_CONV - 1:SSM_CONV, :] * xr + cb_ref[...]
        for j in range(SSM_CONV - 1):
            off = base - (SSM_CONV - 1) + j
            xc = xc + cw_ref[j:j + 1, :] * xbuf[off:off + KT, :]
        return xc * _sigmoid(xc)

    def scan_chunk(k, c, xa, pm, fill):
        cs = slice(c * SSD_L, (c + 1) * SSD_L)
        y = _ssd_chunk(xa[cs, :], pm["zz"][cs, :], pm["dt"][cs, :], pm["dt_t"][:, cs], state, *ssd_consts,
                       fill=fill)
        r0 = k * KT + c * SSD_L
        co_out[r0:r0 + SSD_L, :] = y.astype(BF16)

    def attn_keys(k):
        ckv = _rms(_dot(h[k], wckv_ref[...]), kvn_ref[...]).astype(BF16)
        ckv_out[subs[k], :] = ckv
        ik = _dot(h[k], wsm_ref[...])[:, :IDX_DIM]
        mu = jnp.mean(ik, axis=-1, keepdims=True)
        var = jnp.mean(jnp.square(ik - mu), axis=-1, keepdims=True)
        ik_out[subs[k], :] = ((ik - mu) * lax.rsqrt(var + NORM_EPS) * ikg_ref[...] + ikb_ref[...]).astype(BF16)
        return ckv

    def attn_queries(k):
        qt_out[k] = _dot_nt(wqt_ref[...], h[k]).astype(BF16)
        iqt_out[k] = (_dot_nt(wiqt_ref[...], h[k]) * (IDX_DIM ** -0.5)).astype(BF16)
        iwt_out[k] = _dot_nt(wiwt_ref[...], h[k]) * (IDX_HEADS ** -0.5)

    def attn_values(k, ckv):
        vt_out[0, k] = (_dot_nt(wuv_ref[...], ckv) + vone_ref[...]).astype(BF16)

    assert IN_SUBTILES == 2 and KT // SSD_L == 2
    blk = KT
    tasks = []

    def fill():
        if tasks:
            tasks.pop(0)()

    def drain():
        while tasks:
            fill()

    def col_blocks(k, got, name, w_ref):
        for c0 in range(0, w_ref.shape[1], blk):
            tasks.append(lambda c0=c0: got.setdefault(name, []).append(_dot(h[k], w_ref[:, c0:c0 + blk])))

    pm0 = proj_mixer(0)
    got1 = {}
    col_blocks(1, got1, "bm", wb_ref)
    col_blocks(1, got1, "xr", wxbc_ref)
    col_blocks(1, got1, "zz", wz_ref)
    tasks.append(lambda: got1.update(dt=_dot(h[1], wdt_ref[...])[:, :SSM_HEADS],
                                     dt_t=_dot_nt(wdtt_ref[...], h[1])))
    short_conv(0, pm0["bm"])
    fill()
    xa0 = ssm_conv(0, pm0["xr"])
    fill()
    scan_chunk(0, 0, xa0, pm0, fill)
    scan_chunk(0, 1, xa0, pm0, fill)
    drain()
    pm1 = dict(bm=jnp.concatenate(got1["bm"], axis=1), xr=jnp.concatenate(got1["xr"], axis=1),
               zz=jnp.concatenate(got1["zz"], axis=1), dt=got1["dt"], dt_t=got1["dt_t"])

    ckvs = {}
    tasks.extend([lambda: ckvs.update({0: attn_keys(0)}), lambda: attn_queries(0),
                  lambda: ckvs.update({1: attn_keys(1)}), lambda: attn_values(0, ckvs[0]),
                  lambda: attn_queries(1), lambda: attn_values(1, ckvs[1])])
    short_conv(1, pm1["bm"])
    fill()
    xa1 = ssm_conv(1, pm1["xr"])
    fill()
    scan_chunk(1, 0, xa1, pm1, fill)
    scan_chunk(1, 1, xa1, pm1, fill)
    drain()

    ubuf[0:P, :] = ubuf[tm:tm + P, :]
    xbuf[0:P, :] = xbuf[tm:tm + P, :]


def _inproj_mixer_call(x2, norm_consts, proj_weights, wuv, vone, mixer_params, n_kt):
    t = x2.shape[0]
    assert KT == QB and n_kt % IN_SUBTILES == 0 and KT % SSD_L == 0
    tm = IN_SUBTILES * KT
    steps_per_seq = n_kt // IN_SUBTILES
    scw, cw, cb, dtb, alog, dexp, nw = mixer_params
    L = SSD_L
    tril = jnp.asarray(np.tril(np.ones((L, L), np.float32)), BF16)
    triu = jnp.asarray(np.triu(np.ones((L, L), np.float32)), BF16)
    eh = jnp.asarray(np.tile(np.repeat(np.eye(SSM_HEADS, dtype=np.float32), SSM_HEAD_DIM, axis=1), (3, 1)), BF16)
    mixer_consts = [scw, cw, cb.reshape(1, -1), dtb.reshape(1, -1), dtb.reshape(-1, 1),
                    alog.reshape(1, -1), alog.reshape(-1, 1), dexp.reshape(1, -1), nw.reshape(1, -1),
                    tril, triu, eh]
    wqt, wckv, wiqt, wiwt = proj_weights[0], proj_weights[1], proj_weights[2], proj_weights[4]
    row = lambda w: pl.BlockSpec((tm, w), lambda r: (r, 0))
    rows = lambda w, dt: (row(w), jax.ShapeDtypeStruct((t, w), dt))
    qblk = lambda w, dt: (pl.BlockSpec((IN_SUBTILES, w, QB), lambda r: (r, 0, 0)),
                          jax.ShapeDtypeStruct((t // QB, w, QB), dt))
    hv = wuv.shape[0]
    outs = [qblk(wqt.shape[0], BF16), rows(wckv.shape[1], BF16),
            (pl.BlockSpec((1, IN_SUBTILES, hv, KT), lambda r: (r // steps_per_seq, r % steps_per_seq, 0, 0)),
             jax.ShapeDtypeStruct((t // (n_kt * KT), n_kt, hv, KT), BF16)),
            qblk(wiqt.shape[0], BF16), qblk(wiwt.shape[0], F32), rows(IDX_DIM, BF16),
            rows(B_WIDTH, BF16), rows(SSM_INNER, BF16)]
    consts = [*norm_consts, *proj_weights, wuv, vone, *mixer_consts]
    return pl.pallas_call(
        functools.partial(_inproj_mixer_kernel, steps_per_seq=steps_per_seq),
        grid=(t // tm,),
        in_specs=[row(D_MODEL)] + [_const_spec(c.shape) for c in consts],
        out_specs=[spec for spec, _ in outs],
        out_shape=[shape for _, shape in outs],
        scratch_shapes=[pltpu.VMEM((tm + SUBLANES, B_WIDTH), F32),
                        pltpu.VMEM((tm + SUBLANES, SSM_XBC), F32),
                        pltpu.VMEM((SSM_GROUPS, SSM_STATE, SSM_INNER // SSM_GROUPS), F32)],
        compiler_params=_params(("arbitrary",)),
        name="inproj_conv_ssd",
    )(x2, *consts)


def _dense_kernel(x_ref, a_ref, b_ref, c_ref, p_ref, g_post_ref, g_pre_ref, g_fpost_ref,
                  woa_ref, wob_ref, woc_ref, wg_ref, wu_ref, wd_ref, wpg_ref, wpp_ref, o_ref):
    subs = [slice(r * TM_SUB, (r + 1) * TM_SUB) for r in range(ROW_SUBTILES)]
    mix = [_dot(a_ref[rows, :], woa_ref[...]) + _dot(b_ref[rows, :], wob_ref[...])
           + _dot(c_ref[rows, :], woc_ref[...]) for rows in subs]
    ple = [_dot(p_ref[rows, :].astype(BF16), wpp_ref[...]) for rows in subs]
    x = [x_ref[rows, :] + _rms(m, g_post_ref[...]) for rows, m in zip(subs, mix)]
    h = [_rms(xx, g_pre_ref[...]).astype(BF16) for xx in x]
    gate = [_dot(hh, wg_ref[...]) for hh in h]
    up = [_dot(hh, wu_ref[...]) for hh in h]
    act = [(g * _sigmoid(g) * u).astype(BF16) for g, u in zip(gate, up)]
    f = [_dot(aa, wd_ref[...]) for aa in act]
    x = [xx + _rms(ff, g_fpost_ref[...]) for xx, ff in zip(x, f)]
    pg = [_sigmoid(_dot(xx.astype(BF16), wpg_ref[...])) for xx in x]
    for rows, xx, g, e in zip(subs, x, pg, ple):
        o_ref[rows, :] = xx + g * e


def _dense_call(x2, a, b, c, p_all, layer, g_post, g_pre, g_fpost, woa, wob, woc, wg, wu, wd, wpg, wpp):
    t = x2.shape[0]
    tm = TM_OUT
    row = lambda w: pl.BlockSpec((tm, w), lambda r: (r, 0))
    steps_per_seq = p_all.shape[2] // tm
    p_spec = pl.BlockSpec((None, None, tm, PLE_DIM),
                          lambda r: (layer, r // steps_per_seq, r % steps_per_seq, 0))
    consts = [g_post, g_pre, g_fpost, woa, wob, woc, wg, wu, wd, wpg, wpp]
    return pl.pallas_call(
        _dense_kernel,
        grid=(t // tm,),
        in_specs=[row(D_MODEL), row(a.shape[1]), row(b.shape[1]), row(c.shape[1]), p_spec]
                 + [_const_spec(w.shape) for w in consts],
        out_specs=row(D_MODEL),
        out_shape=jax.ShapeDtypeStruct((t, D_MODEL), F32),
        compiler_params=_params(("parallel",)),
        name="outproj_ffn_ple",
    )(x2, a, b, c, p_all, *consts)


def kernel(x, p, pre_mix_norm, post_mix_norm, pre_ffn_norm, post_ffn_norm, w_in, kv_norm, idx_k_norm_g, idx_k_norm_b, w_uk, w_uv, rel_bias, short_conv_w, ssm_conv_w, ssm_conv_b, ssm_dt_bias, ssm_a_log, ssm_d, ssm_norm, w_out, w_ffn_gate, w_ffn_up, w_ffn_down, w_ple_proj, w_ple_gate):
    bsz, s, d = x.shape
    assert d == D_MODEL and s % (IN_SUBTILES * KT) == 0 and s % QB == 0 and s % SSD_L == 0
    assert (bsz * s) % TM_OUT == 0
    t = bsz * s
    k_top = min(TOPK_MAX, s // 4)

    bucket_np, near, far_bucket = _bucket_tiles()
    bias_tiles = _bias_tiles_call(rel_bias.astype(F32), jnp.asarray(bucket_np), far_bucket)
    tri = jnp.asarray(np.tril(np.ones((KT, KT), np.float32)), BF16)
    row1 = lambda v: v.reshape(1, -1).astype(F32)
    vone_np = np.zeros((A_HEADS, V_ROWS, 1), np.float32)
    vone_np[:, A_V_DIM] = 1.0
    vone = jnp.asarray(vone_np.reshape(A_HEADS * V_ROWS, 1))

    x2 = x.reshape(t, d)
    p_all = p.astype(F32)
    w_in_all = w_in.astype(F32)
    for i in range(DEPTH):
        wparts = _wprep_call(w_in_all, i)
        wuv_t = jnp.transpose(w_uv[i], (1, 2, 0)).astype(BF16)
        wuv_t = jnp.pad(wuv_t, ((0, 0), (0, V_ROWS - A_V_DIM), (0, 0))).reshape(A_HEADS * V_ROWS, A_KV_RANK)
        wq, wckv, wiq, wsm, wb, wz, wxbc, wdt = wparts
        wiw_t = wsm[:, IW_OFF:IW_OFF + SUBLANES].T
        wdt_t = wdt[:, :SSM_HEADS].T
        q_t, ckv, vt4, iq_t, iw_t, ik, b_out, c_out = _inproj_mixer_call(
            x2, (row1(pre_mix_norm[i]), row1(kv_norm[i]), row1(idx_k_norm_g[i]), row1(idx_k_norm_b[i])),
            (wq.T, wckv, wiq.T, wsm, wiw_t, wb, wz, wxbc, wdt, wdt_t), wuv_t, vone,
            (short_conv_w[i].astype(F32), ssm_conv_w[i].astype(F32), ssm_conv_b[i].astype(F32),
             ssm_dt_bias[i].astype(F32), ssm_a_log[i].astype(F32),
             jnp.repeat(ssm_d[i].astype(F32), SSM_HEAD_DIM), ssm_norm[i].astype(F32)), s // KT)

        a_out = _attn_call(
            q_t, iq_t, iw_t, ckv.reshape(bsz, s, A_KV_RANK), vt4, ik.reshape(bsz, s, IDX_DIM),
            jnp.transpose(w_uk[i], (1, 0, 2)).astype(BF16),
            bias_tiles, tri, k_top=k_top, near=near)

        wo = w_out[i].astype(BF16)
        na, nbw = A_HEADS * A_V_DIM, A_HEADS * A_V_DIM + B_WIDTH
        x2 = _dense_call(
            x2, a_out, b_out, c_out, p_all, i,
            row1(post_mix_norm[i]), row1(pre_ffn_norm[i]), row1(post_ffn_norm[i]),
            wo[:na], wo[na:nbw], wo[nbw:], w_ffn_gate[i].astype(BF16), w_ffn_up[i].astype(BF16),
            w_ffn_down[i].astype(BF16), w_ple_gate[i].astype(BF16), w_ple_proj[i].astype(BF16))
    return x2.reshape(bsz, s, d)
```

```python
import functools
import math

import numpy as np
import jax
import jax.numpy as jnp
from jax import lax
from jax.experimental import pallas as pl
from jax.experimental.pallas import tpu as pltpu

F32, BF16, I32 = jnp.float32, jnp.bfloat16, jnp.int32

D_MODEL = 1024
DEPTH = 2
CHUNK = 64
A_HEADS = 8
A_QK_DIM = 64
A_V_DIM = 64
A_KV_RANK = 256
IDX_HEADS = 4
IDX_DIM = 64
TOPK_MAX = 256
REL_BUCKETS = 32
REL_MAX_DIST = 1024
B_WIDTH = 512
SHORT_CONV = 3
SSM_HEADS = 16
SSM_HEAD_DIM = 64
SSM_INNER = SSM_HEADS * SSM_HEAD_DIM
SSM_GROUPS = 2
SSM_STATE = 128
SSM_CONV = 4
SSM_XBC = SSM_INNER + 2 * SSM_GROUPS * SSM_STATE
PLE_DIM = 256
NORM_EPS = 1e-6
IN_SPLITS = (A_HEADS * A_QK_DIM, A_KV_RANK, IDX_HEADS * IDX_DIM, IDX_DIM, IDX_HEADS,
             B_WIDTH, B_WIDTH, B_WIDTH, SSM_INNER, SSM_XBC, SSM_HEADS)
IN_OFFSETS = tuple(int(v) for v in np.concatenate([[0], np.cumsum(IN_SPLITS)]))

LANES = 128
SUBLANES = 8
VMEM_LIMIT_BYTES = 56 * 1024 * 1024

QB = 2 * LANES
QGROUPS = QB // LANES
KT = 256
SSD_L = 128
IN_SUBTILES = 2
TM_SUB = 256
ROW_SUBTILES = 2
TM_OUT = TM_SUB * ROW_SUBTILES
SMALL_W = LANES
IW_OFF = IDX_DIM
INT_MIN = np.int32(-2 ** 31)
LOG2E = math.log2(math.e)
BF16_ROWS = 2 * SUBLANES
V_ROWS = A_V_DIM + BF16_ROWS

def _const_spec(shape):
    nd = len(shape)
    return pl.BlockSpec(shape, lambda *_: (0,) * nd, pipeline_mode=pl.Buffered(1))


def _params(sem):
    return pltpu.CompilerParams(dimension_semantics=sem, vmem_limit_bytes=VMEM_LIMIT_BYTES)


def _rms(x, g):
    return x * lax.rsqrt(jnp.mean(x * x, axis=-1, keepdims=True) + NORM_EPS) * g


def _sigmoid(x):
    return 1.0 / (1.0 + jnp.exp(-x))


def _split3(v):
    hi = v.astype(BF16)
    r1 = v - hi.astype(F32)
    mid = r1.astype(BF16)
    lo = (r1 - mid.astype(F32)).astype(BF16)
    return hi, mid, lo


def _dot(a, b):
    return jnp.dot(a, b, preferred_element_type=F32)


def _t5_bucket_np(rel):
    half = REL_BUCKETS // 2
    max_exact = half // 2
    ret = np.where(rel > 0, half, 0)
    n = np.abs(rel)
    nf = np.maximum(n, 1).astype(np.float32)
    large = max_exact + (np.log(nf / np.float32(max_exact)) / np.float32(math.log(REL_MAX_DIST / max_exact))
                         * np.float32(half - max_exact)).astype(np.int32)
    large = np.minimum(large, half - 1)
    return (ret + np.where(n < max_exact, n, large)).astype(np.int32)


@functools.lru_cache(maxsize=None)
def _bucket_tiles():
    kr = np.arange(KT)[:, None]
    qc = np.arange(QB)[None, :]
    tiles, near = [], None
    for o in range(64):
        t = _t5_bucket_np(kr - qc - o * QB)
        tiles.append(t)
    for o in range(63, -1, -1):
        if not (np.all(tiles[o] == tiles[63][0, 0])):
            near = o + 1
            break
    assert near is not None and 0 < near < 63
    return np.stack(tiles[:near + 1]), near, int(tiles[63][0, 0])


def _bias_kernel(rb_ref, bk_ref, out_ref, *, far_bucket):
    bk = bk_ref[0]
    for h in range(A_HEADS):
        acc = jnp.zeros(bk.shape, F32)
        for b in range(REL_BUCKETS):
            acc = jnp.where(bk == b, (rb_ref[b, h] - rb_ref[far_bucket, h]) * LOG2E, acc)
        out_ref[0, :, h * QB:(h + 1) * QB] = acc


def _bias_tiles_call(rel_bias, bucket_tiles, far_bucket):
    n = bucket_tiles.shape[0]
    return pl.pallas_call(
        functools.partial(_bias_kernel, far_bucket=far_bucket),
        grid=(n,),
        in_specs=[pl.BlockSpec(memory_space=pltpu.SMEM),
                  pl.BlockSpec((1, KT, QB), lambda o: (o, 0, 0))],
        out_specs=pl.BlockSpec((1, KT, A_HEADS * QB), lambda o: (o, 0, 0)),
        out_shape=jax.ShapeDtypeStruct((n, KT, A_HEADS * QB), F32),
        compiler_params=_params(("arbitrary",)),
        name="rel_bias_tiles",
    )(rel_bias, bucket_tiles)


def _wprep_kernel(w_ref, wq_out, wckv_out, wiq_out, wsm_out, wb_out, wz_out, wxbc_out, wdt_out):
    o_q, o_ckv, o_iq, o_ik, _, o_bg, _, _, o_z, o_xbc, o_dt, o_end = IN_OFFSETS
    wq_out[...] = w_ref[:, o_q:o_ckv].astype(BF16)
    wckv_out[...] = w_ref[:, o_ckv:o_iq].astype(BF16)
    wiq_out[...] = w_ref[:, o_iq:o_ik].astype(BF16)
    wsm_out[...] = w_ref[:, o_ik:o_ik + SMALL_W].astype(BF16)
    tail = w_ref[:, o_bg:o_end].astype(BF16)
    wb_out[...] = tail[:, :o_z - o_bg]
    wz_out[...] = tail[:, o_z - o_bg:o_xbc - o_bg]
    wxbc_out[...] = tail[:, o_xbc - o_bg:o_dt - o_bg]
    wdt_out[...] = jnp.zeros(wdt_out.shape, BF16)
    wdt_out[:, :o_end - o_dt] = tail[:, o_dt - o_bg:]


def _wprep_call(w_all, layer):
    _, d, n = w_all.shape
    tr = 256
    widths = [IN_SPLITS[0], IN_SPLITS[1], IN_SPLITS[2], SMALL_W, 3 * B_WIDTH, SSM_INNER, SSM_XBC, SMALL_W]
    return pl.pallas_call(
        _wprep_kernel,
        grid=(d // tr,),
        in_specs=[pl.BlockSpec((None, tr, n), lambda r: (layer, r, 0))],
        out_specs=[pl.BlockSpec((tr, wd), lambda r: (r, 0)) for wd in widths],
        out_shape=[jax.ShapeDtypeStruct((d, wd), BF16) for wd in widths],
        compiler_params=_params(("parallel",)),
        name="inproj_weight_prep",
    )(w_all)


def _dot_nt(a, b):
    return lax.dot_general(a, b, (((1,), (1,)), ((), ())), preferred_element_type=F32)


def _bit_transpose32(words):
    a = list(words)
    m, j = 0x0000FFFF, 16
    while j:
        k = 0
        while k < 32:
            t = (a[k] ^ lax.shift_right_logical(a[k + j], np.int32(j))) & np.int32(m)
            a[k] = a[k] ^ t
            a[k + j] = a[k + j] ^ (t << j)
            k = (k + j + 1) & ~j
        j >>= 1
        m = (m ^ (m << j)) & 0xFFFFFFFF
    return a


def _attn_kernel(q_ref, iq_ref, iw_ref, ckv_ref, vt_ref, ik_ref, wuk_ref, bias_ref, tri_ref,
                 o_ref, keys_ref, planes_ref, qlat_ref, acc_ref, p0_ref, p1_ref, lg0_ref, lg1_ref,
                 *, k_top, near):
    i = pl.program_id(1)
    nt = ((i + 1) * QB + KT - 1) // KT
    lane = lax.broadcasted_iota(I32, (1, QB), 1)
    qchunk = (i * QB + lane) >> int(math.log2(CHUNK))
    krow = lax.broadcasted_iota(I32, (KT, QB), 0)

    q_t = q_ref[0]
    for h in range(A_HEADS):
        ql = _dot(wuk_ref[h], q_t[h * A_QK_DIM:(h + 1) * A_QK_DIM, :])
        qlat_ref[:, h * QB:(h + 1) * QB] = (ql * (A_QK_DIM ** -0.5 * LOG2E)).astype(BF16)

    iq_t = iq_ref[0]
    iw = iw_ref[0][:IDX_HEADS, :]

    def score_tile(j, carry, masked):
        r0 = pl.multiple_of(j * KT, KT)
        ikt = ik_ref[0, pl.ds(r0, KT), :]
        s = jnp.zeros((KT, QB), F32)
        for h in range(IDX_HEADS):
            raw = _dot(ikt, iq_t[h * IDX_DIM:(h + 1) * IDX_DIM, :])
            s = s + jnp.maximum(raw, 0.0) * iw[h:h + 1, :]
        if masked:
            adm = ((r0 + krow) >> int(math.log2(CHUNK))) <= qchunk
            s = jnp.where(adm, s, -jnp.inf)
        bits = lax.bitcast_convert_type(s, I32)
        bits = jnp.where(bits == INT_MIN, 0, bits)
        key = bits ^ ((bits >> 31) & np.int32(0x7FFFFFFF))
        keys_ref[pl.ds(r0, KT), :] = key
        ukey = key ^ INT_MIN
        planes = _bit_transpose32([ukey[v * SUBLANES:(v + 1) * SUBLANES, :] for v in range(32)])
        p0 = pl.multiple_of(j * SUBLANES, SUBLANES)
        for b in range(32):
            planes_ref[b, pl.ds(p0, SUBLANES), :] = planes[31 - b]
        return carry

    @pl.when(i == 0)
    def _():
        planes_ref[...] = jnp.zeros(planes_ref.shape, I32)

    def score_pair(u, carry):
        score_tile(2 * u, 0, masked=False)
        score_tile(2 * u + 1, 0, masked=False)
        return carry

    lax.fori_loop(0, (nt - 1) // 2, score_pair, 0)

    @pl.when((nt - 1) % 2 == 1)
    def _():
        score_tile(nt - 2, 0, masked=False)

    score_tile(nt - 1, 0, masked=True)

    kf = float(k_top)

    def search(n_prow):
        prow = lax.broadcasted_iota(I32, (n_prow, QB), 0)

        def search_step(k, carry):
            live, c_gt, thr_u = carry
            b = 31 - k
            x = live & planes_ref[b, 0:n_prow, :]
            pc = lax.population_count(x)
            parts = [pc[r * SUBLANES:(r + 1) * SUBLANES, :] for r in range(n_prow // SUBLANES)]
            while len(parts) > 1:
                parts = [parts[k2] + parts[k2 + 1] for k2 in range(0, len(parts) - 1, 2)] + (
                    [parts[-1]] if len(parts) % 2 else [])
            cnt = c_gt + jnp.sum(parts[0].astype(F32), axis=0, keepdims=True)
            take = cnt >= kf
            live = jnp.where(take, x, live ^ x)
            c_gt = jnp.where(take, c_gt, cnt)
            thr_u = thr_u | jnp.where(take, jnp.left_shift(jnp.int32(1), b), np.int32(0))
            return live, c_gt, thr_u

        _, c_gt, thr_u = lax.fori_loop(
            0, 32, search_step,
            (jnp.where(prow < nt * SUBLANES, np.int32(-1), np.int32(0)),
             jnp.zeros((1, QB), F32), jnp.zeros((1, QB), I32)))
        return c_gt, thr_u

    n_all = planes_ref.shape[1]
    if n_all % (4 * SUBLANES) == 0:
        c_gt, thr_u = lax.cond(
            nt * SUBLANES <= n_all // 4, lambda: search(n_all // 4),
            lambda: lax.cond(nt * SUBLANES <= n_all // 2, lambda: search(n_all // 2), lambda: search(n_all)))
    else:
        c_gt, thr_u = search(n_all)
    thr = thr_u ^ INT_MIN
    need = kf - c_gt

    p_refs, lg_refs = (p0_ref, p1_ref), (lg0_ref, lg1_ref)
    acc_ref[...] = jnp.zeros(acc_ref.shape, F32)
    p1_ref[...] = jnp.zeros(p1_ref.shape, BF16)

    def stage_scores(j, cnt_eq, slot, far=False):
        r0 = pl.multiple_of(j * KT, KT)
        kt = keys_ref[pl.ds(r0, KT), :]
        eq = kt == thr
        eqf = jnp.where(eq, 1.0, 0.0)
        rank = _dot(tri_ref[...], eqf.astype(BF16)) + cnt_eq
        sel = (kt > thr) | (eq & (rank <= need))
        if not far:
            sel = sel & (((r0 + krow) >> int(math.log2(CHUNK))) <= qchunk)
            bias = bias_ref[jnp.minimum(i - j * (KT // QB), near)]
        logits = _dot(ckv_ref[0, pl.ds(r0, KT), :], qlat_ref[...])
        tmax = [[] for _ in range(QGROUPS)]
        for h in range(A_HEADS):
            for g in range(QGROUPS):
                sl = slice(h * QB + g * LANES, h * QB + (g + 1) * LANES)
                lh = logits[:, sl] if far else logits[:, sl] + bias[:, sl]
                lh = jnp.where(sel[:, g * LANES:(g + 1) * LANES], lh, -jnp.inf)
                lg_refs[slot][:, sl] = lh
                tmax[g].append(jnp.max(lh, axis=0, keepdims=True))
        tmax = tuple(jnp.concatenate(rows, axis=0) for rows in tmax)
        return tmax, cnt_eq + jnp.sum(eqf, axis=0, keepdims=True)

    def stage_probs(ms, tmax, slot):
        new_ms, alphas = [], []
        for g in range(QGROUPS):
            m_new = jnp.maximum(ms[g], tmax[g])
            m_safe = jnp.where(m_new == -jnp.inf, 0.0, m_new)
            alphas.append(jnp.exp2(ms[g] - m_safe))
            new_ms.append(m_new)
            for h in range(A_HEADS):
                sl = slice(h * QB + g * LANES, h * QB + (g + 1) * LANES)
                p_refs[slot][:, sl] = jnp.exp2(lg_refs[slot][:, sl] - m_safe[h:h + 1, :]).astype(BF16)
        return tuple(new_ms), tuple(alphas)

    def stage_values(alphas, jp, slot):
        for h in range(A_HEADS):
            vs = slice(h * V_ROWS, (h + 1) * V_ROWS)
            pv = _dot(vt_ref[0, jp, vs, :], p_refs[slot][:, h * QB:(h + 1) * QB])
            alpha = jnp.concatenate([a[h:h + 1, :] for a in alphas], axis=1)
            acc_ref[vs, :] = acc_ref[vs, :] * alpha + pv

    def trip(t, carry, slot, far=False):
        ms, tmax, cnt_eq, alphas = carry
        stage_values(alphas, jnp.maximum(t - 1, 0), 1 - slot)
        ms, alphas = stage_probs(ms, tmax, slot)
        tmax, cnt_eq = stage_scores(t + 1, cnt_eq, 1 - slot, far)
        return ms, tmax, cnt_eq, alphas

    def finish(t, carry, slot):
        ms, tmax, _, alphas = carry
        stage_values(alphas, jnp.maximum(t - 1, 0), 1 - slot)
        _, alphas = stage_probs(ms, tmax, slot)
        stage_values(alphas, t, slot)

    def pair(u, carry, far=False):
        return trip(2 * u + 1, trip(2 * u, carry, 0, far), 1, far)

    tmax0, cnt_eq0 = stage_scores(0, jnp.zeros((1, QB), F32), 0)
    init = (tuple(jnp.full((A_HEADS, LANES), -jnp.inf, F32) for _ in range(QGROUPS)), tmax0, cnt_eq0,
            tuple(jnp.ones((A_HEADS, LANES), F32) for _ in range(QGROUPS)))
    n_trips = nt - 1
    n_far_pairs = jnp.maximum(i * (QB // KT) - near, 0) // 2
    carry = lax.fori_loop(0, n_far_pairs, functools.partial(pair, far=True), init)
    carry = lax.fori_loop(n_far_pairs, n_trips // 2, pair, carry)

    @pl.when(n_trips % 2 == 0)
    def _():
        finish(nt - 1, carry, 0)

    @pl.when(n_trips % 2 == 1)
    def _():
        finish(nt - 1, trip(nt - 2, carry, 0), 1)

    a_t = jnp.concatenate(
        [acc_ref[h * V_ROWS:h * V_ROWS + A_V_DIM, :]
         * (1.0 / acc_ref[h * V_ROWS + A_V_DIM:h * V_ROWS + A_V_DIM + 1, :]) for h in range(A_HEADS)],
        axis=0)
    o_ref[...] = a_t.T.astype(BF16)


def _attn_call(q_t, iq_t, iw_t, ckv3, vt4, ik3, wuk, bias_tiles, tri, *, k_top, near):
    b, s = ckv3.shape[0], ckv3.shape[1]
    nb = s // QB
    n_kt = s // KT
    hv = A_HEADS * A_V_DIM
    hvr = A_HEADS * V_ROWS
    kern = functools.partial(_attn_kernel, k_top=k_top, near=near)
    row = lambda w: pl.BlockSpec((QB, w), lambda bb, i: (bb * nb + i, 0))
    qblk = lambda a: pl.BlockSpec((1, a.shape[1], QB), lambda bb, i: (bb * nb + i, 0, 0))
    return pl.pallas_call(
        kern,
        grid=(b, nb),
        in_specs=[
            qblk(q_t), qblk(iq_t), qblk(iw_t),
            pl.BlockSpec((1, s, A_KV_RANK), lambda bb, i: (bb, 0, 0)),
            pl.BlockSpec((1, n_kt, hvr, KT), lambda bb, i: (bb, 0, 0, 0)),
            pl.BlockSpec((1, s, IDX_DIM), lambda bb, i: (bb, 0, 0)),
            _const_spec(wuk.shape), _const_spec(bias_tiles.shape), _const_spec(tri.shape),
        ],
        out_specs=row(hv),
        out_shape=jax.ShapeDtypeStruct((b * s, hv), BF16),
        scratch_shapes=[pltpu.VMEM((s, QB), I32), pltpu.VMEM((32, s // 32, QB), I32),
                        pltpu.VMEM((A_KV_RANK, A_HEADS * QB), BF16),
                        pltpu.VMEM((hvr, QB), F32),
                        pltpu.VMEM((KT, A_HEADS * QB), BF16), pltpu.VMEM((KT, A_HEADS * QB), BF16),
                        pltpu.VMEM((KT, A_HEADS * QB), F32), pltpu.VMEM((KT, A_HEADS * QB), F32)],
        compiler_params=_params(("parallel", "arbitrary")),
        name="dsa_attention",
    )(q_t, iq_t, iw_t, ckv3, vt4, ik3, wuk, bias_tiles, tri)


def _softplus(x):
    return jnp.maximum(x, 0.0) + jnp.log1p(jnp.exp(-jnp.abs(x)))


def _ssd_chunk(xa, zz, dt_raw, dt_t_raw, state, dtb_r, dtb_c, alog_r, alog_c, dexp, nw, tril, triu, eh,
               fill=lambda: None):
    L = SSD_L
    gw = SSM_INNER // SSM_GROUPS
    xs = xa[:, :SSM_INNER]
    dt = _softplus(dt_raw + dtb_r)
    a = dt * (-jnp.exp(alog_r))
    a_cs = sum(_dot(tril, part) for part in _split3(a))
    fill()
    dt_t = _softplus(dt_t_raw + dtb_c)
    a_t = dt_t * (-jnp.exp(alog_c))
    a_cs_t = sum(_dot(part, triu) for part in _split3(a_t))

    def expand(v):
        v3 = jnp.concatenate([v, v, v], axis=1)
        hi, mid, lo = _split3(v3)
        term = lax.broadcasted_iota(I32, v3.shape, 1) // SSM_HEADS
        return _dot(jnp.where(term == 0, hi, jnp.where(term == 1, mid, lo)), eh)

    xdt = xs * expand(dt)
    fill()
    e_cs = expand(jnp.exp(a_cs))
    dec = expand(jnp.exp(a_cs[L - 1:L, :] - a_cs))
    fill()
    chunk_dec = e_cs[L - 1:L, :]

    row = lax.broadcasted_iota(I32, (L, L), 0)
    col = lax.broadcasted_iota(I32, (L, L), 1)
    causal = row >= col
    lo_half = lax.broadcasted_iota(I32, (L, LANES), 1) < SSM_HEAD_DIM
    hpg = SSM_HEADS // SSM_GROUPS

    y_parts = []
    for g in range(SSM_GROUPS):
        bg = xa[:, SSM_INNER + g * SSM_STATE:SSM_INNER + (g + 1) * SSM_STATE].astype(BF16)
        cg = xa[:, SSM_INNER + (SSM_GROUPS + g) * SSM_STATE:
                SSM_INNER + (SSM_GROUPS + g + 1) * SSM_STATE].astype(BF16)
        cb = _dot_nt(cg, bg)
        fill()
        sl = slice(g * gw, (g + 1) * gw)
        st = state[g]
        y_off = _dot(cg, st.astype(BF16)) * e_cs[:, sl]
        for k in range(hpg // 2):
            xp = xdt[:, g * gw + k * LANES:g * gw + (k + 1) * LANES]
            x2 = jnp.concatenate([jnp.where(lo_half, xp, 0.0), jnp.where(lo_half, 0.0, xp)], axis=0)
            ms = []
            for hh in range(2):
                h = g * hpg + 2 * k + hh
                seg = a_cs[:, h:h + 1] - a_cs_t[h:h + 1, :]
                ms.append((cb * jnp.exp(jnp.where(causal, seg, -jnp.inf))).astype(BF16))
            y_parts.append(y_off[:, k * LANES:(k + 1) * LANES]
                           + _dot(jnp.concatenate(ms, axis=1), x2.astype(BF16)))
            if k % 2 == 1:
                fill()
        xw = (dec[:, sl] * xdt[:, sl]).astype(BF16)
        upd = lax.dot_general(bg, xw, (((0,), (0,)), ((), ())), preferred_element_type=F32)
        state[g] = chunk_dec[:, sl] * st + upd

    y = jnp.concatenate(y_parts, axis=1) + dexp * xs
    y = y * (zz * _sigmoid(zz))
    return jnp.concatenate(
        [_rms(y[:, g * gw:(g + 1) * gw], nw[:, g * gw:(g + 1) * gw]) for g in range(SSM_GROUPS)], axis=1)


def _inproj_mixer_kernel(x_ref, g_ref, kvn_ref, ikg_ref, ikb_ref,
                         wqt_ref, wckv_ref, wiqt_ref, wsm_ref, wiwt_ref, wb_ref, wz_ref, wxbc_ref,
                         wdt_ref, wdtt_ref, wuv_ref, vone_ref,
                         scw_ref, cw_ref, cb_ref, dtb_r_ref, dtb_c_ref, alog_r_ref, alog_c_ref,
                         dexp_ref, nw_ref, tril_ref, triu_ref, eh_ref,
                         qt_out, ckv_out, vt_out, iqt_out, iwt_out, ik_out, bo_out, co_out,
                         ubuf, xbuf, state, *, steps_per_seq):
    P = SUBLANES
    tm = IN_SUBTILES * KT

    @pl.when(pl.program_id(0) % steps_per_seq == 0)
    def _():
        ubuf[0:P, :] = jnp.zeros((P, B_WIDTH), F32)
        xbuf[0:P, :] = jnp.zeros((P, SSM_XBC), F32)
        state[...] = jnp.zeros(state.shape, F32)

    subs = [slice(k * KT, (k + 1) * KT) for k in range(IN_SUBTILES)]
    h = [_rms(x_ref[rows, :], g_ref[...]).astype(BF16) for rows in subs]
    ssd_consts = (dtb_r_ref[...], dtb_c_ref[...], alog_r_ref[...], alog_c_ref[...], dexp_ref[...],
                  nw_ref[...], tril_ref[...], triu_ref[...], eh_ref[...])

    def proj_mixer(k):
        return dict(bm=_dot(h[k], wb_ref[...]),
                    xr=_dot(h[k], wxbc_ref[...]), zz=_dot(h[k], wz_ref[...]),
                    dt=_dot(h[k], wdt_ref[...])[:, :SSM_HEADS],
                    dt_t=_dot_nt(wdtt_ref[...], h[k]))

    def short_conv(k, bm):
        base = P + k * KT
        u = bm[:, B_WIDTH:2 * B_WIDTH] * bm[:, 2 * B_WIDTH:3 * B_WIDTH]
        ubuf[base:base + KT, :] = u
        conv = scw_ref[SHORT_CONV - 1:SHORT_CONV, :] * u
        for j in range(SHORT_CONV - 1):
            off = base - (SHORT_CONV - 1) + j
            conv = conv + scw_ref[j:j + 1, :] * ubuf[off:off + KT, :]
        bo_out[subs[k], :] = (bm[:, :B_WIDTH] * conv).astype(BF16)

    def ssm_conv(k, xr):
        base = P + k * KT
        xbuf[base:base + KT, :] = xr
        xc = cw_ref[SSM_CONV - 1:SSM_CONV, :] * xr + cb_ref[...]
        for j in range(SSM_CONV - 1):
            off = base - (SSM_CONV - 1) + j
            xc = xc + cw_ref[j:j + 1, :] * xbuf[off:off + KT, :]
        return xc * _sigmoid(xc)

    def scan_chunk(k, c, xa, pm, fill):
        cs = slice(c * SSD_L, (c + 1) * SSD_L)
        y = _ssd_chunk(xa[cs, :], pm["zz"][cs, :], pm["dt"][cs, :], pm["dt_t"][:, cs], state, *ssd_consts,
                       fill=fill)
        r0 = k * KT + c * SSD_L
        co_out[r0:r0 + SSD_L, :] = y.astype(BF16)

    def attn_keys(k):
        ckv = _rms(_dot(h[k], wckv_ref[...]), kvn_ref[...]).astype(BF16)
        ckv_out[subs[k], :] = ckv
        ik = _dot(h[k], wsm_ref[...])[:, :IDX_DIM]
        mu = jnp.mean(ik, axis=-1, keepdims=True)
        var = jnp.mean(jnp.square(ik - mu), axis=-1, keepdims=True)
        ik_out[subs[k], :] = ((ik - mu) * lax.rsqrt(var + NORM_EPS) * ikg_ref[...] + ikb_ref[...]).astype(BF16)
        return ckv

    def attn_queries(k):
        qt_out[k] = _dot_nt(wqt_ref[...], h[k]).astype(BF16)
        iqt_out[k] = (_dot_nt(wiqt_ref[...], h[k]) * (IDX_DIM ** -0.5)).astype(BF16)
        iwt_out[k] = _dot_nt(wiwt_ref[...], h[k]) * (IDX_HEADS ** -0.5)

    def attn_values(k, ckv):
        vt_out[0, k] = (_dot_nt(wuv_ref[...], ckv) + vone_ref[...]).astype(BF16)

    assert IN_SUBTILES == 2 and KT // SSD_L == 2
    blk = KT
    tasks = []

    def fill():
        if tasks:
            tasks.pop(0)()

    def drain():
        while tasks:
            fill()

    def col_blocks(k, got, name, w_ref):
        for c0 in range(0, w_ref.shape[1], blk):
            tasks.append(lambda c0=c0: got.setdefault(name, []).append(_dot(h[k], w_ref[:, c0:c0 + blk])))

    pm0 = proj_mixer(0)
    got1 = {}
    col_blocks(1, got1, "bm", wb_ref)
    col_blocks(1, got1, "xr", wxbc_ref)
    col_blocks(1, got1, "zz", wz_ref)
    tasks.append(lambda: got1.update(dt=_dot(h[1], wdt_ref[...])[:, :SSM_HEADS],
                                     dt_t=_dot_nt(wdtt_ref[...], h[1])))
    short_conv(0, pm0["bm"])
    fill()
    xa0 = ssm_conv(0, pm0["xr"])
    fill()
    scan_chunk(0, 0, xa0, pm0, fill)
    scan_chunk(0, 1, xa0, pm0, fill)
    drain()
    pm1 = dict(bm=jnp.concatenate(got1["bm"], axis=1), xr=jnp.concatenate(got1["xr"], axis=1),
               zz=jnp.concatenate(got1["zz"], axis=1), dt=got1["dt"], dt_t=got1["dt_t"])

    ckvs = {}
    tasks.extend([lambda: ckvs.update({0: attn_keys(0)}), lambda: attn_queries(0),
                  lambda: ckvs.update({1: attn_keys(1)}), lambda: attn_values(0, ckvs[0]),
                  lambda: attn_queries(1), lambda: attn_values(1, ckvs[1])])
    short_conv(1, pm1["bm"])
    fill()
    xa1 = ssm_conv(1, pm1["xr"])
    fill()
    scan_chunk(1, 0, xa1, pm1, fill)
    scan_chunk(1, 1, xa1, pm1, fill)
    drain()

    ubuf[0:P, :] = ubuf[tm:tm + P, :]
    xbuf[0:P, :] = xbuf[tm:tm + P, :]


def _inproj_mixer_call(x2, norm_consts, proj_weights, wuv, vone, mixer_params, n_kt):
    t = x2.shape[0]
    assert KT == QB and n_kt % IN_SUBTILES == 0 and KT % SSD_L == 0
    tm = IN_SUBTILES * KT
    steps_per_seq = n_kt // IN_SUBTILES
    scw, cw, cb, dtb, alog, dexp, nw = mixer_params
    L = SSD_L
    tril = jnp.asarray(np.tril(np.ones((L, L), np.float32)), BF16)
    triu = jnp.asarray(np.triu(np.ones((L, L), np.float32)), BF16)
    eh = jnp.asarray(np.tile(np.repeat(np.eye(SSM_HEADS, dtype=np.float32), SSM_HEAD_DIM, axis=1), (3, 1)), BF16)
    mixer_consts = [scw, cw, cb.reshape(1, -1), dtb.reshape(1, -1), dtb.reshape(-1, 1),
                    alog.reshape(1, -1), alog.reshape(-1, 1), dexp.reshape(1, -1), nw.reshape(1, -1),
                    tril, triu, eh]
    wqt, wckv, wiqt, wiwt = proj_weights[0], proj_weights[1], proj_weights[2], proj_weights[4]
    row = lambda w: pl.BlockSpec((tm, w), lambda r: (r, 0))
    rows = lambda w, dt: (row(w), jax.ShapeDtypeStruct((t, w), dt))
    qblk = lambda w, dt: (pl.BlockSpec((IN_SUBTILES, w, QB), lambda r: (r, 0, 0)),
                          jax.ShapeDtypeStruct((t // QB, w, QB), dt))
    hv = wuv.shape[0]
    outs = [qblk(wqt.shape[0], BF16), rows(wckv.shape[1], BF16),
            (pl.BlockSpec((1, IN_SUBTILES, hv, KT), lambda r: (r // steps_per_seq, r % steps_per_seq, 0, 0)),
             jax.ShapeDtypeStruct((t // (n_kt * KT), n_kt, hv, KT), BF16)),
            qblk(wiqt.shape[0], BF16), qblk(wiwt.shape[0], F32), rows(IDX_DIM, BF16),
            rows(B_WIDTH, BF16), rows(SSM_INNER, BF16)]
    consts = [*norm_consts, *proj_weights, wuv, vone, *mixer_consts]
    return pl.pallas_call(
        functools.partial(_inproj_mixer_kernel, steps_per_seq=steps_per_seq),
        grid=(t // tm,),
        in_specs=[row(D_MODEL)] + [_const_spec(c.shape) for c in consts],
        out_specs=[spec for spec, _ in outs],
        out_shape=[shape for _, shape in outs],
        scratch_shapes=[pltpu.VMEM((tm + SUBLANES, B_WIDTH), F32),
                        pltpu.VMEM((tm + SUBLANES, SSM_XBC), F32),
                        pltpu.VMEM((SSM_GROUPS, SSM_STATE, SSM_INNER // SSM_GROUPS), F32)],
        compiler_params=_params(("arbitrary",)),
        name="inproj_conv_ssd",
    )(x2, *consts)


def _dense_kernel(x_ref, a_ref, b_ref, c_ref, p_ref, g_post_ref, g_pre_ref, g_fpost_ref,
                  woa_ref, wob_ref, woc_ref, wg_ref, wu_ref, wd_ref, wpg_ref, wpp_ref, o_ref):
    subs = [slice(r * TM_SUB, (r + 1) * TM_SUB) for r in range(ROW_SUBTILES)]
    mix = [_dot(a_ref[rows, :], woa_ref[...]) + _dot(b_ref[rows, :], wob_ref[...])
           + _dot(c_ref[rows, :], woc_ref[...]) for rows in subs]
    ple = [_dot(p_ref[rows, :].astype(BF16), wpp_ref[...]) for rows in subs]
    x = [x_ref[rows, :] + _rms(m, g_post_ref[...]) for rows, m in zip(subs, mix)]
    h = [_rms(xx, g_pre_ref[...]).astype(BF16) for xx in x]
    gate = [_dot(hh, wg_ref[...]) for hh in h]
    up = [_dot(hh, wu_ref[...]) for hh in h]
    act = [(g * _sigmoid(g) * u).astype(BF16) for g, u in zip(gate, up)]
    f = [_dot(aa, wd_ref[...]) for aa in act]
    x = [xx + _rms(ff, g_fpost_ref[...]) for xx, ff in zip(x, f)]
    pg = [_sigmoid(_dot(xx.astype(BF16), wpg_ref[...])) for xx in x]
    for rows, xx, g, e in zip(subs, x, pg, ple):
        o_ref[rows, :] = xx + g * e


def _dense_call(x2, a, b, c, p_all, layer, g_post, g_pre, g_fpost, woa, wob, woc, wg, wu, wd, wpg, wpp):
    t = x2.shape[0]
    tm = TM_OUT
    row = lambda w: pl.BlockSpec((tm, w), lambda r: (r, 0))
    steps_per_seq = p_all.shape[2] // tm
    p_spec = pl.BlockSpec((None, None, tm, PLE_DIM),
                          lambda r: (layer, r // steps_per_seq, r % steps_per_seq, 0))
    consts = [g_post, g_pre, g_fpost, woa, wob, woc, wg, wu, wd, wpg, wpp]
    return pl.pallas_call(
        _dense_kernel,
        grid=(t // tm,),
        in_specs=[row(D_MODEL), row(a.shape[1]), row(b.shape[1]), row(c.shape[1]), p_spec]
                 + [_const_spec(w.shape) for w in consts],
        out_specs=row(D_MODEL),
        out_shape=jax.ShapeDtypeStruct((t, D_MODEL), F32),
        compiler_params=_params(("parallel",)),
        name="outproj_ffn_ple",
    )(x2, a, b, c, p_all, *consts)


def kernel(x, p, pre_mix_norm, post_mix_norm, pre_ffn_norm, post_ffn_norm, w_in, kv_norm, idx_k_norm_g, idx_k_norm_b, w_uk, w_uv, rel_bias, short_conv_w, ssm_conv_w, ssm_conv_b, ssm_dt_bias, ssm_a_log, ssm_d, ssm_norm, w_out, w_ffn_gate, w_ffn_up, w_ffn_down, w_ple_proj, w_ple_gate):
    bsz, s, d = x.shape
    assert d == D_MODEL and s % (IN_SUBTILES * KT) == 0 and s % QB == 0 and s % SSD_L == 0
    assert (bsz * s) % TM_OUT == 0
    t = bsz * s
    k_top = min(TOPK_MAX, s // 4)

    bucket_np, near, far_bucket = _bucket_tiles()
    bias_tiles = _bias_tiles_call(rel_bias.astype(F32), jnp.asarray(bucket_np), far_bucket)
    tri = jnp.asarray(np.tril(np.ones((KT, KT), np.float32)), BF16)
    row1 = lambda v: v.reshape(1, -1).astype(F32)
    vone_np = np.zeros((A_HEADS, V_ROWS, 1), np.float32)
    vone_np[:, A_V_DIM] = 1.0
    vone = jnp.asarray(vone_np.reshape(A_HEADS * V_ROWS, 1))

    x2 = x.reshape(t, d)
    p_all = p.astype(F32)
    w_in_all = w_in.astype(F32)
    for i in range(DEPTH):
        wparts = _wprep_call(w_in_all, i)
        wuv_t = jnp.transpose(w_uv[i], (1, 2, 0)).astype(BF16)
        wuv_t = jnp.pad(wuv_t, ((0, 0), (0, V_ROWS - A_V_DIM), (0, 0))).reshape(A_HEADS * V_ROWS, A_KV_RANK)
        wq, wckv, wiq, wsm, wb, wz, wxbc, wdt = wparts
        wiw_t = wsm[:, IW_OFF:IW_OFF + SUBLANES].T
        wdt_t = wdt[:, :SSM_HEADS].T
        q_t, ckv, vt4, iq_t, iw_t, ik, b_out, c_out = _inproj_mixer_call(
            x2, (row1(pre_mix_norm[i]), row1(kv_norm[i]), row1(idx_k_norm_g[i]), row1(idx_k_norm_b[i])),
            (wq.T, wckv, wiq.T, wsm, wiw_t, wb, wz, wxbc, wdt, wdt_t), wuv_t, vone,
            (short_conv_w[i].astype(F32), ssm_conv_w[i].astype(F32), ssm_conv_b[i].astype(F32),
             ssm_dt_bias[i].astype(F32), ssm_a_log[i].astype(F32),
             jnp.repeat(ssm_d[i].astype(F32), SSM_HEAD_DIM), ssm_norm[i].astype(F32)), s // KT)

        a_out = _attn_call(
            q_t, iq_t, iw_t, ckv.reshape(bsz, s, A_KV_RANK), vt4, ik.reshape(bsz, s, IDX_DIM),
            jnp.transpose(w_uk[i], (1, 0, 2)).astype(BF16),
            bias_tiles, tri, k_top=k_top, near=near)

        wo = w_out[i].astype(BF16)
        na, nbw = A_HEADS * A_V_DIM, A_HEADS * A_V_DIM + B_WIDTH
        x2 = _dense_call(
            x2, a_out, b_out, c_out, p_all, i,
            row1(post_mix_norm[i]), row1(pre_ffn_norm[i]), row1(post_ffn_norm[i]),
            wo[:na], wo[na:nbw], wo[nbw:], w_ffn_gate[i].astype(BF16), w_ffn_up[i].astype(BF16),
            w_ffn_down[i].astype(BF16), w_ple_gate[i].astype(BF16), w_ple_proj[i].astype(BF16))
    return x2.reshape(bsz, s, d)
```

```python
import functools
import math

import numpy as np
import jax
import jax.numpy as jnp
from jax import lax
from jax.experimental import pallas as pl
from jax.experimental.pallas import tpu as pltpu

F32, BF16, I32 = jnp.float32, jnp.bfloat16, jnp.int32

D_MODEL = 1024
DEPTH = 2
CHUNK = 64
A_HEADS = 8
A_QK_DIM = 64
A_V_DIM = 64
A_KV_RANK = 256
IDX_HEADS = 4
IDX_DIM = 64
TOPK_MAX = 256
REL_BUCKETS = 32
REL_MAX_DIST = 1024
B_WIDTH = 512
SHORT_CONV = 3
SSM_HEADS = 16
SSM_HEAD_DIM = 64
SSM_INNER = SSM_HEADS * SSM_HEAD_DIM
SSM_GROUPS = 2
SSM_STATE = 128
SSM_CONV = 4
SSM_XBC = SSM_INNER + 2 * SSM_GROUPS * SSM_STATE
PLE_DIM = 256
NORM_EPS = 1e-6
IN_SPLITS = (A_HEADS * A_QK_DIM, A_KV_RANK, IDX_HEADS * IDX_DIM, IDX_DIM, IDX_HEADS,
             B_WIDTH, B_WIDTH, B_WIDTH, SSM_INNER, SSM_XBC, SSM_HEADS)
IN_OFFSETS = tuple(int(v) for v in np.concatenate([[0], np.cumsum(IN_SPLITS)]))

LANES = 128
SUBLANES = 8
VMEM_LIMIT_BYTES = 56 * 1024 * 1024

QB = 2 * LANES
QGROUPS = QB // LANES
KT = 256
SSD_L = 128
IN_SUBTILES = 2
TM_SUB = 256
ROW_SUBTILES = 2
TM_OUT = TM_SUB * ROW_SUBTILES
SMALL_W = LANES
IW_OFF = IDX_DIM
INT_MIN = np.int32(-2 ** 31)
LOG2E = math.log2(math.e)
BF16_ROWS = 2 * SUBLANES
V_ROWS = A_V_DIM + BF16_ROWS

def _const_spec(shape):
    nd = len(shape)
    return pl.BlockSpec(shape, lambda *_: (0,) * nd, pipeline_mode=pl.Buffered(1))


def _params(sem):
    return pltpu.CompilerParams(dimension_semantics=sem, vmem_limit_bytes=VMEM_LIMIT_BYTES)


def _rms(x, g):
    return x * lax.rsqrt(jnp.mean(x * x, axis=-1, keepdims=True) + NORM_EPS) * g


def _sigmoid(x):
    return 1.0 / (1.0 + jnp.exp(-x))


def _split3(v):
    hi = v.astype(BF16)
    r1 = v - hi.astype(F32)
    mid = r1.astype(BF16)
    lo = (r1 - mid.astype(F32)).astype(BF16)
    return hi, mid, lo


def _dot(a, b):
    return jnp.dot(a, b, preferred_element_type=F32)


def _t5_bucket_np(rel):
    half = REL_BUCKETS // 2
    max_exact = half // 2
    ret = np.where(rel > 0, half, 0)
    n = np.abs(rel)
    nf = np.maximum(n, 1).astype(np.float32)
    large = max_exact + (np.log(nf / np.float32(max_exact)) / np.float32(math.log(REL_MAX_DIST / max_exact))
                         * np.float32(half - max_exact)).astype(np.int32)
    large = np.minimum(large, half - 1)
    return (ret + np.where(n < max_exact, n, large)).astype(np.int32)


@functools.lru_cache(maxsize=None)
def _bucket_tiles():
    kr = np.arange(KT)[:, None]
    qc = np.arange(QB)[None, :]
    n_probe = REL_MAX_DIST // QB + 2
    tiles = [_t5_bucket_np(kr - qc - o * QB) for o in range(n_probe)]
    far_bucket = int(tiles[-1][0, 0])
    near = next(o + 1 for o in range(n_probe - 1, -1, -1) if not np.all(tiles[o] == far_bucket))
    assert 0 < near < n_probe and np.all(_t5_bucket_np(-np.arange(near * QB - KT + 1, 8 * REL_MAX_DIST)) == far_bucket)
    return np.stack(tiles[:near + 1]), near, far_bucket


def _bias_kernel(rb_ref, bk_ref, out_ref, *, far_bucket):
    bk = bk_ref[0]
    for h in range(A_HEADS):
        acc = jnp.zeros(bk.shape, F32)
        for b in range(REL_BUCKETS):
            acc = jnp.where(bk == b, (rb_ref[b, h] - rb_ref[far_bucket, h]) * LOG2E, acc)
        out_ref[0, :, h * QB:(h + 1) * QB] = acc


def _bias_tiles_call(rel_bias, bucket_tiles, far_bucket):
    n = bucket_tiles.shape[0]
    return pl.pallas_call(
        functools.partial(_bias_kernel, far_bucket=far_bucket),
        grid=(n,),
        in_specs=[pl.BlockSpec(memory_space=pltpu.SMEM),
                  pl.BlockSpec((1, KT, QB), lambda o: (o, 0, 0))],
        out_specs=pl.BlockSpec((1, KT, A_HEADS * QB), lambda o: (o, 0, 0)),
        out_shape=jax.ShapeDtypeStruct((n, KT, A_HEADS * QB), F32),
        compiler_params=_params(("arbitrary",)),
        name="rel_bias_tiles",
    )(rel_bias, bucket_tiles)


def _wprep_kernel(w_ref, wq_out, wckv_out, wiq_out, wsm_out, wb_out, wz_out, wxbc_out, wdt_out):
    o_q, o_ckv, o_iq, o_ik, _, o_bg, _, _, o_z, o_xbc, o_dt, o_end = IN_OFFSETS
    wq_out[...] = w_ref[:, o_q:o_ckv].astype(BF16)
    wckv_out[...] = w_ref[:, o_ckv:o_iq].astype(BF16)
    wiq_out[...] = w_ref[:, o_iq:o_ik].astype(BF16)
    wsm_out[...] = w_ref[:, o_ik:o_ik + SMALL_W].astype(BF16)
    tail = w_ref[:, o_bg:o_end].astype(BF16)
    wb_out[...] = tail[:, :o_z - o_bg]
    wz_out[...] = tail[:, o_z - o_bg:o_xbc - o_bg]
    wxbc_out[...] = tail[:, o_xbc - o_bg:o_dt - o_bg]
    wdt_out[...] = jnp.zeros(wdt_out.shape, BF16)
    wdt_out[:, :o_end - o_dt] = tail[:, o_dt - o_bg:]


def _wprep_call(w_all, layer):
    _, d, n = w_all.shape
    tr = KT
    widths = [IN_SPLITS[0], IN_SPLITS[1], IN_SPLITS[2], SMALL_W, 3 * B_WIDTH, SSM_INNER, SSM_XBC, SMALL_W]
    return pl.pallas_call(
        _wprep_kernel,
        grid=(d // tr,),
        in_specs=[pl.BlockSpec((None, tr, n), lambda r: (layer, r, 0))],
        out_specs=[pl.BlockSpec((tr, wd), lambda r: (r, 0)) for wd in widths],
        out_shape=[jax.ShapeDtypeStruct((d, wd), BF16) for wd in widths],
        compiler_params=_params(("parallel",)),
        name="inproj_weight_prep",
    )(w_all)


def _dot_nt(a, b):
    return lax.dot_general(a, b, (((1,), (1,)), ((), ())), preferred_element_type=F32)


def _bit_transpose32(words):
    a = list(words)
    m, j = 0x0000FFFF, 16
    while j:
        k = 0
        while k < 32:
            t = (a[k] ^ lax.shift_right_logical(a[k + j], np.int32(j))) & np.int32(m)
            a[k] = a[k] ^ t
            a[k + j] = a[k + j] ^ (t << j)
            k = (k + j + 1) & ~j
        j >>= 1
        m = (m ^ (m << j)) & 0xFFFFFFFF
    return a


def _attn_kernel(q_ref, iq_ref, iw_ref, ckv_ref, vt_ref, ik_ref, wuk_ref, bias_ref, tri_ref,
                 o_ref, keys_ref, planes_ref, qlat_ref, acc_ref, p0_ref, p1_ref, lg0_ref, lg1_ref,
                 *, k_top, near):
    i = pl.program_id(1)
    nt = ((i + 1) * QB + KT - 1) // KT
    lane = lax.broadcasted_iota(I32, (1, QB), 1)
    qchunk = (i * QB + lane) >> int(math.log2(CHUNK))
    krow = lax.broadcasted_iota(I32, (KT, QB), 0)

    q_t = q_ref[0]
    for h in range(A_HEADS):
        ql = _dot(wuk_ref[h], q_t[h * A_QK_DIM:(h + 1) * A_QK_DIM, :])
        qlat_ref[:, h * QB:(h + 1) * QB] = (ql * (A_QK_DIM ** -0.5 * LOG2E)).astype(BF16)

    iq_t = iq_ref[0]
    iw = iw_ref[0][:IDX_HEADS, :]

    def score_tile(j, carry, masked):
        r0 = pl.multiple_of(j * KT, KT)
        ikt = ik_ref[0, pl.ds(r0, KT), :]
        s = jnp.zeros((KT, QB), F32)
        for h in range(IDX_HEADS):
            raw = _dot(ikt, iq_t[h * IDX_DIM:(h + 1) * IDX_DIM, :])
            s = s + jnp.maximum(raw, 0.0) * iw[h:h + 1, :]
        if masked:
            adm = ((r0 + krow) >> int(math.log2(CHUNK))) <= qchunk
            s = jnp.where(adm, s, -jnp.inf)
        bits = lax.bitcast_convert_type(s, I32)
        bits = jnp.where(bits == INT_MIN, 0, bits)
        key = bits ^ ((bits >> 31) & np.int32(0x7FFFFFFF))
        keys_ref[pl.ds(r0, KT), :] = key
        ukey = key ^ INT_MIN
        planes = _bit_transpose32([ukey[v * SUBLANES:(v + 1) * SUBLANES, :] for v in range(32)])
        p0 = pl.multiple_of(j * SUBLANES, SUBLANES)
        for b in range(32):
            planes_ref[b, pl.ds(p0, SUBLANES), :] = planes[31 - b]
        return carry

    @pl.when(i == 0)
    def _():
        planes_ref[...] = jnp.zeros(planes_ref.shape, I32)

    def score_pair(u, carry):
        score_tile(2 * u, 0, masked=False)
        score_tile(2 * u + 1, 0, masked=False)
        return carry

    lax.fori_loop(0, (nt - 1) // 2, score_pair, 0)

    @pl.when((nt - 1) % 2 == 1)
    def _():
        score_tile(nt - 2, 0, masked=False)

    score_tile(nt - 1, 0, masked=True)

    kf = float(k_top)

    def search(n_prow):
        prow = lax.broadcasted_iota(I32, (n_prow, QB), 0)

        def search_step(k, carry):
            live, c_gt, thr_u = carry
            b = 31 - k
            x = live & planes_ref[b, 0:n_prow, :]
            pc = lax.population_count(x)
            parts = [pc[r * SUBLANES:(r + 1) * SUBLANES, :] for r in range(n_prow // SUBLANES)]
            while len(parts) > 1:
                parts = [parts[k2] + parts[k2 + 1] for k2 in range(0, len(parts) - 1, 2)] + (
                    [parts[-1]] if len(parts) % 2 else [])
            cnt = c_gt + jnp.sum(parts[0].astype(F32), axis=0, keepdims=True)
            take = cnt >= kf
            live = jnp.where(take, x, live ^ x)
            c_gt = jnp.where(take, c_gt, cnt)
            thr_u = thr_u | jnp.where(take, jnp.left_shift(jnp.int32(1), b), np.int32(0))
            return live, c_gt, thr_u

        _, c_gt, thr_u = lax.fori_loop(
            0, 32, search_step,
            (jnp.where(prow < nt * SUBLANES, np.int32(-1), np.int32(0)),
             jnp.zeros((1, QB), F32), jnp.zeros((1, QB), I32)))
        return c_gt, thr_u

    n_all = planes_ref.shape[1]
    if n_all % (4 * SUBLANES) == 0:
        c_gt, thr_u = lax.cond(
            nt * SUBLANES <= n_all // 4, lambda: search(n_all // 4),
            lambda: lax.cond(nt * SUBLANES <= n_all // 2, lambda: search(n_all // 2), lambda: search(n_all)))
    else:
        c_gt, thr_u = search(n_all)
    thr = thr_u ^ INT_MIN
    need = kf - c_gt

    p_refs, lg_refs = (p0_ref, p1_ref), (lg0_ref, lg1_ref)
    acc_ref[...] = jnp.zeros(acc_ref.shape, F32)
    p1_ref[...] = jnp.zeros(p1_ref.shape, BF16)

    def stage_scores(j, cnt_eq, slot, far=False):
        r0 = pl.multiple_of(j * KT, KT)
        kt = keys_ref[pl.ds(r0, KT), :]
        eq = kt == thr
        eqf = jnp.where(eq, 1.0, 0.0)
        rank = _dot(tri_ref[...], eqf.astype(BF16)) + cnt_eq
        sel = (kt > thr) | (eq & (rank <= need))
        if not far:
            sel = sel & (((r0 + krow) >> int(math.log2(CHUNK))) <= qchunk)
            bias = bias_ref[jnp.minimum(i - j * (KT // QB), near)]
        logits = _dot(ckv_ref[0, pl.ds(r0, KT), :], qlat_ref[...])
        tmax = [[] for _ in range(QGROUPS)]
        for h in range(A_HEADS):
            for g in range(QGROUPS):
                sl = slice(h * QB + g * LANES, h * QB + (g + 1) * LANES)
                lh = logits[:, sl] if far else logits[:, sl] + bias[:, sl]
                lh = jnp.where(sel[:, g * LANES:(g + 1) * LANES], lh, -jnp.inf)
                lg_refs[slot][:, sl] = lh
                tmax[g].append(jnp.max(lh, axis=0, keepdims=True))
        tmax = tuple(jnp.concatenate(rows, axis=0) for rows in tmax)
        return tmax, cnt_eq + jnp.sum(eqf, axis=0, keepdims=True)

    def stage_probs(ms, tmax, slot):
        new_ms, alphas = [], []
        for g in range(QGROUPS):
            m_new = jnp.maximum(ms[g], tmax[g])
            m_safe = jnp.where(m_new == -jnp.inf, 0.0, m_new)
            alphas.append(jnp.exp2(ms[g] - m_safe))
            new_ms.append(m_new)
            for h in range(A_HEADS):
                sl = slice(h * QB + g * LANES, h * QB + (g + 1) * LANES)
                p_refs[slot][:, sl] = jnp.exp2(lg_refs[slot][:, sl] - m_safe[h:h + 1, :]).astype(BF16)
        return tuple(new_ms), tuple(alphas)

    def stage_values(alphas, jp, slot):
        for h in range(A_HEADS):
            vs = slice(h * V_ROWS, (h + 1) * V_ROWS)
            pv = _dot(vt_ref[0, jp, vs, :], p_refs[slot][:, h * QB:(h + 1) * QB])
            alpha = jnp.concatenate([a[h:h + 1, :] for a in alphas], axis=1)
            acc_ref[vs, :] = acc_ref[vs, :] * alpha + pv

    def trip(t, carry, slot, far=False):
        ms, tmax, cnt_eq, alphas = carry
        stage_values(alphas, jnp.maximum(t - 1, 0), 1 - slot)
        ms, alphas = stage_probs(ms, tmax, slot)
        tmax, cnt_eq = stage_scores(t + 1, cnt_eq, 1 - slot, far)
        return ms, tmax, cnt_eq, alphas

    def finish(t, carry, slot):
        ms, tmax, _, alphas = carry
        stage_values(alphas, jnp.maximum(t - 1, 0), 1 - slot)
        _, alphas = stage_probs(ms, tmax, slot)
        stage_values(alphas, t, slot)

    def pair(u, carry, far=False):
        return trip(2 * u + 1, trip(2 * u, carry, 0, far), 1, far)

    tmax0, cnt_eq0 = stage_scores(0, jnp.zeros((1, QB), F32), 0)
    init = (tuple(jnp.full((A_HEADS, LANES), -jnp.inf, F32) for _ in range(QGROUPS)), tmax0, cnt_eq0,
            tuple(jnp.ones((A_HEADS, LANES), F32) for _ in range(QGROUPS)))
    n_trips = nt - 1
    n_far_pairs = jnp.maximum(i * (QB // KT) - near, 0) // 2
    carry = lax.fori_loop(0, n_far_pairs, functools.partial(pair, far=True), init)
    carry = lax.fori_loop(n_far_pairs, n_trips // 2, pair, carry)

    @pl.when(n_trips % 2 == 0)
    def _():
        finish(nt - 1, carry, 0)

    @pl.when(n_trips % 2 == 1)
    def _():
        finish(nt - 1, trip(nt - 2, carry, 0), 1)

    a_t = jnp.concatenate(
        [acc_ref[h * V_ROWS:h * V_ROWS + A_V_DIM, :]
         * (1.0 / acc_ref[h * V_ROWS + A_V_DIM:h * V_ROWS + A_V_DIM + 1, :]) for h in range(A_HEADS)],
        axis=0)
    o_ref[...] = a_t.T.astype(BF16)


def _attn_call(q_t, iq_t, iw_t, ckv3, vt4, ik3, wuk, bias_tiles, tri, *, k_top, near):
    b, s = ckv3.shape[0], ckv3.shape[1]
    nb = s // QB
    n_kt = s // KT
    hv = A_HEADS * A_V_DIM
    hvr = A_HEADS * V_ROWS
    kern = functools.partial(_attn_kernel, k_top=k_top, near=near)
    row = lambda w: pl.BlockSpec((QB, w), lambda bb, i: (bb * nb + i, 0))
    qblk = lambda a: pl.BlockSpec((1, a.shape[1], QB), lambda bb, i: (bb * nb + i, 0, 0))
    return pl.pallas_call(
        kern,
        grid=(b, nb),
        in_specs=[
            qblk(q_t), qblk(iq_t), qblk(iw_t),
            pl.BlockSpec((1, s, A_KV_RANK), lambda bb, i: (bb, 0, 0)),
            pl.BlockSpec((1, n_kt, hvr, KT), lambda bb, i: (bb, 0, 0, 0)),
            pl.BlockSpec((1, s, IDX_DIM), lambda bb, i: (bb, 0, 0)),
            _const_spec(wuk.shape), _const_spec(bias_tiles.shape), _const_spec(tri.shape),
        ],
        out_specs=row(hv),
        out_shape=jax.ShapeDtypeStruct((b * s, hv), BF16),
        scratch_shapes=[pltpu.VMEM((s, QB), I32), pltpu.VMEM((32, s // 32, QB), I32),
                        pltpu.VMEM((A_KV_RANK, A_HEADS * QB), BF16),
                        pltpu.VMEM((hvr, QB), F32),
                        pltpu.VMEM((KT, A_HEADS * QB), BF16), pltpu.VMEM((KT, A_HEADS * QB), BF16),
                        pltpu.VMEM((KT, A_HEADS * QB), F32), pltpu.VMEM((KT, A_HEADS * QB), F32)],
        compiler_params=_params(("parallel", "arbitrary")),
        name="dsa_attention",
    )(q_t, iq_t, iw_t, ckv3, vt4, ik3, wuk, bias_tiles, tri)


def _softplus(x):
    return jnp.maximum(x, 0.0) + jnp.log1p(jnp.exp(-jnp.abs(x)))


def _ssd_chunk(xa, zz, dt_raw, dt_t_raw, state, dtb_r, dtb_c, alog_r, alog_c, dexp, nw, tril, triu, eh,
               fill=lambda: None):
    L = SSD_L
    gw = SSM_INNER // SSM_GROUPS
    xs = xa[:, :SSM_INNER]
    dt = _softplus(dt_raw + dtb_r)
    a = dt * (-jnp.exp(alog_r))
    a_cs = sum(_dot(tril, part) for part in _split3(a))
    fill()
    dt_t = _softplus(dt_t_raw + dtb_c)
    a_t = dt_t * (-jnp.exp(alog_c))
    a_cs_t = sum(_dot(part, triu) for part in _split3(a_t))

    def expand(v):
        v3 = jnp.concatenate([v, v, v], axis=1)
        hi, mid, lo = _split3(v3)
        term = lax.broadcasted_iota(I32, v3.shape, 1) // SSM_HEADS
        return _dot(jnp.where(term == 0, hi, jnp.where(term == 1, mid, lo)), eh)

    xdt = xs * expand(dt)
    fill()
    e_cs = expand(jnp.exp(a_cs))
    dec = expand(jnp.exp(a_cs[L - 1:L, :] - a_cs))
    fill()
    chunk_dec = e_cs[L - 1:L, :]

    row = lax.broadcasted_iota(I32, (L, L), 0)
    col = lax.broadcasted_iota(I32, (L, L), 1)
    causal = row >= col
    lo_half = lax.broadcasted_iota(I32, (L, LANES), 1) < SSM_HEAD_DIM
    hpg = SSM_HEADS // SSM_GROUPS

    y_parts = []
    for g in range(SSM_GROUPS):
        bg = xa[:, SSM_INNER + g * SSM_STATE:SSM_INNER + (g + 1) * SSM_STATE].astype(BF16)
        cg = xa[:, SSM_INNER + (SSM_GROUPS + g) * SSM_STATE:
                SSM_INNER + (SSM_GROUPS + g + 1) * SSM_STATE].astype(BF16)
        cb = _dot_nt(cg, bg)
        fill()
        sl = slice(g * gw, (g + 1) * gw)
        st = state[g]
        y_off = _dot(cg, st.astype(BF16)) * e_cs[:, sl]
        for k in range(hpg // 2):
            xp = xdt[:, g * gw + k * LANES:g * gw + (k + 1) * LANES]
            x2 = jnp.concatenate([jnp.where(lo_half, xp, 0.0), jnp.where(lo_half, 0.0, xp)], axis=0)
            ms = []
            for hh in range(2):
                h = g * hpg + 2 * k + hh
                seg = a_cs[:, h:h + 1] - a_cs_t[h:h + 1, :]
                ms.append((cb * jnp.exp(jnp.where(causal, seg, -jnp.inf))).astype(BF16))
            y_parts.append(y_off[:, k * LANES:(k + 1) * LANES]
                           + _dot(jnp.concatenate(ms, axis=1), x2.astype(BF16)))
            if k % 2 == 1:
                fill()
        xw = (dec[:, sl] * xdt[:, sl]).astype(BF16)
        upd = lax.dot_general(bg, xw, (((0,), (0,)), ((), ())), preferred_element_type=F32)
        state[g] = chunk_dec[:, sl] * st + upd

    y = jnp.concatenate(y_parts, axis=1) + dexp * xs
    y = y * (zz * _sigmoid(zz))
    return jnp.concatenate(
        [_rms(y[:, g * gw:(g + 1) * gw], nw[:, g * gw:(g + 1) * gw]) for g in range(SSM_GROUPS)], axis=1)


def _inproj_mixer_kernel(x_ref, g_ref, kvn_ref, ikg_ref, ikb_ref,
                         wqt_ref, wckv_ref, wiqt_ref, wsm_ref, wiwt_ref, wb_ref, wz_ref, wxbc_ref,
                         wdt_ref, wdtt_ref, wuv_ref, vone_ref,
                         scw_ref, cw_ref, cb_ref, dtb_r_ref, dtb_c_ref, alog_r_ref, alog_c_ref,
                         dexp_ref, nw_ref, tril_ref, triu_ref, eh_ref,
                         qt_out, ckv_out, vt_out, iqt_out, iwt_out, ik_out, bo_out, co_out,
                         ubuf, xbuf, state, *, steps_per_seq):
    P = SUBLANES
    tm = IN_SUBTILES * KT

    @pl.when(pl.program_id(0) % steps_per_seq == 0)
    def _():
        ubuf[0:P, :] = jnp.zeros((P, B_WIDTH), F32)
        xbuf[0:P, :] = jnp.zeros((P, SSM_XBC), F32)
        state[...] = jnp.zeros(state.shape, F32)

    subs = [slice(k * KT, (k + 1) * KT) for k in range(IN_SUBTILES)]
    h = [_rms(x_ref[rows, :], g_ref[...]).astype(BF16) for rows in subs]
    ssd_consts = (dtb_r_ref[...], dtb_c_ref[...], alog_r_ref[...], alog_c_ref[...], dexp_ref[...],
                  nw_ref[...], tril_ref[...], triu_ref[...], eh_ref[...])

    def proj_mixer(k):
        return dict(bm=_dot(h[k], wb_ref[...]),
                    xr=_dot(h[k], wxbc_ref[...]), zz=_dot(h[k], wz_ref[...]),
                    dt=_dot(h[k], wdt_ref[...])[:, :SSM_HEADS],
                    dt_t=_dot_nt(wdtt_ref[...], h[k]))

    def short_conv(k, bm):
        base = P + k * KT
        u = bm[:, B_WIDTH:2 * B_WIDTH] * bm[:, 2 * B_WIDTH:3 * B_WIDTH]
        ubuf[base:base + KT, :] = u
        conv = scw_ref[SHORT_CONV - 1:SHORT_CONV, :] * u
        for j in range(SHORT_CONV - 1):
            off = base - (SHORT_CONV - 1) + j
            conv = conv + scw_ref[j:j + 1, :] * ubuf[off:off + KT, :]
        bo_out[subs[k], :] = (bm[:, :B_WIDTH] * conv).astype(BF16)

    def ssm_conv(k, xr):
        base = P + k * KT
        xbuf[base:base + KT, :] = xr
        xc = cw_ref[SSM_CONV - 1:SSM_CONV, :] * xr + cb_ref[...]
        for j in range(SSM_CONV - 1):
            off = base - (SSM_CONV - 1) + j
            xc = xc + cw_ref[j:j + 1, :] * xbuf[off:off + KT, :]
        return xc * _sigmoid(xc)

    def scan_chunk(k, c, xa, pm, fill):
        cs = slice(c * SSD_L, (c + 1) * SSD_L)
        y = _ssd_chunk(xa[cs, :], pm["zz"][cs, :], pm["dt"][cs, :], pm["dt_t"][:, cs], state, *ssd_consts,
                       fill=fill)
        r0 = k * KT + c * SSD_L
        co_out[r0:r0 + SSD_L, :] = y.astype(BF16)

    def attn_keys(k):
        ckv = _rms(_dot(h[k], wckv_ref[...]), kvn_ref[...]).astype(BF16)
        ckv_out[subs[k], :] = ckv
        ik = _dot(h[k], wsm_ref[...])[:, :IDX_DIM]
        mu = jnp.mean(ik, axis=-1, keepdims=True)
        var = jnp.mean(jnp.square(ik - mu), axis=-1, keepdims=True)
        ik_out[subs[k], :] = ((ik - mu) * lax.rsqrt(var + NORM_EPS) * ikg_ref[...] + ikb_ref[...]).astype(BF16)
        return ckv

    def attn_queries(k):
        qt_out[k] = _dot_nt(wqt_ref[...], h[k]).astype(BF16)
        iqt_out[k] = (_dot_nt(wiqt_ref[...], h[k]) * (IDX_DIM ** -0.5)).astype(BF16)
        iwt_out[k] = _dot_nt(wiwt_ref[...], h[k]) * (IDX_HEADS ** -0.5)

    def attn_values(k, ckv):
        vt_out[0, k] = (_dot_nt(wuv_ref[...], ckv) + vone_ref[...]).astype(BF16)

    assert IN_SUBTILES == 2 and KT // SSD_L == 2
    blk = KT
    tasks = []

    def fill():
        if tasks:
            tasks.pop(0)()

    def drain():
        while tasks:
            fill()

    def col_blocks(k, got, name, w_ref):
        for c0 in range(0, w_ref.shape[1], blk):
            tasks.append(lambda c0=c0: got.setdefault(name, []).append(_dot(h[k], w_ref[:, c0:c0 + blk])))

    pm0 = proj_mixer(0)
    got1 = {}
    col_blocks(1, got1, "bm", wb_ref)
    col_blocks(1, got1, "xr", wxbc_ref)
    col_blocks(1, got1, "zz", wz_ref)
    tasks.append(lambda: got1.update(dt=_dot(h[1], wdt_ref[...])[:, :SSM_HEADS],
                                     dt_t=_dot_nt(wdtt_ref[...], h[1])))
    short_conv(0, pm0["bm"])
    fill()
    xa0 = ssm_conv(0, pm0["xr"])
    fill()
    scan_chunk(0, 0, xa0, pm0, fill)
    scan_chunk(0, 1, xa0, pm0, fill)
    drain()
    pm1 = dict(bm=jnp.concatenate(got1["bm"], axis=1), xr=jnp.concatenate(got1["xr"], axis=1),
               zz=jnp.concatenate(got1["zz"], axis=1), dt=got1["dt"], dt_t=got1["dt_t"])

    ckvs = {}
    tasks.extend([lambda: ckvs.update({0: attn_keys(0)}), lambda: attn_queries(0),
                  lambda: ckvs.update({1: attn_keys(1)}), lambda: attn_values(0, ckvs[0]),
                  lambda: attn_queries(1), lambda: attn_values(1, ckvs[1])])
    short_conv(1, pm1["bm"])
    fill()
    xa1 = ssm_conv(1, pm1["xr"])
    fill()
    scan_chunk(1, 0, xa1, pm1, fill)
    scan_chunk(1, 1, xa1, pm1, fill)
    drain()

    ubuf[0:P, :] = ubuf[tm:tm + P, :]
    xbuf[0:P, :] = xbuf[tm:tm + P, :]


def _inproj_mixer_call(x2, norm_consts, proj_weights, wuv, vone, mixer_params, n_kt):
    t = x2.shape[0]
    assert KT == QB and n_kt % IN_SUBTILES == 0 and KT % SSD_L == 0
    tm = IN_SUBTILES * KT
    steps_per_seq = n_kt // IN_SUBTILES
    scw, cw, cb, dtb, alog, dexp, nw = mixer_params
    L = SSD_L
    tril = jnp.asarray(np.tril(np.ones((L, L), np.float32)), BF16)
    triu = jnp.asarray(np.triu(np.ones((L, L), np.float32)), BF16)
    eh = jnp.asarray(np.tile(np.repeat(np.eye(SSM_HEADS, dtype=np.float32), SSM_HEAD_DIM, axis=1), (3, 1)), BF16)
    mixer_consts = [scw, cw, cb.reshape(1, -1), dtb.reshape(1, -1), dtb.reshape(-1, 1),
                    alog.reshape(1, -1), alog.reshape(-1, 1), dexp.reshape(1, -1), nw.reshape(1, -1),
                    tril, triu, eh]
    wqt, wckv, wiqt, wiwt = proj_weights[0], proj_weights[1], proj_weights[2], proj_weights[4]
    row = lambda w: pl.BlockSpec((tm, w), lambda r: (r, 0))
    rows = lambda w, dt: (row(w), jax.ShapeDtypeStruct((t, w), dt))
    qblk = lambda w, dt: (pl.BlockSpec((IN_SUBTILES, w, QB), lambda r: (r, 0, 0)),
                          jax.ShapeDtypeStruct((t // QB, w, QB), dt))
    hv = wuv.shape[0]
    outs = [qblk(wqt.shape[0], BF16), rows(wckv.shape[1], BF16),
            (pl.BlockSpec((1, IN_SUBTILES, hv, KT), lambda r: (r // steps_per_seq, r % steps_per_seq, 0, 0)),
             jax.ShapeDtypeStruct((t // (n_kt * KT), n_kt, hv, KT), BF16)),
            qblk(wiqt.shape[0], BF16), qblk(wiwt.shape[0], F32), rows(IDX_DIM, BF16),
            rows(B_WIDTH, BF16), rows(SSM_INNER, BF16)]
    consts = [*norm_consts, *proj_weights, wuv, vone, *mixer_consts]
    return pl.pallas_call(
        functools.partial(_inproj_mixer_kernel, steps_per_seq=steps_per_seq),
        grid=(t // tm,),
        in_specs=[row(D_MODEL)] + [_const_spec(c.shape) for c in consts],
        out_specs=[spec for spec, _ in outs],
        out_shape=[shape for _, shape in outs],
        scratch_shapes=[pltpu.VMEM((tm + SUBLANES, B_WIDTH), F32),
                        pltpu.VMEM((tm + SUBLANES, SSM_XBC), F32),
                        pltpu.VMEM((SSM_GROUPS, SSM_STATE, SSM_INNER // SSM_GROUPS), F32)],
        compiler_params=_params(("arbitrary",)),
        name="inproj_conv_ssd",
    )(x2, *consts)


def _dense_kernel(x_ref, a_ref, b_ref, c_ref, p_ref, g_post_ref, g_pre_ref, g_fpost_ref,
                  woa_ref, wob_ref, woc_ref, wg_ref, wu_ref, wd_ref, wpg_ref, wpp_ref, o_ref):
    subs = [slice(r * TM_SUB, (r + 1) * TM_SUB) for r in range(ROW_SUBTILES)]
    mix = [_dot(a_ref[rows, :], woa_ref[...]) + _dot(b_ref[rows, :], wob_ref[...])
           + _dot(c_ref[rows, :], woc_ref[...]) for rows in subs]
    ple = [_dot(p_ref[rows, :].astype(BF16), wpp_ref[...]) for rows in subs]
    x = [x_ref[rows, :] + _rms(m, g_post_ref[...]) for rows, m in zip(subs, mix)]
    h = [_rms(xx, g_pre_ref[...]).astype(BF16) for xx in x]
    gate = [_dot(hh, wg_ref[...]) for hh in h]
    up = [_dot(hh, wu_ref[...]) for hh in h]
    act = [(g * _sigmoid(g) * u).astype(BF16) for g, u in zip(gate, up)]
    f = [_dot(aa, wd_ref[...]) for aa in act]
    x = [xx + _rms(ff, g_fpost_ref[...]) for xx, ff in zip(x, f)]
    pg = [_sigmoid(_dot(xx.astype(BF16), wpg_ref[...])) for xx in x]
    for rows, xx, g, e in zip(subs, x, pg, ple):
        o_ref[rows, :] = xx + g * e


def _dense_call(x2, a, b, c, p_all, layer, g_post, g_pre, g_fpost, woa, wob, woc, wg, wu, wd, wpg, wpp):
    t = x2.shape[0]
    tm = TM_OUT
    row = lambda w: pl.BlockSpec((tm, w), lambda r: (r, 0))
    steps_per_seq = p_all.shape[2] // tm
    p_spec = pl.BlockSpec((None, None, tm, PLE_DIM),
                          lambda r: (layer, r // steps_per_seq, r % steps_per_seq, 0))
    consts = [g_post, g_pre, g_fpost, woa, wob, woc, wg, wu, wd, wpg, wpp]
    return pl.pallas_call(
        _dense_kernel,
        grid=(t // tm,),
        in_specs=[row(D_MODEL), row(a.shape[1]), row(b.shape[1]), row(c.shape[1]), p_spec]
                 + [_const_spec(w.shape) for w in consts],
        out_specs=row(D_MODEL),
        out_shape=jax.ShapeDtypeStruct((t, D_MODEL), F32),
        compiler_params=_params(("parallel",)),
        name="outproj_ffn_ple",
    )(x2, a, b, c, p_all, *consts)


def kernel(x, p, pre_mix_norm, post_mix_norm, pre_ffn_norm, post_ffn_norm, w_in, kv_norm, idx_k_norm_g, idx_k_norm_b, w_uk, w_uv, rel_bias, short_conv_w, ssm_conv_w, ssm_conv_b, ssm_dt_bias, ssm_a_log, ssm_d, ssm_norm, w_out, w_ffn_gate, w_ffn_up, w_ffn_down, w_ple_proj, w_ple_gate):
    bsz, s, d = x.shape
    assert d == D_MODEL and s % (IN_SUBTILES * KT) == 0 and s % QB == 0 and s % SSD_L == 0
    assert (bsz * s) % TM_OUT == 0
    t = bsz * s
    k_top = min(TOPK_MAX, s // 4)

    bucket_np, near, far_bucket = _bucket_tiles()
    bias_tiles = _bias_tiles_call(rel_bias.astype(F32), jnp.asarray(bucket_np), far_bucket)
    tri = jnp.asarray(np.tril(np.ones((KT, KT), np.float32)), BF16)
    row1 = lambda v: v.reshape(1, -1).astype(F32)
    vone_np = np.zeros((A_HEADS, V_ROWS, 1), np.float32)
    vone_np[:, A_V_DIM] = 1.0
    vone = jnp.asarray(vone_np.reshape(A_HEADS * V_ROWS, 1))

    x2 = x.reshape(t, d)
    p_all = p.astype(F32)
    w_in_all = w_in.astype(F32)
    for i in range(DEPTH):
        wparts = _wprep_call(w_in_all, i)
        wuv_t = jnp.transpose(w_uv[i], (1, 2, 0)).astype(BF16)
        wuv_t = jnp.pad(wuv_t, ((0, 0), (0, V_ROWS - A_V_DIM), (0, 0))).reshape(A_HEADS * V_ROWS, A_KV_RANK)
        wq, wckv, wiq, wsm, wb, wz, wxbc, wdt = wparts
        wiw_t = wsm[:, IW_OFF:IW_OFF + SUBLANES].T
        wdt_t = wdt[:, :SSM_HEADS].T
        q_t, ckv, vt4, iq_t, iw_t, ik, b_out, c_out = _inproj_mixer_call(
            x2, (row1(pre_mix_norm[i]), row1(kv_norm[i]), row1(idx_k_norm_g[i]), row1(idx_k_norm_b[i])),
            (wq.T, wckv, wiq.T, wsm, wiw_t, wb, wz, wxbc, wdt, wdt_t), wuv_t, vone,
            (short_conv_w[i].astype(F32), ssm_conv_w[i].astype(F32), ssm_conv_b[i].astype(F32),
             ssm_dt_bias[i].astype(F32), ssm_a_log[i].astype(F32),
             jnp.repeat(ssm_d[i].astype(F32), SSM_HEAD_DIM), ssm_norm[i].astype(F32)), s // KT)

        a_out = _attn_call(
            q_t, iq_t, iw_t, ckv.reshape(bsz, s, A_KV_RANK), vt4, ik.reshape(bsz, s, IDX_DIM),
            jnp.transpose(w_uk[i], (1, 0, 2)).astype(BF16),
            bias_tiles, tri, k_top=k_top, near=near)

        wo = w_out[i].astype(BF16)
        na, nbw = A_HEADS * A_V_DIM, A_HEADS * A_V_DIM + B_WIDTH
        x2 = _dense_call(
            x2, a_out, b_out, c_out, p_all, i,
            row1(post_mix_norm[i]), row1(pre_ffn_norm[i]), row1(post_ffn_norm[i]),
            wo[:na], wo[na:nbw], wo[nbw:], w_ffn_gate[i].astype(BF16), w_ffn_up[i].astype(BF16),
            w_ffn_down[i].astype(BF16), w_ple_gate[i].astype(BF16), w_ple_proj[i].astype(BF16))
    return x2.reshape(bsz, s, d)
```

```python
import functools
import math

import numpy as np
import jax
import jax.numpy as jnp
from jax import lax
from jax.experimental import pallas as pl
from jax.experimental.pallas import tpu as pltpu

F32, BF16, I32 = jnp.float32, jnp.bfloat16, jnp.int32

D_MODEL = 1024
DEPTH = 2
CHUNK = 64
A_HEADS = 8
A_QK_DIM = 64
A_V_DIM = 64
A_KV_RANK = 256
IDX_HEADS = 4
IDX_DIM = 64
TOPK_MAX = 256
REL_BUCKETS = 32
REL_MAX_DIST = 1024
B_WIDTH = 512
SHORT_CONV = 3
SSM_HEADS = 16
SSM_HEAD_DIM = 64
SSM_INNER = SSM_HEADS * SSM_HEAD_DIM
SSM_GROUPS = 2
SSM_STATE = 128
SSM_CONV = 4
SSM_XBC = SSM_INNER + 2 * SSM_GROUPS * SSM_STATE
PLE_DIM = 256
NORM_EPS = 1e-6
IN_SPLITS = (A_HEADS * A_QK_DIM, A_KV_RANK, IDX_HEADS * IDX_DIM, IDX_DIM, IDX_HEADS,
             B_WIDTH, B_WIDTH, B_WIDTH, SSM_INNER, SSM_XBC, SSM_HEADS)
IN_OFFSETS = tuple(int(v) for v in np.concatenate([[0], np.cumsum(IN_SPLITS)]))

LANES = 128
SUBLANES = 8
VMEM_LIMIT_BYTES = 56 * 1024 * 1024

QB = 2 * LANES
QGROUPS = QB // LANES
KT = 256
SSD_L = 128
IN_SUBTILES = 4
TM_SUB = 256
ROW_SUBTILES = 2
TM_OUT = TM_SUB * ROW_SUBTILES
SMALL_W = LANES
IW_OFF = IDX_DIM
INT_MIN = np.int32(-2 ** 31)
LOG2E = math.log2(math.e)
BF16_ROWS = 2 * SUBLANES
V_ROWS = A_V_DIM + BF16_ROWS

def _const_spec(shape):
    nd = len(shape)
    return pl.BlockSpec(shape, lambda *_: (0,) * nd, pipeline_mode=pl.Buffered(1))


def _params(sem):
    return pltpu.CompilerParams(dimension_semantics=sem, vmem_limit_bytes=VMEM_LIMIT_BYTES)


def _rms(x, g):
    return x * lax.rsqrt(jnp.mean(x * x, axis=-1, keepdims=True) + NORM_EPS) * g


def _sigmoid(x):
    return 1.0 / (1.0 + jnp.exp(-x))


def _split3(v):
    hi = v.astype(BF16)
    r1 = v - hi.astype(F32)
    mid = r1.astype(BF16)
    lo = (r1 - mid.astype(F32)).astype(BF16)
    return hi, mid, lo


def _dot(a, b):
    return jnp.dot(a, b, preferred_element_type=F32)


def _t5_bucket_np(rel):
    half = REL_BUCKETS // 2
    max_exact = half // 2
    ret = np.where(rel > 0, half, 0)
    n = np.abs(rel)
    nf = np.maximum(n, 1).astype(np.float32)
    large = max_exact + (np.log(nf / np.float32(max_exact)) / np.float32(math.log(REL_MAX_DIST / max_exact))
                         * np.float32(half - max_exact)).astype(np.int32)
    large = np.minimum(large, half - 1)
    return (ret + np.where(n < max_exact, n, large)).astype(np.int32)


@functools.lru_cache(maxsize=None)
def _bucket_tiles():
    kr = np.arange(KT)[:, None]
    qc = np.arange(QB)[None, :]
    n_probe = REL_MAX_DIST // QB + 2
    tiles = [_t5_bucket_np(kr - qc - o * QB) for o in range(n_probe)]
    far_bucket = int(tiles[-1][0, 0])
    near = next(o + 1 for o in range(n_probe - 1, -1, -1) if not np.all(tiles[o] == far_bucket))
    assert 0 < near < n_probe and np.all(_t5_bucket_np(-np.arange(near * QB - KT + 1, 8 * REL_MAX_DIST)) == far_bucket)
    return np.stack(tiles[:near + 1]), near, far_bucket


def _bias_kernel(rb_ref, bk_ref, out_ref, *, far_bucket):
    bk = bk_ref[0]
    for h in range(A_HEADS):
        acc = jnp.zeros(bk.shape, F32)
        for b in range(REL_BUCKETS):
            acc = jnp.where(bk == b, (rb_ref[b, h] - rb_ref[far_bucket, h]) * LOG2E, acc)
        out_ref[0, :, h * QB:(h + 1) * QB] = acc


def _bias_tiles_call(rel_bias, bucket_tiles, far_bucket):
    n = bucket_tiles.shape[0]
    return pl.pallas_call(
        functools.partial(_bias_kernel, far_bucket=far_bucket),
        grid=(n,),
        in_specs=[pl.BlockSpec(memory_space=pltpu.SMEM),
                  pl.BlockSpec((1, KT, QB), lambda o: (o, 0, 0))],
        out_specs=pl.BlockSpec((1, KT, A_HEADS * QB), lambda o: (o, 0, 0)),
        out_shape=jax.ShapeDtypeStruct((n, KT, A_HEADS * QB), F32),
        compiler_params=_params(("arbitrary",)),
        name="rel_bias_tiles",
    )(rel_bias, bucket_tiles)


def _wprep_kernel(w_ref, wq_out, wckv_out, wiq_out, wsm_out, wb_out, wz_out, wxbc_out, wdt_out):
    o_q, o_ckv, o_iq, o_ik, _, o_bg, _, _, o_z, o_xbc, o_dt, o_end = IN_OFFSETS
    wq_out[...] = w_ref[:, o_q:o_ckv].astype(BF16)
    wckv_out[...] = w_ref[:, o_ckv:o_iq].astype(BF16)
    wiq_out[...] = w_ref[:, o_iq:o_ik].astype(BF16)
    wsm_out[...] = w_ref[:, o_ik:o_ik + SMALL_W].astype(BF16)
    tail = w_ref[:, o_bg:o_end].astype(BF16)
    wb_out[...] = tail[:, :o_z - o_bg]
    wz_out[...] = tail[:, o_z - o_bg:o_xbc - o_bg]
    wxbc_out[...] = tail[:, o_xbc - o_bg:o_dt - o_bg]
    wdt_out[...] = jnp.zeros(wdt_out.shape, BF16)
    wdt_out[:, :o_end - o_dt] = tail[:, o_dt - o_bg:]


def _wprep_call(w_all, layer):
    _, d, n = w_all.shape
    tr = KT
    widths = [IN_SPLITS[0], IN_SPLITS[1], IN_SPLITS[2], SMALL_W, 3 * B_WIDTH, SSM_INNER, SSM_XBC, SMALL_W]
    return pl.pallas_call(
        _wprep_kernel,
        grid=(d // tr,),
        in_specs=[pl.BlockSpec((None, tr, n), lambda r: (layer, r, 0))],
        out_specs=[pl.BlockSpec((tr, wd), lambda r: (r, 0)) for wd in widths],
        out_shape=[jax.ShapeDtypeStruct((d, wd), BF16) for wd in widths],
        compiler_params=_params(("parallel",)),
        name="inproj_weight_prep",
    )(w_all)


def _dot_nt(a, b):
    return lax.dot_general(a, b, (((1,), (1,)), ((), ())), preferred_element_type=F32)


def _bit_transpose32(words):
    a = list(words)
    m, j = 0x0000FFFF, 16
    while j:
        k = 0
        while k < 32:
            t = (a[k] ^ lax.shift_right_logical(a[k + j], np.int32(j))) & np.int32(m)
            a[k] = a[k] ^ t
            a[k + j] = a[k + j] ^ (t << j)
            k = (k + j + 1) & ~j
        j >>= 1
        m = (m ^ (m << j)) & 0xFFFFFFFF
    return a


def _attn_kernel(q_ref, iq_ref, iw_ref, ckv_ref, vt_ref, ik_ref, wuk_ref, bias_ref, tri_ref,
                 o_ref, keys_ref, planes_ref, qlat_ref, acc_ref, p0_ref, p1_ref, lg0_ref, lg1_ref,
                 *, k_top, near):
    i = pl.program_id(1)
    nt = ((i + 1) * QB + KT - 1) // KT
    lane = lax.broadcasted_iota(I32, (1, QB), 1)
    qchunk = (i * QB + lane) >> int(math.log2(CHUNK))
    krow = lax.broadcasted_iota(I32, (KT, QB), 0)

    q_t = q_ref[0]
    for h in range(A_HEADS):
        ql = _dot(wuk_ref[h], q_t[h * A_QK_DIM:(h + 1) * A_QK_DIM, :])
        qlat_ref[:, h * QB:(h + 1) * QB] = (ql * (A_QK_DIM ** -0.5 * LOG2E)).astype(BF16)

    iq_t = iq_ref[0]
    iw = iw_ref[0][:IDX_HEADS, :]

    def score_tile(j, carry, masked):
        r0 = pl.multiple_of(j * KT, KT)
        ikt = ik_ref[0, pl.ds(r0, KT), :]
        s = jnp.zeros((KT, QB), F32)
        for h in range(IDX_HEADS):
            raw = _dot(ikt, iq_t[h * IDX_DIM:(h + 1) * IDX_DIM, :])
            s = s + jnp.maximum(raw, 0.0) * iw[h:h + 1, :]
        if masked:
            adm = ((r0 + krow) >> int(math.log2(CHUNK))) <= qchunk
            s = jnp.where(adm, s, -jnp.inf)
        bits = lax.bitcast_convert_type(s, I32)
        bits = jnp.where(bits == INT_MIN, 0, bits)
        key = bits ^ ((bits >> 31) & np.int32(0x7FFFFFFF))
        keys_ref[pl.ds(r0, KT), :] = key
        ukey = key ^ INT_MIN
        planes = _bit_transpose32([ukey[v * SUBLANES:(v + 1) * SUBLANES, :] for v in range(32)])
        p0 = pl.multiple_of(j * SUBLANES, SUBLANES)
        for b in range(32):
            planes_ref[b, pl.ds(p0, SUBLANES), :] = planes[31 - b]
        return carry

    @pl.when(i == 0)
    def _():
        planes_ref[...] = jnp.zeros(planes_ref.shape, I32)

    def score_pair(u, carry):
        score_tile(2 * u, 0, masked=False)
        score_tile(2 * u + 1, 0, masked=False)
        return carry

    lax.fori_loop(0, (nt - 1) // 2, score_pair, 0)

    @pl.when((nt - 1) % 2 == 1)
    def _():
        score_tile(nt - 2, 0, masked=False)

    score_tile(nt - 1, 0, masked=True)

    kf = float(k_top)

    def search(n_prow):
        prow = lax.broadcasted_iota(I32, (n_prow, QB), 0)

        def search_step(k, carry):
            live, c_gt, thr_u = carry
            b = 31 - k
            x = live & planes_ref[b, 0:n_prow, :]
            pc = lax.population_count(x)
            parts = [pc[r * SUBLANES:(r + 1) * SUBLANES, :] for r in range(n_prow // SUBLANES)]
            while len(parts) > 1:
                parts = [parts[k2] + parts[k2 + 1] for k2 in range(0, len(parts) - 1, 2)] + (
                    [parts[-1]] if len(parts) % 2 else [])
            cnt = c_gt + jnp.sum(parts[0].astype(F32), axis=0, keepdims=True)
            take = cnt >= kf
            live = jnp.where(take, x, live ^ x)
            c_gt = jnp.where(take, c_gt, cnt)
            thr_u = thr_u | jnp.where(take, jnp.left_shift(jnp.int32(1), b), np.int32(0))
            return live, c_gt, thr_u

        _, c_gt, thr_u = lax.fori_loop(
            0, 32, search_step,
            (jnp.where(prow < nt * SUBLANES, np.int32(-1), np.int32(0)),
             jnp.zeros((1, QB), F32), jnp.zeros((1, QB), I32)))
        return c_gt, thr_u

    n_all = planes_ref.shape[1]
    if n_all % (4 * SUBLANES) == 0:
        c_gt, thr_u = lax.cond(
            nt * SUBLANES <= n_all // 4, lambda: search(n_all // 4),
            lambda: lax.cond(nt * SUBLANES <= n_all // 2, lambda: search(n_all // 2), lambda: search(n_all)))
    else:
        c_gt, thr_u = search(n_all)
    thr = thr_u ^ INT_MIN
    need = kf - c_gt

    p_refs, lg_refs = (p0_ref, p1_ref), (lg0_ref, lg1_ref)
    acc_ref[...] = jnp.zeros(acc_ref.shape, F32)
    p1_ref[...] = jnp.zeros(p1_ref.shape, BF16)

    def stage_scores(j, cnt_eq, slot, far=False):
        r0 = pl.multiple_of(j * KT, KT)
        kt = keys_ref[pl.ds(r0, KT), :]
        eq = kt == thr
        eqf = jnp.where(eq, 1.0, 0.0)
        rank = _dot(tri_ref[...], eqf.astype(BF16)) + cnt_eq
        sel = (kt > thr) | (eq & (rank <= need))
        if not far:
            sel = sel & (((r0 + krow) >> int(math.log2(CHUNK))) <= qchunk)
            bias = bias_ref[jnp.minimum(i - j * (KT // QB), near)]
        logits = _dot(ckv_ref[0, pl.ds(r0, KT), :], qlat_ref[...])
        tmax = [[] for _ in range(QGROUPS)]
        for h in range(A_HEADS):
            for g in range(QGROUPS):
                sl = slice(h * QB + g * LANES, h * QB + (g + 1) * LANES)
                lh = logits[:, sl] if far else logits[:, sl] + bias[:, sl]
                lh = jnp.where(sel[:, g * LANES:(g + 1) * LANES], lh, -jnp.inf)
                lg_refs[slot][:, sl] = lh
                tmax[g].append(jnp.max(lh, axis=0, keepdims=True))
        tmax = tuple(jnp.concatenate(rows, axis=0) for rows in tmax)
        return tmax, cnt_eq + jnp.sum(eqf, axis=0, keepdims=True)

    def stage_probs(ms, tmax, slot):
        new_ms, alphas = [], []
        for g in range(QGROUPS):
            m_new = jnp.maximum(ms[g], tmax[g])
            m_safe = jnp.where(m_new == -jnp.inf, 0.0, m_new)
            alphas.append(jnp.exp2(ms[g] - m_safe))
            new_ms.append(m_new)
            for h in range(A_HEADS):
                sl = slice(h * QB + g * LANES, h * QB + (g + 1) * LANES)
                p_refs[slot][:, sl] = jnp.exp2(lg_refs[slot][:, sl] - m_safe[h:h + 1, :]).astype(BF16)
        return tuple(new_ms), tuple(alphas)

    def stage_values(alphas, jp, slot):
        for h in range(A_HEADS):
            vs = slice(h * V_ROWS, (h + 1) * V_ROWS)
            pv = _dot(vt_ref[0, jp, vs, :], p_refs[slot][:, h * QB:(h + 1) * QB])
            alpha = jnp.concatenate([a[h:h + 1, :] for a in alphas], axis=1)
            acc_ref[vs, :] = acc_ref[vs, :] * alpha + pv

    def trip(t, carry, slot, far=False):
        ms, tmax, cnt_eq, alphas = carry
        stage_values(alphas, jnp.maximum(t - 1, 0), 1 - slot)
        ms, alphas = stage_probs(ms, tmax, slot)
        tmax, cnt_eq = stage_scores(t + 1, cnt_eq, 1 - slot, far)
        return ms, tmax, cnt_eq, alphas

    def finish(t, carry, slot):
        ms, tmax, _, alphas = carry
        stage_values(alphas, jnp.maximum(t - 1, 0), 1 - slot)
        _, alphas = stage_probs(ms, tmax, slot)
        stage_values(alphas, t, slot)

    def pair(u, carry, far=False):
        return trip(2 * u + 1, trip(2 * u, carry, 0, far), 1, far)

    tmax0, cnt_eq0 = stage_scores(0, jnp.zeros((1, QB), F32), 0)
    init = (tuple(jnp.full((A_HEADS, LANES), -jnp.inf, F32) for _ in range(QGROUPS)), tmax0, cnt_eq0,
            tuple(jnp.ones((A_HEADS, LANES), F32) for _ in range(QGROUPS)))
    n_trips = nt - 1
    n_far_pairs = jnp.maximum(i * (QB // KT) - near, 0) // 2
    carry = lax.fori_loop(0, n_far_pairs, functools.partial(pair, far=True), init)
    carry = lax.fori_loop(n_far_pairs, n_trips // 2, pair, carry)

    @pl.when(n_trips % 2 == 0)
    def _():
        finish(nt - 1, carry, 0)

    @pl.when(n_trips % 2 == 1)
    def _():
        finish(nt - 1, trip(nt - 2, carry, 0), 1)

    a_t = jnp.concatenate(
        [acc_ref[h * V_ROWS:h * V_ROWS + A_V_DIM, :]
         * (1.0 / acc_ref[h * V_ROWS + A_V_DIM:h * V_ROWS + A_V_DIM + 1, :]) for h in range(A_HEADS)],
        axis=0)
    o_ref[...] = a_t.T.astype(BF16)


def _attn_call(q_t, iq_t, iw_t, ckv3, vt4, ik3, wuk, bias_tiles, tri, *, k_top, near):
    b, s = ckv3.shape[0], ckv3.shape[1]
    nb = s // QB
    n_kt = s // KT
    hv = A_HEADS * A_V_DIM
    hvr = A_HEADS * V_ROWS
    kern = functools.partial(_attn_kernel, k_top=k_top, near=near)
    row = lambda w: pl.BlockSpec((QB, w), lambda bb, i: (bb * nb + i, 0))
    qblk = lambda a: pl.BlockSpec((1, a.shape[1], QB), lambda bb, i: (bb * nb + i, 0, 0))
    return pl.pallas_call(
        kern,
        grid=(b, nb),
        in_specs=[
            qblk(q_t), qblk(iq_t), qblk(iw_t),
            pl.BlockSpec((1, s, A_KV_RANK), lambda bb, i: (bb, 0, 0)),
            pl.BlockSpec((1, n_kt, hvr, KT), lambda bb, i: (bb, 0, 0, 0)),
            pl.BlockSpec((1, s, IDX_DIM), lambda bb, i: (bb, 0, 0)),
            _const_spec(wuk.shape), _const_spec(bias_tiles.shape), _const_spec(tri.shape),
        ],
        out_specs=row(hv),
        out_shape=jax.ShapeDtypeStruct((b * s, hv), BF16),
        scratch_shapes=[pltpu.VMEM((s, QB), I32), pltpu.VMEM((32, s // 32, QB), I32),
                        pltpu.VMEM((A_KV_RANK, A_HEADS * QB), BF16),
                        pltpu.VMEM((hvr, QB), F32),
                        pltpu.VMEM((KT, A_HEADS * QB), BF16), pltpu.VMEM((KT, A_HEADS * QB), BF16),
                        pltpu.VMEM((KT, A_HEADS * QB), F32), pltpu.VMEM((KT, A_HEADS * QB), F32)],
        compiler_params=_params(("parallel", "arbitrary")),
        name="dsa_attention",
    )(q_t, iq_t, iw_t, ckv3, vt4, ik3, wuk, bias_tiles, tri)


def _softplus(x):
    return jnp.maximum(x, 0.0) + jnp.log1p(jnp.exp(-jnp.abs(x)))


def _ssd_chunk(xa, zz, dt_raw, dt_t_raw, state, dtb_r, dtb_c, alog_r, alog_c, dexp, nw, tril, triu, eh,
               fill=lambda: None):
    L = SSD_L
    gw = SSM_INNER // SSM_GROUPS
    xs = xa[:, :SSM_INNER]
    dt = _softplus(dt_raw + dtb_r)
    a = dt * (-jnp.exp(alog_r))
    a_cs = sum(_dot(tril, part) for part in _split3(a))
    fill()
    dt_t = _softplus(dt_t_raw + dtb_c)
    a_t = dt_t * (-jnp.exp(alog_c))
    a_cs_t = sum(_dot(part, triu) for part in _split3(a_t))

    def expand(v):
        v3 = jnp.concatenate([v, v, v], axis=1)
        hi, mid, lo = _split3(v3)
        term = lax.broadcasted_iota(I32, v3.shape, 1) // SSM_HEADS
        return _dot(jnp.where(term == 0, hi, jnp.where(term == 1, mid, lo)), eh)

    xdt = xs * expand(dt)
    fill()
    e_cs = expand(jnp.exp(a_cs))
    dec = expand(jnp.exp(a_cs[L - 1:L, :] - a_cs))
    fill()
    chunk_dec = e_cs[L - 1:L, :]

    row = lax.broadcasted_iota(I32, (L, L), 0)
    col = lax.broadcasted_iota(I32, (L, L), 1)
    causal = row >= col
    lo_half = lax.broadcasted_iota(I32, (L, LANES), 1) < SSM_HEAD_DIM
    hpg = SSM_HEADS // SSM_GROUPS

    y_parts = []
    for g in range(SSM_GROUPS):
        bg = xa[:, SSM_INNER + g * SSM_STATE:SSM_INNER + (g + 1) * SSM_STATE].astype(BF16)
        cg = xa[:, SSM_INNER + (SSM_GROUPS + g) * SSM_STATE:
                SSM_INNER + (SSM_GROUPS + g + 1) * SSM_STATE].astype(BF16)
        cb = _dot_nt(cg, bg)
        fill()
        sl = slice(g * gw, (g + 1) * gw)
        st = state[g]
        y_off = _dot(cg, st.astype(BF16)) * e_cs[:, sl]
        for k in range(hpg // 2):
            xp = xdt[:, g * gw + k * LANES:g * gw + (k + 1) * LANES]
            x2 = jnp.concatenate([jnp.where(lo_half, xp, 0.0), jnp.where(lo_half, 0.0, xp)], axis=0)
            ms = []
            for hh in range(2):
                h = g * hpg + 2 * k + hh
                seg = a_cs[:, h:h + 1] - a_cs_t[h:h + 1, :]
                ms.append((cb * jnp.exp(jnp.where(causal, seg, -jnp.inf))).astype(BF16))
            y_parts.append(y_off[:, k * LANES:(k + 1) * LANES]
                           + _dot(jnp.concatenate(ms, axis=1), x2.astype(BF16)))
            if k % 2 == 1:
                fill()
        xw = (dec[:, sl] * xdt[:, sl]).astype(BF16)
        upd = lax.dot_general(bg, xw, (((0,), (0,)), ((), ())), preferred_element_type=F32)
        state[g] = chunk_dec[:, sl] * st + upd

    y = jnp.concatenate(y_parts, axis=1) + dexp * xs
    y = y * (zz * _sigmoid(zz))
    return jnp.concatenate(
        [_rms(y[:, g * gw:(g + 1) * gw], nw[:, g * gw:(g + 1) * gw]) for g in range(SSM_GROUPS)], axis=1)


def _inproj_mixer_kernel(x_ref, g_ref, kvn_ref, ikg_ref, ikb_ref,
                         wqt_ref, wckv_ref, wiqt_ref, wsm_ref, wiwt_ref, wb_ref, wz_ref, wxbc_ref,
                         wdt_ref, wdtt_ref, wuv_ref, vone_ref,
                         scw_ref, cw_ref, cb_ref, dtb_r_ref, dtb_c_ref, alog_r_ref, alog_c_ref,
                         dexp_ref, nw_ref, tril_ref, triu_ref, eh_ref,
                         qt_out, ckv_out, vt_out, iqt_out, iwt_out, ik_out, bo_out, co_out,
                         ubuf, xbuf, state, *, steps_per_seq):
    P = SUBLANES
    tm = IN_SUBTILES * KT

    @pl.when(pl.program_id(0) % steps_per_seq == 0)
    def _():
        ubuf[0:P, :] = jnp.zeros((P, B_WIDTH), F32)
        xbuf[0:P, :] = jnp.zeros((P, SSM_XBC), F32)
        state[...] = jnp.zeros(state.shape, F32)

    subs = [slice(k * KT, (k + 1) * KT) for k in range(IN_SUBTILES)]
    h = [_rms(x_ref[rows, :], g_ref[...]).astype(BF16) for rows in subs]
    ssd_consts = (dtb_r_ref[...], dtb_c_ref[...], alog_r_ref[...], alog_c_ref[...], dexp_ref[...],
                  nw_ref[...], tril_ref[...], triu_ref[...], eh_ref[...])

    def proj_mixer(k):
        return dict(bm=_dot(h[k], wb_ref[...]),
                    xr=_dot(h[k], wxbc_ref[...]), zz=_dot(h[k], wz_ref[...]),
                    dt=_dot(h[k], wdt_ref[...])[:, :SSM_HEADS],
                    dt_t=_dot_nt(wdtt_ref[...], h[k]))

    def short_conv(k, bm):
        base = P + k * KT
        u = bm[:, B_WIDTH:2 * B_WIDTH] * bm[:, 2 * B_WIDTH:3 * B_WIDTH]
        ubuf[base:base + KT, :] = u
        conv = scw_ref[SHORT_CONV - 1:SHORT_CONV, :] * u
        for j in range(SHORT_CONV - 1):
            off = base - (SHORT_CONV - 1) + j
            conv = conv + scw_ref[j:j + 1, :] * ubuf[off:off + KT, :]
        bo_out[subs[k], :] = (bm[:, :B_WIDTH] * conv).astype(BF16)

    def ssm_conv(k, xr):
        base = P + k * KT
        xbuf[base:base + KT, :] = xr
        xc = cw_ref[SSM_CONV - 1:SSM_CONV, :] * xr + cb_ref[...]
        for j in range(SSM_CONV - 1):
            off = base - (SSM_CONV - 1) + j
            xc = xc + cw_ref[j:j + 1, :] * xbuf[off:off + KT, :]
        return xc * _sigmoid(xc)

    def scan_chunk(k, c, xa, pm, fill):
        cs = slice(c * SSD_L, (c + 1) * SSD_L)
        y = _ssd_chunk(xa[cs, :], pm["zz"][cs, :], pm["dt"][cs, :], pm["dt_t"][:, cs], state, *ssd_consts,
                       fill=fill)
        r0 = k * KT + c * SSD_L
        co_out[r0:r0 + SSD_L, :] = y.astype(BF16)

    def attn_keys(k):
        ckv = _rms(_dot(h[k], wckv_ref[...]), kvn_ref[...]).astype(BF16)
        ckv_out[subs[k], :] = ckv
        ik = _dot(h[k], wsm_ref[...])[:, :IDX_DIM]
        mu = jnp.mean(ik, axis=-1, keepdims=True)
        var = jnp.mean(jnp.square(ik - mu), axis=-1, keepdims=True)
        ik_out[subs[k], :] = ((ik - mu) * lax.rsqrt(var + NORM_EPS) * ikg_ref[...] + ikb_ref[...]).astype(BF16)
        return ckv

    def attn_queries(k):
        qt_out[k] = _dot_nt(wqt_ref[...], h[k]).astype(BF16)
        iqt_out[k] = (_dot_nt(wiqt_ref[...], h[k]) * (IDX_DIM ** -0.5)).astype(BF16)
        iwt_out[k] = _dot_nt(wiwt_ref[...], h[k]) * (IDX_HEADS ** -0.5)

    def attn_values(k, ckv):
        vt_out[0, k] = (_dot_nt(wuv_ref[...], ckv) + vone_ref[...]).astype(BF16)

    blk = KT
    tasks = []

    def fill():
        if tasks:
            tasks.pop(0)()

    def drain():
        while tasks:
            fill()

    def proj_tasks(k, got):
        for name, w_ref in (("bm", wb_ref), ("xr", wxbc_ref), ("zz", wz_ref)):
            for c0 in range(0, w_ref.shape[1], blk):
                tasks.append(lambda name=name, w_ref=w_ref, c0=c0: got.setdefault(name, []).append(
                    _dot(h[k], w_ref[:, c0:c0 + blk])))
        tasks.append(lambda: got.update(dt=_dot(h[k], wdt_ref[...])[:, :SSM_HEADS],
                                        dt_t=_dot_nt(wdtt_ref[...], h[k])))

    pm = proj_mixer(0)
    ckvs = {}
    for k in range(IN_SUBTILES):
        got = {}
        if k + 1 < IN_SUBTILES:
            proj_tasks(k + 1, got)
        else:
            for kk in range(IN_SUBTILES):
                tasks.extend([lambda kk=kk: ckvs.update({kk: attn_keys(kk)}), lambda kk=kk: attn_queries(kk),
                              lambda kk=kk: attn_values(kk, ckvs[kk])])
        short_conv(k, pm["bm"])
        fill()
        xa = ssm_conv(k, pm["xr"])
        fill()
        for c in range(KT // SSD_L):
            scan_chunk(k, c, xa, pm, fill)
        drain()
        if k + 1 < IN_SUBTILES:
            pm = dict(bm=jnp.concatenate(got["bm"], axis=1), xr=jnp.concatenate(got["xr"], axis=1),
                      zz=jnp.concatenate(got["zz"], axis=1), dt=got["dt"], dt_t=got["dt_t"])

    ubuf[0:P, :] = ubuf[tm:tm + P, :]
    xbuf[0:P, :] = xbuf[tm:tm + P, :]


def _inproj_mixer_call(x2, norm_consts, proj_weights, wuv, vone, mixer_params, n_kt):
    t = x2.shape[0]
    assert KT == QB and n_kt % IN_SUBTILES == 0 and KT % SSD_L == 0
    tm = IN_SUBTILES * KT
    steps_per_seq = n_kt // IN_SUBTILES
    scw, cw, cb, dtb, alog, dexp, nw = mixer_params
    L = SSD_L
    tril = jnp.asarray(np.tril(np.ones((L, L), np.float32)), BF16)
    triu = jnp.asarray(np.triu(np.ones((L, L), np.float32)), BF16)
    eh = jnp.asarray(np.tile(np.repeat(np.eye(SSM_HEADS, dtype=np.float32), SSM_HEAD_DIM, axis=1), (3, 1)), BF16)
    mixer_consts = [scw, cw, cb.reshape(1, -1), dtb.reshape(1, -1), dtb.reshape(-1, 1),
                    alog.reshape(1, -1), alog.reshape(-1, 1), dexp.reshape(1, -1), nw.reshape(1, -1),
                    tril, triu, eh]
    wqt, wckv, wiqt, wiwt = proj_weights[0], proj_weights[1], proj_weights[2], proj_weights[4]
    row = lambda w: pl.BlockSpec((tm, w), lambda r: (r, 0))
    rows = lambda w, dt: (row(w), jax.ShapeDtypeStruct((t, w), dt))
    qblk = lambda w, dt: (pl.BlockSpec((IN_SUBTILES, w, QB), lambda r: (r, 0, 0)),
                          jax.ShapeDtypeStruct((t // QB, w, QB), dt))
    hv = wuv.shape[0]
    outs = [qblk(wqt.shape[0], BF16), rows(wckv.shape[1], BF16),
            (pl.BlockSpec((1, IN_SUBTILES, hv, KT), lambda r: (r // steps_per_seq, r % steps_per_seq, 0, 0)),
             jax.ShapeDtypeStruct((t // (n_kt * KT), n_kt, hv, KT), BF16)),
            qblk(wiqt.shape[0], BF16), qblk(wiwt.shape[0], F32), rows(IDX_DIM, BF16),
            rows(B_WIDTH, BF16), rows(SSM_INNER, BF16)]
    consts = [*norm_consts, *proj_weights, wuv, vone, *mixer_consts]
    return pl.pallas_call(
        functools.partial(_inproj_mixer_kernel, steps_per_seq=steps_per_seq),
        grid=(t // tm,),
        in_specs=[row(D_MODEL)] + [_const_spec(c.shape) for c in consts],
        out_specs=[spec for spec, _ in outs],
        out_shape=[shape for _, shape in outs],
        scratch_shapes=[pltpu.VMEM((tm + SUBLANES, B_WIDTH), F32),
                        pltpu.VMEM((tm + SUBLANES, SSM_XBC), F32),
                        pltpu.VMEM((SSM_GROUPS, SSM_STATE, SSM_INNER // SSM_GROUPS), F32)],
        compiler_params=_params(("arbitrary",)),
        name="inproj_conv_ssd",
    )(x2, *consts)


def _dense_kernel(x_ref, a_ref, b_ref, c_ref, p_ref, g_post_ref, g_pre_ref, g_fpost_ref,
                  woa_ref, wob_ref, woc_ref, wg_ref, wu_ref, wd_ref, wpg_ref, wpp_ref, o_ref):
    subs = [slice(r * TM_SUB, (r + 1) * TM_SUB) for r in range(ROW_SUBTILES)]
    mix = [_dot(a_ref[rows, :], woa_ref[...]) + _dot(b_ref[rows, :], wob_ref[...])
           + _dot(c_ref[rows, :], woc_ref[...]) for rows in subs]
    ple = [_dot(p_ref[rows, :].astype(BF16), wpp_ref[...]) for rows in subs]
    x = [x_ref[rows, :] + _rms(m, g_post_ref[...]) for rows, m in zip(subs, mix)]
    h = [_rms(xx, g_pre_ref[...]).astype(BF16) for xx in x]
    gate = [_dot(hh, wg_ref[...]) for hh in h]
    up = [_dot(hh, wu_ref[...]) for hh in h]
    act = [(g * _sigmoid(g) * u).astype(BF16) for g, u in zip(gate, up)]
    f = [_dot(aa, wd_ref[...]) for aa in act]
    x = [xx + _rms(ff, g_fpost_ref[...]) for xx, ff in zip(x, f)]
    pg = [_sigmoid(_dot(xx.astype(BF16), wpg_ref[...])) for xx in x]
    for rows, xx, g, e in zip(subs, x, pg, ple):
        o_ref[rows, :] = xx + g * e


def _dense_call(x2, a, b, c, p_all, layer, g_post, g_pre, g_fpost, woa, wob, woc, wg, wu, wd, wpg, wpp):
    t = x2.shape[0]
    tm = TM_OUT
    row = lambda w: pl.BlockSpec((tm, w), lambda r: (r, 0))
    steps_per_seq = p_all.shape[2] // tm
    p_spec = pl.BlockSpec((None, None, tm, PLE_DIM),
                          lambda r: (layer, r // steps_per_seq, r % steps_per_seq, 0))
    consts = [g_post, g_pre, g_fpost, woa, wob, woc, wg, wu, wd, wpg, wpp]
    return pl.pallas_call(
        _dense_kernel,
        grid=(t // tm,),
        in_specs=[row(D_MODEL), row(a.shape[1]), row(b.shape[1]), row(c.shape[1]), p_spec]
                 + [_const_spec(w.shape) for w in consts],
        out_specs=row(D_MODEL),
        out_shape=jax.ShapeDtypeStruct((t, D_MODEL), F32),
        compiler_params=_params(("parallel",)),
        name="outproj_ffn_ple",
    )(x2, a, b, c, p_all, *consts)


def kernel(x, p, pre_mix_norm, post_mix_norm, pre_ffn_norm, post_ffn_norm, w_in, kv_norm, idx_k_norm_g, idx_k_norm_b, w_uk, w_uv, rel_bias, short_conv_w, ssm_conv_w, ssm_conv_b, ssm_dt_bias, ssm_a_log, ssm_d, ssm_norm, w_out, w_ffn_gate, w_ffn_up, w_ffn_down, w_ple_proj, w_ple_gate):
    bsz, s, d = x.shape
    assert d == D_MODEL and s % (IN_SUBTILES * KT) == 0 and s % QB == 0 and s % SSD_L == 0
    assert (bsz * s) % TM_OUT == 0
    t = bsz * s
    k_top = min(TOPK_MAX, s // 4)

    bucket_np, near, far_bucket = _bucket_tiles()
    bias_tiles = _bias_tiles_call(rel_bias.astype(F32), jnp.asarray(bucket_np), far_bucket)
    tri = jnp.asarray(np.tril(np.ones((KT, KT), np.float32)), BF16)
    row1 = lambda v: v.reshape(1, -1).astype(F32)
    vone_np = np.zeros((A_HEADS, V_ROWS, 1), np.float32)
    vone_np[:, A_V_DIM] = 1.0
    vone = jnp.asarray(vone_np.reshape(A_HEADS * V_ROWS, 1))

    x2 = x.reshape(t, d)
    p_all = p.astype(F32)
    w_in_all = w_in.astype(F32)
    for i in range(DEPTH):
        wparts = _wprep_call(w_in_all, i)
        wuv_t = jnp.transpose(w_uv[i], (1, 2, 0)).astype(BF16)
        wuv_t = jnp.pad(wuv_t, ((0, 0), (0, V_ROWS - A_V_DIM), (0, 0))).reshape(A_HEADS * V_ROWS, A_KV_RANK)
        wq, wckv, wiq, wsm, wb, wz, wxbc, wdt = wparts
        wiw_t = wsm[:, IW_OFF:IW_OFF + SUBLANES].T
        wdt_t = wdt[:, :SSM_HEADS].T
        q_t, ckv, vt4, iq_t, iw_t, ik, b_out, c_out = _inproj_mixer_call(
            x2, (row1(pre_mix_norm[i]), row1(kv_norm[i]), row1(idx_k_norm_g[i]), row1(idx_k_norm_b[i])),
            (wq.T, wckv, wiq.T, wsm, wiw_t, wb, wz, wxbc, wdt, wdt_t), wuv_t, vone,
            (short_conv_w[i].astype(F32), ssm_conv_w[i].astype(F32), ssm_conv_b[i].astype(F32),
             ssm_dt_bias[i].astype(F32), ssm_a_log[i].astype(F32),
             jnp.repeat(ssm_d[i].astype(F32), SSM_HEAD_DIM), ssm_norm[i].astype(F32)), s // KT)

        a_out = _attn_call(
            q_t, iq_t, iw_t, ckv.reshape(bsz, s, A_KV_RANK), vt4, ik.reshape(bsz, s, IDX_DIM),
            jnp.transpose(w_uk[i], (1, 0, 2)).astype(BF16),
            bias_tiles, tri, k_top=k_top, near=near)

        wo = w_out[i].astype(BF16)
        na, nbw = A_HEADS * A_V_DIM, A_HEADS * A_V_DIM + B_WIDTH
        x2 = _dense_call(
            x2, a_out, b_out, c_out, p_all, i,
            row1(post_mix_norm[i]), row1(pre_ffn_norm[i]), row1(post_ffn_norm[i]),
            wo[:na], wo[na:nbw], wo[nbw:], w_ffn_gate[i].astype(BF16), w_ffn_up[i].astype(BF16),
            w_ffn_down[i].astype(BF16), w_ple_gate[i].astype(BF16), w_ple_proj[i].astype(BF16))
    return x2.reshape(bsz, s, d)
```

```python
import functools
import math

import numpy as np
import jax
import jax.numpy as jnp
from jax import lax
from jax.experimental import pallas as pl
from jax.experimental.pallas import tpu as pltpu

F32, BF16, I32 = jnp.float32, jnp.bfloat16, jnp.int32

D_MODEL = 1024
DEPTH = 2
CHUNK = 64
A_HEADS = 8
A_QK_DIM = 64
A_V_DIM = 64
A_KV_RANK = 256
IDX_HEADS = 4
IDX_DIM = 64
TOPK_MAX = 256
REL_BUCKETS = 32
REL_MAX_DIST = 1024
B_WIDTH = 512
SHORT_CONV = 3
SSM_HEADS = 16
SSM_HEAD_DIM = 64
SSM_INNER = SSM_HEADS * SSM_HEAD_DIM
SSM_GROUPS = 2
SSM_STATE = 128
SSM_CONV = 4
SSM_XBC = SSM_INNER + 2 * SSM_GROUPS * SSM_STATE
PLE_DIM = 256
NORM_EPS = 1e-6
IN_SPLITS = (A_HEADS * A_QK_DIM, A_KV_RANK, IDX_HEADS * IDX_DIM, IDX_DIM, IDX_HEADS,
             B_WIDTH, B_WIDTH, B_WIDTH, SSM_INNER, SSM_XBC, SSM_HEADS)
IN_OFFSETS = tuple(int(v) for v in np.concatenate([[0], np.cumsum(IN_SPLITS)]))

LANES = 128
SUBLANES = 8
VMEM_LIMIT_BYTES = 60 * 1024 * 1024

QB = 2 * LANES
QGROUPS = QB // LANES
KT = 256
SSD_L = 128
IN_SUBTILES = 4
TM_SUB = 256
ROW_SUBTILES = 2
TM_OUT = TM_SUB * ROW_SUBTILES
SMALL_W = LANES
IW_OFF = IDX_DIM
INT_MIN = np.int32(-2 ** 31)
LOG2E = math.log2(math.e)
BF16_ROWS = 2 * SUBLANES
V_ROWS = A_V_DIM + BF16_ROWS

def _const_spec(shape):
    nd = len(shape)
    return pl.BlockSpec(shape, lambda *_: (0,) * nd, pipeline_mode=pl.Buffered(1))


def _params(sem):
    return pltpu.CompilerParams(dimension_semantics=sem, vmem_limit_bytes=VMEM_LIMIT_BYTES)


def _rms(x, g):
    return x * lax.rsqrt(jnp.mean(x * x, axis=-1, keepdims=True) + NORM_EPS) * g


def _sigmoid(x):
    return 1.0 / (1.0 + jnp.exp(-x))


def _split3(v):
    hi = v.astype(BF16)
    r1 = v - hi.astype(F32)
    mid = r1.astype(BF16)
    lo = (r1 - mid.astype(F32)).astype(BF16)
    return hi, mid, lo


def _dot(a, b):
    return jnp.dot(a, b, preferred_element_type=F32)


def _t5_bucket_np(rel):
    half = REL_BUCKETS // 2
    max_exact = half // 2
    ret = np.where(rel > 0, half, 0)
    n = np.abs(rel)
    nf = np.maximum(n, 1).astype(np.float32)
    large = max_exact + (np.log(nf / np.float32(max_exact)) / np.float32(math.log(REL_MAX_DIST / max_exact))
                         * np.float32(half - max_exact)).astype(np.int32)
    large = np.minimum(large, half - 1)
    return (ret + np.where(n < max_exact, n, large)).astype(np.int32)


@functools.lru_cache(maxsize=None)
def _bucket_tiles():
    kr = np.arange(KT)[:, None]
    qc = np.arange(QB)[None, :]
    n_probe = REL_MAX_DIST // QB + 2
    tiles = [_t5_bucket_np(kr - qc - o * QB) for o in range(n_probe)]
    far_bucket = int(tiles[-1][0, 0])
    near = next(o + 1 for o in range(n_probe - 1, -1, -1) if not np.all(tiles[o] == far_bucket))
    assert 0 < near < n_probe and np.all(_t5_bucket_np(-np.arange(near * QB - KT + 1, 8 * REL_MAX_DIST)) == far_bucket)
    return np.stack(tiles[:near + 1]), near, far_bucket


def _bias_kernel(rb_ref, bk_ref, out_ref, *, far_bucket):
    bk = bk_ref[0]
    for h in range(A_HEADS):
        acc = jnp.zeros(bk.shape, F32)
        for b in range(REL_BUCKETS):
            acc = jnp.where(bk == b, (rb_ref[b, h] - rb_ref[far_bucket, h]) * LOG2E, acc)
        out_ref[0, :, h * QB:(h + 1) * QB] = acc


def _bias_tiles_call(rel_bias, bucket_tiles, far_bucket):
    n = bucket_tiles.shape[0]
    return pl.pallas_call(
        functools.partial(_bias_kernel, far_bucket=far_bucket),
        grid=(n,),
        in_specs=[pl.BlockSpec(memory_space=pltpu.SMEM),
                  pl.BlockSpec((1, KT, QB), lambda o: (o, 0, 0))],
        out_specs=pl.BlockSpec((1, KT, A_HEADS * QB), lambda o: (o, 0, 0)),
        out_shape=jax.ShapeDtypeStruct((n, KT, A_HEADS * QB), F32),
        compiler_params=_params(("arbitrary",)),
        name="rel_bias_tiles",
    )(rel_bias, bucket_tiles)


def _wprep_kernel(w_ref, wq_out, wckv_out, wiq_out, wsm_out, wb_out, wz_out, wxbc_out, wdt_out):
    o_q, o_ckv, o_iq, o_ik, _, o_bg, _, _, o_z, o_xbc, o_dt, o_end = IN_OFFSETS
    wq_out[...] = w_ref[:, o_q:o_ckv].astype(BF16)
    wckv_out[...] = w_ref[:, o_ckv:o_iq].astype(BF16)
    wiq_out[...] = w_ref[:, o_iq:o_ik].astype(BF16)
    wsm_out[...] = w_ref[:, o_ik:o_ik + SMALL_W].astype(BF16)
    tail = w_ref[:, o_bg:o_end].astype(BF16)
    wb_out[...] = tail[:, :o_z - o_bg]
    wz_out[...] = tail[:, o_z - o_bg:o_xbc - o_bg]
    wxbc_out[...] = tail[:, o_xbc - o_bg:o_dt - o_bg]
    wdt_out[...] = jnp.zeros(wdt_out.shape, BF16)
    wdt_out[:, :o_end - o_dt] = tail[:, o_dt - o_bg:]


def _wprep_call(w_all, layer):
    _, d, n = w_all.shape
    tr = KT
    widths = [IN_SPLITS[0], IN_SPLITS[1], IN_SPLITS[2], SMALL_W, 3 * B_WIDTH, SSM_INNER, SSM_XBC, SMALL_W]
    return pl.pallas_call(
        _wprep_kernel,
        grid=(d // tr,),
        in_specs=[pl.BlockSpec((None, tr, n), lambda r: (layer, r, 0))],
        out_specs=[pl.BlockSpec((tr, wd), lambda r: (r, 0)) for wd in widths],
        out_shape=[jax.ShapeDtypeStruct((d, wd), BF16) for wd in widths],
        compiler_params=_params(("parallel",)),
        name="inproj_weight_prep",
    )(w_all)


def _dot_nt(a, b):
    return lax.dot_general(a, b, (((1,), (1,)), ((), ())), preferred_element_type=F32)


def _bit_transpose32(words):
    a = list(words)
    m, j = 0x0000FFFF, 16
    while j:
        k = 0
        while k < 32:
            t = (a[k] ^ lax.shift_right_logical(a[k + j], np.int32(j))) & np.int32(m)
            a[k] = a[k] ^ t
            a[k + j] = a[k + j] ^ (t << j)
            k = (k + j + 1) & ~j
        j >>= 1
        m = (m ^ (m << j)) & 0xFFFFFFFF
    return a


def _attn_kernel(qlat_ref, iq_ref, iw_ref, ckv_ref, vt_ref, ik_ref, bias_ref, tri_ref,
                 o_ref, keys_ref, planes_ref, acc_ref, p0_ref, p1_ref, lg0_ref, lg1_ref,
                 *, k_top, near):
    i = pl.program_id(1)
    nt = ((i + 1) * QB + KT - 1) // KT
    lane = lax.broadcasted_iota(I32, (1, QB), 1)
    qchunk = (i * QB + lane) >> int(math.log2(CHUNK))
    krow = lax.broadcasted_iota(I32, (KT, QB), 0)

    iq_t = iq_ref[0]
    iw = iw_ref[0][:IDX_HEADS, :]

    def score_tile(j, carry, masked):
        r0 = pl.multiple_of(j * KT, KT)
        ikt = ik_ref[0, pl.ds(r0, KT), :]
        s = jnp.zeros((KT, QB), F32)
        for h in range(IDX_HEADS):
            raw = _dot(ikt, iq_t[h * IDX_DIM:(h + 1) * IDX_DIM, :])
            s = s + jnp.maximum(raw, 0.0) * iw[h:h + 1, :]
        if masked:
            adm = ((r0 + krow) >> int(math.log2(CHUNK))) <= qchunk
            s = jnp.where(adm, s, -jnp.inf)
        bits = lax.bitcast_convert_type(s, I32)
        bits = jnp.where(bits == INT_MIN, 0, bits)
        key = bits ^ ((bits >> 31) & np.int32(0x7FFFFFFF))
        keys_ref[pl.ds(r0, KT), :] = key
        ukey = key ^ INT_MIN
        planes = _bit_transpose32([ukey[v * SUBLANES:(v + 1) * SUBLANES, :] for v in range(32)])
        p0 = pl.multiple_of(j * SUBLANES, SUBLANES)
        for b in range(32):
            planes_ref[b, pl.ds(p0, SUBLANES), :] = planes[31 - b]
        return carry

    @pl.when(i == 0)
    def _():
        planes_ref[...] = jnp.zeros(planes_ref.shape, I32)

    def score_pair(u, carry):
        score_tile(2 * u, 0, masked=False)
        score_tile(2 * u + 1, 0, masked=False)
        return carry

    lax.fori_loop(0, (nt - 1) // 2, score_pair, 0)

    @pl.when((nt - 1) % 2 == 1)
    def _():
        score_tile(nt - 2, 0, masked=False)

    score_tile(nt - 1, 0, masked=True)

    kf = float(k_top)

    def search(n_prow):
        prow = lax.broadcasted_iota(I32, (n_prow, QB), 0)

        def search_step(k, carry):
            live, c_gt, thr_u = carry
            b = 31 - k
            x = live & planes_ref[b, 0:n_prow, :]
            pc = lax.population_count(x)
            parts = [pc[r * SUBLANES:(r + 1) * SUBLANES, :] for r in range(n_prow // SUBLANES)]
            while len(parts) > 1:
                parts = [parts[k2] + parts[k2 + 1] for k2 in range(0, len(parts) - 1, 2)] + (
                    [parts[-1]] if len(parts) % 2 else [])
            cnt = c_gt + jnp.sum(parts[0].astype(F32), axis=0, keepdims=True)
            take = cnt >= kf
            live = jnp.where(take, x, live ^ x)
            c_gt = jnp.where(take, c_gt, cnt)
            thr_u = thr_u | jnp.where(take, jnp.left_shift(jnp.int32(1), b), np.int32(0))
            return live, c_gt, thr_u

        _, c_gt, thr_u = lax.fori_loop(
            0, 32, search_step,
            (jnp.where(prow < nt * SUBLANES, np.int32(-1), np.int32(0)),
             jnp.zeros((1, QB), F32), jnp.zeros((1, QB), I32)))
        return c_gt, thr_u

    n_all = planes_ref.shape[1]
    if n_all % (4 * SUBLANES) == 0:
        c_gt, thr_u = lax.cond(
            nt * SUBLANES <= n_all // 4, lambda: search(n_all // 4),
            lambda: lax.cond(nt * SUBLANES <= n_all // 2, lambda: search(n_all // 2), lambda: search(n_all)))
    else:
        c_gt, thr_u = search(n_all)
    thr = thr_u ^ INT_MIN
    need = kf - c_gt

    p_refs, lg_refs = (p0_ref, p1_ref), (lg0_ref, lg1_ref)
    acc_ref[...] = jnp.zeros(acc_ref.shape, F32)
    p1_ref[...] = jnp.zeros(p1_ref.shape, BF16)

    def stage_scores(j, cnt_eq, slot, far=False):
        r0 = pl.multiple_of(j * KT, KT)
        kt = keys_ref[pl.ds(r0, KT), :]
        eq = kt == thr
        eqf = jnp.where(eq, 1.0, 0.0)
        rank = _dot(tri_ref[...], eqf.astype(BF16)) + cnt_eq
        sel = (kt > thr) | (eq & (rank <= need))
        if not far:
            sel = sel & (((r0 + krow) >> int(math.log2(CHUNK))) <= qchunk)
            bias = bias_ref[jnp.minimum(i - j * (KT // QB), near)]
        logits = _dot(ckv_ref[0, pl.ds(r0, KT), :], qlat_ref[0])
        tmax = [[] for _ in range(QGROUPS)]
        for h in range(A_HEADS):
            for g in range(QGROUPS):
                sl = slice(h * QB + g * LANES, h * QB + (g + 1) * LANES)
                lh = logits[:, sl] if far else logits[:, sl] + bias[:, sl]
                lh = jnp.where(sel[:, g * LANES:(g + 1) * LANES], lh, -jnp.inf)
                lg_refs[slot][:, sl] = lh
                tmax[g].append(jnp.max(lh, axis=0, keepdims=True))
        tmax = tuple(jnp.concatenate(rows, axis=0) for rows in tmax)
        return tmax, cnt_eq + jnp.sum(eqf, axis=0, keepdims=True)

    def stage_probs(ms, tmax, slot):
        new_ms, alphas = [], []
        for g in range(QGROUPS):
            m_new = jnp.maximum(ms[g], tmax[g])
            m_safe = jnp.where(m_new == -jnp.inf, 0.0, m_new)
            alphas.append(jnp.exp2(ms[g] - m_safe))
            new_ms.append(m_new)
            for h in range(A_HEADS):
                sl = slice(h * QB + g * LANES, h * QB + (g + 1) * LANES)
                p_refs[slot][:, sl] = jnp.exp2(lg_refs[slot][:, sl] - m_safe[h:h + 1, :]).astype(BF16)
        return tuple(new_ms), tuple(alphas)

    def stage_values(alphas, jp, slot):
        for h in range(A_HEADS):
            vs = slice(h * V_ROWS, (h + 1) * V_ROWS)
            pv = _dot(vt_ref[0, jp, vs, :], p_refs[slot][:, h * QB:(h + 1) * QB])
            alpha = jnp.concatenate([a[h:h + 1, :] for a in alphas], axis=1)
            acc_ref[vs, :] = acc_ref[vs, :] * alpha + pv

    def trip(t, carry, slot, far=False):
        ms, tmax, cnt_eq, alphas = carry
        stage_values(alphas, jnp.maximum(t - 1, 0), 1 - slot)
        ms, alphas = stage_probs(ms, tmax, slot)
        tmax, cnt_eq = stage_scores(t + 1, cnt_eq, 1 - slot, far)
        return ms, tmax, cnt_eq, alphas

    def finish(t, carry, slot):
        ms, tmax, _, alphas = carry
        stage_values(alphas, jnp.maximum(t - 1, 0), 1 - slot)
        _, alphas = stage_probs(ms, tmax, slot)
        stage_values(alphas, t, slot)

    def pair(u, carry, far=False):
        return trip(2 * u + 1, trip(2 * u, carry, 0, far), 1, far)

    tmax0, cnt_eq0 = stage_scores(0, jnp.zeros((1, QB), F32), 0)
    init = (tuple(jnp.full((A_HEADS, LANES), -jnp.inf, F32) for _ in range(QGROUPS)), tmax0, cnt_eq0,
            tuple(jnp.ones((A_HEADS, LANES), F32) for _ in range(QGROUPS)))
    n_trips = nt - 1
    n_far_pairs = jnp.maximum(i * (QB // KT) - near, 0) // 2
    carry = lax.fori_loop(0, n_far_pairs, functools.partial(pair, far=True), init)
    carry = lax.fori_loop(n_far_pairs, n_trips // 2, pair, carry)

    @pl.when(n_trips % 2 == 0)
    def _():
        finish(nt - 1, carry, 0)

    @pl.when(n_trips % 2 == 1)
    def _():
        finish(nt - 1, trip(nt - 2, carry, 0), 1)

    a_t = jnp.concatenate(
        [acc_ref[h * V_ROWS:h * V_ROWS + A_V_DIM, :]
         * (1.0 / acc_ref[h * V_ROWS + A_V_DIM:h * V_ROWS + A_V_DIM + 1, :]) for h in range(A_HEADS)],
        axis=0)
    o_ref[...] = a_t.T.astype(BF16)


def _attn_call(qlat, iq_t, iw_t, ckv3, vt4, ik3, bias_tiles, tri, *, k_top, near):
    b, s = ckv3.shape[0], ckv3.shape[1]
    nb = s // QB
    n_kt = s // KT
    hv = A_HEADS * A_V_DIM
    hvr = A_HEADS * V_ROWS
    kern = functools.partial(_attn_kernel, k_top=k_top, near=near)
    row = lambda w: pl.BlockSpec((QB, w), lambda bb, i: (bb * nb + i, 0))
    qblk = lambda a: pl.BlockSpec((1,) + a.shape[1:], lambda bb, i: (bb * nb + i, 0, 0))
    return pl.pallas_call(
        kern,
        grid=(b, nb),
        in_specs=[
            qblk(qlat), qblk(iq_t), qblk(iw_t),
            pl.BlockSpec((1, s, A_KV_RANK), lambda bb, i: (bb, 0, 0)),
            pl.BlockSpec((1, n_kt, hvr, KT), lambda bb, i: (bb, 0, 0, 0)),
            pl.BlockSpec((1, s, IDX_DIM), lambda bb, i: (bb, 0, 0)),
            _const_spec(bias_tiles.shape), _const_spec(tri.shape),
        ],
        out_specs=row(hv),
        out_shape=jax.ShapeDtypeStruct((b * s, hv), BF16),
        scratch_shapes=[pltpu.VMEM((s, QB), I32), pltpu.VMEM((32, s // 32, QB), I32),
                        pltpu.VMEM((hvr, QB), F32),
                        pltpu.VMEM((KT, A_HEADS * QB), BF16), pltpu.VMEM((KT, A_HEADS * QB), BF16),
                        pltpu.VMEM((KT, A_HEADS * QB), F32), pltpu.VMEM((KT, A_HEADS * QB), F32)],
        compiler_params=_params(("parallel", "arbitrary")),
        name="dsa_attention",
    )(qlat, iq_t, iw_t, ckv3, vt4, ik3, bias_tiles, tri)


def _softplus(x):
    return jnp.maximum(x, 0.0) + jnp.log1p(jnp.exp(-jnp.abs(x)))


def _ssd_chunk(xa, zz, dt_raw, dt_t_raw, state, dtb_r, dtb_c, alog_r, alog_c, dexp, nw, tril, triu, eh,
               fill=lambda: None):
    L = SSD_L
    gw = SSM_INNER // SSM_GROUPS
    xs = xa[:, :SSM_INNER]
    dt = _softplus(dt_raw + dtb_r)
    a = dt * (-jnp.exp(alog_r))
    a_cs = sum(_dot(tril, part) for part in _split3(a))
    fill()
    dt_t = _softplus(dt_t_raw + dtb_c)
    a_t = dt_t * (-jnp.exp(alog_c))
    a_cs_t = sum(_dot(part, triu) for part in _split3(a_t))

    def expand(v):
        v3 = jnp.concatenate([v, v, v], axis=1)
        hi, mid, lo = _split3(v3)
        term = lax.broadcasted_iota(I32, v3.shape, 1) // SSM_HEADS
        return _dot(jnp.where(term == 0, hi, jnp.where(term == 1, mid, lo)), eh)

    xdt = xs * expand(dt)
    fill()
    e_cs = expand(jnp.exp(a_cs))
    dec = expand(jnp.exp(a_cs[L - 1:L, :] - a_cs))
    fill()
    chunk_dec = e_cs[L - 1:L, :]

    row = lax.broadcasted_iota(I32, (L, L), 0)
    col = lax.broadcasted_iota(I32, (L, L), 1)
    causal = row >= col
    lo_half = lax.broadcasted_iota(I32, (L, LANES), 1) < SSM_HEAD_DIM
    hpg = SSM_HEADS // SSM_GROUPS

    y_parts = []
    for g in range(SSM_GROUPS):
        bg = xa[:, SSM_INNER + g * SSM_STATE:SSM_INNER + (g + 1) * SSM_STATE].astype(BF16)
        cg = xa[:, SSM_INNER + (SSM_GROUPS + g) * SSM_STATE:
                SSM_INNER + (SSM_GROUPS + g + 1) * SSM_STATE].astype(BF16)
        cb = _dot_nt(cg, bg)
        fill()
        sl = slice(g * gw, (g + 1) * gw)
        st = state[g]
        y_off = _dot(cg, st.astype(BF16)) * e_cs[:, sl]
        for k in range(hpg // 2):
            xp = xdt[:, g * gw + k * LANES:g * gw + (k + 1) * LANES]
            x2 = jnp.concatenate([jnp.where(lo_half, xp, 0.0), jnp.where(lo_half, 0.0, xp)], axis=0)
            ms = []
            for hh in range(2):
                h = g * hpg + 2 * k + hh
                seg = a_cs[:, h:h + 1] - a_cs_t[h:h + 1, :]
                ms.append((cb * jnp.exp(jnp.where(causal, seg, -jnp.inf))).astype(BF16))
            y_parts.append(y_off[:, k * LANES:(k + 1) * LANES]
                           + _dot(jnp.concatenate(ms, axis=1), x2.astype(BF16)))
            if k % 2 == 1:
                fill()
        xw = (dec[:, sl] * xdt[:, sl]).astype(BF16)
        upd = lax.dot_general(bg, xw, (((0,), (0,)), ((), ())), preferred_element_type=F32)
        state[g] = chunk_dec[:, sl] * st + upd

    y = jnp.concatenate(y_parts, axis=1) + dexp * xs
    y = y * (zz * _sigmoid(zz))
    return jnp.concatenate(
        [_rms(y[:, g * gw:(g + 1) * gw], nw[:, g * gw:(g + 1) * gw]) for g in range(SSM_GROUPS)], axis=1)


def _inproj_mixer_kernel(x_ref, g_ref, kvn_ref, ikg_ref, ikb_ref,
                         wqt_ref, wckv_ref, wiqt_ref, wsm_ref, wiwt_ref, wb_ref, wz_ref, wxbc_ref,
                         wdt_ref, wdtt_ref, wuv_ref, vone_ref, wuk_ref,
                         scw_ref, cw_ref, cb_ref, dtb_r_ref, dtb_c_ref, alog_r_ref, alog_c_ref,
                         dexp_ref, nw_ref, tril_ref, triu_ref, eh_ref,
                         ql_out, ckv_out, vt_out, iqt_out, iwt_out, ik_out, bo_out, co_out,
                         ubuf, xbuf, state, *, steps_per_seq):
    P = SUBLANES
    tm = IN_SUBTILES * KT

    @pl.when(pl.program_id(0) % steps_per_seq == 0)
    def _():
        ubuf[0:P, :] = jnp.zeros((P, B_WIDTH), F32)
        xbuf[0:P, :] = jnp.zeros((P, SSM_XBC), F32)
        state[...] = jnp.zeros(state.shape, F32)

    subs = [slice(k * KT, (k + 1) * KT) for k in range(IN_SUBTILES)]
    h = [_rms(x_ref[rows, :], g_ref[...]).astype(BF16) for rows in subs]
    ssd_consts = (dtb_r_ref[...], dtb_c_ref[...], alog_r_ref[...], alog_c_ref[...], dexp_ref[...],
                  nw_ref[...], tril_ref[...], triu_ref[...], eh_ref[...])

    def proj_mixer(k):
        return dict(bm=_dot(h[k], wb_ref[...]),
                    xr=_dot(h[k], wxbc_ref[...]), zz=_dot(h[k], wz_ref[...]),
                    dt=_dot(h[k], wdt_ref[...])[:, :SSM_HEADS],
                    dt_t=_dot_nt(wdtt_ref[...], h[k]))

    def short_conv(k, bm):
        base = P + k * KT
        u = bm[:, B_WIDTH:2 * B_WIDTH] * bm[:, 2 * B_WIDTH:3 * B_WIDTH]
        ubuf[base:base + KT, :] = u
        conv = scw_ref[SHORT_CONV - 1:SHORT_CONV, :] * u
        for j in range(SHORT_CONV - 1):
            off = base - (SHORT_CONV - 1) + j
            conv = conv + scw_ref[j:j + 1, :] * ubuf[off:off + KT, :]
        bo_out[subs[k], :] = (bm[:, :B_WIDTH] * conv).astype(BF16)

    def ssm_conv(k, xr):
        base = P + k * KT
        xbuf[base:base + KT, :] = xr
        xc = cw_ref[SSM_CONV - 1:SSM_CONV, :] * xr + cb_ref[...]
        for j in range(SSM_CONV - 1):
            off = base - (SSM_CONV - 1) + j
            xc = xc + cw_ref[j:j + 1, :] * xbuf[off:off + KT, :]
        return xc * _sigmoid(xc)

    def scan_chunk(k, c, xa, pm, fill):
        cs = slice(c * SSD_L, (c + 1) * SSD_L)
        y = _ssd_chunk(xa[cs, :], pm["zz"][cs, :], pm["dt"][cs, :], pm["dt_t"][:, cs], state, *ssd_consts,
                       fill=fill)
        r0 = k * KT + c * SSD_L
        co_out[r0:r0 + SSD_L, :] = y.astype(BF16)

    def attn_keys(k):
        ckv = _rms(_dot(h[k], wckv_ref[...]), kvn_ref[...]).astype(BF16)
        ckv_out[subs[k], :] = ckv
        ik = _dot(h[k], wsm_ref[...])[:, :IDX_DIM]
        mu = jnp.mean(ik, axis=-1, keepdims=True)
        var = jnp.mean(jnp.square(ik - mu), axis=-1, keepdims=True)
        ik_out[subs[k], :] = ((ik - mu) * lax.rsqrt(var + NORM_EPS) * ikg_ref[...] + ikb_ref[...]).astype(BF16)
        return ckv

    def attn_queries(k):
        q_t = _dot_nt(wqt_ref[...], h[k]).astype(BF16)
        for hh in range(A_HEADS):
            ql = _dot(wuk_ref[hh], q_t[hh * A_QK_DIM:(hh + 1) * A_QK_DIM, :])
            ql_out[k, :, hh * QB:(hh + 1) * QB] = (ql * (A_QK_DIM ** -0.5 * LOG2E)).astype(BF16)
        iqt_out[k] = (_dot_nt(wiqt_ref[...], h[k]) * (IDX_DIM ** -0.5)).astype(BF16)
        iwt_out[k] = _dot_nt(wiwt_ref[...], h[k]) * (IDX_HEADS ** -0.5)

    def attn_values(k, ckv):
        vt_out[0, k] = (_dot_nt(wuv_ref[...], ckv) + vone_ref[...]).astype(BF16)

    blk = KT
    tasks = []

    def fill():
        if tasks:
            tasks.pop(0)()

    def drain():
        while tasks:
            fill()

    def proj_tasks(k, got):
        for name, w_ref in (("bm", wb_ref), ("xr", wxbc_ref), ("zz", wz_ref)):
            for c0 in range(0, w_ref.shape[1], blk):
                tasks.append(lambda name=name, w_ref=w_ref, c0=c0: got.setdefault(name, []).append(
                    _dot(h[k], w_ref[:, c0:c0 + blk])))
        tasks.append(lambda: got.update(dt=_dot(h[k], wdt_ref[...])[:, :SSM_HEADS],
                                        dt_t=_dot_nt(wdtt_ref[...], h[k])))

    pm = proj_mixer(0)
    ckvs = {}
    for k in range(IN_SUBTILES):
        got = {}
        if k + 1 < IN_SUBTILES:
            proj_tasks(k + 1, got)
        else:
            for kk in range(IN_SUBTILES):
                tasks.extend([lambda kk=kk: ckvs.update({kk: attn_keys(kk)}), lambda kk=kk: attn_queries(kk),
                              lambda kk=kk: attn_values(kk, ckvs[kk])])
        short_conv(k, pm["bm"])
        fill()
        xa = ssm_conv(k, pm["xr"])
        fill()
        for c in range(KT // SSD_L):
            scan_chunk(k, c, xa, pm, fill)
        drain()
        if k + 1 < IN_SUBTILES:
            pm = dict(bm=jnp.concatenate(got["bm"], axis=1), xr=jnp.concatenate(got["xr"], axis=1),
                      zz=jnp.concatenate(got["zz"], axis=1), dt=got["dt"], dt_t=got["dt_t"])

    ubuf[0:P, :] = ubuf[tm:tm + P, :]
    xbuf[0:P, :] = xbuf[tm:tm + P, :]


def _inproj_mixer_call(x2, norm_consts, proj_weights, wuv, vone, wuk, mixer_params, n_kt):
    t = x2.shape[0]
    assert KT == QB and n_kt % IN_SUBTILES == 0 and KT % SSD_L == 0
    tm = IN_SUBTILES * KT
    steps_per_seq = n_kt // IN_SUBTILES
    scw, cw, cb, dtb, alog, dexp, nw = mixer_params
    L = SSD_L
    tril = jnp.asarray(np.tril(np.ones((L, L), np.float32)), BF16)
    triu = jnp.asarray(np.triu(np.ones((L, L), np.float32)), BF16)
    eh = jnp.asarray(np.tile(np.repeat(np.eye(SSM_HEADS, dtype=np.float32), SSM_HEAD_DIM, axis=1), (3, 1)), BF16)
    mixer_consts = [scw, cw, cb.reshape(1, -1), dtb.reshape(1, -1), dtb.reshape(-1, 1),
                    alog.reshape(1, -1), alog.reshape(-1, 1), dexp.reshape(1, -1), nw.reshape(1, -1),
                    tril, triu, eh]
    wqt, wckv, wiqt, wiwt = proj_weights[0], proj_weights[1], proj_weights[2], proj_weights[4]
    row = lambda w: pl.BlockSpec((tm, w), lambda r: (r, 0))
    rows = lambda w, dt: (row(w), jax.ShapeDtypeStruct((t, w), dt))
    qblk = lambda w, dt: (pl.BlockSpec((IN_SUBTILES, w, QB), lambda r: (r, 0, 0)),
                          jax.ShapeDtypeStruct((t // QB, w, QB), dt))
    hv = wuv.shape[0]
    outs = [(pl.BlockSpec((IN_SUBTILES, A_KV_RANK, A_HEADS * QB), lambda r: (r, 0, 0)),
             jax.ShapeDtypeStruct((t // QB, A_KV_RANK, A_HEADS * QB), BF16)),
            rows(wckv.shape[1], BF16),
            (pl.BlockSpec((1, IN_SUBTILES, hv, KT), lambda r: (r // steps_per_seq, r % steps_per_seq, 0, 0)),
             jax.ShapeDtypeStruct((t // (n_kt * KT), n_kt, hv, KT), BF16)),
            qblk(wiqt.shape[0], BF16), qblk(wiwt.shape[0], F32), rows(IDX_DIM, BF16),
            rows(B_WIDTH, BF16), rows(SSM_INNER, BF16)]
    consts = [*norm_consts, *proj_weights, wuv, vone, wuk, *mixer_consts]
    return pl.pallas_call(
        functools.partial(_inproj_mixer_kernel, steps_per_seq=steps_per_seq),
        grid=(t // tm,),
        in_specs=[row(D_MODEL)] + [_const_spec(c.shape) for c in consts],
        out_specs=[spec for spec, _ in outs],
        out_shape=[shape for _, shape in outs],
        scratch_shapes=[pltpu.VMEM((tm + SUBLANES, B_WIDTH), F32),
                        pltpu.VMEM((tm + SUBLANES, SSM_XBC), F32),
                        pltpu.VMEM((SSM_GROUPS, SSM_STATE, SSM_INNER // SSM_GROUPS), F32)],
        compiler_params=_params(("arbitrary",)),
        name="inproj_conv_ssd",
    )(x2, *consts)


def _dense_kernel(x_ref, a_ref, b_ref, c_ref, p_ref, g_post_ref, g_pre_ref, g_fpost_ref,
                  woa_ref, wob_ref, woc_ref, wg_ref, wu_ref, wd_ref, wpg_ref, wpp_ref, o_ref):
    subs = [slice(r * TM_SUB, (r + 1) * TM_SUB) for r in range(ROW_SUBTILES)]
    mix = [_dot(a_ref[rows, :], woa_ref[...]) + _dot(b_ref[rows, :], wob_ref[...])
           + _dot(c_ref[rows, :], woc_ref[...]) for rows in subs]
    ple = [_dot(p_ref[rows, :].astype(BF16), wpp_ref[...]) for rows in subs]
    x = [x_ref[rows, :] + _rms(m, g_post_ref[...]) for rows, m in zip(subs, mix)]
    h = [_rms(xx, g_pre_ref[...]).astype(BF16) for xx in x]
    gate = [_dot(hh, wg_ref[...]) for hh in h]
    up = [_dot(hh, wu_ref[...]) for hh in h]
    act = [(g * _sigmoid(g) * u).astype(BF16) for g, u in zip(gate, up)]
    f = [_dot(aa, wd_ref[...]) for aa in act]
    x = [xx + _rms(ff, g_fpost_ref[...]) for xx, ff in zip(x, f)]
    pg = [_sigmoid(_dot(xx.astype(BF16), wpg_ref[...])) for xx in x]
    for rows, xx, g, e in zip(subs, x, pg, ple):
        o_ref[rows, :] = xx + g * e


def _dense_call(x2, a, b, c, p_all, layer, g_post, g_pre, g_fpost, woa, wob, woc, wg, wu, wd, wpg, wpp):
    t = x2.shape[0]
    tm = TM_OUT
    row = lambda w: pl.BlockSpec((tm, w), lambda r: (r, 0))
    steps_per_seq = p_all.shape[2] // tm
    p_spec = pl.BlockSpec((None, None, tm, PLE_DIM),
                          lambda r: (layer, r // steps_per_seq, r % steps_per_seq, 0))
    consts = [g_post, g_pre, g_fpost, woa, wob, woc, wg, wu, wd, wpg, wpp]
    return pl.pallas_call(
        _dense_kernel,
        grid=(t // tm,),
        in_specs=[row(D_MODEL), row(a.shape[1]), row(b.shape[1]), row(c.shape[1]), p_spec]
                 + [_const_spec(w.shape) for w in consts],
        out_specs=row(D_MODEL),
        out_shape=jax.ShapeDtypeStruct((t, D_MODEL), F32),
        compiler_params=_params(("parallel",)),
        name="outproj_ffn_ple",
    )(x2, a, b, c, p_all, *consts)


def kernel(x, p, pre_mix_norm, post_mix_norm, pre_ffn_norm, post_ffn_norm, w_in, kv_norm, idx_k_norm_g, idx_k_norm_b, w_uk, w_uv, rel_bias, short_conv_w, ssm_conv_w, ssm_conv_b, ssm_dt_bias, ssm_a_log, ssm_d, ssm_norm, w_out, w_ffn_gate, w_ffn_up, w_ffn_down, w_ple_proj, w_ple_gate):
    bsz, s, d = x.shape
    assert d == D_MODEL and s % (IN_SUBTILES * KT) == 0 and s % QB == 0 and s % SSD_L == 0
    assert (bsz * s) % TM_OUT == 0
    t = bsz * s
    k_top = min(TOPK_MAX, s // 4)

    bucket_np, near, far_bucket = _bucket_tiles()
    bias_tiles = _bias_tiles_call(rel_bias.astype(F32), jnp.asarray(bucket_np), far_bucket)
    tri = jnp.asarray(np.tril(np.ones((KT, KT), np.float32)), BF16)
    row1 = lambda v: v.reshape(1, -1).astype(F32)
    vone_np = np.zeros((A_HEADS, V_ROWS, 1), np.float32)
    vone_np[:, A_V_DIM] = 1.0
    vone = jnp.asarray(vone_np.reshape(A_HEADS * V_ROWS, 1))

    x2 = x.reshape(t, d)
    p_all = p.astype(F32)
    w_in_all = w_in.astype(F32)
    for i in range(DEPTH):
        wparts = _wprep_call(w_in_all, i)
        wuv_t = jnp.transpose(w_uv[i], (1, 2, 0)).astype(BF16)
        wuv_t = jnp.pad(wuv_t, ((0, 0), (0, V_ROWS - A_V_DIM), (0, 0))).reshape(A_HEADS * V_ROWS, A_KV_RANK)
        wq, wckv, wiq, wsm, wb, wz, wxbc, wdt = wparts
        wiw_t = wsm[:, IW_OFF:IW_OFF + SUBLANES].T
        wdt_t = wdt[:, :SSM_HEADS].T
        wuk = jnp.transpose(w_uk[i], (1, 0, 2)).astype(BF16)
        qlat, ckv, vt4, iq_t, iw_t, ik, b_out, c_out = _inproj_mixer_call(
            x2, (row1(pre_mix_norm[i]), row1(kv_norm[i]), row1(idx_k_norm_g[i]), row1(idx_k_norm_b[i])),
            (wq.T, wckv, wiq.T, wsm, wiw_t, wb, wz, wxbc, wdt, wdt_t), wuv_t, vone, wuk,
            (short_conv_w[i].astype(F32), ssm_conv_w[i].astype(F32), ssm_conv_b[i].astype(F32),
             ssm_dt_bias[i].astype(F32), ssm_a_log[i].astype(F32),
             jnp.repeat(ssm_d[i].astype(F32), SSM_HEAD_DIM), ssm_norm[i].astype(F32)), s // KT)

        a_out = _attn_call(
            qlat, iq_t, iw_t, ckv.reshape(bsz, s, A_KV_RANK), vt4, ik.reshape(bsz, s, IDX_DIM),
            bias_tiles, tri, k_top=k_top, near=near)

        wo = w_out[i].astype(BF16)
        na, nbw = A_HEADS * A_V_DIM, A_HEADS * A_V_DIM + B_WIDTH
        x2 = _dense_call(
            x2, a_out, b_out, c_out, p_all, i,
            row1(post_mix_norm[i]), row1(pre_ffn_norm[i]), row1(post_ffn_norm[i]),
            wo[:na], wo[na:nbw], wo[nbw:], w_ffn_gate[i].astype(BF16), w_ffn_up[i].astype(BF16),
            w_ffn_down[i].astype(BF16), w_ple_gate[i].astype(BF16), w_ple_proj[i].astype(BF16))
    return x2.reshape(bsz, s, d)
```

```python
import functools
import math

import numpy as np
import jax
import jax.numpy as jnp
from jax import lax
from jax.experimental import pallas as pl
from jax.experimental.pallas import tpu as pltpu

F32, BF16, I32 = jnp.float32, jnp.bfloat16, jnp.int32

D_MODEL = 1024
DEPTH = 2
CHUNK = 64
A_HEADS = 8
A_QK_DIM = 64
A_V_DIM = 64
A_KV_RANK = 256
IDX_HEADS = 4
IDX_DIM = 64
TOPK_MAX = 256
REL_BUCKETS = 32
REL_MAX_DIST = 1024
B_WIDTH = 512
SHORT_CONV = 3
SSM_HEADS = 16
SSM_HEAD_DIM = 64
SSM_INNER = SSM_HEADS * SSM_HEAD_DIM
SSM_GROUPS = 2
SSM_STATE = 128
SSM_CONV = 4
SSM_XBC = SSM_INNER + 2 * SSM_GROUPS * SSM_STATE
PLE_DIM = 256
NORM_EPS = 1e-6
IN_SPLITS = (A_HEADS * A_QK_DIM, A_KV_RANK, IDX_HEADS * IDX_DIM, IDX_DIM, IDX_HEADS,
             B_WIDTH, B_WIDTH, B_WIDTH, SSM_INNER, SSM_XBC, SSM_HEADS)
IN_OFFSETS = tuple(int(v) for v in np.concatenate([[0], np.cumsum(IN_SPLITS)]))

LANES = 128
SUBLANES = 8
VMEM_LIMIT_BYTES = 56 * 1024 * 1024

QB = 2 * LANES
QGROUPS = QB // LANES
KT = 256
SSD_L = 128
IN_SUBTILES = 4
TM_SUB = 256
ROW_SUBTILES = 2
TM_OUT = TM_SUB * ROW_SUBTILES
SMALL_W = LANES
IW_OFF = IDX_DIM
INT_MIN = np.int32(-2 ** 31)
LOG2E = math.log2(math.e)
BF16_ROWS = 2 * SUBLANES
V_ROWS = A_V_DIM + BF16_ROWS

def _const_spec(shape):
    nd = len(shape)
    return pl.BlockSpec(shape, lambda *_: (0,) * nd, pipeline_mode=pl.Buffered(1))


def _params(sem):
    return pltpu.CompilerParams(dimension_semantics=sem, vmem_limit_bytes=VMEM_LIMIT_BYTES)


def _rms(x, g):
    return x * lax.rsqrt(jnp.mean(x * x, axis=-1, keepdims=True) + NORM_EPS) * g


def _sigmoid(x):
    return 1.0 / (1.0 + jnp.exp(-x))


def _split3(v):
    hi = v.astype(BF16)
    r1 = v - hi.astype(F32)
    mid = r1.astype(BF16)
    lo = (r1 - mid.astype(F32)).astype(BF16)
    return hi, mid, lo


def _dot(a, b):
    return jnp.dot(a, b, preferred_element_type=F32)


def _t5_bucket_np(rel):
    half = REL_BUCKETS // 2
    max_exact = half // 2
    ret = np.where(rel > 0, half, 0)
    n = np.abs(rel)
    nf = np.maximum(n, 1).astype(np.float32)
    large = max_exact + (np.log(nf / np.float32(max_exact)) / np.float32(math.log(REL_MAX_DIST / max_exact))
                         * np.float32(half - max_exact)).astype(np.int32)
    large = np.minimum(large, half - 1)
    return (ret + np.where(n < max_exact, n, large)).astype(np.int32)


@functools.lru_cache(maxsize=None)
def _bucket_tiles():
    kr = np.arange(KT)[:, None]
    qc = np.arange(QB)[None, :]
    n_probe = REL_MAX_DIST // QB + 2
    tiles = [_t5_bucket_np(kr - qc - o * QB) for o in range(n_probe)]
    far_bucket = int(tiles[-1][0, 0])
    near = next(o + 1 for o in range(n_probe - 1, -1, -1) if not np.all(tiles[o] == far_bucket))
    assert 0 < near < n_probe and np.all(_t5_bucket_np(-np.arange(near * QB - KT + 1, 8 * REL_MAX_DIST)) == far_bucket)
    return np.stack(tiles[:near + 1]), near, far_bucket


def _bias_kernel(rb_ref, bk_ref, out_ref, *, far_bucket):
    bk = bk_ref[0]
    for h in range(A_HEADS):
        acc = jnp.zeros(bk.shape, F32)
        for b in range(REL_BUCKETS):
            acc = jnp.where(bk == b, (rb_ref[b, h] - rb_ref[far_bucket, h]) * LOG2E, acc)
        out_ref[0, :, h * QB:(h + 1) * QB] = acc


def _bias_tiles_call(rel_bias, bucket_tiles, far_bucket):
    n = bucket_tiles.shape[0]
    return pl.pallas_call(
        functools.partial(_bias_kernel, far_bucket=far_bucket),
        grid=(n,),
        in_specs=[pl.BlockSpec(memory_space=pltpu.SMEM),
                  pl.BlockSpec((1, KT, QB), lambda o: (o, 0, 0))],
        out_specs=pl.BlockSpec((1, KT, A_HEADS * QB), lambda o: (o, 0, 0)),
        out_shape=jax.ShapeDtypeStruct((n, KT, A_HEADS * QB), F32),
        compiler_params=_params(("arbitrary",)),
        name="rel_bias_tiles",
    )(rel_bias, bucket_tiles)


def _wprep_kernel(w_ref, wq_out, wckv_out, wiq_out, wsm_out, wb_out, wz_out, wxbc_out, wdt_out):
    o_q, o_ckv, o_iq, o_ik, _, o_bg, _, _, o_z, o_xbc, o_dt, o_end = IN_OFFSETS
    wq_out[...] = w_ref[:, o_q:o_ckv].astype(BF16)
    wckv_out[...] = w_ref[:, o_ckv:o_iq].astype(BF16)
    wiq_out[...] = w_ref[:, o_iq:o_ik].astype(BF16)
    wsm_out[...] = w_ref[:, o_ik:o_ik + SMALL_W].astype(BF16)
    tail = w_ref[:, o_bg:o_end].astype(BF16)
    wb_out[...] = tail[:, :o_z - o_bg]
    wz_out[...] = tail[:, o_z - o_bg:o_xbc - o_bg]
    wxbc_out[...] = tail[:, o_xbc - o_bg:o_dt - o_bg]
    wdt_out[...] = jnp.zeros(wdt_out.shape, BF16)
    wdt_out[:, :o_end - o_dt] = tail[:, o_dt - o_bg:]


def _wprep_call(w_all, layer):
    _, d, n = w_all.shape
    tr = KT
    widths = [IN_SPLITS[0], IN_SPLITS[1], IN_SPLITS[2], SMALL_W, 3 * B_WIDTH, SSM_INNER, SSM_XBC, SMALL_W]
    return pl.pallas_call(
        _wprep_kernel,
        grid=(d // tr,),
        in_specs=[pl.BlockSpec((None, tr, n), lambda r: (layer, r, 0))],
        out_specs=[pl.BlockSpec((tr, wd), lambda r: (r, 0)) for wd in widths],
        out_shape=[jax.ShapeDtypeStruct((d, wd), BF16) for wd in widths],
        compiler_params=_params(("parallel",)),
        name="inproj_weight_prep",
    )(w_all)


def _dot_nt(a, b):
    return lax.dot_general(a, b, (((1,), (1,)), ((), ())), preferred_element_type=F32)


def _bit_transpose32(words):
    a = list(words)
    m, j = 0x0000FFFF, 16
    while j:
        k = 0
        while k < 32:
            t = (a[k] ^ lax.shift_right_logical(a[k + j], np.int32(j))) & np.int32(m)
            a[k] = a[k] ^ t
            a[k + j] = a[k + j] ^ (t << j)
            k = (k + j + 1) & ~j
        j >>= 1
        m = (m ^ (m << j)) & 0xFFFFFFFF
    return a


def _attn_kernel(q_ref, iq_ref, iw_ref, ckv_ref, vt_ref, ik_ref, wuk_ref, bias_ref, tri_ref,
                 o_ref, keys_ref, planes_ref, qlat_ref, acc_ref, p0_ref, p1_ref, lg0_ref, lg1_ref,
                 *, k_top, near):
    i = pl.program_id(1)
    nt = ((i + 1) * QB + KT - 1) // KT
    lane = lax.broadcasted_iota(I32, (1, QB), 1)
    qchunk = (i * QB + lane) >> int(math.log2(CHUNK))
    krow = lax.broadcasted_iota(I32, (KT, QB), 0)

    iq_t = iq_ref[0]
    iw = iw_ref[0][:IDX_HEADS, :]

    def score_tile(j, carry, masked):
        r0 = pl.multiple_of(j * KT, KT)
        ikt = ik_ref[0, pl.ds(r0, KT), :]
        s = jnp.zeros((KT, QB), F32)
        for h in range(IDX_HEADS):
            raw = _dot(ikt, iq_t[h * IDX_DIM:(h + 1) * IDX_DIM, :])
            s = s + jnp.maximum(raw, 0.0) * iw[h:h + 1, :]
        if masked:
            adm = ((r0 + krow) >> int(math.log2(CHUNK))) <= qchunk
            s = jnp.where(adm, s, -jnp.inf)
        bits = lax.bitcast_convert_type(s, I32)
        bits = jnp.where(bits == INT_MIN, 0, bits)
        key = bits ^ ((bits >> 31) & np.int32(0x7FFFFFFF))
        keys_ref[pl.ds(r0, KT), :] = key
        ukey = key ^ INT_MIN
        planes = _bit_transpose32([ukey[v * SUBLANES:(v + 1) * SUBLANES, :] for v in range(32)])
        p0 = pl.multiple_of(j * SUBLANES, SUBLANES)
        for b in range(32):
            planes_ref[b, pl.ds(p0, SUBLANES), :] = planes[31 - b]
        return carry

    @pl.when(i == 0)
    def _():
        planes_ref[...] = jnp.zeros(planes_ref.shape, I32)

    def score_pair(u, carry):
        score_tile(2 * u, 0, masked=False)
        score_tile(2 * u + 1, 0, masked=False)
        return carry

    lax.fori_loop(0, (nt - 1) // 2, score_pair, 0)

    @pl.when((nt - 1) % 2 == 1)
    def _():
        score_tile(nt - 2, 0, masked=False)

    q_t = q_ref[0]
    for h in range(A_HEADS):
        ql = _dot(wuk_ref[h], q_t[h * A_QK_DIM:(h + 1) * A_QK_DIM, :])
        qlat_ref[:, h * QB:(h + 1) * QB] = (ql * (A_QK_DIM ** -0.5 * LOG2E)).astype(BF16)

    score_tile(nt - 1, 0, masked=True)

    kf = float(k_top)

    def search(n_prow):
        prow = lax.broadcasted_iota(I32, (n_prow, QB), 0)

        def search_step(k, carry):
            live, c_gt, thr_u = carry
            b = 31 - k
            x = live & planes_ref[b, 0:n_prow, :]
            pc = lax.population_count(x)
            parts = [pc[r * SUBLANES:(r + 1) * SUBLANES, :] for r in range(n_prow // SUBLANES)]
            while len(parts) > 1:
                parts = [parts[k2] + parts[k2 + 1] for k2 in range(0, len(parts) - 1, 2)] + (
                    [parts[-1]] if len(parts) % 2 else [])
            cnt = c_gt + jnp.sum(parts[0].astype(F32), axis=0, keepdims=True)
            take = cnt >= kf
            live = jnp.where(take, x, live ^ x)
            c_gt = jnp.where(take, c_gt, cnt)
            thr_u = thr_u | jnp.where(take, jnp.left_shift(jnp.int32(1), b), np.int32(0))
            return live, c_gt, thr_u

        _, c_gt, thr_u = lax.fori_loop(
            0, 32, search_step,
            (jnp.where(prow < nt * SUBLANES, np.int32(-1), np.int32(0)),
             jnp.zeros((1, QB), F32), jnp.zeros((1, QB), I32)))
        return c_gt, thr_u

    n_all = planes_ref.shape[1]
    if n_all % (4 * SUBLANES) == 0:
        c_gt, thr_u = lax.cond(
            nt * SUBLANES <= n_all // 4, lambda: search(n_all // 4),
            lambda: lax.cond(nt * SUBLANES <= n_all // 2, lambda: search(n_all // 2), lambda: search(n_all)))
    else:
        c_gt, thr_u = search(n_all)
    thr = thr_u ^ INT_MIN
    need = kf - c_gt

    p_refs, lg_refs = (p0_ref, p1_ref), (lg0_ref, lg1_ref)
    acc_ref[...] = jnp.zeros(acc_ref.shape, F32)
    p1_ref[...] = jnp.zeros(p1_ref.shape, BF16)

    def stage_scores(j, cnt_eq, slot, far=False):
        r0 = pl.multiple_of(j * KT, KT)
        kt = keys_ref[pl.ds(r0, KT), :]
        eq = kt == thr
        eqf = jnp.where(eq, 1.0, 0.0)
        rank = _dot(tri_ref[...], eqf.astype(BF16)) + cnt_eq
        sel = (kt > thr) | (eq & (rank <= need))
        if not far:
            sel = sel & (((r0 + krow) >> int(math.log2(CHUNK))) <= qchunk)
            bias = bias_ref[jnp.minimum(i - j * (KT // QB), near)]
        logits = _dot(ckv_ref[0, pl.ds(r0, KT), :], qlat_ref[...])
        tmax = [[] for _ in range(QGROUPS)]
        for h in range(A_HEADS):
            for g in range(QGROUPS):
                sl = slice(h * QB + g * LANES, h * QB + (g + 1) * LANES)
                lh = logits[:, sl] if far else logits[:, sl] + bias[:, sl]
                lh = jnp.where(sel[:, g * LANES:(g + 1) * LANES], lh, -jnp.inf)
                lg_refs[slot][:, sl] = lh
                tmax[g].append(jnp.max(lh, axis=0, keepdims=True))
        tmax = tuple(jnp.concatenate(rows, axis=0) for rows in tmax)
        return tmax, cnt_eq + jnp.sum(eqf, axis=0, keepdims=True)

    def stage_probs(ms, tmax, slot):
        new_ms, alphas = [], []
        for g in range(QGROUPS):
            m_new = jnp.maximum(ms[g], tmax[g])
            m_safe = jnp.where(m_new == -jnp.inf, 0.0, m_new)
            alphas.append(jnp.exp2(ms[g] - m_safe))
            new_ms.append(m_new)
            for h in range(A_HEADS):
                sl = slice(h * QB + g * LANES, h * QB + (g + 1) * LANES)
                p_refs[slot][:, sl] = jnp.exp2(lg_refs[slot][:, sl] - m_safe[h:h + 1, :]).astype(BF16)
        return tuple(new_ms), tuple(alphas)

    def stage_values(alphas, jp, slot):
        for h in range(A_HEADS):
            vs = slice(h * V_ROWS, (h + 1) * V_ROWS)
            pv = _dot(vt_ref[0, jp, vs, :], p_refs[slot][:, h * QB:(h + 1) * QB])
            alpha = jnp.concatenate([a[h:h + 1, :] for a in alphas], axis=1)
            acc_ref[vs, :] = acc_ref[vs, :] * alpha + pv

    def trip(t, carry, slot, far=False):
        ms, tmax, cnt_eq, alphas = carry
        stage_values(alphas, jnp.maximum(t - 1, 0), 1 - slot)
        ms, alphas = stage_probs(ms, tmax, slot)
        tmax, cnt_eq = stage_scores(t + 1, cnt_eq, 1 - slot, far)
        return ms, tmax, cnt_eq, alphas

    def finish(t, carry, slot):
        ms, tmax, _, alphas = carry
        stage_values(alphas, jnp.maximum(t - 1, 0), 1 - slot)
        _, alphas = stage_probs(ms, tmax, slot)
        stage_values(alphas, t, slot)

    def pair(u, carry, far=False):
        return trip(2 * u + 1, trip(2 * u, carry, 0, far), 1, far)

    tmax0, cnt_eq0 = stage_scores(0, jnp.zeros((1, QB), F32), 0)
    init = (tuple(jnp.full((A_HEADS, LANES), -jnp.inf, F32) for _ in range(QGROUPS)), tmax0, cnt_eq0,
            tuple(jnp.ones((A_HEADS, LANES), F32) for _ in range(QGROUPS)))
    n_trips = nt - 1
    n_far_pairs = jnp.maximum(i * (QB // KT) - near, 0) // 2
    carry = lax.fori_loop(0, n_far_pairs, functools.partial(pair, far=True), init)
    carry = lax.fori_loop(n_far_pairs, n_trips // 2, pair, carry)

    @pl.when(n_trips % 2 == 0)
    def _():
        finish(nt - 1, carry, 0)

    @pl.when(n_trips % 2 == 1)
    def _():
        finish(nt - 1, trip(nt - 2, carry, 0), 1)

    a_t = jnp.concatenate(
        [acc_ref[h * V_ROWS:h * V_ROWS + A_V_DIM, :]
         * (1.0 / acc_ref[h * V_ROWS + A_V_DIM:h * V_ROWS + A_V_DIM + 1, :]) for h in range(A_HEADS)],
        axis=0)
    o_ref[...] = a_t.T.astype(BF16)


def _attn_call(q_t, iq_t, iw_t, ckv3, vt4, ik3, wuk, bias_tiles, tri, *, k_top, near):
    b, s = ckv3.shape[0], ckv3.shape[1]
    nb = s // QB
    n_kt = s // KT
    hv = A_HEADS * A_V_DIM
    hvr = A_HEADS * V_ROWS
    kern = functools.partial(_attn_kernel, k_top=k_top, near=near)
    row = lambda w: pl.BlockSpec((QB, w), lambda bb, i: (bb * nb + i, 0))
    qblk = lambda a: pl.BlockSpec((1, a.shape[1], QB), lambda bb, i: (bb * nb + i, 0, 0))
    return pl.pallas_call(
        kern,
        grid=(b, nb),
        in_specs=[
            qblk(q_t), qblk(iq_t), qblk(iw_t),
            pl.BlockSpec((1, s, A_KV_RANK), lambda bb, i: (bb, 0, 0)),
            pl.BlockSpec((1, n_kt, hvr, KT), lambda bb, i: (bb, 0, 0, 0)),
            pl.BlockSpec((1, s, IDX_DIM), lambda bb, i: (bb, 0, 0)),
            _const_spec(wuk.shape), _const_spec(bias_tiles.shape), _const_spec(tri.shape),
        ],
        out_specs=row(hv),
        out_shape=jax.ShapeDtypeStruct((b * s, hv), BF16),
        scratch_shapes=[pltpu.VMEM((s, QB), I32), pltpu.VMEM((32, s // 32, QB), I32),
                        pltpu.VMEM((A_KV_RANK, A_HEADS * QB), BF16),
                        pltpu.VMEM((hvr, QB), F32),
                        pltpu.VMEM((KT, A_HEADS * QB), BF16), pltpu.VMEM((KT, A_HEADS * QB), BF16),
                        pltpu.VMEM((KT, A_HEADS * QB), F32), pltpu.VMEM((KT, A_HEADS * QB), F32)],
        compiler_params=_params(("parallel", "arbitrary")),
        name="dsa_attention",
    )(q_t, iq_t, iw_t, ckv3, vt4, ik3, wuk, bias_tiles, tri)


def _softplus(x):
    return jnp.maximum(x, 0.0) + jnp.log1p(jnp.exp(-jnp.abs(x)))


def _ssd_chunk(xa, zz, dt_raw, dt_t_raw, state, dtb_r, dtb_c, alog_r, alog_c, dexp, nw, tril, triu, eh,
               fill=lambda: None):
    L = SSD_L
    gw = SSM_INNER // SSM_GROUPS
    xs = xa[:, :SSM_INNER]
    dt = _softplus(dt_raw + dtb_r)
    a = dt * (-jnp.exp(alog_r))
    a_cs = sum(_dot(tril, part) for part in _split3(a))
    fill()
    dt_t = _softplus(dt_t_raw + dtb_c)
    a_t = dt_t * (-jnp.exp(alog_c))
    a_cs_t = sum(_dot(part, triu) for part in _split3(a_t))

    def expand(v):
        v3 = jnp.concatenate([v, v, v], axis=1)
        hi, mid, lo = _split3(v3)
        term = lax.broadcasted_iota(I32, v3.shape, 1) // SSM_HEADS
        return _dot(jnp.where(term == 0, hi, jnp.where(term == 1, mid, lo)), eh)

    xdt = xs * expand(dt)
    fill()
    e_cs = expand(jnp.exp(a_cs))
    dec = expand(jnp.exp(a_cs[L - 1:L, :] - a_cs))
    fill()
    chunk_dec = e_cs[L - 1:L, :]

    row = lax.broadcasted_iota(I32, (L, L), 0)
    col = lax.broadcasted_iota(I32, (L, L), 1)
    causal = row >= col
    lo_half = lax.broadcasted_iota(I32, (L, LANES), 1) < SSM_HEAD_DIM
    hpg = SSM_HEADS // SSM_GROUPS

    y_parts = []
    for g in range(SSM_GROUPS):
        bg = xa[:, SSM_INNER + g * SSM_STATE:SSM_INNER + (g + 1) * SSM_STATE].astype(BF16)
        cg = xa[:, SSM_INNER + (SSM_GROUPS + g) * SSM_STATE:
                SSM_INNER + (SSM_GROUPS + g + 1) * SSM_STATE].astype(BF16)
        cb = _dot_nt(cg, bg)
        fill()
        sl = slice(g * gw, (g + 1) * gw)
        st = state[g]
        y_off = _dot(cg, st.astype(BF16)) * e_cs[:, sl]
        for k in range(hpg // 2):
            xp = xdt[:, g * gw + k * LANES:g * gw + (k + 1) * LANES]
            x2 = jnp.concatenate([jnp.where(lo_half, xp, 0.0), jnp.where(lo_half, 0.0, xp)], axis=0)
            ms = []
            for hh in range(2):
                h = g * hpg + 2 * k + hh
                seg = a_cs[:, h:h + 1] - a_cs_t[h:h + 1, :]
                ms.append((cb * jnp.exp(jnp.where(causal, seg, -jnp.inf))).astype(BF16))
            y_parts.append(y_off[:, k * LANES:(k + 1) * LANES]
                           + _dot(jnp.concatenate(ms, axis=1), x2.astype(BF16)))
            if k % 2 == 1:
                fill()
        xw = (dec[:, sl] * xdt[:, sl]).astype(BF16)
        upd = lax.dot_general(bg, xw, (((0,), (0,)), ((), ())), preferred_element_type=F32)
        state[g] = chunk_dec[:, sl] * st + upd

    y = jnp.concatenate(y_parts, axis=1) + dexp * xs
    y = y * (zz * _sigmoid(zz))
    return jnp.concatenate(
        [_rms(y[:, g * gw:(g + 1) * gw], nw[:, g * gw:(g + 1) * gw]) for g in range(SSM_GROUPS)], axis=1)


def _inproj_mixer_kernel(x_ref, g_ref, kvn_ref, ikg_ref, ikb_ref,
                         wqt_ref, wckv_ref, wiqt_ref, wsm_ref, wiwt_ref, wb_ref, wz_ref, wxbc_ref,
                         wdt_ref, wdtt_ref, wuv_ref, vone_ref,
                         scw_ref, cw_ref, cb_ref, dtb_r_ref, dtb_c_ref, alog_r_ref, alog_c_ref,
                         dexp_ref, nw_ref, tril_ref, triu_ref, eh_ref,
                         qt_out, ckv_out, vt_out, iqt_out, iwt_out, ik_out, bo_out, co_out,
                         ubuf, xbuf, state, *, steps_per_seq):
    P = SUBLANES
    tm = IN_SUBTILES * KT

    @pl.when(pl.program_id(0) % steps_per_seq == 0)
    def _():
        ubuf[0:P, :] = jnp.zeros((P, B_WIDTH), F32)
        xbuf[0:P, :] = jnp.zeros((P, SSM_XBC), F32)
        state[...] = jnp.zeros(state.shape, F32)

    subs = [slice(k * KT, (k + 1) * KT) for k in range(IN_SUBTILES)]
    h = [_rms(x_ref[rows, :], g_ref[...]).astype(BF16) for rows in subs]
    ssd_consts = (dtb_r_ref[...], dtb_c_ref[...], alog_r_ref[...], alog_c_ref[...], dexp_ref[...],
                  nw_ref[...], tril_ref[...], triu_ref[...], eh_ref[...])

    def proj_mixer(k):
        return dict(bm=_dot(h[k], wb_ref[...]),
                    xr=_dot(h[k], wxbc_ref[...]), zz=_dot(h[k], wz_ref[...]),
                    dt=_dot(h[k], wdt_ref[...])[:, :SSM_HEADS],
                    dt_t=_dot_nt(wdtt_ref[...], h[k]))

    def short_conv(k, bm):
        base = P + k * KT
        u = bm[:, B_WIDTH:2 * B_WIDTH] * bm[:, 2 * B_WIDTH:3 * B_WIDTH]
        ubuf[base:base + KT, :] = u
        conv = scw_ref[SHORT_CONV - 1:SHORT_CONV, :] * u
        for j in range(SHORT_CONV - 1):
            off = base - (SHORT_CONV - 1) + j
            conv = conv + scw_ref[j:j + 1, :] * ubuf[off:off + KT, :]
        bo_out[subs[k], :] = (bm[:, :B_WIDTH] * conv).astype(BF16)

    def ssm_conv(k, xr):
        base = P + k * KT
        xbuf[base:base + KT, :] = xr
        xc = cw_ref[SSM_CONV - 1:SSM_CONV, :] * xr + cb_ref[...]
        for j in range(SSM_CONV - 1):
            off = base - (SSM_CONV - 1) + j
            xc = xc + cw_ref[j:j + 1, :] * xbuf[off:off + KT, :]
        return xc * _sigmoid(xc)

    def scan_chunk(k, c, xa, pm, fill):
        cs = slice(c * SSD_L, (c + 1) * SSD_L)
        y = _ssd_chunk(xa[cs, :], pm["zz"][cs, :], pm["dt"][cs, :], pm["dt_t"][:, cs], state, *ssd_consts,
                       fill=fill)
        r0 = k * KT + c * SSD_L
        co_out[r0:r0 + SSD_L, :] = y.astype(BF16)

    def attn_keys(k):
        ckv = _rms(_dot(h[k], wckv_ref[...]), kvn_ref[...]).astype(BF16)
        ckv_out[subs[k], :] = ckv
        ik = _dot(h[k], wsm_ref[...])[:, :IDX_DIM]
        mu = jnp.mean(ik, axis=-1, keepdims=True)
        var = jnp.mean(jnp.square(ik - mu), axis=-1, keepdims=True)
        ik_out[subs[k], :] = ((ik - mu) * lax.rsqrt(var + NORM_EPS) * ikg_ref[...] + ikb_ref[...]).astype(BF16)
        return ckv

    def attn_queries(k):
        qt_out[k] = _dot_nt(wqt_ref[...], h[k]).astype(BF16)
        iqt_out[k] = (_dot_nt(wiqt_ref[...], h[k]) * (IDX_DIM ** -0.5)).astype(BF16)
        iwt_out[k] = _dot_nt(wiwt_ref[...], h[k]) * (IDX_HEADS ** -0.5)

    def attn_values(k, ckv):
        vt_out[0, k] = (_dot_nt(wuv_ref[...], ckv) + vone_ref[...]).astype(BF16)

    blk = KT
    tasks = []

    def fill():
        if tasks:
            tasks.pop(0)()

    def drain():
        while tasks:
            fill()

    def proj_tasks(k, got):
        for name, w_ref in (("bm", wb_ref), ("xr", wxbc_ref), ("zz", wz_ref)):
            for c0 in range(0, w_ref.shape[1], blk):
                tasks.append(lambda name=name, w_ref=w_ref, c0=c0: got.setdefault(name, []).append(
                    _dot(h[k], w_ref[:, c0:c0 + blk])))
        tasks.append(lambda: got.update(dt=_dot(h[k], wdt_ref[...])[:, :SSM_HEADS],
                                        dt_t=_dot_nt(wdtt_ref[...], h[k])))

    pm = proj_mixer(0)
    ckvs = {}
    for k in range(IN_SUBTILES):
        got = {}
        if k + 1 < IN_SUBTILES:
            proj_tasks(k + 1, got)
        else:
            for kk in range(IN_SUBTILES):
                tasks.extend([lambda kk=kk: ckvs.update({kk: attn_keys(kk)}), lambda kk=kk: attn_queries(kk),
                              lambda kk=kk: attn_values(kk, ckvs[kk])])
        short_conv(k, pm["bm"])
        fill()
        xa = ssm_conv(k, pm["xr"])
        fill()
        for c in range(KT // SSD_L):
            scan_chunk(k, c, xa, pm, fill)
        drain()
        if k + 1 < IN_SUBTILES:
            pm = dict(bm=jnp.concatenate(got["bm"], axis=1), xr=jnp.concatenate(got["xr"], axis=1),
                      zz=jnp.concatenate(got["zz"], axis=1), dt=got["dt"], dt_t=got["dt_t"])

    ubuf[0:P, :] = ubuf[tm:tm + P, :]
    xbuf[0:P, :] = xbuf[tm:tm + P, :]


def _inproj_mixer_call(x2, norm_consts, proj_weights, wuv, vone, mixer_params, n_kt):
    t = x2.shape[0]
    assert KT == QB and n_kt % IN_SUBTILES == 0 and KT % SSD_L == 0
    tm = IN_SUBTILES * KT
    steps_per_seq = n_kt // IN_SUBTILES
    scw, cw, cb, dtb, alog, dexp, nw = mixer_params
    L = SSD_L
    tril = jnp.asarray(np.tril(np.ones((L, L), np.float32)), BF16)
    triu = jnp.asarray(np.triu(np.ones((L, L), np.float32)), BF16)
    eh = jnp.asarray(np.tile(np.repeat(np.eye(SSM_HEADS, dtype=np.float32), SSM_HEAD_DIM, axis=1), (3, 1)), BF16)
    mixer_consts = [scw, cw, cb.reshape(1, -1), dtb.reshape(1, -1), dtb.reshape(-1, 1),
                    alog.reshape(1, -1), alog.reshape(-1, 1), dexp.reshape(1, -1), nw.reshape(1, -1),
                    tril, triu, eh]
    wqt, wckv, wiqt, wiwt = proj_weights[0], proj_weights[1], proj_weights[2], proj_weights[4]
    row = lambda w: pl.BlockSpec((tm, w), lambda r: (r, 0))
    rows = lambda w, dt: (row(w), jax.ShapeDtypeStruct((t, w), dt))
    qblk = lambda w, dt: (pl.BlockSpec((IN_SUBTILES, w, QB), lambda r: (r, 0, 0)),
                          jax.ShapeDtypeStruct((t // QB, w, QB), dt))
    hv = wuv.shape[0]
    outs = [qblk(wqt.shape[0], BF16), rows(wckv.shape[1], BF16),
            (pl.BlockSpec((1, IN_SUBTILES, hv, KT), lambda r: (r // steps_per_seq, r % steps_per_seq, 0, 0)),
             jax.ShapeDtypeStruct((t // (n_kt * KT), n_kt, hv, KT), BF16)),
            qblk(wiqt.shape[0], BF16), qblk(wiwt.shape[0], F32), rows(IDX_DIM, BF16),
            rows(B_WIDTH, BF16), rows(SSM_INNER, BF16)]
    consts = [*norm_consts, *proj_weights, wuv, vone, *mixer_consts]
    return pl.pallas_call(
        functools.partial(_inproj_mixer_kernel, steps_per_seq=steps_per_seq),
        grid=(t // tm,),
        in_specs=[row(D_MODEL)] + [_const_spec(c.shape) for c in consts],
        out_specs=[spec for spec, _ in outs],
        out_shape=[shape for _, shape in outs],
        scratch_shapes=[pltpu.VMEM((tm + SUBLANES, B_WIDTH), F32),
                        pltpu.VMEM((tm + SUBLANES, SSM_XBC), F32),
                        pltpu.VMEM((SSM_GROUPS, SSM_STATE, SSM_INNER // SSM_GROUPS), F32)],
        compiler_params=_params(("arbitrary",)),
        name="inproj_conv_ssd",
    )(x2, *consts)


def _dense_kernel(x_ref, a_ref, b_ref, c_ref, p_ref, g_post_ref, g_pre_ref, g_fpost_ref,
                  woa_ref, wob_ref, woc_ref, wg_ref, wu_ref, wd_ref, wpg_ref, wpp_ref, o_ref):
    subs = [slice(r * TM_SUB, (r + 1) * TM_SUB) for r in range(ROW_SUBTILES)]
    mix = [_dot(a_ref[rows, :], woa_ref[...]) + _dot(b_ref[rows, :], wob_ref[...])
           + _dot(c_ref[rows, :], woc_ref[...]) for rows in subs]
    ple = [_dot(p_ref[rows, :].astype(BF16), wpp_ref[...]) for rows in subs]
    x = [x_ref[rows, :] + _rms(m, g_post_ref[...]) for rows, m in zip(subs, mix)]
    h = [_rms(xx, g_pre_ref[...]).astype(BF16) for xx in x]
    gate = [_dot(hh, wg_ref[...]) for hh in h]
    up = [_dot(hh, wu_ref[...]) for hh in h]
    act = [(g * _sigmoid(g) * u).astype(BF16) for g, u in zip(gate, up)]
    f = [_dot(aa, wd_ref[...]) for aa in act]
    x = [xx + _rms(ff, g_fpost_ref[...]) for xx, ff in zip(x, f)]
    pg = [_sigmoid(_dot(xx.astype(BF16), wpg_ref[...])) for xx in x]
    for rows, xx, g, e in zip(subs, x, pg, ple):
        o_ref[rows, :] = xx + g * e


def _dense_call(x2, a, b, c, p_all, layer, g_post, g_pre, g_fpost, woa, wob, woc, wg, wu, wd, wpg, wpp):
    t = x2.shape[0]
    tm = TM_OUT
    row = lambda w: pl.BlockSpec((tm, w), lambda r: (r, 0))
    steps_per_seq = p_all.shape[2] // tm
    p_spec = pl.BlockSpec((None, None, tm, PLE_DIM),
                          lambda r: (layer, r // steps_per_seq, r % steps_per_seq, 0))
    consts = [g_post, g_pre, g_fpost, woa, wob, woc, wg, wu, wd, wpg, wpp]
    return pl.pallas_call(
        _dense_kernel,
        grid=(t // tm,),
        in_specs=[row(D_MODEL), row(a.shape[1]), row(b.shape[1]), row(c.shape[1]), p_spec]
                 + [_const_spec(w.shape) for w in consts],
        out_specs=row(D_MODEL),
        out_shape=jax.ShapeDtypeStruct((t, D_MODEL), F32),
        compiler_params=_params(("parallel",)),
        name="outproj_ffn_ple",
    )(x2, a, b, c, p_all, *consts)


def kernel(x, p, pre_mix_norm, post_mix_norm, pre_ffn_norm, post_ffn_norm, w_in, kv_norm, idx_k_norm_g, idx_k_norm_b, w_uk, w_uv, rel_bias, short_conv_w, ssm_conv_w, ssm_conv_b, ssm_dt_bias, ssm_a_log, ssm_d, ssm_norm, w_out, w_ffn_gate, w_ffn_up, w_ffn_down, w_ple_proj, w_ple_gate):
    bsz, s, d = x.shape
    assert d == D_MODEL and s % (IN_SUBTILES * KT) == 0 and s % QB == 0 and s % SSD_L == 0
    assert (bsz * s) % TM_OUT == 0
    t = bsz * s
    k_top = min(TOPK_MAX, s // 4)

    bucket_np, near, far_bucket = _bucket_tiles()
    bias_tiles = _bias_tiles_call(rel_bias.astype(F32), jnp.asarray(bucket_np), far_bucket)
    tri = jnp.asarray(np.tril(np.ones((KT, KT), np.float32)), BF16)
    row1 = lambda v: v.reshape(1, -1).astype(F32)
    vone_np = np.zeros((A_HEADS, V_ROWS, 1), np.float32)
    vone_np[:, A_V_DIM] = 1.0
    vone = jnp.asarray(vone_np.reshape(A_HEADS * V_ROWS, 1))

    x2 = x.reshape(t, d)
    p_all = p.astype(F32)
    w_in_all = w_in.astype(F32)
    for i in range(DEPTH):
        wparts = _wprep_call(w_in_all, i)
        wuv_t = jnp.transpose(w_uv[i], (1, 2, 0)).astype(BF16)
        wuv_t = jnp.pad(wuv_t, ((0, 0), (0, V_ROWS - A_V_DIM), (0, 0))).reshape(A_HEADS * V_ROWS, A_KV_RANK)
        wq, wckv, wiq, wsm, wb, wz, wxbc, wdt = wparts
        wiw_t = wsm[:, IW_OFF:IW_OFF + SUBLANES].T
        wdt_t = wdt[:, :SSM_HEADS].T
        q_t, ckv, vt4, iq_t, iw_t, ik, b_out, c_out = _inproj_mixer_call(
            x2, (row1(pre_mix_norm[i]), row1(kv_norm[i]), row1(idx_k_norm_g[i]), row1(idx_k_norm_b[i])),
            (wq.T, wckv, wiq.T, wsm, wiw_t, wb, wz, wxbc, wdt, wdt_t), wuv_t, vone,
            (short_conv_w[i].astype(F32), ssm_conv_w[i].astype(F32), ssm_conv_b[i].astype(F32),
             ssm_dt_bias[i].astype(F32), ssm_a_log[i].astype(F32),
             jnp.repeat(ssm_d[i].astype(F32), SSM_HEAD_DIM), ssm_norm[i].astype(F32)), s // KT)

        a_out = _attn_call(
            q_t, iq_t, iw_t, ckv.reshape(bsz, s, A_KV_RANK), vt4, ik.reshape(bsz, s, IDX_DIM),
            jnp.transpose(w_uk[i], (1, 0, 2)).astype(BF16),
            bias_tiles, tri, k_top=k_top, near=near)

        wo = w_out[i].astype(BF16)
        na, nbw = A_HEADS * A_V_DIM, A_HEADS * A_V_DIM + B_WIDTH
        x2 = _dense_call(
            x2, a_out, b_out, c_out, p_all, i,
            row1(post_mix_norm[i]), row1(pre_ffn_norm[i]), row1(post_ffn_norm[i]),
            wo[:na], wo[na:nbw], wo[nbw:], w_ffn_gate[i].astype(BF16), w_ffn_up[i].astype(BF16),
            w_ffn_down[i].astype(BF16), w_ple_gate[i].astype(BF16), w_ple_proj[i].astype(BF16))
    return x2.reshape(bsz, s, d)
```
